```python
import math
import jax, jax.numpy as jnp
from jax import lax
import numpy as np

D_MODEL = 1024
BATCH = 2
SEQ = 8192
DEPTH = 4
DEC_BATCH = 128
DEC_SEQ = 8
PAST_LEN = 8192
PAGE_SIZE = 128

N_A_LAYERS = DEPTH // 2
N_B_LAYERS = DEPTH - N_A_LAYERS
N_META = 16
SSM_EXPAND = 2
D_INNER = SSM_EXPAND * D_MODEL
SSM_HEAD_DIM = 64
N_SSM_HEADS = D_INNER // SSM_HEAD_DIM
N_GROUPS = 8
HEADS_PER_GROUP = N_SSM_HEADS // N_GROUPS
D_STATE = 128
CONV_K = 4
CONV_DIM = D_INNER + 2 * N_GROUPS * D_STATE
IN_PROJ_DIM = D_INNER + CONV_DIM + N_SSM_HEADS
SSD_CHUNK = 128
ATTN_HEAD_DIM = 64
N_Q_HEADS = D_MODEL // ATTN_HEAD_DIM
N_KV_HEADS = 4
Q_PER_KV = N_Q_HEADS // N_KV_HEADS
WINDOW = 128
ATTN_SCALE = ATTN_HEAD_DIM ** -0.5
D_FF = 4 * D_MODEL
EPS = 1e-5

kernel_name = 'yoco_mamba2_swa_sink_hybrid_step'


def rmsnorm(x, w):
    xf = x.astype(jnp.float32)
    xf = xf * lax.rsqrt(jnp.mean(xf * xf, axis=-1, keepdims=True) + EPS)
    return (xf * w.astype(jnp.float32)).astype(x.dtype)


def gated_group_rmsnorm(y, z, w):
    g = (y * jax.nn.silu(z)).astype(jnp.float32)
    shp = g.shape
    g = g.reshape(shp[:-1] + (N_GROUPS, shp[-1] // N_GROUPS))
    g = g * lax.rsqrt(jnp.mean(g * g, axis=-1, keepdims=True) + EPS)
    return (g.reshape(shp) * w.astype(jnp.float32)).astype(y.dtype)


def sq_relu_mlp(h, norm_w, w_up, w_down):
    return h + jnp.square(jax.nn.relu(rmsnorm(h, norm_w) @ w_up)) @ w_down


def _chunk_len(length):
    return SSD_CHUNK if length % SSD_CHUNK == 0 else length


def ssd_scan(x, dt, a_neg, bm, cm, h0):
    f32 = jnp.float32
    b, L, g, r, p = x.shape
    chunk = _chunk_len(L)
    nc = L // chunk

    def blocks(t):
        return t.astype(f32).reshape((b, nc, chunk) + t.shape[2:])

    xdt = blocks(x.astype(f32) * dt[..., None])
    acs = jnp.cumsum(blocks(dt * a_neg), axis=2)
    bb, cc = blocks(bm), blocks(cm)
    causal = jnp.tril(jnp.ones((chunk, chunk), bool))[:, :, None, None]
    seg = acs[:, :, :, None] - acs[:, :, None, :]
    decay = jnp.exp(jnp.where(causal, seg, -jnp.inf))
    cb = jnp.einsum('bclgn,bcsgn->bclsg', cc, bb)
    y_diag = jnp.einsum('bclsgr,bcsgrp->bclgrp', cb[..., None] * decay, xdt)
    decay_to_end = jnp.exp(acs[:, :, -1:] - acs)
    chunk_states = jnp.einsum('bclgn,bclgrp->bcgrpn', bb, decay_to_end[..., None] * xdt)
    chunk_decay = jnp.exp(acs[:, :, -1])

    def step(h, inp):
        s, d = inp
        return d[..., None, None] * h + s, h

    h_final, h_start = lax.scan(step, h0.astype(f32),
                                (jnp.moveaxis(chunk_states, 1, 0), jnp.moveaxis(chunk_decay, 1, 0)))
    h_start = jnp.moveaxis(h_start, 0, 1)
    y_off = jnp.einsum('bclgn,bcgrpn->bclgrp', cc, h_start) * jnp.exp(acs)[..., None]
    y = (y_diag + y_off).reshape(b, L, g, r, p)
    return y, h_final


def mamba_mixer(h, conv_state, ssm_state, norm_w, w_in, conv_w, conv_b, dt_bias, a_log, d_skip, gate_w, w_out):
    b, L, _ = h.shape
    zxbcdt = rmsnorm(h, norm_w) @ w_in
    z = zxbcdt[..., :D_INNER]
    xbc = zxbcdt[..., D_INNER:D_INNER + CONV_DIM]
    dt_raw = zxbcdt[..., D_INNER + CONV_DIM:]
    xpad = jnp.concatenate([conv_state.astype(xbc.dtype), xbc], axis=1)
    new_conv = xpad[:, L:]
    conv = conv_b
    for k in range(CONV_K):
        conv = conv + xpad[:, k:k + L] * conv_w[k]
    xbc = jax.nn.silu(conv)
    xs = xbc[..., :D_INNER].reshape(b, L, N_GROUPS, HEADS_PER_GROUP, SSM_HEAD_DIM)
    bm = xbc[..., D_INNER:D_INNER + N_GROUPS * D_STATE].reshape(b, L, N_GROUPS, D_STATE)
    cm = xbc[..., D_INNER + N_GROUPS * D_STATE:].reshape(b, L, N_GROUPS, D_STATE)
    dt = jax.nn.softplus(dt_raw.astype(jnp.float32) + dt_bias.astype(jnp.float32))
    dt = dt.reshape(b, L, N_GROUPS, HEADS_PER_GROUP)
    a_neg = -jnp.exp(a_log.astype(jnp.float32)).reshape(N_GROUPS, HEADS_PER_GROUP)
    h0 = ssm_state.reshape(b, N_GROUPS, HEADS_PER_GROUP, SSM_HEAD_DIM, D_STATE)
    y, h_new = ssd_scan(xs, dt, a_neg, bm, cm, h0)
    y = y + d_skip.astype(jnp.float32).reshape(N_GROUPS, HEADS_PER_GROUP)[:, :, None] * xs.astype(jnp.float32)
    y = y.reshape(b, L, D_INNER).astype(h.dtype)
    out = gated_group_rmsnorm(y, z, gate_w) @ w_out
    return out, new_conv, h_new.reshape(b, N_SSM_HEADS, SSM_HEAD_DIM, D_STATE).astype(ssm_state.dtype)


def shared_kv(h, kv_norm_w, w_kv):
    b, L, _ = h.shape
    kv = (rmsnorm(h, kv_norm_w) @ w_kv).reshape(b, L, 2, N_KV_HEADS, ATTN_HEAD_DIM)
    return kv[:, :, 0], kv[:, :, 1]


def queries(h, norm_w, w_q):
    b, L, _ = h.shape
    return (rmsnorm(h, norm_w) @ w_q).reshape(b, L, N_KV_HEADS, Q_PER_KV, ATTN_HEAD_DIM)


def sink_softmax(scores, mask, sinks):
    s = jnp.where(mask, scores, -jnp.inf)
    sink = jnp.broadcast_to(sinks.astype(jnp.float32).reshape(N_KV_HEADS, Q_PER_KV, 1, 1), s.shape[:-1] + (1,))
    return jax.nn.softmax(jnp.concatenate([s, sink], axis=-1), axis=-1)[..., :-1]


def swa_prompt(q, k, v, k_meta, v_meta, sinks):
    b, S = q.shape[:2]
    m = k_meta.shape[0]
    nb = S // WINDOW
    qb = q.reshape(b, nb, WINDOW, N_KV_HEADS, Q_PER_KV, ATTN_HEAD_DIM)

    def band(t, t_meta):
        tb = t.reshape(b, nb, WINDOW, N_KV_HEADS, ATTN_HEAD_DIM)
        prev = jnp.concatenate([jnp.zeros_like(tb[:, :1]), tb[:, :-1]], axis=1)
        meta = jnp.broadcast_to(t_meta, (b, nb) + t_meta.shape)
        return jnp.concatenate([meta, prev, tb], axis=2)

    keys, vals = band(k, k_meta), band(v, v_meta)
    scores = jnp.einsum('bnqkrd,bnskd->bnkrqs', qb, keys, preferred_element_type=jnp.float32) * ATTN_SCALE
    qi = jnp.arange(WINDOW)[:, None]
    ci = jnp.arange(WINDOW)[None, :]
    blk = jnp.arange(nb)[:, None, None]
    meta_m = jnp.ones((nb, WINDOW, m), bool)
    prev_m = (ci > qi)[None] & (blk > 0)
    cur_m = jnp.broadcast_to((ci <= qi)[None], (nb, WINDOW, WINDOW))
    mask = jnp.concatenate([meta_m, prev_m, cur_m], axis=-1)
    probs = sink_softmax(scores, mask[None, :, None, None], sinks)
    out = jnp.einsum('bnkrqs,bnskd->bnqkrd', probs.astype(vals.dtype), vals)
    return out.reshape(b, S, N_Q_HEADS * ATTN_HEAD_DIM)


def swa_sample(q, k_new, v_new, k_buf, v_buf, k_meta, v_meta, sinks):
    b, T = q.shape[:2]
    w_buf = k_buf.shape[1]
    m = k_meta.shape[0]
    keys = jnp.concatenate([jnp.broadcast_to(k_meta, (b,) + k_meta.shape), k_buf, k_new], axis=1)
    vals = jnp.concatenate([jnp.broadcast_to(v_meta, (b,) + v_meta.shape), v_buf, v_new], axis=1)
    scores = jnp.einsum('bqkrd,bskd->bkrqs', q, keys, preferred_element_type=jnp.float32) * ATTN_SCALE
    qi = jnp.arange(T)[:, None]
    pos_q = PAST_LEN + qi
    pos_buf = PAST_LEN - w_buf + jnp.arange(w_buf)[None, :]
    buf_m = (pos_q - pos_buf < WINDOW) & (pos_buf >= m)
    ci = jnp.arange(T)[None, :]
    new_m = (ci <= qi) & (qi - ci < WINDOW)
    mask = jnp.concatenate([jnp.ones((T, m), bool), buf_m, new_m], axis=-1)
    probs = sink_softmax(scores, mask[None, None, None], sinks)
    out = jnp.einsum('bkrqs,bskd->bqkrd', probs.astype(vals.dtype), vals)
    return out.reshape(b, T, N_Q_HEADS * ATTN_HEAD_DIM)


def setup_inputs(seed: int = 0) -> dict:
    key = jax.random.key(seed)
    ks = jax.random.split(key, 32)
    f32 = jnp.float32

    def nrm(k, shape, scale):
        return scale * jax.random.normal(k, shape, f32)

    w_buf = min(WINDOW, PAST_LEN)
    na, nb = N_A_LAYERS, N_B_LAYERS
    dt0 = jnp.exp(jax.random.uniform(ks[10], (na, N_SSM_HEADS), f32, math.log(1e-3), math.log(1e-1)))
    return {
        'x_prompt': nrm(ks[0], (BATCH, SEQ, D_MODEL), 1.0),
        'x_sample': nrm(ks[1], (DEC_BATCH, DEC_SEQ, D_MODEL), 1.0),
        'state_conv': nrm(ks[2], (na, DEC_BATCH, CONV_K - 1, CONV_DIM), 1.0),
        'state_ssm': nrm(ks[3], (na, DEC_BATCH, N_SSM_HEADS, SSM_HEAD_DIM, D_STATE), 0.1),
        'cache_k_win': nrm(ks[4], (DEC_BATCH, w_buf, N_KV_HEADS, ATTN_HEAD_DIM), 1.0),
        'cache_v_win': nrm(ks[5], (DEC_BATCH, w_buf, N_KV_HEADS, ATTN_HEAD_DIM), 1.0),
        'meta_tokens': nrm(ks[6], (N_META, D_MODEL), 1.0),
        'a_norm_w': 1.0 + nrm(ks[7], (na, D_MODEL), 0.02),
        'a_in_proj': nrm(ks[8], (na, D_MODEL, IN_PROJ_DIM), D_MODEL ** -0.5),
        'a_conv_w': nrm(ks[9], (na, CONV_K, CONV_DIM), CONV_K ** -0.5),
        'a_conv_b': nrm(ks[11], (na, CONV_DIM), 0.02),
        'a_dt_bias': dt0 + jnp.log(-jnp.expm1(-dt0)),
        'a_log': jnp.log(jax.random.uniform(ks[12], (na, N_SSM_HEADS), f32, 1.0, 16.0)),
        'a_d_skip': 1.0 + nrm(ks[13], (na, N_SSM_HEADS), 0.02),
        'a_gate_norm_w': 1.0 + nrm(ks[14], (na, D_INNER), 0.02),
        'a_out_proj': nrm(ks[15], (na, D_INNER, D_MODEL), D_INNER ** -0.5),
        'kv_norm_w': 1.0 + nrm(ks[16], (D_MODEL,), 0.02),
        'w_kv': nrm(ks[17], (D_MODEL, 2 * N_KV_HEADS * ATTN_HEAD_DIM), D_MODEL ** -0.5),
        'b_norm_w': 1.0 + nrm(ks[18], (nb, D_MODEL), 0.02),
        'w_q': nrm(ks[19], (nb, D_MODEL, N_Q_HEADS * ATTN_HEAD_DIM), D_MODEL ** -0.5),
        'attn_sinks': nrm(ks[20], (nb, N_Q_HEADS), 0.5),
        'w_o': nrm(ks[21], (nb, N_Q_HEADS * ATTN_HEAD_DIM, D_MODEL), (N_Q_HEADS * ATTN_HEAD_DIM) ** -0.5),
        'mlp_norm_w': 1.0 + nrm(ks[22], (DEPTH, D_MODEL), 0.02),
        'w_up': nrm(ks[23], (DEPTH, D_MODEL, D_FF), D_MODEL ** -0.5),
        'w_down': nrm(ks[24], (DEPTH, D_FF, D_MODEL), D_FF ** -0.5),
        'final_norm_w': 1.0 + nrm(ks[25], (D_MODEL,), 0.02),
    }


def reference(x_prompt, x_sample, state_conv, state_ssm, cache_k_win, cache_v_win, meta_tokens,
              a_norm_w, a_in_proj, a_conv_w, a_conv_b, a_dt_bias, a_log, a_d_skip, a_gate_norm_w, a_out_proj,
              kv_norm_w, w_kv, b_norm_w, w_q, attn_sinks, w_o, mlp_norm_w, w_up, w_down, final_norm_w):
    n_prompt = x_prompt.shape[0]
    w_buf = cache_k_win.shape[1]
    hm = meta_tokens[None].astype(x_prompt.dtype)
    hp = x_prompt
    hs = x_sample
    conv_p_list, ssm_p_list, conv_s_list, ssm_s_list = [], [], [], []
    k_meta = v_meta = k_p = v_p = k_s = v_s = None
    for layer in range(DEPTH):
        if layer < N_A_LAYERS:
            i = layer
            prm = (a_norm_w[i], a_in_proj[i], a_conv_w[i], a_conv_b[i], a_dt_bias[i], a_log[i],
                   a_d_skip[i], a_gate_norm_w[i], a_out_proj[i])
            zero_conv = jnp.zeros((1, CONV_K - 1, CONV_DIM), hm.dtype)
            zero_ssm = jnp.zeros((1, N_SSM_HEADS, SSM_HEAD_DIM, D_STATE), hm.dtype)
            ym, conv_m, ssm_m = mamba_mixer(hm, zero_conv, zero_ssm, *prm)
            yp, conv_p, ssm_p = mamba_mixer(hp, jnp.broadcast_to(conv_m, (n_prompt,) + conv_m.shape[1:]),
                                            jnp.broadcast_to(ssm_m, (n_prompt,) + ssm_m.shape[1:]), *prm)
            ys, conv_s, ssm_s = mamba_mixer(hs, state_conv[i], state_ssm[i], *prm)
            hm, hp, hs = hm + ym, hp + yp, hs + ys
            conv_p_list.append(conv_p)
            ssm_p_list.append(ssm_p)
            conv_s_list.append(conv_s)
            ssm_s_list.append(ssm_s)
            hm = sq_relu_mlp(hm, mlp_norm_w[layer], w_up[layer], w_down[layer])
        else:
            j = layer - N_A_LAYERS
            if j == 0:
                km, vm = shared_kv(hm, kv_norm_w, w_kv)
                k_meta, v_meta = km[0], vm[0]
                k_p, v_p = shared_kv(hp, kv_norm_w, w_kv)
                k_s, v_s = shared_kv(hs, kv_norm_w, w_kv)
            q_p = queries(hp, b_norm_w[j], w_q[j])
            q_s = queries(hs, b_norm_w[j], w_q[j])
            hp = hp + swa_prompt(q_p, k_p, v_p, k_meta, v_meta, attn_sinks[j]) @ w_o[j]
            hs = hs + swa_sample(q_s, k_s, v_s, cache_k_win, cache_v_win, k_meta, v_meta, attn_sinks[j]) @ w_o[j]
        hp = sq_relu_mlp(hp, mlp_norm_w[layer], w_up[layer], w_down[layer])
        hs = sq_relu_mlp(hs, mlp_norm_w[layer], w_up[layer], w_down[layer])
    y_prompt = rmsnorm(hp, final_norm_w)
    y_sample = rmsnorm(hs, final_norm_w)
    prompt_state_conv = jnp.stack(conv_p_list)
    prompt_state_ssm = jnp.stack(ssm_p_list)
    prompt_cache_k_win = k_p[:, -w_buf:]
    prompt_cache_v_win = v_p[:, -w_buf:]
    sample_state_conv = jnp.stack(conv_s_list)
    sample_state_ssm = jnp.stack(ssm_s_list)
    sample_cache_k_win = jnp.concatenate([cache_k_win, k_s], axis=1)[:, -w_buf:]
    sample_cache_v_win = jnp.concatenate([cache_v_win, v_s], axis=1)[:, -w_buf:]
    return (y_prompt, y_sample, prompt_state_conv, prompt_state_ssm, prompt_cache_k_win, prompt_cache_v_win,
            sample_state_conv, sample_state_ssm, sample_cache_k_win, sample_cache_v_win)
```

```python
import functools

import jax
import jax.numpy as jnp
from jax import lax
from jax.experimental import pallas as pl
from jax.experimental.pallas import tpu as pltpu

F32 = jnp.float32
BF16 = jnp.bfloat16

N_GROUPS = 8
SSM_HEAD_DIM = 64
D_STATE = 128
CONV_K = 4
SSD_CHUNK = 128
ATTN_HEAD_DIM = 64
N_KV_HEADS = 4
WINDOW = 128
PAST_LEN = 8192
EPS = 1e-5

LANES = 128
SUBLANES = 8
VMEM_LIMIT_BYTES = 52 * 1024 * 1024

_NT = (((1,), (1,)), ((), ()))
_TN = (((0,), (0,)), ((), ()))


def _params(*sem):
    return pltpu.CompilerParams(dimension_semantics=sem, vmem_limit_bytes=VMEM_LIMIT_BYTES)


def _rms(x, w):
    ms = jnp.mean(x * x, axis=-1, keepdims=True)
    return x * lax.rsqrt(ms + EPS) * w


def _silu(x):
    return x * jax.nn.sigmoid(x)


def _row_tile(m, cap):
    t = min(m, cap)
    assert m % t == 0, (m, t)
    return t


def _norm_matmul_kernel(x_ref, nw_ref, w_ref, o_ref, xn_ref):
    @pl.when(pl.program_id(1) == 0)
    def _():
        xn_ref[...] = _rms(x_ref[...], nw_ref[...]).astype(BF16)

    o_ref[...] = jnp.dot(xn_ref[...], w_ref[...], preferred_element_type=F32)


def norm_matmul(x, nw, w, *, tm_cap=1024, tn_cap=512):
    m, d = x.shape
    n = w.shape[1]
    tm, tn = _row_tile(m, tm_cap), _row_tile(n, tn_cap)
    return pl.pallas_call(
        _norm_matmul_kernel,
        grid=(m // tm, n // tn),
        in_specs=[
            pl.BlockSpec((tm, d), lambda i, j: (i, 0)),
            pl.BlockSpec((1, d), lambda i, j: (0, 0)),
            pl.BlockSpec((d, tn), lambda i, j: (0, j)),
        ],
        out_specs=pl.BlockSpec((tm, tn), lambda i, j: (i, j)),
        out_shape=jax.ShapeDtypeStruct((m, n), F32),
        scratch_shapes=[pltpu.VMEM((tm, d), BF16)],
        compiler_params=_params("parallel", "arbitrary"),
        name="norm_matmul",
    )(x, nw.reshape(1, d), w)


def _in_proj_kernel(x_ref, nw_ref, w_ref, wdt_ref, o_ref, dt_ref, xn_ref):
    @pl.when(pl.program_id(1) == 0)
    def _():
        xn = _rms(x_ref[...], nw_ref[...]).astype(BF16)
        xn_ref[...] = xn
        dt_ref[...] = jnp.dot(xn, wdt_ref[...], preferred_element_type=F32)

    o_ref[...] = jnp.dot(xn_ref[...], w_ref[...], preferred_element_type=F32)


def in_proj(x, nw, w, wdt, *, tm_cap=1024, tn_cap=512):
    m, d = x.shape
    n = w.shape[1]
    tm, tn = _row_tile(m, tm_cap), _row_tile(n, tn_cap)
    return pl.pallas_call(
        _in_proj_kernel,
        grid=(m // tm, n // tn),
        in_specs=[
            pl.BlockSpec((tm, d), lambda i, j: (i, 0)),
            pl.BlockSpec((1, d), lambda i, j: (0, 0)),
            pl.BlockSpec((d, tn), lambda i, j: (0, j)),
            pl.BlockSpec((d, LANES), lambda i, j: (0, 0)),
        ],
        out_specs=[
            pl.BlockSpec((tm, tn), lambda i, j: (i, j)),
            pl.BlockSpec((tm, LANES), lambda i, j: (i, 0)),
        ],
        out_shape=[jax.ShapeDtypeStruct((m, n), F32), jax.ShapeDtypeStruct((m, LANES), F32)],
        scratch_shapes=[pltpu.VMEM((tm, d), BF16)],
        compiler_params=_params("parallel", "arbitrary"),
        name="in_proj",
    )(x, nw.reshape(1, d), w, wdt)


def _matmul_residual_kernel(a_ref, w_ref, r_ref, o_ref):
    o_ref[...] = r_ref[...] + jnp.dot(a_ref[...], w_ref[...], preferred_element_type=F32)


def matmul_residual(a, w, res, *, tm_cap=512):
    m, k = a.shape
    n = w.shape[1]
    tm = _row_tile(m, tm_cap)
    return pl.pallas_call(
        _matmul_residual_kernel,
        grid=(m // tm,),
        in_specs=[
            pl.BlockSpec((tm, k), lambda i: (i, 0)),
            pl.BlockSpec((k, n), lambda i: (0, 0)),
            pl.BlockSpec((tm, n), lambda i: (i, 0)),
        ],
        out_specs=pl.BlockSpec((tm, n), lambda i: (i, 0)),
        out_shape=jax.ShapeDtypeStruct((m, n), F32),
        compiler_params=_params("parallel"),
        name="matmul_residual",
    )(a, w, res)


def _mlp_kernel(x_ref, nw_ref, wu_ref, wd_ref, fw_ref, o_ref, xn_ref, *, final_norm):
    f = pl.program_id(1)

    @pl.when(f == 0)
    def _():
        x = x_ref[...]
        xn_ref[...] = _rms(x, nw_ref[...]).astype(BF16)
        o_ref[...] = x

    h = jnp.dot(xn_ref[...], wu_ref[...], preferred_element_type=F32)
    h = jnp.square(jnp.maximum(h, 0.0)).astype(BF16)
    o_ref[...] += jnp.dot(h, wd_ref[...], preferred_element_type=F32)

    if final_norm:
        @pl.when(f == pl.num_programs(1) - 1)
        def _():
            o_ref[...] = _rms(o_ref[...], fw_ref[...])


def mlp(x, nw, wu, wd, fw, *, final_norm, tm_cap=1024, tf_cap=512):
    m, d = x.shape
    dff = wu.shape[1]
    tm, tf = _row_tile(m, tm_cap), _row_tile(dff, tf_cap)
    return pl.pallas_call(
        functools.partial(_mlp_kernel, final_norm=final_norm),
        grid=(m // tm, dff // tf),
        in_specs=[
            pl.BlockSpec((tm, d), lambda i, f: (i, 0)),
            pl.BlockSpec((1, d), lambda i, f: (0, 0)),
            pl.BlockSpec((d, tf), lambda i, f: (0, f)),
            pl.BlockSpec((tf, d), lambda i, f: (f, 0)),
            pl.BlockSpec((1, d), lambda i, f: (0, 0)),
        ],
        out_specs=pl.BlockSpec((tm, d), lambda i, f: (i, 0)),
        out_shape=jax.ShapeDtypeStruct((m, d), F32),
        scratch_shapes=[pltpu.VMEM((tm, d), BF16)],
        compiler_params=_params("parallel", "arbitrary"),
        name="mlp",
    )(x, nw.reshape(1, d), wu, wd, fw.reshape(1, d))


def _softplus(x):
    return jnp.maximum(x, 0.0) + jnp.log1p(jnp.exp(-jnp.abs(x)))


def _split3(a):
    hi = a.astype(BF16)
    r1 = a - hi.astype(F32)
    mid = r1.astype(BF16)
    lo = (r1 - mid.astype(F32)).astype(BF16)
    return hi, mid, lo


def _ssd_kernel(xbc_ref, z_ref, dt_ref, conv0_ref, ssm0_ref,
                cw_ref, cb_ref, dtb_ref, alog_ref, dsk_ref, gw_ref,
                g_ref, convo_ref, ssmo_ref,
                xpad_ref, act_ref, state_ref, y_ref, *, q, n_heads):
    c = pl.program_id(1)
    p = SSM_HEAD_DIM
    d_inner = n_heads * p
    hpg = n_heads // N_GROUPS
    conv_dim = d_inner + 2 * N_GROUPS * D_STATE
    pad = SUBLANES

    @pl.when(c == 0)
    def _():
        xpad_ref[0:pad, :] = conv0_ref[0]
        state_ref[...] = ssm0_ref[0]

    xpad_ref[pad:pad + q, :] = xbc_ref[0]
    cblk = 512
    for j in range(conv_dim // cblk):
        cs = slice(j * cblk, (j + 1) * cblk)
        conv = cb_ref[:, cs]
        for k in range(CONV_K):
            off = pad - (CONV_K - 1) + k
            conv = conv + xpad_ref[off:off + q, cs] * cw_ref[k:k + 1, cs]
        act_ref[:, cs] = _silu(conv)
    xpad_ref[0:pad, :] = xpad_ref[q:q + pad, :]

    dt = _softplus(dt_ref[0] + dtb_ref[...])
    a = dt * (-jnp.exp(alog_ref[...]))
    row = lax.broadcasted_iota(jnp.int32, (q, q), 0)
    col = lax.broadcasted_iota(jnp.int32, (q, q), 1)
    causal = row >= col
    tri = jnp.where(causal, 1.0, 0.0).astype(BF16)
    acs = None
    for part in _split3(a):
        t = jnp.dot(tri, part, preferred_element_type=F32)
        acs = t if acs is None else acs + t
    if q < LANES:
        acs_sq = jnp.concatenate([acs, jnp.zeros((LANES - q, LANES), F32)], axis=0)
    else:
        acs_sq = acs
    acs_t = acs_sq.T
    eacs = jnp.exp(acs)
    last = acs[q - 1:q, :]
    dte = jnp.exp(last - acs)
    cdecay = jnp.exp(last)

    for g in range(N_GROUPS):
        b0 = d_inner + g * D_STATE
        c0 = d_inner + N_GROUPS * D_STATE + g * D_STATE
        bb = act_ref[:, b0:b0 + D_STATE].astype(BF16)
        cc = act_ref[:, c0:c0 + D_STATE].astype(BF16)
        cbm = lax.dot_general(cc, bb, _NT, preferred_element_type=F32)
        for pair in range(hpg // 2):
            xs2 = act_ref[:, (g * hpg + 2 * pair) * p:(g * hpg + 2 * pair + 2) * p]
            dsk2 = dsk_ref[:, (g * hpg + 2 * pair) * p:(g * hpg + 2 * pair + 2) * p]
            ys = []
            for e in range(2):
                h = g * hpg + 2 * pair + e
                xs = xs2[:, e * p:(e + 1) * p]
                seg = acs[:, h:h + 1] - acs_t[h:h + 1, 0:q]
                decay = jnp.exp(jnp.where(causal, seg, -jnp.inf))
                m = (cbm * decay).astype(BF16)
                xdt = xs * dt[:, h:h + 1]
                y = jnp.dot(m, xdt.astype(BF16), preferred_element_type=F32)
                st = state_ref[h * p:(h + 1) * p, :]
                yoff = lax.dot_general(cc, st.astype(BF16), _NT, preferred_element_type=F32)
                y = y + yoff * eacs[:, h:h + 1] + dsk2[:, e * p:(e + 1) * p] * xs
                ys.append(y)
                xdtw = (xdt * dte[:, h:h + 1]).astype(BF16)
                snew = lax.dot_general(xdtw, bb, _TN, preferred_element_type=F32)
                state_ref[h * p:(h + 1) * p, :] = (
                    jnp.broadcast_to(cdecay[:, h:h + 1], (p, D_STATE)) * st + snew)
            y_ref[:, (g * hpg + 2 * pair) * p:(g * hpg + 2 * pair + 2) * p] = (
                jnp.concatenate(ys, axis=1))

    gsz = d_inner // N_GROUPS
    for g in range(N_GROUPS):
        gs = slice(g * gsz, (g + 1) * gsz)
        gg = y_ref[:, gs] * _silu(z_ref[0, :, gs])
        ms = jnp.mean(gg * gg, axis=-1, keepdims=True)
        g_ref[0, :, gs] = (gg * lax.rsqrt(ms + EPS) * gw_ref[:, gs]).astype(BF16)

    @pl.when(c == pl.num_programs(1) - 1)
    def _():
        convo_ref[0] = xpad_ref[0:pad, :]
        ssmo_ref[0] = state_ref[...]


def ssd_mixer(xbcz, dt, conv0, ssm0, cw, cb, dtb, alog, dsk, gw, *, shared_state):
    bsz, length, _ = xbcz.shape
    conv_dim = cw.shape[1]
    d_inner = gw.shape[1]
    n_heads = d_inner // SSM_HEAD_DIM
    q = SSD_CHUNK if length % SSD_CHUNK == 0 else length
    assert q % SUBLANES == 0 and q >= SUBLANES and conv_dim % d_inner == 0
    nc = length // q
    zblk = conv_dim // d_inner
    if shared_state:
        st_map = lambda b, c: (0, 0, 0)
    else:
        st_map = lambda b, c: (b, 0, 0)
    const = lambda b, c: (0, 0)
    return pl.pallas_call(
        functools.partial(_ssd_kernel, q=q, n_heads=n_heads),
        grid=(bsz, nc),
        in_specs=[
            pl.BlockSpec((1, q, conv_dim), lambda b, c: (b, c, 0)),
            pl.BlockSpec((1, q, d_inner), lambda b, c: (b, c, zblk)),
            pl.BlockSpec((1, q, LANES), lambda b, c: (b, c, 0)),
            pl.BlockSpec((1, SUBLANES, conv_dim), st_map),
            pl.BlockSpec((1, n_heads * SSM_HEAD_DIM, D_STATE), st_map),
            pl.BlockSpec((CONV_K, conv_dim), const),
            pl.BlockSpec((1, conv_dim), const),
            pl.BlockSpec((1, LANES), const),
            pl.BlockSpec((1, LANES), const),
            pl.BlockSpec((1, d_inner), const),
            pl.BlockSpec((1, d_inner), const),
        ],
        out_specs=[
            pl.BlockSpec((1, q, d_inner), lambda b, c: (b, c, 0)),
            pl.BlockSpec((1, SUBLANES, conv_dim), lambda b, c: (b, 0, 0)),
            pl.BlockSpec((1, n_heads * SSM_HEAD_DIM, D_STATE), lambda b, c: (b, 0, 0)),
        ],
        out_shape=[
            jax.ShapeDtypeStruct((bsz, length, d_inner), BF16),
            jax.ShapeDtypeStruct((bsz, SUBLANES, conv_dim), F32),
            jax.ShapeDtypeStruct((bsz, n_heads * SSM_HEAD_DIM, D_STATE), F32),
        ],
        scratch_shapes=[
            pltpu.VMEM((q + SUBLANES, conv_dim), F32),
            pltpu.VMEM((q, conv_dim), F32),
            pltpu.VMEM((n_heads * SSM_HEAD_DIM, D_STATE), F32),
            pltpu.VMEM((q, d_inner), F32),
        ],
        compiler_params=_params("parallel", "arbitrary"),
        name="ssd_mixer",
    )(xbcz, xbcz, dt, conv0, ssm0, cw, cb, dtb, alog, dsk, gw)


def _sink_softmax_pv(scores, values, sink):
    mx = sink
    for s in scores:
        mx = jnp.maximum(mx, jnp.max(s, axis=-1, keepdims=True))
    den = jnp.exp(sink - mx)
    acc = None
    for s, v in zip(scores, values):
        pr = jnp.exp(s - mx)
        den = den + jnp.sum(pr, axis=-1, keepdims=True)
        t = jnp.dot(pr.astype(BF16), v, preferred_element_type=F32)
        acc = t if acc is None else acc + t
    return acc / den


def _attn_prompt_kernel(sink_ref, q_ref, kvc_ref, kvp_ref, kvm_ref, o_ref, *, n_q_heads):
    n = pl.program_id(1)
    d = ATTN_HEAD_DIM
    kvw = N_KV_HEADS * d
    rep = n_q_heads // N_KV_HEADS
    scale = d ** -0.5
    w = q_ref.shape[1]
    qi = lax.broadcasted_iota(jnp.int32, (w, w), 0)
    ci = lax.broadcasted_iota(jnp.int32, (w, w), 1)
    cur_mask = ci <= qi
    prev_mask = ci > qi + jnp.where(n > 0, 0, w)
    kvm, kvp, kvc = kvm_ref[...], kvp_ref[0], kvc_ref[0]
    for k in range(N_KV_HEADS):
        ks, vs = slice(k * d, (k + 1) * d), slice(kvw + k * d, kvw + (k + 1) * d)
        km, vm = kvm[:, ks].astype(BF16), kvm[:, vs].astype(BF16)
        kp, vp = kvp[:, ks].astype(BF16), kvp[:, vs].astype(BF16)
        kc, vc = kvc[:, ks].astype(BF16), kvc[:, vs].astype(BF16)
        qk = q_ref[0, :, k * rep * d:(k + 1) * rep * d]
        outs = []
        for r in range(rep):
            qh = qk[:, r * d:(r + 1) * d].astype(BF16)
            s_m = lax.dot_general(qh, km, _NT, preferred_element_type=F32) * scale
            s_p = lax.dot_general(qh, kp, _NT, preferred_element_type=F32) * scale
            s_c = lax.dot_general(qh, kc, _NT, preferred_element_type=F32) * scale
            s_p = jnp.where(prev_mask, s_p, -jnp.inf)
            s_c = jnp.where(cur_mask, s_c, -jnp.inf)
            outs.append(_sink_softmax_pv([s_m, s_p, s_c], [vm, vp, vc], sink_ref[k * rep + r]))
        o_ref[0, :, k * rep * d:(k + 1) * rep * d] = jnp.concatenate(outs, axis=1).astype(BF16)


def attn_prompt(q, kv, kvm, sinks):
    bsz, s, dq = q.shape
    kvd = kv.shape[2]
    nb = s // WINDOW
    return pl.pallas_call(
        functools.partial(_attn_prompt_kernel, n_q_heads=dq // ATTN_HEAD_DIM),
        grid=(bsz, nb),
        in_specs=[
            pl.BlockSpec(memory_space=pltpu.SMEM),
            pl.BlockSpec((1, WINDOW, dq), lambda b, n: (b, n, 0)),
            pl.BlockSpec((1, WINDOW, kvd), lambda b, n: (b, n, 0)),
            pl.BlockSpec((1, WINDOW, kvd), lambda b, n: (b, jnp.maximum(n - 1, 0), 0)),
            pl.BlockSpec(kvm.shape, lambda b, n: (0, 0)),
        ],
        out_specs=pl.BlockSpec((1, WINDOW, dq), lambda b, n: (b, n, 0)),
        out_shape=jax.ShapeDtypeStruct((bsz, s, dq), BF16),
        compiler_params=_params("parallel", "arbitrary"),
        name="attn_prompt",
    )(sinks, q, kv, kv, kvm)


def _attn_sample_kernel(sink_ref, q_ref, kvn_ref, ck_ref, cv_ref, kvm_ref, o_ref, *, n_q_heads, n_meta):
    d = ATTN_HEAD_DIM
    kvw = N_KV_HEADS * d
    rep = n_q_heads // N_KV_HEADS
    scale = d ** -0.5
    nseq, t, _ = q_ref.shape
    w_buf = ck_ref.shape[1]
    rows = rep * t
    tq_buf = lax.broadcasted_iota(jnp.int32, (rows, w_buf), 0) % t
    pos_buf = PAST_LEN - w_buf + lax.broadcasted_iota(jnp.int32, (rows, w_buf), 1)
    buf_mask = jnp.logical_and(PAST_LEN + tq_buf - pos_buf < WINDOW, pos_buf >= n_meta)
    tq_new = lax.broadcasted_iota(jnp.int32, (rows, t), 0) % t
    c_new = lax.broadcasted_iota(jnp.int32, (rows, t), 1)
    new_mask = jnp.logical_and(c_new <= tq_new, tq_new - c_new < WINDOW)
    head_in_group = lax.broadcasted_iota(jnp.int32, (rows, 1), 0) // t

    def body(i, carry):
        q = q_ref[i]
        kvm = kvm_ref[...]
        kvn = kvn_ref[i]
        ck = ck_ref[i]
        cv = cv_ref[i]
        outs = []
        for k in range(N_KV_HEADS):
            ks = slice(k * d, (k + 1) * d)
            vs = slice(kvw + k * d, kvw + (k + 1) * d)
            q4 = jnp.concatenate(
                [q[:, (k * rep + r) * d:(k * rep + r + 1) * d] for r in range(rep)], axis=0).astype(BF16)
            km, vm = kvm[:, ks].astype(BF16), kvm[:, vs].astype(BF16)
            kb, vb = ck[:, ks].astype(BF16), cv[:, ks].astype(BF16)
            kn, vn = kvn[:, ks].astype(BF16), kvn[:, vs].astype(BF16)
            s_m = lax.dot_general(q4, km, _NT, preferred_element_type=F32) * scale
            s_b = lax.dot_general(q4, kb, _NT, preferred_element_type=F32) * scale
            s_n = lax.dot_general(q4, kn, _NT, preferred_element_type=F32) * scale
            s_b = jnp.where(buf_mask, s_b, -jnp.inf)
            s_n = jnp.where(new_mask, s_n, -jnp.inf)
            sink = jnp.zeros((rows, 1), F32)
            for r in range(rep):
                sink = jnp.where(head_in_group == r, sink_ref[k * rep + r], sink)
            o4 = _sink_softmax_pv([s_m, s_b, s_n], [vm, vb, vn], sink)
            outs.extend(o4[r * t:(r + 1) * t, :] for r in range(rep))
        o_ref[i] = jnp.concatenate(outs, axis=1).astype(BF16)
        return carry

    lax.fori_loop(0, nseq, body, 0)


def attn_sample(q, kvn, ck, cv, kvm, sinks, *, seqs_per_step=8):
    bsz, t, dq = q.shape
    kvd = kvn.shape[2]
    w_buf = ck.shape[1]
    g = _row_tile(bsz, seqs_per_step)
    return pl.pallas_call(
        functools.partial(_attn_sample_kernel, n_q_heads=dq // ATTN_HEAD_DIM, n_meta=kvm.shape[0]),
        grid=(bsz // g,),
        in_specs=[
            pl.BlockSpec(memory_space=pltpu.SMEM),
            pl.BlockSpec((g, t, dq), lambda b: (b, 0, 0)),
            pl.BlockSpec((g, t, kvd), lambda b: (b, 0, 0)),
            pl.BlockSpec((g, w_buf, kvd // 2), lambda b: (b, 0, 0)),
            pl.BlockSpec((g, w_buf, kvd // 2), lambda b: (b, 0, 0)),
            pl.BlockSpec(kvm.shape, lambda b: (0, 0)),
        ],
        out_specs=pl.BlockSpec((g, t, dq), lambda b: (b, 0, 0)),
        out_shape=jax.ShapeDtypeStruct((bsz, t, dq), BF16),
        compiler_params=_params("parallel"),
        name="attn_sample",
    )(sinks, q, kvn, ck, cv, kvm)


def kernel(x_prompt, x_sample, state_conv, state_ssm, cache_k_win, cache_v_win, meta_tokens, a_norm_w, a_in_proj, a_conv_w, a_conv_b, a_dt_bias, a_log, a_d_skip, a_gate_norm_w, a_out_proj, kv_norm_w, w_kv, b_norm_w, w_q, attn_sinks, w_o, mlp_norm_w, w_up, w_down, final_norm_w):
    n_prompt, seq, d_model = x_prompt.shape
    n_dec, dec_seq, _ = x_sample.shape
    n_a = a_in_proj.shape[0]
    depth = w_up.shape[0]
    n_meta = meta_tokens.shape[0]
    d_inner = a_out_proj.shape[1]
    conv_dim = a_conv_w.shape[2]
    n_heads = a_log.shape[1]
    w_buf = cache_k_win.shape[1]
    kvw = N_KV_HEADS * ATTN_HEAD_DIM
    assert n_heads * SSM_HEAD_DIM == d_inner and n_heads <= LANES

    hm = meta_tokens.astype(F32)
    hp = x_prompt.reshape(n_prompt * seq, d_model)
    hs = x_sample.reshape(n_dec * dec_seq, d_model)

    def pad_lanes(v):
        return jnp.pad(v, (0, LANES - v.shape[0])).reshape(1, LANES)

    def pad_conv_state(s):
        return jnp.pad(s, ((0, 0), (SUBLANES - (CONV_K - 1), 0), (0, 0)))

    conv_p_list, ssm_p_list, conv_s_list, ssm_s_list = [], [], [], []
    kvm = kv_p = kv_s = None
    for layer in range(depth):
        wu = w_up[layer].astype(BF16)
        wd = w_down[layer].astype(BF16)
        last = layer == depth - 1
        if layer < n_a:
            i = layer
            w_in = a_in_proj[i]
            w_main = jnp.concatenate([w_in[:, d_inner:d_inner + conv_dim], w_in[:, :d_inner]], axis=1).astype(BF16)
            w_dt = jnp.pad(w_in[:, d_inner + conv_dim:], ((0, 0), (0, LANES - n_heads))).astype(BF16)
            w_out = a_out_proj[i].astype(BF16)
            prm = (a_conv_w[i], a_conv_b[i].reshape(1, conv_dim), pad_lanes(a_dt_bias[i]), pad_lanes(a_log[i]),
                   jnp.repeat(a_d_skip[i], SSM_HEAD_DIM).reshape(1, d_inner), a_gate_norm_w[i].reshape(1, d_inner))

            def mixer(h, bsz, length, conv0, ssm0, shared):
                xbcz, dt = in_proj(h, a_norm_w[i], w_main, w_dt)
                g, conv_o, ssm_o = ssd_mixer(
                    xbcz.reshape(bsz, length, conv_dim + d_inner), dt.reshape(bsz, length, LANES),
                    conv0, ssm0, *prm, shared_state=shared)
                h = matmul_residual(g.reshape(bsz * length, d_inner), w_out, h)
                return h, conv_o, ssm_o

            zero_conv = jnp.zeros((1, SUBLANES, conv_dim), F32)
            zero_ssm = jnp.zeros((1, d_inner, D_STATE), F32)
            hm, conv_m, ssm_m = mixer(hm, 1, n_meta, zero_conv, zero_ssm, True)
            hp, conv_p, ssm_p = mixer(hp, n_prompt, seq, conv_m, ssm_m, True)
            hs, conv_s, ssm_s = mixer(hs, n_dec, dec_seq, pad_conv_state(state_conv[i]),
                                      state_ssm[i].reshape(n_dec, d_inner, D_STATE), False)
            conv_p_list.append(conv_p[:, SUBLANES - (CONV_K - 1):])
            ssm_p_list.append(ssm_p.reshape(n_prompt, n_heads, SSM_HEAD_DIM, D_STATE))
            conv_s_list.append(conv_s[:, SUBLANES - (CONV_K - 1):])
            ssm_s_list.append(ssm_s.reshape(n_dec, n_heads, SSM_HEAD_DIM, D_STATE))
            hm = mlp(hm, mlp_norm_w[layer], wu, wd, final_norm_w, final_norm=False)
        else:
            j = layer - n_a
            if j == 0:
                wkv = w_kv.astype(BF16)
                kvm = norm_matmul(hm, kv_norm_w, wkv)
                kv_p = norm_matmul(hp, kv_norm_w, wkv).reshape(n_prompt, seq, 2 * kvw)
                kv_s = norm_matmul(hs, kv_norm_w, wkv).reshape(n_dec, dec_seq, 2 * kvw)
            wq = w_q[j].astype(BF16)
            wo = w_o[j].astype(BF16)
            dq = wq.shape[1]
            q_p = norm_matmul(hp, b_norm_w[j], wq).reshape(n_prompt, seq, dq)
            q_s = norm_matmul(hs, b_norm_w[j], wq).reshape(n_dec, dec_seq, dq)
            o_p = attn_prompt(q_p, kv_p, kvm, attn_sinks[j])
            o_s = attn_sample(q_s, kv_s, cache_k_win.reshape(n_dec, w_buf, kvw),
                              cache_v_win.reshape(n_dec, w_buf, kvw), kvm, attn_sinks[j])
            hp = matmul_residual(o_p.reshape(n_prompt * seq, dq), wo, hp)
            hs = matmul_residual(o_s.reshape(n_dec * dec_seq, dq), wo, hs)
        hp = mlp(hp, mlp_norm_w[layer], wu, wd, final_norm_w, final_norm=last)
        hs = mlp(hs, mlp_norm_w[layer], wu, wd, final_norm_w, final_norm=last)

    y_prompt = hp.reshape(n_prompt, seq, d_model)
    y_sample = hs.reshape(n_dec, dec_seq, d_model)
    kv_heads = (N_KV_HEADS, ATTN_HEAD_DIM)
    k_p = kv_p[:, seq - w_buf:, :kvw].reshape((n_prompt, w_buf) + kv_heads)
    v_p = kv_p[:, seq - w_buf:, kvw:].reshape((n_prompt, w_buf) + kv_heads)
    k_s = kv_s[:, :, :kvw].reshape((n_dec, dec_seq) + kv_heads)
    v_s = kv_s[:, :, kvw:].reshape((n_dec, dec_seq) + kv_heads)
    k_s_win = jnp.concatenate([cache_k_win, k_s], axis=1)[:, -w_buf:]
    v_s_win = jnp.concatenate([cache_v_win, v_s], axis=1)[:, -w_buf:]
    return (y_prompt, y_sample, jnp.stack(conv_p_list), jnp.stack(ssm_p_list), k_p, v_p,
            jnp.stack(conv_s_list), jnp.stack(ssm_s_list), k_s_win, v_s_win)
```

```python
import functools

import jax
import jax.numpy as jnp
from jax import lax
from jax.experimental import pallas as pl
from jax.experimental.pallas import tpu as pltpu

F32 = jnp.float32
BF16 = jnp.bfloat16

N_GROUPS = 8
SSM_HEAD_DIM = 64
D_STATE = 128
CONV_K = 4
SSD_CHUNK = 128
ATTN_HEAD_DIM = 64
N_KV_HEADS = 4
WINDOW = 128
PAST_LEN = 8192
EPS = 1e-5

LANES = 128
SUBLANES = 8
VMEM_LIMIT_BYTES = 52 * 1024 * 1024

_NT = (((1,), (1,)), ((), ()))
_TN = (((0,), (0,)), ((), ()))


def _params(*sem):
    return pltpu.CompilerParams(dimension_semantics=sem, vmem_limit_bytes=VMEM_LIMIT_BYTES)


def _rms(x, w):
    ms = jnp.mean(x * x, axis=-1, keepdims=True)
    return x * lax.rsqrt(ms + EPS) * w


def _silu(x):
    s = 0.5 * x
    return s + s * jnp.tanh(s)


def _row_tile(m, cap):
    t = min(m, cap)
    assert m % t == 0, (m, t)
    return t


def _norm_matmul_kernel(x_ref, nw_ref, w_ref, o_ref, xn_ref):
    @pl.when(pl.program_id(1) == 0)
    def _():
        xn_ref[...] = _rms(x_ref[...], nw_ref[...]).astype(BF16)

    o_ref[...] = jnp.dot(xn_ref[...], w_ref[...], preferred_element_type=F32)


def norm_matmul(x, nw, w, *, tm_cap=1024, tn_cap=512):
    m, d = x.shape
    n = w.shape[1]
    tm, tn = _row_tile(m, tm_cap), _row_tile(n, tn_cap)
    return pl.pallas_call(
        _norm_matmul_kernel,
        grid=(m // tm, n // tn),
        in_specs=[
            pl.BlockSpec((tm, d), lambda i, j: (i, 0)),
            pl.BlockSpec((1, d), lambda i, j: (0, 0)),
            pl.BlockSpec((d, tn), lambda i, j: (0, j)),
        ],
        out_specs=pl.BlockSpec((tm, tn), lambda i, j: (i, j)),
        out_shape=jax.ShapeDtypeStruct((m, n), F32),
        scratch_shapes=[pltpu.VMEM((tm, d), BF16)],
        compiler_params=_params("parallel", "arbitrary"),
        name="norm_matmul",
    )(x, nw.reshape(1, d), w)


def _in_proj_kernel(x_ref, nw_ref, w_ref, wdt_ref, o_ref, dt_ref, xn_ref):
    @pl.when(pl.program_id(1) == 0)
    def _():
        xn = _rms(x_ref[...], nw_ref[...]).astype(BF16)
        xn_ref[...] = xn
        dt_ref[...] = jnp.dot(xn, wdt_ref[...], preferred_element_type=F32)

    o_ref[...] = jnp.dot(xn_ref[...], w_ref[...], preferred_element_type=F32)


def in_proj(x, nw, w, wdt, *, tm_cap=1024, tn_cap=512):
    m, d = x.shape
    n = w.shape[1]
    tm, tn = _row_tile(m, tm_cap), _row_tile(n, tn_cap)
    return pl.pallas_call(
        _in_proj_kernel,
        grid=(m // tm, n // tn),
        in_specs=[
            pl.BlockSpec((tm, d), lambda i, j: (i, 0)),
            pl.BlockSpec((1, d), lambda i, j: (0, 0)),
            pl.BlockSpec((d, tn), lambda i, j: (0, j)),
            pl.BlockSpec((d, LANES), lambda i, j: (0, 0)),
        ],
        out_specs=[
            pl.BlockSpec((tm, tn), lambda i, j: (i, j)),
            pl.BlockSpec((tm, LANES), lambda i, j: (i, 0)),
        ],
        out_shape=[jax.ShapeDtypeStruct((m, n), F32), jax.ShapeDtypeStruct((m, LANES), F32)],
        scratch_shapes=[pltpu.VMEM((tm, d), BF16)],
        compiler_params=_params("parallel", "arbitrary"),
        name="in_proj",
    )(x, nw.reshape(1, d), w, wdt)


def _matmul_residual_kernel(a_ref, w_ref, r_ref, o_ref):
    o_ref[...] = r_ref[...] + jnp.dot(a_ref[...], w_ref[...], preferred_element_type=F32)


def matmul_residual(a, w, res, *, tm_cap=512):
    m, k = a.shape
    n = w.shape[1]
    tm = _row_tile(m, tm_cap)
    return pl.pallas_call(
        _matmul_residual_kernel,
        grid=(m // tm,),
        in_specs=[
            pl.BlockSpec((tm, k), lambda i: (i, 0)),
            pl.BlockSpec((k, n), lambda i: (0, 0)),
            pl.BlockSpec((tm, n), lambda i: (i, 0)),
        ],
        out_specs=pl.BlockSpec((tm, n), lambda i: (i, 0)),
        out_shape=jax.ShapeDtypeStruct((m, n), F32),
        compiler_params=_params("parallel"),
        name="matmul_residual",
    )(a, w, res)


def _mlp_kernel(x_ref, nw_ref, wu_ref, wd_ref, fw_ref, o_ref, xn_ref, *, final_norm):
    f = pl.program_id(1)

    @pl.when(f == 0)
    def _():
        x = x_ref[...]
        xn_ref[...] = _rms(x, nw_ref[...]).astype(BF16)
        o_ref[...] = x

    h = jnp.dot(xn_ref[...], wu_ref[...], preferred_element_type=F32)
    h = jnp.square(jnp.maximum(h, 0.0)).astype(BF16)
    o_ref[...] += jnp.dot(h, wd_ref[...], preferred_element_type=F32)

    if final_norm:
        @pl.when(f == pl.num_programs(1) - 1)
        def _():
            o_ref[...] = _rms(o_ref[...], fw_ref[...])


def mlp(x, nw, wu, wd, fw, *, final_norm, tm_cap=1024, tf_cap=512):
    m, d = x.shape
    dff = wu.shape[1]
    tm, tf = _row_tile(m, tm_cap), _row_tile(dff, tf_cap)
    return pl.pallas_call(
        functools.partial(_mlp_kernel, final_norm=final_norm),
        grid=(m // tm, dff // tf),
        in_specs=[
            pl.BlockSpec((tm, d), lambda i, f: (i, 0)),
            pl.BlockSpec((1, d), lambda i, f: (0, 0)),
            pl.BlockSpec((d, tf), lambda i, f: (0, f)),
            pl.BlockSpec((tf, d), lambda i, f: (f, 0)),
            pl.BlockSpec((1, d), lambda i, f: (0, 0)),
        ],
        out_specs=pl.BlockSpec((tm, d), lambda i, f: (i, 0)),
        out_shape=jax.ShapeDtypeStruct((m, d), F32),
        scratch_shapes=[pltpu.VMEM((tm, d), BF16)],
        compiler_params=_params("parallel", "arbitrary"),
        name="mlp",
    )(x, nw.reshape(1, d), wu, wd, fw.reshape(1, d))


def _softplus(x):
    return jnp.maximum(x, 0.0) + jnp.log(1.0 + jnp.exp(-jnp.abs(x)))


def _split3(a):
    hi = a.astype(BF16)
    r1 = a - hi.astype(F32)
    mid = r1.astype(BF16)
    lo = (r1 - mid.astype(F32)).astype(BF16)
    return hi, mid, lo


def _ssd_kernel(xbc_ref, z_ref, dt_ref, conv0_ref, ssm0_ref,
                cw_ref, cb_ref, dtb_ref, alog_ref, dsk_ref, gw_ref,
                g_ref, convo_ref, ssmo_ref,
                xpad_ref, act_ref, state_ref, y_ref, *, q, n_heads):
    c = pl.program_id(1)
    p = SSM_HEAD_DIM
    d_inner = n_heads * p
    hpg = n_heads // N_GROUPS
    conv_dim = d_inner + 2 * N_GROUPS * D_STATE
    pad = SUBLANES

    @pl.when(c == 0)
    def _():
        xpad_ref[0:pad, :] = conv0_ref[0]
        state_ref[...] = ssm0_ref[0]

    xpad_ref[pad:pad + q, :] = xbc_ref[0]
    cblk = 512
    for j in range(conv_dim // cblk):
        cs = slice(j * cblk, (j + 1) * cblk)
        conv = cb_ref[:, cs]
        for k in range(CONV_K):
            off = pad - (CONV_K - 1) + k
            conv = conv + xpad_ref[off:off + q, cs] * cw_ref[k:k + 1, cs]
        act_ref[:, cs] = _silu(conv)
    xpad_ref[0:pad, :] = xpad_ref[q:q + pad, :]

    dt = _softplus(dt_ref[0] + dtb_ref[...])
    a = dt * (-jnp.exp(alog_ref[...]))
    row = lax.broadcasted_iota(jnp.int32, (q, q), 0)
    col = lax.broadcasted_iota(jnp.int32, (q, q), 1)
    causal = row >= col
    tri = jnp.where(causal, 1.0, 0.0).astype(BF16)
    acs = None
    for part in _split3(a):
        t = jnp.dot(tri, part, preferred_element_type=F32)
        acs = t if acs is None else acs + t
    if q < LANES:
        acs_sq = jnp.concatenate([acs, jnp.zeros((LANES - q, LANES), F32)], axis=0)
    else:
        acs_sq = acs
    acs_t = acs_sq.T
    eacs = jnp.exp(acs)
    last = acs[q - 1:q, :]
    dte = jnp.exp(last - acs)
    cdecay = jnp.exp(last)

    for g in range(N_GROUPS):
        b0 = d_inner + g * D_STATE
        c0 = d_inner + N_GROUPS * D_STATE + g * D_STATE
        bb = act_ref[:, b0:b0 + D_STATE].astype(BF16)
        cc = act_ref[:, c0:c0 + D_STATE].astype(BF16)
        cbm = lax.dot_general(cc, bb, _NT, preferred_element_type=F32)
        for pair in range(hpg // 2):
            xs2 = act_ref[:, (g * hpg + 2 * pair) * p:(g * hpg + 2 * pair + 2) * p]
            dsk2 = dsk_ref[:, (g * hpg + 2 * pair) * p:(g * hpg + 2 * pair + 2) * p]
            ys = []
            for e in range(2):
                h = g * hpg + 2 * pair + e
                xs = xs2[:, e * p:(e + 1) * p]
                seg = acs[:, h:h + 1] - acs_t[h:h + 1, 0:q]
                decay = jnp.exp(jnp.where(causal, seg, -jnp.inf))
                m = (cbm * decay).astype(BF16)
                xdt = xs * dt[:, h:h + 1]
                y = jnp.dot(m, xdt.astype(BF16), preferred_element_type=F32)
                st = state_ref[h * p:(h + 1) * p, :]
                yoff = lax.dot_general(cc, st.astype(BF16), _NT, preferred_element_type=F32)
                y = y + yoff * eacs[:, h:h + 1] + dsk2[:, e * p:(e + 1) * p] * xs
                ys.append(y)
                xdtw = (xdt * dte[:, h:h + 1]).astype(BF16)
                snew = lax.dot_general(xdtw, bb, _TN, preferred_element_type=F32)
                state_ref[h * p:(h + 1) * p, :] = (
                    jnp.broadcast_to(cdecay[:, h:h + 1], (p, D_STATE)) * st + snew)
            y_ref[:, (g * hpg + 2 * pair) * p:(g * hpg + 2 * pair + 2) * p] = (
                jnp.concatenate(ys, axis=1))

    gsz = d_inner // N_GROUPS
    for g in range(N_GROUPS):
        gs = slice(g * gsz, (g + 1) * gsz)
        gg = y_ref[:, gs] * _silu(z_ref[0, :, gs])
        ms = jnp.mean(gg * gg, axis=-1, keepdims=True)
        g_ref[0, :, gs] = (gg * lax.rsqrt(ms + EPS) * gw_ref[:, gs]).astype(BF16)

    @pl.when(c == pl.num_programs(1) - 1)
    def _():
        convo_ref[0] = xpad_ref[0:pad, :]
        ssmo_ref[0] = state_ref[...]


def ssd_mixer(xbcz, dt, conv0, ssm0, cw, cb, dtb, alog, dsk, gw, *, shared_state):
    bsz, length, _ = xbcz.shape
    conv_dim = cw.shape[1]
    d_inner = gw.shape[1]
    n_heads = d_inner // SSM_HEAD_DIM
    q = SSD_CHUNK if length % SSD_CHUNK == 0 else length
    assert q % SUBLANES == 0 and q >= SUBLANES and conv_dim % d_inner == 0
    nc = length // q
    zblk = conv_dim // d_inner
    if shared_state:
        st_map = lambda b, c: (0, 0, 0)
    else:
        st_map = lambda b, c: (b, 0, 0)
    const = lambda b, c: (0, 0)
    return pl.pallas_call(
        functools.partial(_ssd_kernel, q=q, n_heads=n_heads),
        grid=(bsz, nc),
        in_specs=[
            pl.BlockSpec((1, q, conv_dim), lambda b, c: (b, c, 0)),
            pl.BlockSpec((1, q, d_inner), lambda b, c: (b, c, zblk)),
            pl.BlockSpec((1, q, LANES), lambda b, c: (b, c, 0)),
            pl.BlockSpec((1, SUBLANES, conv_dim), st_map),
            pl.BlockSpec((1, n_heads * SSM_HEAD_DIM, D_STATE), st_map),
            pl.BlockSpec((CONV_K, conv_dim), const),
            pl.BlockSpec((1, conv_dim), const),
            pl.BlockSpec((1, LANES), const),
            pl.BlockSpec((1, LANES), const),
            pl.BlockSpec((1, d_inner), const),
            pl.BlockSpec((1, d_inner), const),
        ],
        out_specs=[
            pl.BlockSpec((1, q, d_inner), lambda b, c: (b, c, 0)),
            pl.BlockSpec((1, SUBLANES, conv_dim), lambda b, c: (b, 0, 0)),
            pl.BlockSpec((1, n_heads * SSM_HEAD_DIM, D_STATE), lambda b, c: (b, 0, 0)),
        ],
        out_shape=[
            jax.ShapeDtypeStruct((bsz, length, d_inner), BF16),
            jax.ShapeDtypeStruct((bsz, SUBLANES, conv_dim), F32),
            jax.ShapeDtypeStruct((bsz, n_heads * SSM_HEAD_DIM, D_STATE), F32),
        ],
        scratch_shapes=[
            pltpu.VMEM((q + SUBLANES, conv_dim), F32),
            pltpu.VMEM((q, conv_dim), F32),
            pltpu.VMEM((n_heads * SSM_HEAD_DIM, D_STATE), F32),
            pltpu.VMEM((q, d_inner), F32),
        ],
        compiler_params=_params("parallel", "arbitrary"),
        name="ssd_mixer",
    )(xbcz, xbcz, dt, conv0, ssm0, cw, cb, dtb, alog, dsk, gw)


def _expansion_matrices(n_heads):
    h = jnp.arange(LANES)[:, None]
    ex = h == jnp.arange(n_heads * SSM_HEAD_DIM)[None, :] // SSM_HEAD_DIM
    ec = h == jnp.arange(n_heads * SSD_CHUNK)[None, :] // SSD_CHUNK
    return jnp.tile(ex.astype(BF16), (3, 1)), jnp.tile(ec.astype(BF16), (3, 1))


def _ssd_chunk_kernel(xbc_ref, z_ref, dt_ref, conv0_ref, ssm0_ref,
                      cw_ref, cb_ref, dtb_ref, alog_ref, dsk_ref, gw_ref, ex3_ref, ec3_ref,
                      g_ref, convo_ref, ssmo_ref,
                      xpad_ref, xs_ref, bb_ref, cc_ref, st_ref, lhs3_ref, acs3_ref, acst_ref, *, n_heads):
    c = pl.program_id(1)
    q = SSD_CHUNK
    p = SSM_HEAD_DIM
    d_inner = n_heads * p
    hpg = n_heads // N_GROUPS
    gch = hpg * p
    bc = N_GROUPS * D_STATE
    ntile = (d_inner + 2 * bc) // LANES
    pad = SUBLANES

    @pl.when(c == 0)
    def _():
        for j in range(ntile):
            xpad_ref[j, 0:pad, :] = conv0_ref[0, :, j * LANES:(j + 1) * LANES]
        st_ref[...] = ssm0_ref[0].T

    for j in range(ntile):
        cs = slice(j * LANES, (j + 1) * LANES)
        xpad_ref[j, pad:pad + q, :] = xbc_ref[0, :, cs]
        conv = cb_ref[:, cs]
        for k in range(CONV_K):
            off = pad - (CONV_K - 1) + k
            conv = conv + xpad_ref[j, off:off + q, :] * cw_ref[k:k + 1, cs]
        act = _silu(conv)
        if j * LANES < d_inner:
            xs_ref[:, cs] = act
        elif j * LANES < d_inner + bc:
            bb_ref[:, j * LANES - d_inner:(j + 1) * LANES - d_inner] = act.astype(BF16)
        else:
            cc_ref[:, j * LANES - d_inner - bc:(j + 1) * LANES - d_inner - bc] = act.astype(BF16)
        xpad_ref[j, 0:pad, :] = xpad_ref[j, q:q + pad, :]

    dt = _softplus(dt_ref[0] + dtb_ref[...])
    a = dt * (-jnp.exp(alog_ref[...]))
    row = lax.broadcasted_iota(jnp.int32, (q, q), 0)
    col = lax.broadcasted_iota(jnp.int32, (q, q), 1)
    causal = row >= col
    tri = jnp.where(causal, 1.0, 0.0).astype(BF16)
    acs = None
    for part in _split3(a):
        t = jnp.dot(tri, part, preferred_element_type=F32)
        acs = t if acs is None else acs + t
    acst_ref[...] = acs.T
    last = acs[q - 1:q, :]
    stack = jnp.concatenate([dt, jnp.exp(acs), jnp.exp(last - acs)], axis=0)
    lhs3_ref[...] = jnp.concatenate(_split3(stack), axis=1)
    acs3_ref[...] = jnp.concatenate(_split3(acs), axis=1)

    lane_head = lax.broadcasted_iota(jnp.int32, (q, gch), 1) // p
    for g in range(N_GROUPS):
        gs = slice(g * gch, (g + 1) * gch)
        bb = bb_ref[:, g * D_STATE:(g + 1) * D_STATE]
        cc = cc_ref[:, g * D_STATE:(g + 1) * D_STATE]
        cbm = lax.dot_general(cc, bb, _NT, preferred_element_type=F32)
        ex = jnp.dot(lhs3_ref[...], ex3_ref[:, gs], preferred_element_type=F32)
        dtx, eax, dtex = ex[0:q], ex[q:2 * q], ex[2 * q:3 * q]
        colb = jnp.dot(acs3_ref[...], ec3_ref[:, g * hpg * q:(g + 1) * hpg * q],
                       preferred_element_type=F32)
        xs = xs_ref[:, gs]
        xdt = xs * dtx
        ms, rhs = [], []
        for r in range(hpg):
            h = g * hpg + r
            seg = colb[:, r * q:(r + 1) * q] - acst_ref[h:h + 1, :]
            decay = jnp.exp(jnp.where(causal, seg, -jnp.inf))
            ms.append((cbm * decay).astype(BF16))
            rhs.append(jnp.where(lane_head == r, xdt, 0.0).astype(BF16))
        y = jnp.dot(jnp.concatenate(ms, axis=1), jnp.concatenate(rhs, axis=0),
                    preferred_element_type=F32)
        st = st_ref[:, gs]
        y = y + jnp.dot(cc, st.astype(BF16), preferred_element_type=F32) * eax + dsk_ref[:, gs] * xs
        gg = y * _silu(z_ref[0, :, gs])
        msq = jnp.mean(gg * gg, axis=-1, keepdims=True)
        g_ref[0, :, gs] = (gg * lax.rsqrt(msq + EPS) * gw_ref[:, gs]).astype(BF16)
        xdtw = (xdt * dtex).astype(BF16)
        snew = lax.dot_general(bb, xdtw, _TN, preferred_element_type=F32)
        st_ref[:, gs] = st * eax[q - 1:q, :] + snew

    @pl.when(c == pl.num_programs(1) - 1)
    def _():
        for j in range(ntile):
            convo_ref[0, :, j * LANES:(j + 1) * LANES] = xpad_ref[j, 0:pad, :]
        ssmo_ref[0] = st_ref[...].T


def ssd_chunked(xbcz, dt, conv0, ssm0, cw, cb, dtb, alog, dsk, gw):
    bsz, length, _ = xbcz.shape
    conv_dim = cw.shape[1]
    d_inner = gw.shape[1]
    n_heads = d_inner // SSM_HEAD_DIM
    q = SSD_CHUNK
    assert length % q == 0 and conv_dim % d_inner == 0 and d_inner // N_GROUPS == (n_heads // N_GROUPS) * SSM_HEAD_DIM
    ex3, ec3 = _expansion_matrices(n_heads)
    shared = lambda b, c: (0, 0, 0)
    const = lambda b, c: (0, 0)
    return pl.pallas_call(
        functools.partial(_ssd_chunk_kernel, n_heads=n_heads),
        grid=(bsz, length // q),
        in_specs=[
            pl.BlockSpec((1, q, conv_dim), lambda b, c: (b, c, 0)),
            pl.BlockSpec((1, q, d_inner), lambda b, c: (b, c, conv_dim // d_inner)),
            pl.BlockSpec((1, q, LANES), lambda b, c: (b, c, 0)),
            pl.BlockSpec((1, SUBLANES, conv_dim), shared),
            pl.BlockSpec((1, d_inner, D_STATE), shared),
            pl.BlockSpec((CONV_K, conv_dim), const),
            pl.BlockSpec((1, conv_dim), const),
            pl.BlockSpec((1, LANES), const),
            pl.BlockSpec((1, LANES), const),
            pl.BlockSpec((1, d_inner), const),
            pl.BlockSpec((1, d_inner), const),
            pl.BlockSpec(ex3.shape, const),
            pl.BlockSpec(ec3.shape, const),
        ],
        out_specs=[
            pl.BlockSpec((1, q, d_inner), lambda b, c: (b, c, 0)),
            pl.BlockSpec((1, SUBLANES, conv_dim), lambda b, c: (b, 0, 0)),
            pl.BlockSpec((1, d_inner, D_STATE), lambda b, c: (b, 0, 0)),
        ],
        out_shape=[
            jax.ShapeDtypeStruct((bsz, length, d_inner), BF16),
            jax.ShapeDtypeStruct((bsz, SUBLANES, conv_dim), F32),
            jax.ShapeDtypeStruct((bsz, d_inner, D_STATE), F32),
        ],
        scratch_shapes=[
            pltpu.VMEM((conv_dim // LANES, q + SUBLANES, LANES), F32),
            pltpu.VMEM((q, d_inner), F32),
            pltpu.VMEM((q, N_GROUPS * D_STATE), BF16),
            pltpu.VMEM((q, N_GROUPS * D_STATE), BF16),
            pltpu.VMEM((D_STATE, d_inner), F32),
            pltpu.VMEM((3 * q, 3 * LANES), BF16),
            pltpu.VMEM((q, 3 * LANES), BF16),
            pltpu.VMEM((LANES, q), F32),
        ],
        compiler_params=_params("parallel", "arbitrary"),
        name="ssd_chunked",
    )(xbcz, xbcz, dt, conv0, ssm0, cw, cb, dtb, alog, dsk, gw, ex3, ec3)


def _sink_softmax_pv(scores, values, sink):
    mx = sink
    for s in scores:
        mx = jnp.maximum(mx, jnp.max(s, axis=-1, keepdims=True))
    den = jnp.exp(sink - mx)
    acc = None
    for s, v in zip(scores, values):
        pr = jnp.exp(s - mx)
        den = den + jnp.sum(pr, axis=-1, keepdims=True)
        t = jnp.dot(pr.astype(BF16), v, preferred_element_type=F32)
        acc = t if acc is None else acc + t
    return acc / den


def _attn_prompt_kernel(sink_ref, q_ref, kvc_ref, kvp_ref, kvm_ref, o_ref, *, n_q_heads):
    n = pl.program_id(1)
    d = ATTN_HEAD_DIM
    kvw = N_KV_HEADS * d
    rep = n_q_heads // N_KV_HEADS
    scale = d ** -0.5
    w = q_ref.shape[1]
    rows = rep * w
    qi = lax.broadcasted_iota(jnp.int32, (rows, w), 0) % w
    ci = lax.broadcasted_iota(jnp.int32, (rows, w), 1)
    from_prev = ci > qi
    no_prev = jnp.where(n > 0, 0.0, -jnp.inf)
    head_of_row = lax.broadcasted_iota(jnp.int32, (rows, 1), 0) // w
    kvm, kvp, kvc = kvm_ref[...], kvp_ref[0], kvc_ref[0]
    outs = []
    for k in range(N_KV_HEADS):
        ks, vs = slice(k * d, (k + 1) * d), slice(kvw + k * d, kvw + (k + 1) * d)
        qk = q_ref[0, :, k * rep * d:(k + 1) * rep * d]
        q4 = jnp.concatenate([qk[:, r * d:(r + 1) * d] for r in range(rep)], axis=0)
        q4 = (q4 * scale).astype(BF16)
        k2 = jnp.concatenate([kvp[:, ks], kvc[:, ks]], axis=0).astype(BF16)
        v2 = jnp.concatenate([kvp[:, vs], kvc[:, vs]], axis=0).astype(BF16)
        s2 = lax.dot_general(q4, k2, _NT, preferred_element_type=F32)
        s_w = jnp.where(from_prev, s2[:, :w] + no_prev, s2[:, w:])
        s_m = lax.dot_general(q4, kvm[:, ks].astype(BF16), _NT, preferred_element_type=F32)
        sink = jnp.zeros((rows, 1), F32)
        for r in range(rep):
            sink = jnp.where(head_of_row == r, sink_ref[k * rep + r], sink)
        mx = jnp.maximum(sink, jnp.maximum(jnp.max(s_w, axis=-1, keepdims=True),
                                           jnp.max(s_m, axis=-1, keepdims=True)))
        p_w = jnp.exp(s_w - mx)
        p_m = jnp.exp(s_m - mx)
        den = (jnp.exp(sink - mx) + jnp.sum(p_w, axis=-1, keepdims=True)
               + jnp.sum(p_m, axis=-1, keepdims=True))
        p2 = jnp.concatenate([jnp.where(from_prev, p_w, 0.0), jnp.where(from_prev, 0.0, p_w)],
                             axis=1).astype(BF16)
        o4 = (jnp.dot(p2, v2, preferred_element_type=F32)
              + jnp.dot(p_m.astype(BF16), kvm[:, vs].astype(BF16), preferred_element_type=F32)) / den
        outs.extend(o4[r * w:(r + 1) * w, :] for r in range(rep))
    o_ref[0] = jnp.concatenate(outs, axis=1).astype(BF16)


def attn_prompt(q, kv, kvm, sinks):
    bsz, s, dq = q.shape
    kvd = kv.shape[2]
    nb = s // WINDOW
    return pl.pallas_call(
        functools.partial(_attn_prompt_kernel, n_q_heads=dq // ATTN_HEAD_DIM),
        grid=(bsz, nb),
        in_specs=[
            pl.BlockSpec(memory_space=pltpu.SMEM),
            pl.BlockSpec((1, WINDOW, dq), lambda b, n: (b, n, 0)),
            pl.BlockSpec((1, WINDOW, kvd), lambda b, n: (b, n, 0)),
            pl.BlockSpec((1, WINDOW, kvd), lambda b, n: (b, jnp.maximum(n - 1, 0), 0)),
            pl.BlockSpec(kvm.shape, lambda b, n: (0, 0)),
        ],
        out_specs=pl.BlockSpec((1, WINDOW, dq), lambda b, n: (b, n, 0)),
        out_shape=jax.ShapeDtypeStruct((bsz, s, dq), BF16),
        compiler_params=_params("parallel", "arbitrary"),
        name="attn_prompt",
    )(sinks, q, kv, kv, kvm)


def _attn_sample_kernel(sink_ref, q_ref, kvn_ref, ck_ref, cv_ref, kvm_ref, o_ref, *, n_q_heads, n_meta):
    d = ATTN_HEAD_DIM
    kvw = N_KV_HEADS * d
    rep = n_q_heads // N_KV_HEADS
    scale = d ** -0.5
    nseq, t, _ = q_ref.shape
    w_buf = ck_ref.shape[1]
    rows = rep * t
    tq_buf = lax.broadcasted_iota(jnp.int32, (rows, w_buf), 0) % t
    pos_buf = PAST_LEN - w_buf + lax.broadcasted_iota(jnp.int32, (rows, w_buf), 1)
    buf_mask = jnp.logical_and(PAST_LEN + tq_buf - pos_buf < WINDOW, pos_buf >= n_meta)
    tq_new = lax.broadcasted_iota(jnp.int32, (rows, t), 0) % t
    c_new = lax.broadcasted_iota(jnp.int32, (rows, t), 1)
    new_mask = jnp.logical_and(c_new <= tq_new, tq_new - c_new < WINDOW)
    head_in_group = lax.broadcasted_iota(jnp.int32, (rows, 1), 0) // t

    def body(i, carry):
        q = q_ref[i]
        kvm = kvm_ref[...]
        kvn = kvn_ref[i]
        ck = ck_ref[i]
        cv = cv_ref[i]
        outs = []
        for k in range(N_KV_HEADS):
            ks = slice(k * d, (k + 1) * d)
            vs = slice(kvw + k * d, kvw + (k + 1) * d)
            q4 = jnp.concatenate(
                [q[:, (k * rep + r) * d:(k * rep + r + 1) * d] for r in range(rep)], axis=0).astype(BF16)
            km, vm = kvm[:, ks].astype(BF16), kvm[:, vs].astype(BF16)
            kb, vb = ck[:, ks].astype(BF16), cv[:, ks].astype(BF16)
            kn, vn = kvn[:, ks].astype(BF16), kvn[:, vs].astype(BF16)
            s_m = lax.dot_general(q4, km, _NT, preferred_element_type=F32) * scale
            s_b = lax.dot_general(q4, kb, _NT, preferred_element_type=F32) * scale
            s_n = lax.dot_general(q4, kn, _NT, preferred_element_type=F32) * scale
            s_b = jnp.where(buf_mask, s_b, -jnp.inf)
            s_n = jnp.where(new_mask, s_n, -jnp.inf)
            sink = jnp.zeros((rows, 1), F32)
            for r in range(rep):
                sink = jnp.where(head_in_group == r, sink_ref[k * rep + r], sink)
            o4 = _sink_softmax_pv([s_m, s_b, s_n], [vm, vb, vn], sink)
            outs.extend(o4[r * t:(r + 1) * t, :] for r in range(rep))
        o_ref[i] = jnp.concatenate(outs, axis=1).astype(BF16)
        return carry

    lax.fori_loop(0, nseq, body, 0)


def attn_sample(q, kvn, ck, cv, kvm, sinks, *, seqs_per_step=8):
    bsz, t, dq = q.shape
    kvd = kvn.shape[2]
    w_buf = ck.shape[1]
    g = _row_tile(bsz, seqs_per_step)
    return pl.pallas_call(
        functools.partial(_attn_sample_kernel, n_q_heads=dq // ATTN_HEAD_DIM, n_meta=kvm.shape[0]),
        grid=(bsz // g,),
        in_specs=[
            pl.BlockSpec(memory_space=pltpu.SMEM),
            pl.BlockSpec((g, t, dq), lambda b: (b, 0, 0)),
            pl.BlockSpec((g, t, kvd), lambda b: (b, 0, 0)),
            pl.BlockSpec((g, w_buf, kvd // 2), lambda b: (b, 0, 0)),
            pl.BlockSpec((g, w_buf, kvd // 2), lambda b: (b, 0, 0)),
            pl.BlockSpec(kvm.shape, lambda b: (0, 0)),
        ],
        out_specs=pl.BlockSpec((g, t, dq), lambda b: (b, 0, 0)),
        out_shape=jax.ShapeDtypeStruct((bsz, t, dq), BF16),
        compiler_params=_params("parallel"),
        name="attn_sample",
    )(sinks, q, kvn, ck, cv, kvm)


def kernel(x_prompt, x_sample, state_conv, state_ssm, cache_k_win, cache_v_win, meta_tokens, a_norm_w, a_in_proj, a_conv_w, a_conv_b, a_dt_bias, a_log, a_d_skip, a_gate_norm_w, a_out_proj, kv_norm_w, w_kv, b_norm_w, w_q, attn_sinks, w_o, mlp_norm_w, w_up, w_down, final_norm_w):
    n_prompt, seq, d_model = x_prompt.shape
    n_dec, dec_seq, _ = x_sample.shape
    n_a = a_in_proj.shape[0]
    depth = w_up.shape[0]
    n_meta = meta_tokens.shape[0]
    d_inner = a_out_proj.shape[1]
    conv_dim = a_conv_w.shape[2]
    n_heads = a_log.shape[1]
    w_buf = cache_k_win.shape[1]
    kvw = N_KV_HEADS * ATTN_HEAD_DIM
    assert n_heads * SSM_HEAD_DIM == d_inner and n_heads <= LANES

    hm = meta_tokens.astype(F32)
    hp = x_prompt.reshape(n_prompt * seq, d_model)
    hs = x_sample.reshape(n_dec * dec_seq, d_model)

    def pad_lanes(v):
        return jnp.pad(v, (0, LANES - v.shape[0])).reshape(1, LANES)

    def pad_conv_state(s):
        return jnp.pad(s, ((0, 0), (SUBLANES - (CONV_K - 1), 0), (0, 0)))

    conv_p_list, ssm_p_list, conv_s_list, ssm_s_list = [], [], [], []
    kvm = kv_p = kv_s = None
    for layer in range(depth):
        wu = w_up[layer].astype(BF16)
        wd = w_down[layer].astype(BF16)
        last = layer == depth - 1
        if layer < n_a:
            i = layer
            w_in = a_in_proj[i]
            w_main = jnp.concatenate([w_in[:, d_inner:d_inner + conv_dim], w_in[:, :d_inner]], axis=1).astype(BF16)
            w_dt = jnp.pad(w_in[:, d_inner + conv_dim:], ((0, 0), (0, LANES - n_heads))).astype(BF16)
            w_out = a_out_proj[i].astype(BF16)
            prm = (a_conv_w[i], a_conv_b[i].reshape(1, conv_dim), pad_lanes(a_dt_bias[i]), pad_lanes(a_log[i]),
                   jnp.repeat(a_d_skip[i], SSM_HEAD_DIM).reshape(1, d_inner), a_gate_norm_w[i].reshape(1, d_inner))

            def mixer(h, bsz, length, conv0, ssm0, shared):
                xbcz, dt = in_proj(h, a_norm_w[i], w_main, w_dt)
                xbcz = xbcz.reshape(bsz, length, conv_dim + d_inner)
                dt = dt.reshape(bsz, length, LANES)
                if shared and length % SSD_CHUNK == 0:
                    g, conv_o, ssm_o = ssd_chunked(xbcz, dt, conv0, ssm0, *prm)
                else:
                    g, conv_o, ssm_o = ssd_mixer(xbcz, dt, conv0, ssm0, *prm, shared_state=shared)
                h = matmul_residual(g.reshape(bsz * length, d_inner), w_out, h)
                return h, conv_o, ssm_o

            zero_conv = jnp.zeros((1, SUBLANES, conv_dim), F32)
            zero_ssm = jnp.zeros((1, d_inner, D_STATE), F32)
            hm, conv_m, ssm_m = mixer(hm, 1, n_meta, zero_conv, zero_ssm, True)
            hp, conv_p, ssm_p = mixer(hp, n_prompt, seq, conv_m, ssm_m, True)
            hs, conv_s, ssm_s = mixer(hs, n_dec, dec_seq, pad_conv_state(state_conv[i]),
                                      state_ssm[i].reshape(n_dec, d_inner, D_STATE), False)
            conv_p_list.append(conv_p[:, SUBLANES - (CONV_K - 1):])
            ssm_p_list.append(ssm_p.reshape(n_prompt, n_heads, SSM_HEAD_DIM, D_STATE))
            conv_s_list.append(conv_s[:, SUBLANES - (CONV_K - 1):])
            ssm_s_list.append(ssm_s.reshape(n_dec, n_heads, SSM_HEAD_DIM, D_STATE))
            hm = mlp(hm, mlp_norm_w[layer], wu, wd, final_norm_w, final_norm=False)
        else:
            j = layer - n_a
            if j == 0:
                wkv = w_kv.astype(BF16)
                kvm = norm_matmul(hm, kv_norm_w, wkv)
                kv_p = norm_matmul(hp, kv_norm_w, wkv).reshape(n_prompt, seq, 2 * kvw)
                kv_s = norm_matmul(hs, kv_norm_w, wkv).reshape(n_dec, dec_seq, 2 * kvw)
            wq = w_q[j].astype(BF16)
            wo = w_o[j].astype(BF16)
            dq = wq.shape[1]
            q_p = norm_matmul(hp, b_norm_w[j], wq).reshape(n_prompt, seq, dq)
            q_s = norm_matmul(hs, b_norm_w[j], wq).reshape(n_dec, dec_seq, dq)
            o_p = attn_prompt(q_p, kv_p, kvm, attn_sinks[j])
            o_s = attn_sample(q_s, kv_s, cache_k_win.reshape(n_dec, w_buf, kvw),
                              cache_v_win.reshape(n_dec, w_buf, kvw), kvm, attn_sinks[j])
            hp = matmul_residual(o_p.reshape(n_prompt * seq, dq), wo, hp)
            hs = matmul_residual(o_s.reshape(n_dec * dec_seq, dq), wo, hs)
        hp = mlp(hp, mlp_norm_w[layer], wu, wd, final_norm_w, final_norm=last)
        hs = mlp(hs, mlp_norm_w[layer], wu, wd, final_norm_w, final_norm=last)

    y_prompt = hp.reshape(n_prompt, seq, d_model)
    y_sample = hs.reshape(n_dec, dec_seq, d_model)
    kv_heads = (N_KV_HEADS, ATTN_HEAD_DIM)
    k_p = kv_p[:, seq - w_buf:, :kvw].reshape((n_prompt, w_buf) + kv_heads)
    v_p = kv_p[:, seq - w_buf:, kvw:].reshape((n_prompt, w_buf) + kv_heads)
    k_s = kv_s[:, :, :kvw].reshape((n_dec, dec_seq) + kv_heads)
    v_s = kv_s[:, :, kvw:].reshape((n_dec, dec_seq) + kv_heads)
    k_s_win = jnp.concatenate([cache_k_win, k_s], axis=1)[:, -w_buf:]
    v_s_win = jnp.concatenate([cache_v_win, v_s], axis=1)[:, -w_buf:]
    return (y_prompt, y_sample, jnp.stack(conv_p_list), jnp.stack(ssm_p_list), k_p, v_p,
            jnp.stack(conv_s_list), jnp.stack(ssm_s_list), k_s_win, v_s_win)
```

```python
import functools

import jax
import jax.numpy as jnp
from jax import lax
from jax.experimental import pallas as pl
from jax.experimental.pallas import tpu as pltpu

F32 = jnp.float32
BF16 = jnp.bfloat16

N_GROUPS = 8
SSM_HEAD_DIM = 64
D_STATE = 128
CONV_K = 4
SSD_CHUNK = 128
ATTN_HEAD_DIM = 64
N_KV_HEADS = 4
WINDOW = 128
PAST_LEN = 8192
EPS = 1e-5

LANES = 128
SUBLANES = 8
VMEM_LIMIT_BYTES = 52 * 1024 * 1024

_NT = (((1,), (1,)), ((), ()))
_TN = (((0,), (0,)), ((), ()))


def _params(*sem):
    return pltpu.CompilerParams(dimension_semantics=sem, vmem_limit_bytes=VMEM_LIMIT_BYTES)


def _rms(x, w):
    ms = jnp.mean(x * x, axis=-1, keepdims=True)
    return x * lax.rsqrt(ms + EPS) * w


def _silu(x):
    s = 0.5 * x
    return s + s * jnp.tanh(s)


def _row_tile(m, cap):
    t = min(m, cap)
    assert m % t == 0, (m, t)
    return t


def _norm_matmul_kernel(x_ref, nw_ref, w_ref, o_ref, xn_ref):
    @pl.when(pl.program_id(1) == 0)
    def _():
        xn_ref[...] = _rms(x_ref[...], nw_ref[...]).astype(BF16)

    o_ref[...] = jnp.dot(xn_ref[...], w_ref[...], preferred_element_type=F32)


def norm_matmul(x, nw, w, *, tm_cap=1024, tn_cap=1024):
    m, d = x.shape
    n = w.shape[1]
    tm, tn = _row_tile(m, tm_cap), _row_tile(n, tn_cap)
    return pl.pallas_call(
        _norm_matmul_kernel,
        grid=(m // tm, n // tn),
        in_specs=[
            pl.BlockSpec((tm, d), lambda i, j: (i, 0)),
            pl.BlockSpec((1, d), lambda i, j: (0, 0)),
            pl.BlockSpec((d, tn), lambda i, j: (0, j)),
        ],
        out_specs=pl.BlockSpec((tm, tn), lambda i, j: (i, j)),
        out_shape=jax.ShapeDtypeStruct((m, n), F32),
        scratch_shapes=[pltpu.VMEM((tm, d), BF16)],
        compiler_params=_params("parallel", "arbitrary"),
        name="norm_matmul",
    )(x, nw.reshape(1, d), w)


def _in_proj_kernel(x_ref, nw_ref, w_ref, wdt_ref, o_ref, dt_ref, xn_ref):
    @pl.when(pl.program_id(1) == 0)
    def _():
        xn = _rms(x_ref[...], nw_ref[...]).astype(BF16)
        xn_ref[...] = xn
        dt_ref[...] = jnp.dot(xn, wdt_ref[...], preferred_element_type=F32)

    o_ref[...] = jnp.dot(xn_ref[...], w_ref[...], preferred_element_type=F32)


def in_proj(x, nw, w, wdt, *, tm_cap=1024, tn_cap=1024):
    m, d = x.shape
    n = w.shape[1]
    tm, tn = _row_tile(m, tm_cap), _row_tile(n, tn_cap)
    return pl.pallas_call(
        _in_proj_kernel,
        grid=(m // tm, n // tn),
        in_specs=[
            pl.BlockSpec((tm, d), lambda i, j: (i, 0)),
            pl.BlockSpec((1, d), lambda i, j: (0, 0)),
            pl.BlockSpec((d, tn), lambda i, j: (0, j)),
            pl.BlockSpec((d, LANES), lambda i, j: (0, 0)),
        ],
        out_specs=[
            pl.BlockSpec((tm, tn), lambda i, j: (i, j)),
            pl.BlockSpec((tm, LANES), lambda i, j: (i, 0)),
        ],
        out_shape=[jax.ShapeDtypeStruct((m, n), F32), jax.ShapeDtypeStruct((m, LANES), F32)],
        scratch_shapes=[pltpu.VMEM((tm, d), BF16)],
        compiler_params=_params("parallel", "arbitrary"),
        name="in_proj",
    )(x, nw.reshape(1, d), w, wdt)


def _matmul_residual_kernel(a_ref, w_ref, r_ref, o_ref):
    o_ref[...] = r_ref[...] + jnp.dot(a_ref[...], w_ref[...], preferred_element_type=F32)


def matmul_residual(a, w, res, *, tm_cap=512):
    m, k = a.shape
    n = w.shape[1]
    tm = _row_tile(m, tm_cap)
    return pl.pallas_call(
        _matmul_residual_kernel,
        grid=(m // tm,),
        in_specs=[
            pl.BlockSpec((tm, k), lambda i: (i, 0)),
            pl.BlockSpec((k, n), lambda i: (0, 0)),
            pl.BlockSpec((tm, n), lambda i: (i, 0)),
        ],
        out_specs=pl.BlockSpec((tm, n), lambda i: (i, 0)),
        out_shape=jax.ShapeDtypeStruct((m, n), F32),
        compiler_params=_params("parallel"),
        name="matmul_residual",
    )(a, w, res)


def _mlp_kernel(x_ref, nw_ref, wu_ref, wd_ref, fw_ref, o_ref, xn_ref, *, final_norm):
    f = pl.program_id(1)

    @pl.when(f == 0)
    def _():
        x = x_ref[...]
        xn_ref[...] = _rms(x, nw_ref[...]).astype(BF16)
        o_ref[...] = x

    h = jnp.dot(xn_ref[...], wu_ref[...], preferred_element_type=F32)
    h = jnp.square(jnp.maximum(h, 0.0)).astype(BF16)
    o_ref[...] += jnp.dot(h, wd_ref[...], preferred_element_type=F32)

    if final_norm:
        @pl.when(f == pl.num_programs(1) - 1)
        def _():
            o_ref[...] = _rms(o_ref[...], fw_ref[...])


def mlp(x, nw, wu, wd, fw, *, final_norm, tm_cap=1024, tf_cap=1024):
    m, d = x.shape
    dff = wu.shape[1]
    tm, tf = _row_tile(m, tm_cap), _row_tile(dff, tf_cap)
    return pl.pallas_call(
        functools.partial(_mlp_kernel, final_norm=final_norm),
        grid=(m // tm, dff // tf),
        in_specs=[
            pl.BlockSpec((tm, d), lambda i, f: (i, 0)),
            pl.BlockSpec((1, d), lambda i, f: (0, 0)),
            pl.BlockSpec((d, tf), lambda i, f: (0, f)),
            pl.BlockSpec((tf, d), lambda i, f: (f, 0)),
            pl.BlockSpec((1, d), lambda i, f: (0, 0)),
        ],
        out_specs=pl.BlockSpec((tm, d), lambda i, f: (i, 0)),
        out_shape=jax.ShapeDtypeStruct((m, d), F32),
        scratch_shapes=[pltpu.VMEM((tm, d), BF16)],
        compiler_params=_params("parallel", "arbitrary"),
        name="mlp",
    )(x, nw.reshape(1, d), wu, wd, fw.reshape(1, d))


def _softplus(x):
    return jnp.maximum(x, 0.0) + jnp.log(1.0 + jnp.exp(-jnp.abs(x)))


def _split3(a):
    hi = a.astype(BF16)
    r1 = a - hi.astype(F32)
    mid = r1.astype(BF16)
    lo = (r1 - mid.astype(F32)).astype(BF16)
    return hi, mid, lo


def _ssd_kernel(xbc_ref, z_ref, dt_ref, conv0_ref, ssm0_ref,
                cw_ref, cb_ref, dtb_ref, alog_ref, dsk_ref, gw_ref,
                g_ref, convo_ref, ssmo_ref,
                xpad_ref, act_ref, state_ref, y_ref, *, q, n_heads):
    c = pl.program_id(1)
    p = SSM_HEAD_DIM
    d_inner = n_heads * p
    hpg = n_heads // N_GROUPS
    conv_dim = d_inner + 2 * N_GROUPS * D_STATE
    pad = SUBLANES

    @pl.when(c == 0)
    def _():
        xpad_ref[0:pad, :] = conv0_ref[0]
        state_ref[...] = ssm0_ref[0]

    xpad_ref[pad:pad + q, :] = xbc_ref[0]
    cblk = 512
    for j in range(conv_dim // cblk):
        cs = slice(j * cblk, (j + 1) * cblk)
        conv = cb_ref[:, cs]
        for k in range(CONV_K):
            off = pad - (CONV_K - 1) + k
            conv = conv + xpad_ref[off:off + q, cs] * cw_ref[k:k + 1, cs]
        act_ref[:, cs] = _silu(conv)
    xpad_ref[0:pad, :] = xpad_ref[q:q + pad, :]

    dt = _softplus(dt_ref[0] + dtb_ref[...])
    a = dt * (-jnp.exp(alog_ref[...]))
    row = lax.broadcasted_iota(jnp.int32, (q, q), 0)
    col = lax.broadcasted_iota(jnp.int32, (q, q), 1)
    causal = row >= col
    tri = jnp.where(causal, 1.0, 0.0).astype(BF16)
    acs = None
    for part in _split3(a):
        t = jnp.dot(tri, part, preferred_element_type=F32)
        acs = t if acs is None else acs + t
    if q < LANES:
        acs_sq = jnp.concatenate([acs, jnp.zeros((LANES - q, LANES), F32)], axis=0)
    else:
        acs_sq = acs
    acs_t = acs_sq.T
    eacs = jnp.exp(acs)
    last = acs[q - 1:q, :]
    dte = jnp.exp(last - acs)
    cdecay = jnp.exp(last)

    for g in range(N_GROUPS):
        b0 = d_inner + g * D_STATE
        c0 = d_inner + N_GROUPS * D_STATE + g * D_STATE
        bb = act_ref[:, b0:b0 + D_STATE].astype(BF16)
        cc = act_ref[:, c0:c0 + D_STATE].astype(BF16)
        cbm = lax.dot_general(cc, bb, _NT, preferred_element_type=F32)
        for pair in range(hpg // 2):
            xs2 = act_ref[:, (g * hpg + 2 * pair) * p:(g * hpg + 2 * pair + 2) * p]
            dsk2 = dsk_ref[:, (g * hpg + 2 * pair) * p:(g * hpg + 2 * pair + 2) * p]
            ys = []
            for e in range(2):
                h = g * hpg + 2 * pair + e
                xs = xs2[:, e * p:(e + 1) * p]
                seg = acs[:, h:h + 1] - acs_t[h:h + 1, 0:q]
                decay = jnp.exp(jnp.where(causal, seg, -jnp.inf))
                m = (cbm * decay).astype(BF16)
                xdt = xs * dt[:, h:h + 1]
                y = jnp.dot(m, xdt.astype(BF16), preferred_element_type=F32)
                st = state_ref[h * p:(h + 1) * p, :]
                yoff = lax.dot_general(cc, st.astype(BF16), _NT, preferred_element_type=F32)
                y = y + yoff * eacs[:, h:h + 1] + dsk2[:, e * p:(e + 1) * p] * xs
                ys.append(y)
                xdtw = (xdt * dte[:, h:h + 1]).astype(BF16)
                snew = lax.dot_general(xdtw, bb, _TN, preferred_element_type=F32)
                state_ref[h * p:(h + 1) * p, :] = (
                    jnp.broadcast_to(cdecay[:, h:h + 1], (p, D_STATE)) * st + snew)
            y_ref[:, (g * hpg + 2 * pair) * p:(g * hpg + 2 * pair + 2) * p] = (
                jnp.concatenate(ys, axis=1))

    gsz = d_inner // N_GROUPS
    for g in range(N_GROUPS):
        gs = slice(g * gsz, (g + 1) * gsz)
        gg = y_ref[:, gs] * _silu(z_ref[0, :, gs])
        ms = jnp.mean(gg * gg, axis=-1, keepdims=True)
        g_ref[0, :, gs] = (gg * lax.rsqrt(ms + EPS) * gw_ref[:, gs]).astype(BF16)

    @pl.when(c == pl.num_programs(1) - 1)
    def _():
        convo_ref[0] = xpad_ref[0:pad, :]
        ssmo_ref[0] = state_ref[...]


def ssd_mixer(xbcz, dt, conv0, ssm0, cw, cb, dtb, alog, dsk, gw, *, shared_state):
    bsz, length, _ = xbcz.shape
    conv_dim = cw.shape[1]
    d_inner = gw.shape[1]
    n_heads = d_inner // SSM_HEAD_DIM
    q = SSD_CHUNK if length % SSD_CHUNK == 0 else length
    assert q % SUBLANES == 0 and q >= SUBLANES and conv_dim % d_inner == 0
    nc = length // q
    zblk = conv_dim // d_inner
    if shared_state:
        st_map = lambda b, c: (0, 0, 0)
    else:
        st_map = lambda b, c: (b, 0, 0)
    const = lambda b, c: (0, 0)
    return pl.pallas_call(
        functools.partial(_ssd_kernel, q=q, n_heads=n_heads),
        grid=(bsz, nc),
        in_specs=[
            pl.BlockSpec((1, q, conv_dim), lambda b, c: (b, c, 0)),
            pl.BlockSpec((1, q, d_inner), lambda b, c: (b, c, zblk)),
            pl.BlockSpec((1, q, LANES), lambda b, c: (b, c, 0)),
            pl.BlockSpec((1, SUBLANES, conv_dim), st_map),
            pl.BlockSpec((1, n_heads * SSM_HEAD_DIM, D_STATE), st_map),
            pl.BlockSpec((CONV_K, conv_dim), const),
            pl.BlockSpec((1, conv_dim), const),
            pl.BlockSpec((1, LANES), const),
            pl.BlockSpec((1, LANES), const),
            pl.BlockSpec((1, d_inner), const),
            pl.BlockSpec((1, d_inner), const),
        ],
        out_specs=[
            pl.BlockSpec((1, q, d_inner), lambda b, c: (b, c, 0)),
            pl.BlockSpec((1, SUBLANES, conv_dim), lambda b, c: (b, 0, 0)),
            pl.BlockSpec((1, n_heads * SSM_HEAD_DIM, D_STATE), lambda b, c: (b, 0, 0)),
        ],
        out_shape=[
            jax.ShapeDtypeStruct((bsz, length, d_inner), BF16),
            jax.ShapeDtypeStruct((bsz, SUBLANES, conv_dim), F32),
            jax.ShapeDtypeStruct((bsz, n_heads * SSM_HEAD_DIM, D_STATE), F32),
        ],
        scratch_shapes=[
            pltpu.VMEM((q + SUBLANES, conv_dim), F32),
            pltpu.VMEM((q, conv_dim), F32),
            pltpu.VMEM((n_heads * SSM_HEAD_DIM, D_STATE), F32),
            pltpu.VMEM((q, d_inner), F32),
        ],
        compiler_params=_params("parallel", "arbitrary"),
        name="ssd_mixer",
    )(xbcz, xbcz, dt, conv0, ssm0, cw, cb, dtb, alog, dsk, gw)


def _expansion_matrices(n_heads):
    h = jnp.arange(LANES)[:, None]
    ex = h == jnp.arange(n_heads * SSM_HEAD_DIM)[None, :] // SSM_HEAD_DIM
    ec = h == jnp.arange(n_heads * SSD_CHUNK)[None, :] // SSD_CHUNK
    return jnp.tile(ex.astype(BF16), (3, 1)), jnp.tile(ec.astype(BF16), (3, 1))


def _store_act_tile(j, act, xs_ref, bb_ref, cc_ref, d_inner, bc):
    lo = j * LANES
    if lo < d_inner:
        xs_ref[:, lo:lo + LANES] = act
    elif lo < d_inner + bc:
        bb_ref[:, lo - d_inner:lo - d_inner + LANES] = act.astype(BF16)
    else:
        cc_ref[:, lo - d_inner - bc:lo - d_inner - bc + LANES] = act.astype(BF16)


def _ssd_token_math(dt_raw, dtb_ref, alog_ref, lhs3_ref, acs3_ref, acst_ref, *, seg_len):
    q = dt_raw.shape[0]
    dt = _softplus(dt_raw + dtb_ref[...])
    a = dt * (-jnp.exp(alog_ref[...]))
    row = lax.broadcasted_iota(jnp.int32, (q, q), 0)
    col = lax.broadcasted_iota(jnp.int32, (q, q), 1)
    mask = row >= col
    seg_end = q - 1
    if seg_len != q:
        mask = jnp.logical_and(mask, row // seg_len == col // seg_len)
        seg_end = (row // seg_len) * seg_len + (seg_len - 1)
    tri = jnp.where(mask, 1.0, 0.0).astype(BF16)
    sel = jnp.where(col == seg_end, 1.0, 0.0).astype(BF16)
    acs = None
    for part in _split3(a):
        t = jnp.dot(tri, part, preferred_element_type=F32)
        acs = t if acs is None else acs + t
    acs_parts = _split3(acs)
    last = None
    for part in acs_parts:
        t = jnp.dot(sel, part, preferred_element_type=F32)
        last = t if last is None else last + t
    acst_ref[...] = acs.T
    eacs = jnp.exp(acs)
    stack = jnp.concatenate([dt, eacs, jnp.exp(last - acs)], axis=0)
    lhs3_ref[...] = jnp.concatenate(_split3(stack), axis=1)
    acs3_ref[...] = jnp.concatenate(acs_parts, axis=1)
    return mask, eacs


def _ssd_group_diag(g, mask, xs_ref, bb_ref, cc_ref, lhs3_ref, acs3_ref, acst_ref, ex3_ref, ec3_ref, hpg):
    q = xs_ref.shape[0]
    p = SSM_HEAD_DIM
    gch = hpg * p
    gs = slice(g * gch, (g + 1) * gch)
    bb = bb_ref[:, g * D_STATE:(g + 1) * D_STATE]
    cc = cc_ref[:, g * D_STATE:(g + 1) * D_STATE]
    cbm = lax.dot_general(cc, bb, _NT, preferred_element_type=F32)
    ex = jnp.dot(lhs3_ref[...], ex3_ref[:, gs], preferred_element_type=F32)
    dtx, eax, dtex = ex[0:q], ex[q:2 * q], ex[2 * q:3 * q]
    colb = jnp.dot(acs3_ref[...], ec3_ref[:, g * hpg * q:(g + 1) * hpg * q],
                   preferred_element_type=F32)
    xs = xs_ref[:, gs]
    xdt = xs * dtx
    lane_head = lax.broadcasted_iota(jnp.int32, (q, gch), 1) // p
    ms, rhs = [], []
    for r in range(hpg):
        h = g * hpg + r
        seg = colb[:, r * q:(r + 1) * q] - acst_ref[h:h + 1, :]
        decay = jnp.exp(jnp.where(mask, seg, -jnp.inf))
        ms.append((cbm * decay).astype(BF16))
        rhs.append(jnp.where(lane_head == r, xdt, 0.0).astype(BF16))
    y = jnp.dot(jnp.concatenate(ms, axis=1), jnp.concatenate(rhs, axis=0),
                preferred_element_type=F32)
    return y, xs, xdt, eax, dtex, bb, cc


def _ssd_chunk_kernel(xbc_ref, z_ref, dt_ref, conv0_ref, ssm0_ref,
                      cw_ref, cb_ref, dtb_ref, alog_ref, dsk_ref, gw_ref, ex3_ref, ec3_ref,
                      g_ref, convo_ref, ssmo_ref,
                      xpad_ref, xs_ref, bb_ref, cc_ref, st_ref, lhs3_ref, acs3_ref, acst_ref, *, n_heads):
    c = pl.program_id(1)
    q = SSD_CHUNK
    p = SSM_HEAD_DIM
    d_inner = n_heads * p
    hpg = n_heads // N_GROUPS
    gch = hpg * p
    bc = N_GROUPS * D_STATE
    ntile = (d_inner + 2 * bc) // LANES
    pad = SUBLANES

    @pl.when(c == 0)
    def _():
        for j in range(ntile):
            xpad_ref[j, 0:pad, :] = conv0_ref[0, :, j * LANES:(j + 1) * LANES]
        st_ref[...] = ssm0_ref[0].T

    for j in range(ntile):
        cs = slice(j * LANES, (j + 1) * LANES)
        xpad_ref[j, pad:pad + q, :] = xbc_ref[0, :, cs]
        conv = cb_ref[:, cs]
        for k in range(CONV_K):
            off = pad - (CONV_K - 1) + k
            conv = conv + xpad_ref[j, off:off + q, :] * cw_ref[k:k + 1, cs]
        _store_act_tile(j, _silu(conv), xs_ref, bb_ref, cc_ref, d_inner, bc)
        xpad_ref[j, 0:pad, :] = xpad_ref[j, q:q + pad, :]

    mask, _ = _ssd_token_math(dt_ref[0], dtb_ref, alog_ref, lhs3_ref, acs3_ref, acst_ref, seg_len=q)

    for g in range(N_GROUPS):
        gs = slice(g * gch, (g + 1) * gch)
        y, xs, xdt, eax, dtex, bb, cc = _ssd_group_diag(
            g, mask, xs_ref, bb_ref, cc_ref, lhs3_ref, acs3_ref, acst_ref, ex3_ref, ec3_ref, hpg)
        st = st_ref[:, gs]
        y = y + jnp.dot(cc, st.astype(BF16), preferred_element_type=F32) * eax + dsk_ref[:, gs] * xs
        gg = y * _silu(z_ref[0, :, gs])
        msq = jnp.mean(gg * gg, axis=-1, keepdims=True)
        g_ref[0, :, gs] = (gg * lax.rsqrt(msq + EPS) * gw_ref[:, gs]).astype(BF16)
        xdtw = (xdt * dtex).astype(BF16)
        snew = lax.dot_general(bb, xdtw, _TN, preferred_element_type=F32)
        st_ref[:, gs] = st * eax[q - 1:q, :] + snew

    @pl.when(c == pl.num_programs(1) - 1)
    def _():
        for j in range(ntile):
            convo_ref[0, :, j * LANES:(j + 1) * LANES] = xpad_ref[j, 0:pad, :]
        ssmo_ref[0] = st_ref[...].T


def ssd_chunked(xbcz, dt, conv0, ssm0, cw, cb, dtb, alog, dsk, gw):
    bsz, length, _ = xbcz.shape
    conv_dim = cw.shape[1]
    d_inner = gw.shape[1]
    n_heads = d_inner // SSM_HEAD_DIM
    q = SSD_CHUNK
    assert length % q == 0 and conv_dim % d_inner == 0 and d_inner // N_GROUPS == (n_heads // N_GROUPS) * SSM_HEAD_DIM
    ex3, ec3 = _expansion_matrices(n_heads)
    shared = lambda b, c: (0, 0, 0)
    const = lambda b, c: (0, 0)
    return pl.pallas_call(
        functools.partial(_ssd_chunk_kernel, n_heads=n_heads),
        grid=(bsz, length // q),
        in_specs=[
            pl.BlockSpec((1, q, conv_dim), lambda b, c: (b, c, 0)),
            pl.BlockSpec((1, q, d_inner), lambda b, c: (b, c, conv_dim // d_inner)),
            pl.BlockSpec((1, q, LANES), lambda b, c: (b, c, 0)),
            pl.BlockSpec((1, SUBLANES, conv_dim), shared),
            pl.BlockSpec((1, d_inner, D_STATE), shared),
            pl.BlockSpec((CONV_K, conv_dim), const),
            pl.BlockSpec((1, conv_dim), const),
            pl.BlockSpec((1, LANES), const),
            pl.BlockSpec((1, LANES), const),
            pl.BlockSpec((1, d_inner), const),
            pl.BlockSpec((1, d_inner), const),
            pl.BlockSpec(ex3.shape, const),
            pl.BlockSpec(ec3.shape, const),
        ],
        out_specs=[
            pl.BlockSpec((1, q, d_inner), lambda b, c: (b, c, 0)),
            pl.BlockSpec((1, SUBLANES, conv_dim), lambda b, c: (b, 0, 0)),
            pl.BlockSpec((1, d_inner, D_STATE), lambda b, c: (b, 0, 0)),
        ],
        out_shape=[
            jax.ShapeDtypeStruct((bsz, length, d_inner), BF16),
            jax.ShapeDtypeStruct((bsz, SUBLANES, conv_dim), F32),
            jax.ShapeDtypeStruct((bsz, d_inner, D_STATE), F32),
        ],
        scratch_shapes=[
            pltpu.VMEM((conv_dim // LANES, q + SUBLANES, LANES), F32),
            pltpu.VMEM((q, d_inner), F32),
            pltpu.VMEM((q, N_GROUPS * D_STATE), BF16),
            pltpu.VMEM((q, N_GROUPS * D_STATE), BF16),
            pltpu.VMEM((D_STATE, d_inner), F32),
            pltpu.VMEM((3 * q, 3 * LANES), BF16),
            pltpu.VMEM((q, 3 * LANES), BF16),
            pltpu.VMEM((LANES, q), F32),
        ],
        compiler_params=_params("parallel", "arbitrary"),
        name="ssd_chunked",
    )(xbcz, xbcz, dt, conv0, ssm0, cw, cb, dtb, alog, dsk, gw, ex3, ec3)


def _ssd_step_tokens_kernel(xbc_ref, dt_ref, conv0_ref, cw_ref, cb_ref, dtb_ref, alog_ref, dsk_ref,
                            ex3_ref, ec3_ref,
                            ypart_ref, eaxo_ref, xdtw_ref, bbo_ref, cco_ref, eacs_ref,
                            xpad_ref, xs_ref, bb_ref, cc_ref, lhs3_ref, acs3_ref, acst_ref, *, n_heads, t):
    q = SSD_CHUNK
    p = SSM_HEAD_DIM
    d_inner = n_heads * p
    hpg = n_heads // N_GROUPS
    gch = hpg * p
    bc = N_GROUPS * D_STATE
    ntile = (d_inner + 2 * bc) // LANES
    nseq = q // t
    slot = 2 * t
    span = nseq * slot - t

    for j in range(ntile):
        cs = slice(j * LANES, (j + 1) * LANES)
        for i in range(nseq):
            xpad_ref[j, i * slot:i * slot + t, :] = conv0_ref[i, :, cs]
            xpad_ref[j, i * slot + t:(i + 1) * slot, :] = xbc_ref[i * t:(i + 1) * t, cs]
        conv = cb_ref[:, cs]
        for k in range(CONV_K):
            off = t - (CONV_K - 1) + k
            conv = conv + xpad_ref[j, off:off + span, :] * cw_ref[k:k + 1, cs]
        act = _silu(jnp.concatenate([conv[i * slot:i * slot + t] for i in range(nseq)], axis=0))
        _store_act_tile(j, act, xs_ref, bb_ref, cc_ref, d_inner, bc)
        lo = j * LANES
        if d_inner <= lo < d_inner + bc:
            bbo_ref[:, lo - d_inner:lo - d_inner + LANES] = act
        elif lo >= d_inner + bc:
            cco_ref[:, lo - d_inner - bc:lo - d_inner - bc + LANES] = act

    mask, eacs = _ssd_token_math(dt_ref[...], dtb_ref, alog_ref, lhs3_ref, acs3_ref, acst_ref, seg_len=t)
    eacs_ref[...] = eacs
    for g in range(N_GROUPS):
        gs = slice(g * gch, (g + 1) * gch)
        y, xs, xdt, eax, dtex, _, _ = _ssd_group_diag(
            g, mask, xs_ref, bb_ref, cc_ref, lhs3_ref, acs3_ref, acst_ref, ex3_ref, ec3_ref, hpg)
        ypart_ref[:, gs] = y + dsk_ref[:, gs] * xs
        eaxo_ref[:, gs] = eax
        xdtw_ref[:, gs] = xdt * dtex


def ssd_step_tokens(xbcz, dt, conv0, cw, cb, dtb, alog, dsk, *, t):
    n_tok = xbcz.shape[0]
    conv_dim = cw.shape[1]
    d_inner = dsk.shape[1]
    n_heads = d_inner // SSM_HEAD_DIM
    q = SSD_CHUNK
    bc = N_GROUPS * D_STATE
    assert t == SUBLANES and n_tok % q == 0
    ex3, ec3 = _expansion_matrices(n_heads)
    const = lambda s: (0, 0)
    rows = lambda s: (s, 0)
    f32 = lambda n: jax.ShapeDtypeStruct((n_tok, n), F32)
    return pl.pallas_call(
        functools.partial(_ssd_step_tokens_kernel, n_heads=n_heads, t=t),
        grid=(n_tok // q,),
        in_specs=[
            pl.BlockSpec((q, conv_dim), rows),
            pl.BlockSpec((q, LANES), rows),
            pl.BlockSpec((q // t, SUBLANES, conv_dim), lambda s: (s, 0, 0)),
            pl.BlockSpec((CONV_K, conv_dim), const),
            pl.BlockSpec((1, conv_dim), const),
            pl.BlockSpec((1, LANES), const),
            pl.BlockSpec((1, LANES), const),
            pl.BlockSpec((1, d_inner), const),
            pl.BlockSpec(ex3.shape, const),
            pl.BlockSpec(ec3.shape, const),
        ],
        out_specs=[
            pl.BlockSpec((q, d_inner), rows),
            pl.BlockSpec((q, d_inner), rows),
            pl.BlockSpec((q, d_inner), rows),
            pl.BlockSpec((q, bc), rows),
            pl.BlockSpec((q, bc), rows),
            pl.BlockSpec((q, LANES), rows),
        ],
        out_shape=[f32(d_inner), f32(d_inner), f32(d_inner), f32(bc), f32(bc), f32(LANES)],
        scratch_shapes=[
            pltpu.VMEM((conv_dim // LANES, 2 * q, LANES), F32),
            pltpu.VMEM((q, d_inner), F32),
            pltpu.VMEM((q, bc), BF16),
            pltpu.VMEM((q, bc), BF16),
            pltpu.VMEM((3 * q, 3 * LANES), BF16),
            pltpu.VMEM((q, 3 * LANES), BF16),
            pltpu.VMEM((LANES, q), F32),
        ],
        compiler_params=_params("parallel"),
        name="ssd_step_tokens",
    )(xbcz, dt, conv0, cw, cb, dtb, alog, dsk, ex3, ec3)


def _ssd_step_state_kernel(cd_ref, st_ref, cc_ref, bb_ref, xdtw_ref, eax_ref, ypart_ref, z_ref, gw_ref,
                           *rest, n_heads, t, nseq):
    g_ref, sto_ref = rest[-2:]
    s = pl.program_id(0)
    p = SSM_HEAD_DIM
    hpg = n_heads // N_GROUPS
    gch = hpg * p
    for g in range(N_GROUPS):
        gs = slice(g * gch, (g + 1) * gch)
        ns = slice(g * D_STATE, (g + 1) * D_STATE)
        ys = []
        for i in range(nseq):
            rows = slice(i * t, (i + 1) * t)
            h0 = st_ref[0, i, gs, :]
            yoff = lax.dot_general(cc_ref[rows, ns].astype(BF16), h0.astype(BF16), _NT,
                                   preferred_element_type=F32)
            ys.append(ypart_ref[rows, gs] + yoff * eax_ref[rows, gs])
            upd = lax.dot_general(xdtw_ref[rows, gs].astype(BF16), bb_ref[rows, ns].astype(BF16), _TN,
                                  preferred_element_type=F32)
            for r in range(hpg):
                hs = slice(r * p, (r + 1) * p)
                sto_ref[0, i, g * gch + r * p:g * gch + (r + 1) * p, :] = (
                    h0[hs] * cd_ref[s * nseq + i, g * hpg + r] + upd[hs])
        gg = jnp.concatenate(ys, axis=0) * _silu(z_ref[:, gs])
        msq = jnp.mean(gg * gg, axis=-1, keepdims=True)
        g_ref[:, gs] = (gg * lax.rsqrt(msq + EPS) * gw_ref[:, gs]).astype(BF16)


def ssd_step_state(cd, state_all, layer, cc, bb, xdtw, eax, ypart, xbcz, gw, stacked_out, *, t, seqs_per_step=4):
    n_layers, n_seq, d_inner, d_state = state_all.shape
    n_tok = ypart.shape[0]
    bc = cc.shape[1]
    conv_dim = xbcz.shape[1] - d_inner
    nseq = _row_tile(n_seq, seqs_per_step)
    rt = nseq * t
    rows = lambda s, cd_: (s, 0)
    st_map = lambda s, cd_: (layer, s, 0, 0)
    in_specs = [
        pl.BlockSpec((1, nseq, d_inner, d_state), st_map),
        pl.BlockSpec((rt, bc), rows),
        pl.BlockSpec((rt, bc), rows),
        pl.BlockSpec((rt, d_inner), rows),
        pl.BlockSpec((rt, d_inner), rows),
        pl.BlockSpec((rt, d_inner), rows),
        pl.BlockSpec((rt, d_inner), lambda s, cd_: (s, conv_dim // d_inner)),
        pl.BlockSpec((1, d_inner), lambda s, cd_: (0, 0)),
    ]
    args = [cd, state_all, cc, bb, xdtw, eax, ypart, xbcz, gw]
    aliases = {}
    if stacked_out is not None:
        in_specs.append(pl.BlockSpec(memory_space=pl.ANY))
        aliases = {len(args): 1}
        args.append(stacked_out)
    return pl.pallas_call(
        functools.partial(_ssd_step_state_kernel, n_heads=d_inner // SSM_HEAD_DIM, t=t, nseq=nseq),
        grid_spec=pltpu.PrefetchScalarGridSpec(
            num_scalar_prefetch=1,
            grid=(n_seq // nseq,),
            in_specs=in_specs,
            out_specs=[
                pl.BlockSpec((rt, d_inner), rows),
                pl.BlockSpec((1, nseq, d_inner, d_state), st_map),
            ],
        ),
        out_shape=[
            jax.ShapeDtypeStruct((n_tok, d_inner), BF16),
            jax.ShapeDtypeStruct((n_layers, n_seq, d_inner, d_state), F32),
        ],
        input_output_aliases=aliases,
        compiler_params=_params("parallel"),
        name="ssd_step_state",
    )(*args)


def _attn_prompt_kernel(sink_ref, q_ref, kvc_ref, kvp_ref, kvm_ref, o_ref, *, n_q_heads):
    n = pl.program_id(1)
    d = ATTN_HEAD_DIM
    kvw = N_KV_HEADS * d
    rep = n_q_heads // N_KV_HEADS
    scale = d ** -0.5
    w = q_ref.shape[1]
    rows = rep * w
    qi = lax.broadcasted_iota(jnp.int32, (rows, w), 0) % w
    ci = lax.broadcasted_iota(jnp.int32, (rows, w), 1)
    from_prev = ci > qi
    no_prev = jnp.where(n > 0, 0.0, -jnp.inf)
    head_of_row = lax.broadcasted_iota(jnp.int32, (rows, 1), 0) // w
    kvm, kvp, kvc = kvm_ref[...], kvp_ref[0], kvc_ref[0]
    outs = []
    for k in range(N_KV_HEADS):
        ks, vs = slice(k * d, (k + 1) * d), slice(kvw + k * d, kvw + (k + 1) * d)
        qk = q_ref[0, :, k * rep * d:(k + 1) * rep * d]
        q4 = jnp.concatenate([qk[:, r * d:(r + 1) * d] for r in range(rep)], axis=0)
        q4 = (q4 * scale).astype(BF16)
        k2 = jnp.concatenate([kvp[:, ks], kvc[:, ks]], axis=0).astype(BF16)
        v2 = jnp.concatenate([kvp[:, vs], kvc[:, vs]], axis=0).astype(BF16)
        s2 = lax.dot_general(q4, k2, _NT, preferred_element_type=F32)
        s_w = jnp.where(from_prev, s2[:, :w] + no_prev, s2[:, w:])
        s_m = lax.dot_general(q4, kvm[:, ks].astype(BF16), _NT, preferred_element_type=F32)
        sink = jnp.zeros((rows, 1), F32)
        for r in range(rep):
            sink = jnp.where(head_of_row == r, sink_ref[k * rep + r], sink)
        mx = jnp.maximum(sink, jnp.maximum(jnp.max(s_w, axis=-1, keepdims=True),
                                           jnp.max(s_m, axis=-1, keepdims=True)))
        p_w = jnp.exp(s_w - mx)
        p_m = jnp.exp(s_m - mx)
        den = (jnp.exp(sink - mx) + jnp.sum(p_w, axis=-1, keepdims=True)
               + jnp.sum(p_m, axis=-1, keepdims=True))
        p2 = jnp.concatenate([jnp.where(from_prev, p_w, 0.0), jnp.where(from_prev, 0.0, p_w)],
                             axis=1).astype(BF16)
        o4 = (jnp.dot(p2, v2, preferred_element_type=F32)
              + jnp.dot(p_m.astype(BF16), kvm[:, vs].astype(BF16), preferred_element_type=F32)) / den
        outs.extend(o4[r * w:(r + 1) * w, :] for r in range(rep))
    o_ref[0] = jnp.concatenate(outs, axis=1).astype(BF16)


def attn_prompt(q, kv, kvm, sinks):
    bsz, s, dq = q.shape
    kvd = kv.shape[2]
    nb = s // WINDOW
    return pl.pallas_call(
        functools.partial(_attn_prompt_kernel, n_q_heads=dq // ATTN_HEAD_DIM),
        grid=(bsz, nb),
        in_specs=[
            pl.BlockSpec(memory_space=pltpu.SMEM),
            pl.BlockSpec((1, WINDOW, dq), lambda b, n: (b, n, 0)),
            pl.BlockSpec((1, WINDOW, kvd), lambda b, n: (b, n, 0)),
            pl.BlockSpec((1, WINDOW, kvd), lambda b, n: (b, jnp.maximum(n - 1, 0), 0)),
            pl.BlockSpec(kvm.shape, lambda b, n: (0, 0)),
        ],
        out_specs=pl.BlockSpec((1, WINDOW, dq), lambda b, n: (b, n, 0)),
        out_shape=jax.ShapeDtypeStruct((bsz, s, dq), BF16),
        compiler_params=_params("parallel", "arbitrary"),
        name="attn_prompt",
    )(sinks, q, kv, kv, kvm)


def _attn_sample_kernel(sink_ref, q_ref, kvn_ref, ck_ref, cv_ref, kvm_ref, o_ref, *, n_q_heads, n_meta):
    d = ATTN_HEAD_DIM
    kvw = N_KV_HEADS * d
    rep = n_q_heads // N_KV_HEADS
    scale = d ** -0.5
    nseq, t, _ = q_ref.shape
    w_buf = ck_ref.shape[1]
    rows = rep * t
    nx = n_meta + t
    tq_buf = lax.broadcasted_iota(jnp.int32, (rows, w_buf), 0) % t
    pos_buf = PAST_LEN - w_buf + lax.broadcasted_iota(jnp.int32, (rows, w_buf), 1)
    buf_mask = jnp.logical_and(PAST_LEN + tq_buf - pos_buf < WINDOW, pos_buf >= n_meta)
    tq_x = lax.broadcasted_iota(jnp.int32, (rows, nx), 0) % t
    c_x = lax.broadcasted_iota(jnp.int32, (rows, nx), 1) - n_meta
    x_mask = jnp.logical_or(c_x < 0, jnp.logical_and(c_x <= tq_x, tq_x - c_x < WINDOW))
    head_in_group = lax.broadcasted_iota(jnp.int32, (rows, 1), 0) // t
    kvm = kvm_ref[...]
    sinks = []
    for k in range(N_KV_HEADS):
        sink = jnp.zeros((rows, 1), F32)
        for r in range(rep):
            sink = jnp.where(head_in_group == r, sink_ref[k * rep + r], sink)
        sinks.append(sink)
    chains = [(i, k) for i in range(nseq) for k in range(N_KV_HEADS)]

    scores, values = {}, {}
    for i in range(nseq):
        q, kvn, ck, cv = q_ref[i], kvn_ref[i], ck_ref[i], cv_ref[i]
        for k in range(N_KV_HEADS):
            ks, vs = slice(k * d, (k + 1) * d), slice(kvw + k * d, kvw + (k + 1) * d)
            q4 = jnp.concatenate([q[:, (k * rep + r) * d:(k * rep + r + 1) * d] for r in range(rep)], axis=0)
            q4 = (q4 * scale).astype(BF16)
            kx = jnp.concatenate([kvm[:, ks], kvn[:, ks]], axis=0).astype(BF16)
            vx = jnp.concatenate([kvm[:, vs], kvn[:, vs]], axis=0).astype(BF16)
            s_b = lax.dot_general(q4, ck[:, ks].astype(BF16), _NT, preferred_element_type=F32)
            s_x = lax.dot_general(q4, kx, _NT, preferred_element_type=F32)
            scores[i, k] = (jnp.where(buf_mask, s_b, -jnp.inf), jnp.where(x_mask, s_x, -jnp.inf))
            values[i, k] = (cv[:, ks].astype(BF16), vx)

    maxes = {}
    for c in chains:
        s_b, s_x = scores[c]
        maxes[c] = jnp.maximum(sinks[c[1]], jnp.maximum(jnp.max(s_b, axis=-1, keepdims=True),
                                                         jnp.max(s_x, axis=-1, keepdims=True)))
    probs, dens = {}, {}
    for c in chains:
        s_b, s_x = scores[c]
        p_b, p_x = jnp.exp(s_b - maxes[c]), jnp.exp(s_x - maxes[c])
        probs[c] = (p_b.astype(BF16), p_x.astype(BF16))
        dens[c] = (jnp.exp(sinks[c[1]] - maxes[c]) + jnp.sum(p_b, axis=-1, keepdims=True)
                   + jnp.sum(p_x, axis=-1, keepdims=True))
    outs = {}
    for c in chains:
        outs[c] = (jnp.dot(probs[c][0], values[c][0], preferred_element_type=F32)
                   + jnp.dot(probs[c][1], values[c][1], preferred_element_type=F32)) / dens[c]
    for i in range(nseq):
        heads = [outs[i, k][r * t:(r + 1) * t, :] for k in range(N_KV_HEADS) for r in range(rep)]
        o_ref[i] = jnp.concatenate(heads, axis=1).astype(BF16)


def attn_sample(q, kvn, ck, cv, kvm, sinks, *, seqs_per_step=8):
    bsz, t, dq = q.shape
    kvd = kvn.shape[2]
    w_buf = ck.shape[1]
    g = _row_tile(bsz, seqs_per_step)
    return pl.pallas_call(
        functools.partial(_attn_sample_kernel, n_q_heads=dq // ATTN_HEAD_DIM, n_meta=kvm.shape[0]),
        grid=(bsz // g,),
        in_specs=[
            pl.BlockSpec(memory_space=pltpu.SMEM),
            pl.BlockSpec((g, t, dq), lambda b: (b, 0, 0)),
            pl.BlockSpec((g, t, kvd), lambda b: (b, 0, 0)),
            pl.BlockSpec((g, w_buf, kvd // 2), lambda b: (b, 0, 0)),
            pl.BlockSpec((g, w_buf, kvd // 2), lambda b: (b, 0, 0)),
            pl.BlockSpec(kvm.shape, lambda b: (0, 0)),
        ],
        out_specs=pl.BlockSpec((g, t, dq), lambda b: (b, 0, 0)),
        out_shape=jax.ShapeDtypeStruct((bsz, t, dq), BF16),
        compiler_params=_params("parallel"),
        name="attn_sample",
    )(sinks, q, kvn, ck, cv, kvm)


def kernel(x_prompt, x_sample, state_conv, state_ssm, cache_k_win, cache_v_win, meta_tokens, a_norm_w, a_in_proj, a_conv_w, a_conv_b, a_dt_bias, a_log, a_d_skip, a_gate_norm_w, a_out_proj, kv_norm_w, w_kv, b_norm_w, w_q, attn_sinks, w_o, mlp_norm_w, w_up, w_down, final_norm_w):
    n_prompt, seq, d_model = x_prompt.shape
    n_dec, dec_seq, _ = x_sample.shape
    n_a = a_in_proj.shape[0]
    depth = w_up.shape[0]
    n_meta = meta_tokens.shape[0]
    d_inner = a_out_proj.shape[1]
    conv_dim = a_conv_w.shape[2]
    n_heads = a_log.shape[1]
    w_buf = cache_k_win.shape[1]
    kvw = N_KV_HEADS * ATTN_HEAD_DIM
    assert n_heads * SSM_HEAD_DIM == d_inner and n_heads <= LANES

    hm = meta_tokens.astype(F32)
    hp = x_prompt.reshape(n_prompt * seq, d_model)
    hs = x_sample.reshape(n_dec * dec_seq, d_model)

    def pad_lanes(v):
        return jnp.pad(v, (0, LANES - v.shape[0])).reshape(1, LANES)

    def pad_conv_state(s):
        return jnp.pad(s, ((0, 0), (SUBLANES - (CONV_K - 1), 0), (0, 0)))

    conv_p_list, ssm_p_list, conv_s_list, ssm_s_list = [], [], [], []
    step_path = dec_seq == SUBLANES and (n_dec * dec_seq) % SSD_CHUNK == 0
    ssm_in_all = state_ssm.reshape(n_a, n_dec, d_inner, D_STATE)
    ssm_s_all = None
    kvm = kv_p = kv_s = None
    for layer in range(depth):
        wu = w_up[layer].astype(BF16)
        wd = w_down[layer].astype(BF16)
        last = layer == depth - 1
        if layer < n_a:
            i = layer
            w_in = a_in_proj[i]
            w_main = jnp.concatenate([w_in[:, d_inner:d_inner + conv_dim], w_in[:, :d_inner]], axis=1).astype(BF16)
            w_dt = jnp.pad(w_in[:, d_inner + conv_dim:], ((0, 0), (0, LANES - n_heads))).astype(BF16)
            w_out = a_out_proj[i].astype(BF16)
            prm = (a_conv_w[i], a_conv_b[i].reshape(1, conv_dim), pad_lanes(a_dt_bias[i]), pad_lanes(a_log[i]),
                   jnp.repeat(a_d_skip[i], SSM_HEAD_DIM).reshape(1, d_inner), a_gate_norm_w[i].reshape(1, d_inner))

            def mixer(h, bsz, length, conv0, ssm0, shared):
                xbcz, dt = in_proj(h, a_norm_w[i], w_main, w_dt)
                xbcz = xbcz.reshape(bsz, length, conv_dim + d_inner)
                dt = dt.reshape(bsz, length, LANES)
                if shared and length % SSD_CHUNK == 0:
                    g, conv_o, ssm_o = ssd_chunked(xbcz, dt, conv0, ssm0, *prm)
                else:
                    g, conv_o, ssm_o = ssd_mixer(xbcz, dt, conv0, ssm0, *prm, shared_state=shared)
                h = matmul_residual(g.reshape(bsz * length, d_inner), w_out, h)
                return h, conv_o, ssm_o

            zero_conv = jnp.zeros((1, SUBLANES, conv_dim), F32)
            zero_ssm = jnp.zeros((1, d_inner, D_STATE), F32)
            hm, conv_m, ssm_m = mixer(hm, 1, n_meta, zero_conv, zero_ssm, True)
            hp, conv_p, ssm_p = mixer(hp, n_prompt, seq, conv_m, ssm_m, True)
            conv0_s = pad_conv_state(state_conv[i])
            if step_path:
                xbcz_s, dt_s = in_proj(hs, a_norm_w[i], w_main, w_dt)
                ypart, eax, xdtw, bb_s, cc_s, eacs = ssd_step_tokens(xbcz_s, dt_s, conv0_s, *prm[:5], t=dec_seq)
                cdecay = eacs.reshape(n_dec, dec_seq, LANES)[:, dec_seq - 1]
                g_s, ssm_s_all = ssd_step_state(cdecay, ssm_in_all, i, cc_s, bb_s, xdtw, eax, ypart, xbcz_s,
                                                prm[5], ssm_s_all, t=dec_seq)
                hs = matmul_residual(g_s, w_out, hs)
                conv_s = xbcz_s[:, :conv_dim].reshape(n_dec, dec_seq, conv_dim)[:, dec_seq - (CONV_K - 1):]
            else:
                hs, conv_s, ssm_s = mixer(hs, n_dec, dec_seq, conv0_s, ssm_in_all[i], False)
                conv_s = conv_s[:, SUBLANES - (CONV_K - 1):]
                ssm_s_list.append(ssm_s)
            conv_p_list.append(conv_p[:, SUBLANES - (CONV_K - 1):])
            ssm_p_list.append(ssm_p.reshape(n_prompt, n_heads, SSM_HEAD_DIM, D_STATE))
            conv_s_list.append(conv_s)
            hm = mlp(hm, mlp_norm_w[layer], wu, wd, final_norm_w, final_norm=False)
        else:
            j = layer - n_a
            if j == 0:
                wkv = w_kv.astype(BF16)
                kvm = norm_matmul(hm, kv_norm_w, wkv)
                kv_p = norm_matmul(hp, kv_norm_w, wkv).reshape(n_prompt, seq, 2 * kvw)
                kv_s = norm_matmul(hs, kv_norm_w, wkv).reshape(n_dec, dec_seq, 2 * kvw)
            wq = w_q[j].astype(BF16)
            wo = w_o[j].astype(BF16)
            dq = wq.shape[1]
            q_p = norm_matmul(hp, b_norm_w[j], wq).reshape(n_prompt, seq, dq)
            q_s = norm_matmul(hs, b_norm_w[j], wq).reshape(n_dec, dec_seq, dq)
            o_p = attn_prompt(q_p, kv_p, kvm, attn_sinks[j])
            o_s = attn_sample(q_s, kv_s, cache_k_win.reshape(n_dec, w_buf, kvw),
                              cache_v_win.reshape(n_dec, w_buf, kvw), kvm, attn_sinks[j])
            hp = matmul_residual(o_p.reshape(n_prompt * seq, dq), wo, hp)
            hs = matmul_residual(o_s.reshape(n_dec * dec_seq, dq), wo, hs)
        hp = mlp(hp, mlp_norm_w[layer], wu, wd, final_norm_w, final_norm=last)
        hs = mlp(hs, mlp_norm_w[layer], wu, wd, final_norm_w, final_norm=last)

    y_prompt = hp.reshape(n_prompt, seq, d_model)
    y_sample = hs.reshape(n_dec, dec_seq, d_model)
    kv_heads = (N_KV_HEADS, ATTN_HEAD_DIM)
    k_p = kv_p[:, seq - w_buf:, :kvw].reshape((n_prompt, w_buf) + kv_heads)
    v_p = kv_p[:, seq - w_buf:, kvw:].reshape((n_prompt, w_buf) + kv_heads)
    k_s = kv_s[:, :, :kvw].reshape((n_dec, dec_seq) + kv_heads)
    v_s = kv_s[:, :, kvw:].reshape((n_dec, dec_seq) + kv_heads)
    k_s_win = jnp.concatenate([cache_k_win, k_s], axis=1)[:, -w_buf:]
    v_s_win = jnp.concatenate([cache_v_win, v_s], axis=1)[:, -w_buf:]
    if not step_path:
        ssm_s_all = jnp.stack(ssm_s_list)
    return (y_prompt, y_sample, jnp.stack(conv_p_list), jnp.stack(ssm_p_list), k_p, v_p,
            jnp.stack(conv_s_list), ssm_s_all.reshape(state_ssm.shape), k_s_win, v_s_win)
```

```python
import functools

import jax
import jax.numpy as jnp
from jax import lax
from jax.experimental import pallas as pl
from jax.experimental.pallas import tpu as pltpu

F32 = jnp.float32
BF16 = jnp.bfloat16

N_GROUPS = 8
SSM_HEAD_DIM = 64
D_STATE = 128
CONV_K = 4
SSD_CHUNK = 128
ATTN_HEAD_DIM = 64
N_KV_HEADS = 4
WINDOW = 128
PAST_LEN = 8192
EPS = 1e-5
LOG2_E = 1.4426950408889634

LANES = 128
SUBLANES = 8
VMEM_LIMIT_BYTES = 52 * 1024 * 1024

_NT = (((1,), (1,)), ((), ()))
_TN = (((0,), (0,)), ((), ()))


def _params(*sem):
    return pltpu.CompilerParams(dimension_semantics=sem, vmem_limit_bytes=VMEM_LIMIT_BYTES)


def _rms(x, w):
    ms = jnp.mean(x * x, axis=-1, keepdims=True)
    return x * lax.rsqrt(ms + EPS) * w


def _silu(x):
    s = 0.5 * x
    return s + s * jnp.tanh(s)


def _row_tile(m, cap):
    t = min(m, cap)
    assert m % t == 0, (m, t)
    return t


def _norm_matmul_kernel(x_ref, nw_ref, w_ref, o_ref, xn_ref):
    @pl.when(pl.program_id(1) == 0)
    def _():
        xn_ref[...] = _rms(x_ref[...], nw_ref[...]).astype(BF16)

    o_ref[...] = jnp.dot(xn_ref[...], w_ref[...], preferred_element_type=F32)


def norm_matmul(x, nw, w, *, tm_cap=1024, tn_cap=1024):
    m, d = x.shape
    n = w.shape[1]
    tm, tn = _row_tile(m, tm_cap), _row_tile(n, tn_cap)
    return pl.pallas_call(
        _norm_matmul_kernel,
        grid=(m // tm, n // tn),
        in_specs=[
            pl.BlockSpec((tm, d), lambda i, j: (i, 0)),
            pl.BlockSpec((1, d), lambda i, j: (0, 0)),
            pl.BlockSpec((d, tn), lambda i, j: (0, j)),
        ],
        out_specs=pl.BlockSpec((tm, tn), lambda i, j: (i, j)),
        out_shape=jax.ShapeDtypeStruct((m, n), F32),
        scratch_shapes=[pltpu.VMEM((tm, d), BF16)],
        compiler_params=_params("parallel", "arbitrary"),
        name="norm_matmul",
    )(x, nw.reshape(1, d), w)


def _in_proj_kernel(x_ref, nw_ref, w_ref, wdt_ref, o_ref, dt_ref, xn_ref):
    @pl.when(pl.program_id(1) == 0)
    def _():
        xn = _rms(x_ref[...], nw_ref[...]).astype(BF16)
        xn_ref[...] = xn
        dt_ref[...] = jnp.dot(xn, wdt_ref[...], preferred_element_type=F32)

    o_ref[...] = jnp.dot(xn_ref[...], w_ref[...], preferred_element_type=F32)


def in_proj(x, nw, w, wdt, *, tm_cap=1024, tn_cap=1024):
    m, d = x.shape
    n = w.shape[1]
    tm, tn = _row_tile(m, tm_cap), _row_tile(n, tn_cap)
    return pl.pallas_call(
        _in_proj_kernel,
        grid=(m // tm, n // tn),
        in_specs=[
            pl.BlockSpec((tm, d), lambda i, j: (i, 0)),
            pl.BlockSpec((1, d), lambda i, j: (0, 0)),
            pl.BlockSpec((d, tn), lambda i, j: (0, j)),
            pl.BlockSpec((d, LANES), lambda i, j: (0, 0)),
        ],
        out_specs=[
            pl.BlockSpec((tm, tn), lambda i, j: (i, j)),
            pl.BlockSpec((tm, LANES), lambda i, j: (i, 0)),
        ],
        out_shape=[jax.ShapeDtypeStruct((m, n), F32), jax.ShapeDtypeStruct((m, LANES), F32)],
        scratch_shapes=[pltpu.VMEM((tm, d), BF16)],
        compiler_params=_params("parallel", "arbitrary"),
        name="in_proj",
    )(x, nw.reshape(1, d), w, wdt)


def _mlp_steps(x_fn, nw_ref, wu_ref, wd_ref, fw_ref, o_ref, xn_ref, final_norm):
    f = pl.program_id(1)

    @pl.when(f == 0)
    def _():
        x = x_fn()
        xn_ref[...] = _rms(x, nw_ref[...]).astype(BF16)
        o_ref[...] = x

    h = jnp.dot(xn_ref[...], wu_ref[...], preferred_element_type=F32)
    h = jnp.square(jnp.maximum(h, 0.0)).astype(BF16)
    o_ref[...] += jnp.dot(h, wd_ref[...], preferred_element_type=F32)

    if final_norm:
        @pl.when(f == pl.num_programs(1) - 1)
        def _():
            o_ref[...] = _rms(o_ref[...], fw_ref[...])


def _mlp_kernel(x_ref, nw_ref, wu_ref, wd_ref, fw_ref, o_ref, xn_ref, *, final_norm):
    _mlp_steps(lambda: x_ref[...], nw_ref, wu_ref, wd_ref, fw_ref, o_ref, xn_ref, final_norm)


def _proj_mlp_kernel(a_ref, wa_ref, x_ref, nw_ref, wu_ref, wd_ref, fw_ref, o_ref, xn_ref, *, final_norm):
    def block_input():
        return x_ref[...] + jnp.dot(a_ref[...], wa_ref[...], preferred_element_type=F32)

    _mlp_steps(block_input, nw_ref, wu_ref, wd_ref, fw_ref, o_ref, xn_ref, final_norm)


def mlp(x, nw, wu, wd, fw, *, final_norm, proj=None, tm_cap=1024, tf_cap=1024):
    m, d = x.shape
    dff = wu.shape[1]
    tm, tf = _row_tile(m, tm_cap), _row_tile(dff, tf_cap)
    in_specs = [
        pl.BlockSpec((tm, d), lambda i, f: (i, 0)),
        pl.BlockSpec((1, d), lambda i, f: (0, 0)),
        pl.BlockSpec((d, tf), lambda i, f: (0, f)),
        pl.BlockSpec((tf, d), lambda i, f: (f, 0)),
        pl.BlockSpec((1, d), lambda i, f: (0, 0)),
    ]
    args = [x, nw.reshape(1, d), wu, wd, fw.reshape(1, d)]
    body = _mlp_kernel
    if proj is not None:
        a, wa = proj
        k = a.shape[1]
        in_specs = [pl.BlockSpec((tm, k), lambda i, f: (i, 0)),
                    pl.BlockSpec((k, d), lambda i, f: (0, 0), pipeline_mode=pl.Buffered(1))] + in_specs
        args = [a, wa] + args
        body = _proj_mlp_kernel
    return pl.pallas_call(
        functools.partial(body, final_norm=final_norm),
        grid=(m // tm, dff // tf),
        in_specs=in_specs,
        out_specs=pl.BlockSpec((tm, d), lambda i, f: (i, 0)),
        out_shape=jax.ShapeDtypeStruct((m, d), F32),
        scratch_shapes=[pltpu.VMEM((tm, d), BF16)],
        compiler_params=_params("parallel", "arbitrary"),
        name="mlp" if proj is None else "proj_mlp",
    )(*args)


def _softplus(x):
    return jnp.maximum(x, 0.0) + jnp.log(1.0 + jnp.exp(-jnp.abs(x)))


def _split3(a):
    hi = a.astype(BF16)
    r1 = a - hi.astype(F32)
    mid = r1.astype(BF16)
    lo = (r1 - mid.astype(F32)).astype(BF16)
    return hi, mid, lo


def _ssd_kernel(xbc_ref, z_ref, dt_ref, conv0_ref, ssm0_ref,
                cw_ref, cb_ref, dtb_ref, alog_ref, dsk_ref, gw_ref,
                g_ref, convo_ref, ssmo_ref,
                xpad_ref, act_ref, state_ref, y_ref, *, q, n_heads):
    c = pl.program_id(1)
    p = SSM_HEAD_DIM
    d_inner = n_heads * p
    hpg = n_heads // N_GROUPS
    conv_dim = d_inner + 2 * N_GROUPS * D_STATE
    pad = SUBLANES

    @pl.when(c == 0)
    def _():
        xpad_ref[0:pad, :] = conv0_ref[0]
        state_ref[...] = ssm0_ref[0]

    xpad_ref[pad:pad + q, :] = xbc_ref[0]
    cblk = 512
    for j in range(conv_dim // cblk):
        cs = slice(j * cblk, (j + 1) * cblk)
        conv = cb_ref[:, cs]
        for k in range(CONV_K):
            off = pad - (CONV_K - 1) + k
            conv = conv + xpad_ref[off:off + q, cs] * cw_ref[k:k + 1, cs]
        act_ref[:, cs] = _silu(conv)
    xpad_ref[0:pad, :] = xpad_ref[q:q + pad, :]

    dt = _softplus(dt_ref[0] + dtb_ref[...])
    a = dt * (-jnp.exp(alog_ref[...]))
    row = lax.broadcasted_iota(jnp.int32, (q, q), 0)
    col = lax.broadcasted_iota(jnp.int32, (q, q), 1)
    causal = row >= col
    tri = jnp.where(causal, 1.0, 0.0).astype(BF16)
    acs = None
    for part in _split3(a):
        t = jnp.dot(tri, part, preferred_element_type=F32)
        acs = t if acs is None else acs + t
    if q < LANES:
        acs_sq = jnp.concatenate([acs, jnp.zeros((LANES - q, LANES), F32)], axis=0)
    else:
        acs_sq = acs
    acs_t = acs_sq.T
    eacs = jnp.exp(acs)
    last = acs[q - 1:q, :]
    dte = jnp.exp(last - acs)
    cdecay = jnp.exp(last)

    for g in range(N_GROUPS):
        b0 = d_inner + g * D_STATE
        c0 = d_inner + N_GROUPS * D_STATE + g * D_STATE
        bb = act_ref[:, b0:b0 + D_STATE].astype(BF16)
        cc = act_ref[:, c0:c0 + D_STATE].astype(BF16)
        cbm = lax.dot_general(cc, bb, _NT, preferred_element_type=F32)
        for pair in range(hpg // 2):
            xs2 = act_ref[:, (g * hpg + 2 * pair) * p:(g * hpg + 2 * pair + 2) * p]
            dsk2 = dsk_ref[:, (g * hpg + 2 * pair) * p:(g * hpg + 2 * pair + 2) * p]
            ys = []
            for e in range(2):
                h = g * hpg + 2 * pair + e
                xs = xs2[:, e * p:(e + 1) * p]
                seg = acs[:, h:h + 1] - acs_t[h:h + 1, 0:q]
                decay = jnp.exp(jnp.where(causal, seg, -jnp.inf))
                m = (cbm * decay).astype(BF16)
                xdt = xs * dt[:, h:h + 1]
                y = jnp.dot(m, xdt.astype(BF16), preferred_element_type=F32)
                st = state_ref[h * p:(h + 1) * p, :]
                yoff = lax.dot_general(cc, st.astype(BF16), _NT, preferred_element_type=F32)
                y = y + yoff * eacs[:, h:h + 1] + dsk2[:, e * p:(e + 1) * p] * xs
                ys.append(y)
                xdtw = (xdt * dte[:, h:h + 1]).astype(BF16)
                snew = lax.dot_general(xdtw, bb, _TN, preferred_element_type=F32)
                state_ref[h * p:(h + 1) * p, :] = (
                    jnp.broadcast_to(cdecay[:, h:h + 1], (p, D_STATE)) * st + snew)
            y_ref[:, (g * hpg + 2 * pair) * p:(g * hpg + 2 * pair + 2) * p] = (
                jnp.concatenate(ys, axis=1))

    gsz = d_inner // N_GROUPS
    for g in range(N_GROUPS):
        gs = slice(g * gsz, (g + 1) * gsz)
        gg = y_ref[:, gs] * _silu(z_ref[0, :, gs])
        ms = jnp.mean(gg * gg, axis=-1, keepdims=True)
        g_ref[0, :, gs] = (gg * lax.rsqrt(ms + EPS) * gw_ref[:, gs]).astype(BF16)

    @pl.when(c == pl.num_programs(1) - 1)
    def _():
        convo_ref[0] = xpad_ref[0:pad, :]
        ssmo_ref[0] = state_ref[...]


def ssd_mixer(xbcz, dt, conv0, ssm0, cw, cb, dtb, alog, dsk, gw, *, shared_state):
    bsz, length, _ = xbcz.shape
    conv_dim = cw.shape[1]
    d_inner = gw.shape[1]
    n_heads = d_inner // SSM_HEAD_DIM
    q = SSD_CHUNK if length % SSD_CHUNK == 0 else length
    assert q % SUBLANES == 0 and q >= SUBLANES and conv_dim % d_inner == 0
    nc = length // q
    zblk = conv_dim // d_inner
    if shared_state:
        st_map = lambda b, c: (0, 0, 0)
    else:
        st_map = lambda b, c: (b, 0, 0)
    const = lambda b, c: (0, 0)
    return pl.pallas_call(
        functools.partial(_ssd_kernel, q=q, n_heads=n_heads),
        grid=(bsz, nc),
        in_specs=[
            pl.BlockSpec((1, q, conv_dim), lambda b, c: (b, c, 0)),
            pl.BlockSpec((1, q, d_inner), lambda b, c: (b, c, zblk)),
            pl.BlockSpec((1, q, LANES), lambda b, c: (b, c, 0)),
            pl.BlockSpec((1, SUBLANES, conv_dim), st_map),
            pl.BlockSpec((1, n_heads * SSM_HEAD_DIM, D_STATE), st_map),
            pl.BlockSpec((CONV_K, conv_dim), const),
            pl.BlockSpec((1, conv_dim), const),
            pl.BlockSpec((1, LANES), const),
            pl.BlockSpec((1, LANES), const),
            pl.BlockSpec((1, d_inner), const),
            pl.BlockSpec((1, d_inner), const),
        ],
        out_specs=[
            pl.BlockSpec((1, q, d_inner), lambda b, c: (b, c, 0)),
            pl.BlockSpec((1, SUBLANES, conv_dim), lambda b, c: (b, 0, 0)),
            pl.BlockSpec((1, n_heads * SSM_HEAD_DIM, D_STATE), lambda b, c: (b, 0, 0)),
        ],
        out_shape=[
            jax.ShapeDtypeStruct((bsz, length, d_inner), BF16),
            jax.ShapeDtypeStruct((bsz, SUBLANES, conv_dim), F32),
            jax.ShapeDtypeStruct((bsz, n_heads * SSM_HEAD_DIM, D_STATE), F32),
        ],
        scratch_shapes=[
            pltpu.VMEM((q + SUBLANES, conv_dim), F32),
            pltpu.VMEM((q, conv_dim), F32),
            pltpu.VMEM((n_heads * SSM_HEAD_DIM, D_STATE), F32),
            pltpu.VMEM((q, d_inner), F32),
        ],
        compiler_params=_params("parallel", "arbitrary"),
        name="ssd_mixer",
    )(xbcz, xbcz, dt, conv0, ssm0, cw, cb, dtb, alog, dsk, gw)


def _expansion_matrix(n_heads):
    h = jnp.arange(LANES)[:, None]
    ex = h == jnp.arange(n_heads * SSM_HEAD_DIM)[None, :] // SSM_HEAD_DIM
    return jnp.tile(ex.astype(BF16), (3, 1))


def _store_act_tile(j, act, xs_ref, bb_ref, cc_ref, d_inner, bc):
    lo = j * LANES
    if lo < d_inner:
        xs_ref[:, lo:lo + LANES] = act
    elif lo < d_inner + bc:
        bb_ref[:, lo - d_inner:lo - d_inner + LANES] = act.astype(BF16)
    else:
        cc_ref[:, lo - d_inner - bc:lo - d_inner - bc + LANES] = act.astype(BF16)


def _ssd_token_math(dt_raw, dtb_ref, alog_ref, lhs3_ref, acst_ref, *, seg_len):
    q = dt_raw.shape[0]
    dt = _softplus(dt_raw + dtb_ref[...])
    a = dt * (-jnp.exp(alog_ref[...]))
    row = lax.broadcasted_iota(jnp.int32, (q, q), 0)
    col = lax.broadcasted_iota(jnp.int32, (q, q), 1)
    mask = row >= col
    if seg_len != q:
        mask = jnp.logical_and(mask, row // seg_len == col // seg_len)
        seg_end = (row // seg_len) * seg_len + (seg_len - 1)
    tri = jnp.where(mask, 1.0, 0.0).astype(BF16)
    acs = None
    for part in _split3(a):
        t = jnp.dot(tri, part, preferred_element_type=F32)
        acs = t if acs is None else acs + t
    if seg_len == q:
        last = acs[q - 1:q, :]
    else:
        sel = jnp.where(col == seg_end, 1.0, 0.0).astype(BF16)
        last = None
        for part in _split3(acs):
            t = jnp.dot(sel, part, preferred_element_type=F32)
            last = t if last is None else last + t
    acst_ref[...] = (acs * LOG2_E).T
    eacs = jnp.exp(acs)
    stack = jnp.concatenate([dt, eacs, jnp.exp(last - acs)], axis=0)
    lhs3_ref[...] = jnp.concatenate(_split3(stack), axis=1)
    return mask, eacs


def _ssd_group_diag(g, mask, xs_ref, bb_ref, cc_ref, lhs3_ref, acst_ref, ex3_ref, hpg):
    q = xs_ref.shape[0]
    p = SSM_HEAD_DIM
    gch = hpg * p
    gs = slice(g * gch, (g + 1) * gch)
    bb = bb_ref[:, g * D_STATE:(g + 1) * D_STATE]
    cc = cc_ref[:, g * D_STATE:(g + 1) * D_STATE]
    cbm = lax.dot_general(cc, bb, _NT, preferred_element_type=F32)
    ex = jnp.dot(lhs3_ref[...], ex3_ref[:, gs], preferred_element_type=F32)
    dtx, eax, dtex = ex[0:q], ex[q:2 * q], ex[2 * q:3 * q]
    xs = xs_ref[:, gs]
    xdt = xs * dtx
    lane_head = lax.broadcasted_iota(jnp.int32, (q, gch), 1) // p
    ms, rhs = [], []
    for r in range(hpg):
        h = g * hpg + r
        rowb = jnp.broadcast_to(acst_ref[h:h + 1, :], (q, q))
        decay = jnp.exp2(jnp.where(mask, rowb.T - rowb, -jnp.inf))
        ms.append((cbm * decay).astype(BF16))
        rhs.append(jnp.where(lane_head == r, xdt, 0.0).astype(BF16))
    y = jnp.dot(jnp.concatenate(ms, axis=1), jnp.concatenate(rhs, axis=0),
                preferred_element_type=F32)
    return y, xs, xdt, eax, dtex, bb, cc


def _ssd_chunk_kernel(xbc_ref, z_ref, dt_ref, conv0_ref, ssm0_ref,
                      cw_ref, cb_ref, dtb_ref, alog_ref, dsk_ref, gw_ref, ex3_ref,
                      g_ref, convo_ref, ssmo_ref,
                      xpad_ref, xs_ref, bb_ref, cc_ref, st_ref, lhs3_ref, acst_ref, *, n_heads):
    c = pl.program_id(1)
    q = SSD_CHUNK
    p = SSM_HEAD_DIM
    d_inner = n_heads * p
    hpg = n_heads // N_GROUPS
    gch = hpg * p
    bc = N_GROUPS * D_STATE
    ntile = (d_inner + 2 * bc) // LANES
    pad = SUBLANES

    @pl.when(c == 0)
    def _():
        for j in range(ntile):
            xpad_ref[j, 0:pad, :] = conv0_ref[0, :, j * LANES:(j + 1) * LANES]
        st_ref[...] = ssm0_ref[0].T

    mask, _ = _ssd_token_math(dt_ref[0], dtb_ref, alog_ref, lhs3_ref, acst_ref, seg_len=q)

    for j in range(ntile):
        cs = slice(j * LANES, (j + 1) * LANES)
        xpad_ref[j, pad:pad + q, :] = xbc_ref[0, :, cs]
        conv = cb_ref[:, cs]
        for k in range(CONV_K):
            off = pad - (CONV_K - 1) + k
            conv = conv + xpad_ref[j, off:off + q, :] * cw_ref[k:k + 1, cs]
        _store_act_tile(j, _silu(conv), xs_ref, bb_ref, cc_ref, d_inner, bc)
        xpad_ref[j, 0:pad, :] = xpad_ref[j, q:q + pad, :]

    for g in range(N_GROUPS):
        gs = slice(g * gch, (g + 1) * gch)
        y, xs, xdt, eax, dtex, bb, cc = _ssd_group_diag(
            g, mask, xs_ref, bb_ref, cc_ref, lhs3_ref, acst_ref, ex3_ref, hpg)
        st = st_ref[:, gs]
        y = y + jnp.dot(cc, st.astype(BF16), preferred_element_type=F32) * eax + dsk_ref[:, gs] * xs
        gg = y * _silu(z_ref[0, :, gs])
        msq = jnp.mean(gg * gg, axis=-1, keepdims=True)
        g_ref[0, :, gs] = (gg * lax.rsqrt(msq + EPS) * gw_ref[:, gs]).astype(BF16)
        xdtw = (xdt * dtex).astype(BF16)
        snew = lax.dot_general(bb, xdtw, _TN, preferred_element_type=F32)
        st_ref[:, gs] = st * eax[q - 1:q, :] + snew

    @pl.when(c == pl.num_programs(1) - 1)
    def _():
        for j in range(ntile):
            convo_ref[0, :, j * LANES:(j + 1) * LANES] = xpad_ref[j, 0:pad, :]
        ssmo_ref[0] = st_ref[...].T


def ssd_chunked(xbcz, dt, conv0, ssm0, cw, cb, dtb, alog, dsk, gw):
    bsz, length, _ = xbcz.shape
    conv_dim = cw.shape[1]
    d_inner = gw.shape[1]
    n_heads = d_inner // SSM_HEAD_DIM
    q = SSD_CHUNK
    assert length % q == 0 and conv_dim % d_inner == 0 and d_inner // N_GROUPS == (n_heads // N_GROUPS) * SSM_HEAD_DIM
    ex3 = _expansion_matrix(n_heads)
    shared = lambda b, c: (0, 0, 0)
    const = lambda b, c: (0, 0)
    return pl.pallas_call(
        functools.partial(_ssd_chunk_kernel, n_heads=n_heads),
        grid=(bsz, length // q),
        in_specs=[
            pl.BlockSpec((1, q, conv_dim), lambda b, c: (b, c, 0)),
            pl.BlockSpec((1, q, d_inner), lambda b, c: (b, c, conv_dim // d_inner)),
            pl.BlockSpec((1, q, LANES), lambda b, c: (b, c, 0)),
            pl.BlockSpec((1, SUBLANES, conv_dim), shared),
            pl.BlockSpec((1, d_inner, D_STATE), shared),
            pl.BlockSpec((CONV_K, conv_dim), const),
            pl.BlockSpec((1, conv_dim), const),
            pl.BlockSpec((1, LANES), const),
            pl.BlockSpec((1, LANES), const),
            pl.BlockSpec((1, d_inner), const),
            pl.BlockSpec((1, d_inner), const),
            pl.BlockSpec(ex3.shape, const),
        ],
        out_specs=[
            pl.BlockSpec((1, q, d_inner), lambda b, c: (b, c, 0)),
            pl.BlockSpec((1, SUBLANES, conv_dim), lambda b, c: (b, 0, 0)),
            pl.BlockSpec((1, d_inner, D_STATE), lambda b, c: (b, 0, 0)),
        ],
        out_shape=[
            jax.ShapeDtypeStruct((bsz, length, d_inner), BF16),
            jax.ShapeDtypeStruct((bsz, SUBLANES, conv_dim), F32),
            jax.ShapeDtypeStruct((bsz, d_inner, D_STATE), F32),
        ],
        scratch_shapes=[
            pltpu.VMEM((conv_dim // LANES, q + SUBLANES, LANES), F32),
            pltpu.VMEM((q, d_inner), F32),
            pltpu.VMEM((q, N_GROUPS * D_STATE), BF16),
            pltpu.VMEM((q, N_GROUPS * D_STATE), BF16),
            pltpu.VMEM((D_STATE, d_inner), F32),
            pltpu.VMEM((3 * q, 3 * LANES), BF16),
            pltpu.VMEM((LANES, q), F32),
        ],
        compiler_params=_params("parallel", "arbitrary"),
        name="ssd_chunked",
    )(xbcz, xbcz, dt, conv0, ssm0, cw, cb, dtb, alog, dsk, gw, ex3)


def _ssd_step_tokens_kernel(xbc_ref, dt_ref, conv0_ref, cw_ref, cb_ref, dtb_ref, alog_ref, dsk_ref,
                            ex3_ref,
                            ypart_ref, eaxo_ref, xdtw_ref, bbo_ref, cco_ref, eacs_ref,
                            xpad_ref, xs_ref, bb_ref, cc_ref, lhs3_ref, acst_ref, *, n_heads, t):
    q = SSD_CHUNK
    p = SSM_HEAD_DIM
    d_inner = n_heads * p
    hpg = n_heads // N_GROUPS
    gch = hpg * p
    bc = N_GROUPS * D_STATE
    ntile = (d_inner + 2 * bc) // LANES
    nseq = q // t
    slot = 2 * t
    span = nseq * slot - t

    for j in range(ntile):
        cs = slice(j * LANES, (j + 1) * LANES)
        for i in range(nseq):
            xpad_ref[j, i * slot:i * slot + t, :] = conv0_ref[i, :, cs]
            xpad_ref[j, i * slot + t:(i + 1) * slot, :] = xbc_ref[i * t:(i + 1) * t, cs]
        conv = cb_ref[:, cs]
        for k in range(CONV_K):
            off = t - (CONV_K - 1) + k
            conv = conv + xpad_ref[j, off:off + span, :] * cw_ref[k:k + 1, cs]
        act = _silu(jnp.concatenate([conv[i * slot:i * slot + t] for i in range(nseq)], axis=0))
        _store_act_tile(j, act, xs_ref, bb_ref, cc_ref, d_inner, bc)
        lo = j * LANES
        if d_inner <= lo < d_inner + bc:
            bbo_ref[:, lo - d_inner:lo - d_inner + LANES] = act
        elif lo >= d_inner + bc:
            cco_ref[:, lo - d_inner - bc:lo - d_inner - bc + LANES] = act

    mask, eacs = _ssd_token_math(dt_ref[...], dtb_ref, alog_ref, lhs3_ref, acst_ref, seg_len=t)
    eacs_ref[...] = eacs
    for g in range(N_GROUPS):
        gs = slice(g * gch, (g + 1) * gch)
        y, xs, xdt, eax, dtex, _, _ = _ssd_group_diag(
            g, mask, xs_ref, bb_ref, cc_ref, lhs3_ref, acst_ref, ex3_ref, hpg)
        ypart_ref[:, gs] = y + dsk_ref[:, gs] * xs
        eaxo_ref[:, gs] = eax
        xdtw_ref[:, gs] = xdt * dtex


def ssd_step_tokens(xbcz, dt, conv0, cw, cb, dtb, alog, dsk, *, t):
    n_tok = xbcz.shape[0]
    conv_dim = cw.shape[1]
    d_inner = dsk.shape[1]
    n_heads = d_inner // SSM_HEAD_DIM
    q = SSD_CHUNK
    bc = N_GROUPS * D_STATE
    assert t == SUBLANES and n_tok % q == 0
    ex3 = _expansion_matrix(n_heads)
    const = lambda s: (0, 0)
    rows = lambda s: (s, 0)
    f32 = lambda n: jax.ShapeDtypeStruct((n_tok, n), F32)
    return pl.pallas_call(
        functools.partial(_ssd_step_tokens_kernel, n_heads=n_heads, t=t),
        grid=(n_tok // q,),
        in_specs=[
            pl.BlockSpec((q, conv_dim), rows),
            pl.BlockSpec((q, LANES), rows),
            pl.BlockSpec((q // t, SUBLANES, conv_dim), lambda s: (s, 0, 0)),
            pl.BlockSpec((CONV_K, conv_dim), const),
            pl.BlockSpec((1, conv_dim), const),
            pl.BlockSpec((1, LANES), const),
            pl.BlockSpec((1, LANES), const),
            pl.BlockSpec((1, d_inner), const),
            pl.BlockSpec(ex3.shape, const),
        ],
        out_specs=[
            pl.BlockSpec((q, d_inner), rows),
            pl.BlockSpec((q, d_inner), rows),
            pl.BlockSpec((q, d_inner), rows),
            pl.BlockSpec((q, bc), rows),
            pl.BlockSpec((q, bc), rows),
            pl.BlockSpec((q, LANES), rows),
        ],
        out_shape=[f32(d_inner), f32(d_inner), f32(d_inner), f32(bc), f32(bc), f32(LANES)],
        scratch_shapes=[
            pltpu.VMEM((conv_dim // LANES, 2 * q, LANES), F32),
            pltpu.VMEM((q, d_inner), F32),
            pltpu.VMEM((q, bc), BF16),
            pltpu.VMEM((q, bc), BF16),
            pltpu.VMEM((3 * q, 3 * LANES), BF16),
            pltpu.VMEM((LANES, q), F32),
        ],
        compiler_params=_params("parallel"),
        name="ssd_step_tokens",
    )(xbcz, dt, conv0, cw, cb, dtb, alog, dsk, ex3)


def _ssd_step_state_kernel(cd_ref, st_ref, cc_ref, bb_ref, xdtw_ref, eax_ref, ypart_ref, z_ref, gw_ref,
                           *rest, n_heads, t, nseq):
    g_ref, sto_ref = rest[-2:]
    s = pl.program_id(0)
    p = SSM_HEAD_DIM
    hpg = n_heads // N_GROUPS
    gch = hpg * p
    for g in range(N_GROUPS):
        gs = slice(g * gch, (g + 1) * gch)
        ns = slice(g * D_STATE, (g + 1) * D_STATE)
        ys = []
        for i in range(nseq):
            rows = slice(i * t, (i + 1) * t)
            h0 = st_ref[0, i, gs, :]
            yoff = lax.dot_general(cc_ref[rows, ns].astype(BF16), h0.astype(BF16), _NT,
                                   preferred_element_type=F32)
            ys.append(ypart_ref[rows, gs] + yoff * eax_ref[rows, gs])
            upd = lax.dot_general(xdtw_ref[rows, gs].astype(BF16), bb_ref[rows, ns].astype(BF16), _TN,
                                  preferred_element_type=F32)
            for r in range(hpg):
                hs = slice(r * p, (r + 1) * p)
                sto_ref[0, i, g * gch + r * p:g * gch + (r + 1) * p, :] = (
                    h0[hs] * cd_ref[s * nseq + i, g * hpg + r] + upd[hs])
        gg = jnp.concatenate(ys, axis=0) * _silu(z_ref[:, gs])
        msq = jnp.mean(gg * gg, axis=-1, keepdims=True)
        g_ref[:, gs] = (gg * lax.rsqrt(msq + EPS) * gw_ref[:, gs]).astype(BF16)


def ssd_step_state(cd, state_all, layer, cc, bb, xdtw, eax, ypart, xbcz, gw, stacked_out, *, t, seqs_per_step=4):
    n_layers, n_seq, d_inner, d_state = state_all.shape
    n_tok = ypart.shape[0]
    bc = cc.shape[1]
    conv_dim = xbcz.shape[1] - d_inner
    nseq = _row_tile(n_seq, seqs_per_step)
    rt = nseq * t
    rows = lambda s, cd_: (s, 0)
    st_map = lambda s, cd_: (layer, s, 0, 0)
    in_specs = [
        pl.BlockSpec((1, nseq, d_inner, d_state), st_map),
        pl.BlockSpec((rt, bc), rows),
        pl.BlockSpec((rt, bc), rows),
        pl.BlockSpec((rt, d_inner), rows),
        pl.BlockSpec((rt, d_inner), rows),
        pl.BlockSpec((rt, d_inner), rows),
        pl.BlockSpec((rt, d_inner), lambda s, cd_: (s, conv_dim // d_inner)),
        pl.BlockSpec((1, d_inner), lambda s, cd_: (0, 0)),
    ]
    args = [cd, state_all, cc, bb, xdtw, eax, ypart, xbcz, gw]
    aliases = {}
    if stacked_out is not None:
        in_specs.append(pl.BlockSpec(memory_space=pl.ANY))
        aliases = {len(args): 1}
        args.append(stacked_out)
    return pl.pallas_call(
        functools.partial(_ssd_step_state_kernel, n_heads=d_inner // SSM_HEAD_DIM, t=t, nseq=nseq),
        grid_spec=pltpu.PrefetchScalarGridSpec(
            num_scalar_prefetch=1,
            grid=(n_seq // nseq,),
            in_specs=in_specs,
            out_specs=[
                pl.BlockSpec((rt, d_inner), rows),
                pl.BlockSpec((1, nseq, d_inner, d_state), st_map),
            ],
        ),
        out_shape=[
            jax.ShapeDtypeStruct((n_tok, d_inner), BF16),
            jax.ShapeDtypeStruct((n_layers, n_seq, d_inner, d_state), F32),
        ],
        input_output_aliases=aliases,
        compiler_params=_params("parallel"),
        name="ssd_step_state",
    )(*args)


def _attn_prompt_kernel(sink_ref, q_ref, kvc_ref, kvp_ref, kvm_ref, o_ref, *, n_q_heads):
    n = pl.program_id(1)
    d = ATTN_HEAD_DIM
    kvw = N_KV_HEADS * d
    rep = n_q_heads // N_KV_HEADS
    scale = d ** -0.5
    w = q_ref.shape[1]
    rows = rep * w
    qi = lax.broadcasted_iota(jnp.int32, (rows, w), 0) % w
    ci = lax.broadcasted_iota(jnp.int32, (rows, w), 1)
    from_prev = ci > qi
    no_prev = jnp.where(n > 0, 0.0, -jnp.inf)
    kvm, kvp, kvc = kvm_ref[...], kvp_ref[0], kvc_ref[0]
    outs = []
    for k in range(N_KV_HEADS):
        ks, vs = slice(k * d, (k + 1) * d), slice(kvw + k * d, kvw + (k + 1) * d)
        qk = q_ref[0, :, k * rep * d:(k + 1) * rep * d]
        q4 = jnp.concatenate([qk[:, r * d:(r + 1) * d] for r in range(rep)], axis=0)
        q4 = (q4 * scale).astype(BF16)
        k2 = jnp.concatenate([kvp[:, ks], kvc[:, ks]], axis=0).astype(BF16)
        v2 = jnp.concatenate([kvp[:, vs], kvc[:, vs]], axis=0).astype(BF16)
        s2 = lax.dot_general(q4, k2, _NT, preferred_element_type=F32)
        s_w = jnp.where(from_prev, s2[:, :w] + no_prev, s2[:, w:])
        s_m = lax.dot_general(q4, kvm[:, ks].astype(BF16), _NT, preferred_element_type=F32)
        p_ws, p_ms, dens = [], [], []
        for r in range(rep):
            sl = slice(r * w, (r + 1) * w)
            sink = sink_ref[k * rep + r]
            mx = jnp.maximum(jnp.maximum(jnp.max(s_w[sl], axis=-1, keepdims=True),
                                         jnp.max(s_m[sl], axis=-1, keepdims=True)), sink)
            p_w = jnp.exp(s_w[sl] - mx)
            p_m = jnp.exp(s_m[sl] - mx)
            dens.append(jnp.exp(sink - mx) + jnp.sum(p_w, axis=-1, keepdims=True)
                        + jnp.sum(p_m, axis=-1, keepdims=True))
            p_ws.append(p_w)
            p_ms.append(p_m.astype(BF16))
        p_w = jnp.concatenate(p_ws, axis=0)
        p2 = jnp.concatenate([jnp.where(from_prev, p_w, 0.0), jnp.where(from_prev, 0.0, p_w)],
                             axis=1).astype(BF16)
        o4 = (jnp.dot(p2, v2, preferred_element_type=F32)
              + jnp.dot(jnp.concatenate(p_ms, axis=0), kvm[:, vs].astype(BF16), preferred_element_type=F32))
        outs.extend(o4[r * w:(r + 1) * w, :] / dens[r] for r in range(rep))
    o_ref[0] = jnp.concatenate(outs, axis=1).astype(BF16)


def attn_prompt(q, kv, kvm, sinks):
    bsz, s, dq = q.shape
    kvd = kv.shape[2]
    nb = s // WINDOW
    return pl.pallas_call(
        functools.partial(_attn_prompt_kernel, n_q_heads=dq // ATTN_HEAD_DIM),
        grid=(bsz, nb),
        in_specs=[
            pl.BlockSpec(memory_space=pltpu.SMEM),
            pl.BlockSpec((1, WINDOW, dq), lambda b, n: (b, n, 0)),
            pl.BlockSpec((1, WINDOW, kvd), lambda b, n: (b, n, 0)),
            pl.BlockSpec((1, WINDOW, kvd), lambda b, n: (b, jnp.maximum(n - 1, 0), 0)),
            pl.BlockSpec(kvm.shape, lambda b, n: (0, 0)),
        ],
        out_specs=pl.BlockSpec((1, WINDOW, dq), lambda b, n: (b, n, 0)),
        out_shape=jax.ShapeDtypeStruct((bsz, s, dq), BF16),
        compiler_params=_params("parallel", "arbitrary"),
        name="attn_prompt",
    )(sinks, q, kv, kv, kvm)


def _attn_sample_kernel(sink_ref, q_ref, kvn_ref, ck_ref, cv_ref, kvm_ref, o_ref, *, n_q_heads, n_meta):
    d = ATTN_HEAD_DIM
    kvw = N_KV_HEADS * d
    rep = n_q_heads // N_KV_HEADS
    scale = d ** -0.5
    nseq, t, _ = q_ref.shape
    w_buf = ck_ref.shape[1]
    rows = rep * t
    nx = n_meta + t
    tq_buf = lax.broadcasted_iota(jnp.int32, (rows, w_buf), 0) % t
    pos_buf = PAST_LEN - w_buf + lax.broadcasted_iota(jnp.int32, (rows, w_buf), 1)
    buf_mask = jnp.logical_and(PAST_LEN + tq_buf - pos_buf < WINDOW, pos_buf >= n_meta)
    tq_x = lax.broadcasted_iota(jnp.int32, (rows, nx), 0) % t
    c_x = lax.broadcasted_iota(jnp.int32, (rows, nx), 1) - n_meta
    x_mask = jnp.logical_or(c_x < 0, jnp.logical_and(c_x <= tq_x, tq_x - c_x < WINDOW))
    head_in_group = lax.broadcasted_iota(jnp.int32, (rows, 1), 0) // t
    kvm = kvm_ref[...]
    sinks = []
    for k in range(N_KV_HEADS):
        sink = jnp.zeros((rows, 1), F32)
        for r in range(rep):
            sink = jnp.where(head_in_group == r, sink_ref[k * rep + r], sink)
        sinks.append(sink)
    chains = [(i, k) for i in range(nseq) for k in range(N_KV_HEADS)]

    scores, values = {}, {}
    for i in range(nseq):
        q, kvn, ck, cv = q_ref[i], kvn_ref[i], ck_ref[i], cv_ref[i]
        for k in range(N_KV_HEADS):
            ks, vs = slice(k * d, (k + 1) * d), slice(kvw + k * d, kvw + (k + 1) * d)
            q4 = jnp.concatenate([q[:, (k * rep + r) * d:(k * rep + r + 1) * d] for r in range(rep)], axis=0)
            q4 = (q4 * scale).astype(BF16)
            kx = jnp.concatenate([kvm[:, ks], kvn[:, ks]], axis=0).astype(BF16)
            vx = jnp.concatenate([kvm[:, vs], kvn[:, vs]], axis=0).astype(BF16)
            s_b = lax.dot_general(q4, ck[:, ks].astype(BF16), _NT, preferred_element_type=F32)
            s_x = lax.dot_general(q4, kx, _NT, preferred_element_type=F32)
            scores[i, k] = (jnp.where(buf_mask, s_b, -jnp.inf), jnp.where(x_mask, s_x, -jnp.inf))
            values[i, k] = (cv[:, ks].astype(BF16), vx)

    maxes = {}
    for c in chains:
        s_b, s_x = scores[c]
        maxes[c] = jnp.maximum(sinks[c[1]], jnp.maximum(jnp.max(s_b, axis=-1, keepdims=True),
                                                         jnp.max(s_x, axis=-1, keepdims=True)))
    probs, dens = {}, {}
    for c in chains:
        s_b, s_x = scores[c]
        p_b, p_x = jnp.exp(s_b - maxes[c]), jnp.exp(s_x - maxes[c])
        probs[c] = (p_b.astype(BF16), p_x.astype(BF16))
        dens[c] = (jnp.exp(sinks[c[1]] - maxes[c]) + jnp.sum(p_b, axis=-1, keepdims=True)
                   + jnp.sum(p_x, axis=-1, keepdims=True))
    outs = {}
    for c in chains:
        outs[c] = (jnp.dot(probs[c][0], values[c][0], preferred_element_type=F32)
                   + jnp.dot(probs[c][1], values[c][1], preferred_element_type=F32)) / dens[c]
    for i in range(nseq):
        heads = [outs[i, k][r * t:(r + 1) * t, :] for k in range(N_KV_HEADS) for r in range(rep)]
        o_ref[i] = jnp.concatenate(heads, axis=1).astype(BF16)


def attn_sample(q, kvn, ck, cv, kvm, sinks, *, seqs_per_step=8):
    bsz, t, dq = q.shape
    kvd = kvn.shape[2]
    w_buf = ck.shape[1]
    g = _row_tile(bsz, seqs_per_step)
    return pl.pallas_call(
        functools.partial(_attn_sample_kernel, n_q_heads=dq // ATTN_HEAD_DIM, n_meta=kvm.shape[0]),
        grid=(bsz // g,),
        in_specs=[
            pl.BlockSpec(memory_space=pltpu.SMEM),
            pl.BlockSpec((g, t, dq), lambda b: (b, 0, 0)),
            pl.BlockSpec((g, t, kvd), lambda b: (b, 0, 0)),
            pl.BlockSpec((g, w_buf, kvd // 2), lambda b: (b, 0, 0)),
            pl.BlockSpec((g, w_buf, kvd // 2), lambda b: (b, 0, 0)),
            pl.BlockSpec(kvm.shape, lambda b: (0, 0)),
        ],
        out_specs=pl.BlockSpec((g, t, dq), lambda b: (b, 0, 0)),
        out_shape=jax.ShapeDtypeStruct((bsz, t, dq), BF16),
        compiler_params=_params("parallel"),
        name="attn_sample",
    )(sinks, q, kvn, ck, cv, kvm)


def kernel(x_prompt, x_sample, state_conv, state_ssm, cache_k_win, cache_v_win, meta_tokens, a_norm_w, a_in_proj, a_conv_w, a_conv_b, a_dt_bias, a_log, a_d_skip, a_gate_norm_w, a_out_proj, kv_norm_w, w_kv, b_norm_w, w_q, attn_sinks, w_o, mlp_norm_w, w_up, w_down, final_norm_w):
    n_prompt, seq, d_model = x_prompt.shape
    n_dec, dec_seq, _ = x_sample.shape
    n_a = a_in_proj.shape[0]
    depth = w_up.shape[0]
    n_meta = meta_tokens.shape[0]
    d_inner = a_out_proj.shape[1]
    conv_dim = a_conv_w.shape[2]
    n_heads = a_log.shape[1]
    w_buf = cache_k_win.shape[1]
    kvw = N_KV_HEADS * ATTN_HEAD_DIM
    assert n_heads * SSM_HEAD_DIM == d_inner and n_heads <= LANES

    hm = meta_tokens.astype(F32)
    hp = x_prompt.reshape(n_prompt * seq, d_model)
    hs = x_sample.reshape(n_dec * dec_seq, d_model)

    def pad_lanes(v):
        return jnp.pad(v, (0, LANES - v.shape[0])).reshape(1, LANES)

    def pad_conv_state(s):
        return jnp.pad(s, ((0, 0), (SUBLANES - (CONV_K - 1), 0), (0, 0)))

    conv_p_list, ssm_p_list, conv_s_list, ssm_s_list = [], [], [], []
    step_path = dec_seq == SUBLANES and (n_dec * dec_seq) % SSD_CHUNK == 0
    ssm_in_all = state_ssm.reshape(n_a, n_dec, d_inner, D_STATE)
    ssm_s_all = None
    kvm = kv_p = kv_s = None
    for layer in range(depth):
        wu = w_up[layer].astype(BF16)
        wd = w_down[layer].astype(BF16)
        last = layer == depth - 1
        if layer < n_a:
            i = layer
            w_in = a_in_proj[i]
            w_main = jnp.concatenate([w_in[:, d_inner:d_inner + conv_dim], w_in[:, :d_inner]], axis=1).astype(BF16)
            w_dt = jnp.pad(w_in[:, d_inner + conv_dim:], ((0, 0), (0, LANES - n_heads))).astype(BF16)
            w_out = a_out_proj[i].astype(BF16)
            prm = (a_conv_w[i], a_conv_b[i].reshape(1, conv_dim), pad_lanes(a_dt_bias[i]), pad_lanes(a_log[i]),
                   jnp.repeat(a_d_skip[i], SSM_HEAD_DIM).reshape(1, d_inner), a_gate_norm_w[i].reshape(1, d_inner))

            def mixer(h, bsz, length, conv0, ssm0, shared):
                xbcz, dt = in_proj(h, a_norm_w[i], w_main, w_dt)
                xbcz = xbcz.reshape(bsz, length, conv_dim + d_inner)
                dt = dt.reshape(bsz, length, LANES)
                if shared and length % SSD_CHUNK == 0:
                    g, conv_o, ssm_o = ssd_chunked(xbcz, dt, conv0, ssm0, *prm)
                else:
                    g, conv_o, ssm_o = ssd_mixer(xbcz, dt, conv0, ssm0, *prm, shared_state=shared)
                return g.reshape(bsz * length, d_inner), conv_o, ssm_o

            zero_conv = jnp.zeros((1, SUBLANES, conv_dim), F32)
            zero_ssm = jnp.zeros((1, d_inner, D_STATE), F32)
            g_m, conv_m, ssm_m = mixer(hm, 1, n_meta, zero_conv, zero_ssm, True)
            g_p, conv_p, ssm_p = mixer(hp, n_prompt, seq, conv_m, ssm_m, True)
            conv0_s = pad_conv_state(state_conv[i])
            if step_path:
                xbcz_s, dt_s = in_proj(hs, a_norm_w[i], w_main, w_dt)
                ypart, eax, xdtw, bb_s, cc_s, eacs = ssd_step_tokens(xbcz_s, dt_s, conv0_s, *prm[:5], t=dec_seq)
                cdecay = eacs.reshape(n_dec, dec_seq, LANES)[:, dec_seq - 1]
                g_s, ssm_s_all = ssd_step_state(cdecay, ssm_in_all, i, cc_s, bb_s, xdtw, eax, ypart, xbcz_s,
                                                prm[5], ssm_s_all, t=dec_seq)
                conv_s = xbcz_s[:, :conv_dim].reshape(n_dec, dec_seq, conv_dim)[:, dec_seq - (CONV_K - 1):]
            else:
                g_s, conv_s, ssm_s = mixer(hs, n_dec, dec_seq, conv0_s, ssm_in_all[i], False)
                conv_s = conv_s[:, SUBLANES - (CONV_K - 1):]
                ssm_s_list.append(ssm_s)
            conv_p_list.append(conv_p[:, SUBLANES - (CONV_K - 1):])
            ssm_p_list.append(ssm_p.reshape(n_prompt, n_heads, SSM_HEAD_DIM, D_STATE))
            conv_s_list.append(conv_s)
            hm = mlp(hm, mlp_norm_w[layer], wu, wd, final_norm_w, final_norm=False, proj=(g_m, w_out))
            proj_p, proj_s = (g_p, w_out), (g_s, w_out)
        else:
            j = layer - n_a
            if j == 0:
                wkv = w_kv.astype(BF16)
                kvm = norm_matmul(hm, kv_norm_w, wkv)
                kv_p = norm_matmul(hp, kv_norm_w, wkv).reshape(n_prompt, seq, 2 * kvw)
                kv_s = norm_matmul(hs, kv_norm_w, wkv).reshape(n_dec, dec_seq, 2 * kvw)
            wq = w_q[j].astype(BF16)
            wo = w_o[j].astype(BF16)
            dq = wq.shape[1]
            q_p = norm_matmul(hp, b_norm_w[j], wq).reshape(n_prompt, seq, dq)
            q_s = norm_matmul(hs, b_norm_w[j], wq).reshape(n_dec, dec_seq, dq)
            o_p = attn_prompt(q_p, kv_p, kvm, attn_sinks[j])
            o_s = attn_sample(q_s, kv_s, cache_k_win.reshape(n_dec, w_buf, kvw),
                              cache_v_win.reshape(n_dec, w_buf, kvw), kvm, attn_sinks[j])
            proj_p = (o_p.reshape(n_prompt * seq, dq), wo)
            proj_s = (o_s.reshape(n_dec * dec_seq, dq), wo)
        hp = mlp(hp, mlp_norm_w[layer], wu, wd, final_norm_w, final_norm=last, proj=proj_p)
        hs = mlp(hs, mlp_norm_w[layer], wu, wd, final_norm_w, final_norm=last, proj=proj_s)

    y_prompt = hp.reshape(n_prompt, seq, d_model)
    y_sample = hs.reshape(n_dec, dec_seq, d_model)
    kv_heads = (N_KV_HEADS, ATTN_HEAD_DIM)
    k_p = kv_p[:, seq - w_buf:, :kvw].reshape((n_prompt, w_buf) + kv_heads)
    v_p = kv_p[:, seq - w_buf:, kvw:].reshape((n_prompt, w_buf) + kv_heads)
    k_s = kv_s[:, :, :kvw].reshape((n_dec, dec_seq) + kv_heads)
    v_s = kv_s[:, :, kvw:].reshape((n_dec, dec_seq) + kv_heads)
    k_s_win = jnp.concatenate([cache_k_win, k_s], axis=1)[:, -w_buf:]
    v_s_win = jnp.concatenate([cache_v_win, v_s], axis=1)[:, -w_buf:]
    if not step_path:
        ssm_s_all = jnp.stack(ssm_s_list)
    return (y_prompt, y_sample, jnp.stack(conv_p_list), jnp.stack(ssm_p_list), k_p, v_p,
            jnp.stack(conv_s_list), ssm_s_all.reshape(state_ssm.shape), k_s_win, v_s_win)
```

```python
import functools

import jax
import jax.numpy as jnp
from jax import lax
from jax.experimental import pallas as pl
from jax.experimental.pallas import tpu as pltpu

F32 = jnp.float32
BF16 = jnp.bfloat16

N_GROUPS = 8
SSM_HEAD_DIM = 64
D_STATE = 128
CONV_K = 4
SSD_CHUNK = 128
ATTN_HEAD_DIM = 64
N_KV_HEADS = 4
WINDOW = 128
PAST_LEN = 8192
EPS = 1e-5
LOG2_E = 1.4426950408889634

LANES = 128
SUBLANES = 8
VMEM_LIMIT_BYTES = 52 * 1024 * 1024

_NT = (((1,), (1,)), ((), ()))
_TN = (((0,), (0,)), ((), ()))


def _params(*sem):
    return pltpu.CompilerParams(dimension_semantics=sem, vmem_limit_bytes=VMEM_LIMIT_BYTES)


def _rms(x, w):
    ms = jnp.mean(x * x, axis=-1, keepdims=True)
    return x * lax.rsqrt(ms + EPS) * w


def _silu(x):
    s = 0.5 * x
    return s + s * jnp.tanh(s)


def _row_tile(m, cap):
    t = min(m, cap)
    assert m % t == 0, (m, t)
    return t


def _norm_matmul_kernel(x_ref, nw_ref, w_ref, o_ref, xn_ref):
    @pl.when(pl.program_id(1) == 0)
    def _():
        xn_ref[...] = _rms(x_ref[...], nw_ref[...]).astype(BF16)

    o_ref[...] = jnp.dot(xn_ref[...], w_ref[...], preferred_element_type=F32)


def norm_matmul(x, nw, w, *, tm_cap=1024, tn_cap=1024):
    m, d = x.shape
    n = w.shape[1]
    tm, tn = _row_tile(m, tm_cap), _row_tile(n, tn_cap)
    return pl.pallas_call(
        _norm_matmul_kernel,
        grid=(m // tm, n // tn),
        in_specs=[
            pl.BlockSpec((tm, d), lambda i, j: (i, 0)),
            pl.BlockSpec((1, d), lambda i, j: (0, 0)),
            pl.BlockSpec((d, tn), lambda i, j: (0, j)),
        ],
        out_specs=pl.BlockSpec((tm, tn), lambda i, j: (i, j)),
        out_shape=jax.ShapeDtypeStruct((m, n), F32),
        scratch_shapes=[pltpu.VMEM((tm, d), BF16)],
        compiler_params=_params("parallel", "arbitrary"),
        name="norm_matmul",
    )(x, nw.reshape(1, d), w)


def _in_proj_kernel(x_ref, nw_ref, w_ref, wdt_ref, o_ref, dt_ref, xn_ref):
    @pl.when(pl.program_id(1) == 0)
    def _():
        xn = _rms(x_ref[...], nw_ref[...]).astype(BF16)
        xn_ref[...] = xn
        dt_ref[...] = jnp.dot(xn, wdt_ref[...], preferred_element_type=F32)

    o_ref[...] = jnp.dot(xn_ref[...], w_ref[...], preferred_element_type=F32)


def in_proj(x, nw, w, wdt, *, tm_cap=1024, tn_cap=1024):
    m, d = x.shape
    n = w.shape[1]
    tm, tn = _row_tile(m, tm_cap), _row_tile(n, tn_cap)
    return pl.pallas_call(
        _in_proj_kernel,
        grid=(m // tm, n // tn),
        in_specs=[
            pl.BlockSpec((tm, d), lambda i, j: (i, 0)),
            pl.BlockSpec((1, d), lambda i, j: (0, 0)),
            pl.BlockSpec((d, tn), lambda i, j: (0, j)),
            pl.BlockSpec((d, LANES), lambda i, j: (0, 0)),
        ],
        out_specs=[
            pl.BlockSpec((tm, tn), lambda i, j: (i, j)),
            pl.BlockSpec((tm, LANES), lambda i, j: (i, 0)),
        ],
        out_shape=[jax.ShapeDtypeStruct((m, n), F32), jax.ShapeDtypeStruct((m, LANES), F32)],
        scratch_shapes=[pltpu.VMEM((tm, d), BF16)],
        compiler_params=_params("parallel", "arbitrary"),
        name="in_proj",
    )(x, nw.reshape(1, d), w, wdt)


def _in_proj_conv_kernel(x_ref, nw_ref, w_ref, wdt_ref, cw_ref, cb_ref, conv0_ref,
                         xs_ref, bb_ref, cc_ref, sz_ref, dt_ref, tail_ref,
                         xn_ref, xpad_ref, halo_ref, *, tiles_per_seq, n_x, n_b):
    i = pl.program_id(0)
    j = pl.program_id(1)
    tm = x_ref.shape[0]
    tn = w_ref.shape[1]
    pad = SUBLANES
    sub = 2 * LANES

    @pl.when(j == 0)
    def _():
        xn = _rms(x_ref[...], nw_ref[...]).astype(BF16)
        xn_ref[...] = xn
        dt_ref[...] = jnp.dot(xn, wdt_ref[...], preferred_element_type=F32)

    def conv_tile(out_ref):
        @pl.when(i % tiles_per_seq == 0)
        def _():
            for l in range(tn // LANES):
                halo_ref[j, l] = conv0_ref[0, :, l * LANES:(l + 1) * LANES]

        def conv_act(s, raw):
            for e in range(sub // LANES):
                l = s * (sub // LANES) + e
                cs = slice(l * LANES, (l + 1) * LANES)
                xpad_ref[l, 0:pad, :] = halo_ref[j, l]
                xpad_ref[l, pad:pad + tm, :] = raw[:, e * LANES:(e + 1) * LANES]
                conv = cb_ref[:, cs]
                for k in range(CONV_K):
                    off = pad - (CONV_K - 1) + k
                    conv = conv + xpad_ref[l, off:off + tm, :] * cw_ref[k:k + 1, cs]
                out_ref[:, cs] = _silu(conv).astype(out_ref.dtype)
                last_rows = xpad_ref[l, tm:tm + pad, :]
                halo_ref[j, l] = last_rows
                tail_ref[0, :, cs] = last_rows

        _staggered(tn // sub, [matmul_cols, conv_act])

    def matmul_cols(s, _):
        return jnp.dot(xn_ref[...], w_ref[:, s * sub:(s + 1) * sub], preferred_element_type=F32)

    def gate_act(s, raw):
        sz_ref[:, s * sub:(s + 1) * sub] = _silu(raw).astype(BF16)

    pl.when(j < n_x)(lambda: conv_tile(xs_ref))
    pl.when(jnp.logical_and(j >= n_x, j < n_x + n_b))(lambda: conv_tile(bb_ref))
    pl.when(jnp.logical_and(j >= n_x + n_b, j < n_x + 2 * n_b))(lambda: conv_tile(cc_ref))
    pl.when(j >= n_x + 2 * n_b)(lambda: _staggered(tn // sub, [matmul_cols, gate_act]))


def in_proj_conv(x, nw, w, wdt, cw, cb, conv0, *, seq_len, d_inner, tm_cap=1024):
    m, d = x.shape
    conv_dim = cw.shape[1]
    bc = (conv_dim - d_inner) // 2
    tn = bc
    tm = _row_tile(seq_len, tm_cap)
    assert d_inner % tn == 0 and w.shape[1] == conv_dim + d_inner and m % seq_len == 0
    n_x, n_b, n_z = d_inner // tn, 1, d_inner // tn
    n_conv = n_x + 2 * n_b
    conv_col = lambda i, j: (0, jnp.minimum(j, n_conv - 1))
    bf = lambda n: jax.ShapeDtypeStruct((m, n), BF16)
    tiles_per_seq = seq_len // tm
    *acts, tails = pl.pallas_call(
        functools.partial(_in_proj_conv_kernel, tiles_per_seq=tiles_per_seq, n_x=n_x, n_b=n_b),
        grid=(m // tm, n_conv + n_z),
        in_specs=[
            pl.BlockSpec((tm, d), lambda i, j: (i, 0)),
            pl.BlockSpec((1, d), lambda i, j: (0, 0)),
            pl.BlockSpec((d, tn), lambda i, j: (0, j)),
            pl.BlockSpec((d, LANES), lambda i, j: (0, 0)),
            pl.BlockSpec((CONV_K, tn), conv_col),
            pl.BlockSpec((1, tn), conv_col),
            pl.BlockSpec((1, SUBLANES, tn), lambda i, j: (0, 0, jnp.minimum(j, n_conv - 1))),
        ],
        out_specs=[
            pl.BlockSpec((tm, tn), lambda i, j: (i, jnp.minimum(j, n_x - 1))),
            pl.BlockSpec((tm, tn), lambda i, j: (i, 0)),
            pl.BlockSpec((tm, tn), lambda i, j: (i, 0)),
            pl.BlockSpec((tm, tn), lambda i, j: (i, jnp.clip(j - n_conv, 0, n_z - 1))),
            pl.BlockSpec((tm, LANES), lambda i, j: (i, 0)),
            pl.BlockSpec((1, SUBLANES, tn), lambda i, j: (i, 0, jnp.minimum(j, n_conv - 1))),
        ],
        out_shape=[jax.ShapeDtypeStruct((m, d_inner), F32), bf(bc), bf(bc), bf(d_inner),
                   jax.ShapeDtypeStruct((m, LANES), F32),
                   jax.ShapeDtypeStruct((m // tm, SUBLANES, conv_dim), F32)],
        scratch_shapes=[
            pltpu.VMEM((tm, d), BF16),
            pltpu.VMEM((tn // LANES, tm + SUBLANES, LANES), F32),
            pltpu.VMEM((n_conv, tn // LANES, SUBLANES, LANES), F32),
        ],
        compiler_params=_params("arbitrary", "arbitrary"),
        name="in_proj_conv",
    )(x, nw.reshape(1, d), w, wdt, cw, cb, conv0)
    return (*acts, tails[tiles_per_seq - 1::tiles_per_seq])


def _mlp_steps(x_fn, nw_ref, wu_ref, wd_ref, fw_ref, o_ref, xn_ref, final_norm):
    f = pl.program_id(1)

    @pl.when(f == 0)
    def _():
        x = x_fn()
        xn_ref[...] = _rms(x, nw_ref[...]).astype(BF16)
        o_ref[...] = x

    h = jnp.dot(xn_ref[...], wu_ref[...], preferred_element_type=F32)
    h = jnp.square(jnp.maximum(h, 0.0)).astype(BF16)
    o_ref[...] += jnp.dot(h, wd_ref[...], preferred_element_type=F32)

    if final_norm:
        @pl.when(f == pl.num_programs(1) - 1)
        def _():
            o_ref[...] = _rms(o_ref[...], fw_ref[...])


def _mlp_kernel(x_ref, nw_ref, wu_ref, wd_ref, fw_ref, o_ref, xn_ref, *, final_norm):
    _mlp_steps(lambda: x_ref[...], nw_ref, wu_ref, wd_ref, fw_ref, o_ref, xn_ref, final_norm)


def _proj_mlp_kernel(a_ref, wa_ref, x_ref, nw_ref, wu_ref, wd_ref, fw_ref, o_ref, xn_ref, *, final_norm):
    def block_input():
        return x_ref[...] + jnp.dot(a_ref[...], wa_ref[...], preferred_element_type=F32)

    _mlp_steps(block_input, nw_ref, wu_ref, wd_ref, fw_ref, o_ref, xn_ref, final_norm)


def mlp(x, nw, wu, wd, fw, *, final_norm, proj=None, tm_cap=1024, tf_cap=1024):
    m, d = x.shape
    dff = wu.shape[1]
    tm, tf = _row_tile(m, tm_cap), _row_tile(dff, tf_cap)
    in_specs = [
        pl.BlockSpec((tm, d), lambda i, f: (i, 0)),
        pl.BlockSpec((1, d), lambda i, f: (0, 0)),
        pl.BlockSpec((d, tf), lambda i, f: (0, f)),
        pl.BlockSpec((tf, d), lambda i, f: (f, 0)),
        pl.BlockSpec((1, d), lambda i, f: (0, 0)),
    ]
    args = [x, nw.reshape(1, d), wu, wd, fw.reshape(1, d)]
    body = _mlp_kernel
    if proj is not None:
        a, wa = proj
        k = a.shape[1]
        in_specs = [pl.BlockSpec((tm, k), lambda i, f: (i, 0)),
                    pl.BlockSpec((k, d), lambda i, f: (0, 0), pipeline_mode=pl.Buffered(1))] + in_specs
        args = [a, wa] + args
        body = _proj_mlp_kernel
    return pl.pallas_call(
        functools.partial(body, final_norm=final_norm),
        grid=(m // tm, dff // tf),
        in_specs=in_specs,
        out_specs=pl.BlockSpec((tm, d), lambda i, f: (i, 0)),
        out_shape=jax.ShapeDtypeStruct((m, d), F32),
        scratch_shapes=[pltpu.VMEM((tm, d), BF16)],
        compiler_params=_params("parallel", "arbitrary"),
        name="mlp" if proj is None else "proj_mlp",
    )(*args)


def _softplus(x):
    return jnp.maximum(x, 0.0) + jnp.log(1.0 + jnp.exp(-jnp.abs(x)))


def _split3(a):
    hi = a.astype(BF16)
    r1 = a - hi.astype(F32)
    mid = r1.astype(BF16)
    lo = (r1 - mid.astype(F32)).astype(BF16)
    return hi, mid, lo


def _ssd_kernel(xbc_ref, z_ref, dt_ref, conv0_ref, ssm0_ref,
                cw_ref, cb_ref, dtb_ref, alog_ref, dsk_ref, gw_ref,
                g_ref, convo_ref, ssmo_ref,
                xpad_ref, act_ref, state_ref, y_ref, *, q, n_heads):
    c = pl.program_id(1)
    p = SSM_HEAD_DIM
    d_inner = n_heads * p
    hpg = n_heads // N_GROUPS
    conv_dim = d_inner + 2 * N_GROUPS * D_STATE
    pad = SUBLANES

    @pl.when(c == 0)
    def _():
        xpad_ref[0:pad, :] = conv0_ref[0]
        state_ref[...] = ssm0_ref[0]

    xpad_ref[pad:pad + q, :] = xbc_ref[0]
    cblk = 512
    for j in range(conv_dim // cblk):
        cs = slice(j * cblk, (j + 1) * cblk)
        conv = cb_ref[:, cs]
        for k in range(CONV_K):
            off = pad - (CONV_K - 1) + k
            conv = conv + xpad_ref[off:off + q, cs] * cw_ref[k:k + 1, cs]
        act_ref[:, cs] = _silu(conv)
    xpad_ref[0:pad, :] = xpad_ref[q:q + pad, :]

    dt = _softplus(dt_ref[0] + dtb_ref[...])
    a = dt * (-jnp.exp(alog_ref[...]))
    row = lax.broadcasted_iota(jnp.int32, (q, q), 0)
    col = lax.broadcasted_iota(jnp.int32, (q, q), 1)
    causal = row >= col
    tri = jnp.where(causal, 1.0, 0.0).astype(BF16)
    acs = None
    for part in _split3(a):
        t = jnp.dot(tri, part, preferred_element_type=F32)
        acs = t if acs is None else acs + t
    if q < LANES:
        acs_sq = jnp.concatenate([acs, jnp.zeros((LANES - q, LANES), F32)], axis=0)
    else:
        acs_sq = acs
    acs_t = acs_sq.T
    eacs = jnp.exp(acs)
    last = acs[q - 1:q, :]
    dte = jnp.exp(last - acs)
    cdecay = jnp.exp(last)

    for g in range(N_GROUPS):
        b0 = d_inner + g * D_STATE
        c0 = d_inner + N_GROUPS * D_STATE + g * D_STATE
        bb = act_ref[:, b0:b0 + D_STATE].astype(BF16)
        cc = act_ref[:, c0:c0 + D_STATE].astype(BF16)
        cbm = lax.dot_general(cc, bb, _NT, preferred_element_type=F32)
        for pair in range(hpg // 2):
            xs2 = act_ref[:, (g * hpg + 2 * pair) * p:(g * hpg + 2 * pair + 2) * p]
            dsk2 = dsk_ref[:, (g * hpg + 2 * pair) * p:(g * hpg + 2 * pair + 2) * p]
            ys = []
            for e in range(2):
                h = g * hpg + 2 * pair + e
                xs = xs2[:, e * p:(e + 1) * p]
                seg = acs[:, h:h + 1] - acs_t[h:h + 1, 0:q]
                decay = jnp.exp(jnp.where(causal, seg, -jnp.inf))
                m = (cbm * decay).astype(BF16)
                xdt = xs * dt[:, h:h + 1]
                y = jnp.dot(m, xdt.astype(BF16), preferred_element_type=F32)
                st = state_ref[h * p:(h + 1) * p, :]
                yoff = lax.dot_general(cc, st.astype(BF16), _NT, preferred_element_type=F32)
                y = y + yoff * eacs[:, h:h + 1] + dsk2[:, e * p:(e + 1) * p] * xs
                ys.append(y)
                xdtw = (xdt * dte[:, h:h + 1]).astype(BF16)
                snew = lax.dot_general(xdtw, bb, _TN, preferred_element_type=F32)
                state_ref[h * p:(h + 1) * p, :] = (
                    jnp.broadcast_to(cdecay[:, h:h + 1], (p, D_STATE)) * st + snew)
            y_ref[:, (g * hpg + 2 * pair) * p:(g * hpg + 2 * pair + 2) * p] = (
                jnp.concatenate(ys, axis=1))

    gsz = d_inner // N_GROUPS
    for g in range(N_GROUPS):
        gs = slice(g * gsz, (g + 1) * gsz)
        gg = y_ref[:, gs] * _silu(z_ref[0, :, gs])
        ms = jnp.mean(gg * gg, axis=-1, keepdims=True)
        g_ref[0, :, gs] = (gg * lax.rsqrt(ms + EPS) * gw_ref[:, gs]).astype(BF16)

    @pl.when(c == pl.num_programs(1) - 1)
    def _():
        convo_ref[0] = xpad_ref[0:pad, :]
        ssmo_ref[0] = state_ref[...]


def ssd_mixer(xbcz, dt, conv0, ssm0, cw, cb, dtb, alog, dsk, gw, *, shared_state):
    bsz, length, _ = xbcz.shape
    conv_dim = cw.shape[1]
    d_inner = gw.shape[1]
    n_heads = d_inner // SSM_HEAD_DIM
    q = SSD_CHUNK if length % SSD_CHUNK == 0 else length
    assert q % SUBLANES == 0 and q >= SUBLANES and conv_dim % d_inner == 0
    nc = length // q
    zblk = conv_dim // d_inner
    if shared_state:
        st_map = lambda b, c: (0, 0, 0)
    else:
        st_map = lambda b, c: (b, 0, 0)
    const = lambda b, c: (0, 0)
    return pl.pallas_call(
        functools.partial(_ssd_kernel, q=q, n_heads=n_heads),
        grid=(bsz, nc),
        in_specs=[
            pl.BlockSpec((1, q, conv_dim), lambda b, c: (b, c, 0)),
            pl.BlockSpec((1, q, d_inner), lambda b, c: (b, c, zblk)),
            pl.BlockSpec((1, q, LANES), lambda b, c: (b, c, 0)),
            pl.BlockSpec((1, SUBLANES, conv_dim), st_map),
            pl.BlockSpec((1, n_heads * SSM_HEAD_DIM, D_STATE), st_map),
            pl.BlockSpec((CONV_K, conv_dim), const),
            pl.BlockSpec((1, conv_dim), const),
            pl.BlockSpec((1, LANES), const),
            pl.BlockSpec((1, LANES), const),
            pl.BlockSpec((1, d_inner), const),
            pl.BlockSpec((1, d_inner), const),
        ],
        out_specs=[
            pl.BlockSpec((1, q, d_inner), lambda b, c: (b, c, 0)),
            pl.BlockSpec((1, SUBLANES, conv_dim), lambda b, c: (b, 0, 0)),
            pl.BlockSpec((1, n_heads * SSM_HEAD_DIM, D_STATE), lambda b, c: (b, 0, 0)),
        ],
        out_shape=[
            jax.ShapeDtypeStruct((bsz, length, d_inner), BF16),
            jax.ShapeDtypeStruct((bsz, SUBLANES, conv_dim), F32),
            jax.ShapeDtypeStruct((bsz, n_heads * SSM_HEAD_DIM, D_STATE), F32),
        ],
        scratch_shapes=[
            pltpu.VMEM((q + SUBLANES, conv_dim), F32),
            pltpu.VMEM((q, conv_dim), F32),
            pltpu.VMEM((n_heads * SSM_HEAD_DIM, D_STATE), F32),
            pltpu.VMEM((q, d_inner), F32),
        ],
        compiler_params=_params("parallel", "arbitrary"),
        name="ssd_mixer",
    )(xbcz, xbcz, dt, conv0, ssm0, cw, cb, dtb, alog, dsk, gw)


def _expansion_matrix(n_heads):
    h = jnp.arange(LANES)[:, None]
    ex = h == jnp.arange(n_heads * SSM_HEAD_DIM)[None, :] // SSM_HEAD_DIM
    return jnp.tile(ex.astype(BF16), (3, 1))


def _store_act_tile(j, act, xs_ref, bb_ref, cc_ref, d_inner, bc):
    lo = j * LANES
    if lo < d_inner:
        xs_ref[:, lo:lo + LANES] = act
    elif lo < d_inner + bc:
        bb_ref[:, lo - d_inner:lo - d_inner + LANES] = act.astype(BF16)
    else:
        cc_ref[:, lo - d_inner - bc:lo - d_inner - bc + LANES] = act.astype(BF16)


def _ssd_token_math(dt_raw, dtb_ref, alog_ref, lhs3_ref, acst_ref, *, seg_len):
    q = dt_raw.shape[0]
    dt = _softplus(dt_raw + dtb_ref[...])
    a = dt * (-jnp.exp(alog_ref[...]))
    row = lax.broadcasted_iota(jnp.int32, (q, q), 0)
    col = lax.broadcasted_iota(jnp.int32, (q, q), 1)
    mask = row >= col
    if seg_len != q:
        mask = jnp.logical_and(mask, row // seg_len == col // seg_len)
        seg_end = (row // seg_len) * seg_len + (seg_len - 1)
    tri = jnp.where(mask, 1.0, 0.0).astype(BF16)
    acs = None
    for part in _split3(a):
        t = jnp.dot(tri, part, preferred_element_type=F32)
        acs = t if acs is None else acs + t
    if seg_len == q:
        last = acs[q - 1:q, :]
    else:
        sel = jnp.where(col == seg_end, 1.0, 0.0).astype(BF16)
        last = None
        for part in _split3(acs):
            t = jnp.dot(sel, part, preferred_element_type=F32)
            last = t if last is None else last + t
    acst_ref[...] = (acs * LOG2_E).T
    eacs = jnp.exp(acs)
    stack = jnp.concatenate([dt, eacs, jnp.exp(last - acs)], axis=0)
    lhs3_ref[...] = jnp.concatenate(_split3(stack), axis=1)
    return mask, eacs


def _staggered(n, stages):
    carried = {}
    for t in range(n + len(stages) - 1):
        for k, stage in enumerate(stages):
            g = t - k
            if 0 <= g < n:
                carried[g] = stage(g, carried.get(g))


def _ssd_group_inputs(g, xs_ref, bb_ref, cc_ref, lhs3_ref, ex3_ref, hpg):
    q = xs_ref.shape[0]
    gch = hpg * SSM_HEAD_DIM
    gs = slice(g * gch, (g + 1) * gch)
    bb = bb_ref[:, g * D_STATE:(g + 1) * D_STATE]
    cc = cc_ref[:, g * D_STATE:(g + 1) * D_STATE]
    cbm = lax.dot_general(cc, bb, _NT, preferred_element_type=F32)
    ex = jnp.dot(lhs3_ref[...], ex3_ref[:, gs], preferred_element_type=F32)
    xs = xs_ref[:, gs]
    return dict(bb=bb, cc=cc, cbm=cbm, xs=xs, xdt=xs * ex[0:q], eax=ex[q:2 * q], dtex=ex[2 * q:3 * q])


def _ssd_group_diag(g, v, mask, acst_ref, hpg):
    q = v["xs"].shape[0]
    p = SSM_HEAD_DIM
    lane_head = lax.broadcasted_iota(jnp.int32, (q, hpg * p), 1) // p
    ms, rhs = [], []
    for r in range(hpg):
        h = g * hpg + r
        rowb = jnp.broadcast_to(acst_ref[h:h + 1, :], (q, q))
        decay = jnp.exp2(jnp.where(mask, rowb.T - rowb, -jnp.inf))
        ms.append((v["cbm"] * decay).astype(BF16))
        rhs.append(jnp.where(lane_head == r, v["xdt"], 0.0).astype(BF16))
    y = jnp.dot(jnp.concatenate(ms, axis=1), jnp.concatenate(rhs, axis=0),
                preferred_element_type=F32)
    return dict(v, y=y, cbm=None)


def _ssd_chunk_kernel(xs_ref, bb_ref, cc_ref, sz_ref, dt_ref, ssm0_ref,
                      dtb_ref, alog_ref, dsk_ref, gw_ref, ex3_ref,
                      g_ref, ssmo_ref,
                      st_ref, lhs3_ref, acst_ref, *, n_heads):
    c = pl.program_id(1)
    q = SSD_CHUNK
    hpg = n_heads // N_GROUPS
    gch = hpg * SSM_HEAD_DIM

    @pl.when(c == 0)
    def _():
        st_ref[...] = ssm0_ref[0].T

    mask, _ = _ssd_token_math(dt_ref[0], dtb_ref, alog_ref, lhs3_ref, acst_ref, seg_len=q)

    def finish(g, v):
        gs = slice(g * gch, (g + 1) * gch)
        st = st_ref[:, gs]
        y = (v["y"] + jnp.dot(v["cc"], st.astype(BF16), preferred_element_type=F32) * v["eax"]
             + dsk_ref[:, gs] * v["xs"])
        gg = y * sz_ref[0, :, gs].astype(F32)
        msq = jnp.mean(gg * gg, axis=-1, keepdims=True)
        g_ref[0, :, gs] = (gg * lax.rsqrt(msq + EPS) * gw_ref[:, gs]).astype(BF16)
        xdtw = (v["xdt"] * v["dtex"]).astype(BF16)
        snew = lax.dot_general(v["bb"], xdtw, _TN, preferred_element_type=F32)
        st_ref[:, gs] = st * v["eax"][q - 1:q, :] + snew

    _staggered(N_GROUPS, [
        lambda g, _: _ssd_group_inputs(g, xs_ref.at[0], bb_ref.at[0], cc_ref.at[0], lhs3_ref, ex3_ref, hpg),
        lambda g, v: _ssd_group_diag(g, v, mask, acst_ref, hpg),
        finish,
    ])

    @pl.when(c == pl.num_programs(1) - 1)
    def _():
        ssmo_ref[0] = st_ref[...].T


def ssd_chunked(xs, bb, cc, sz, dt, ssm0, dtb, alog, dsk, gw):
    bsz, length, d_inner = xs.shape
    bc = bb.shape[2]
    n_heads = d_inner // SSM_HEAD_DIM
    q = SSD_CHUNK
    assert length % q == 0 and bc == N_GROUPS * D_STATE and n_heads % N_GROUPS == 0
    ex3 = _expansion_matrix(n_heads)
    rows = lambda b, c: (b, c, 0)
    const = lambda b, c: (0, 0)
    return pl.pallas_call(
        functools.partial(_ssd_chunk_kernel, n_heads=n_heads),
        grid=(bsz, length // q),
        in_specs=[
            pl.BlockSpec((1, q, d_inner), rows),
            pl.BlockSpec((1, q, bc), rows),
            pl.BlockSpec((1, q, bc), rows),
            pl.BlockSpec((1, q, d_inner), rows),
            pl.BlockSpec((1, q, LANES), rows),
            pl.BlockSpec((1, d_inner, D_STATE), lambda b, c: (0, 0, 0)),
            pl.BlockSpec((1, LANES), const),
            pl.BlockSpec((1, LANES), const),
            pl.BlockSpec((1, d_inner), const),
            pl.BlockSpec((1, d_inner), const),
            pl.BlockSpec(ex3.shape, const),
        ],
        out_specs=[
            pl.BlockSpec((1, q, d_inner), rows),
            pl.BlockSpec((1, d_inner, D_STATE), lambda b, c: (b, 0, 0)),
        ],
        out_shape=[
            jax.ShapeDtypeStruct((bsz, length, d_inner), BF16),
            jax.ShapeDtypeStruct((bsz, d_inner, D_STATE), F32),
        ],
        scratch_shapes=[
            pltpu.VMEM((D_STATE, d_inner), F32),
            pltpu.VMEM((3 * q, 3 * LANES), BF16),
            pltpu.VMEM((LANES, q), F32),
        ],
        compiler_params=_params("parallel", "arbitrary"),
        name="ssd_chunked",
    )(xs, bb, cc, sz, dt, ssm0, dtb, alog, dsk, gw, ex3)


def _ssd_step_tokens_kernel(xbc_ref, dt_ref, conv0_ref, cw_ref, cb_ref, dtb_ref, alog_ref, dsk_ref,
                            ex3_ref,
                            ypart_ref, eaxo_ref, xdtw_ref, bbo_ref, cco_ref, eacs_ref,
                            xpad_ref, xs_ref, bb_ref, cc_ref, lhs3_ref, acst_ref, *, n_heads, t):
    q = SSD_CHUNK
    p = SSM_HEAD_DIM
    d_inner = n_heads * p
    hpg = n_heads // N_GROUPS
    gch = hpg * p
    bc = N_GROUPS * D_STATE
    ntile = (d_inner + 2 * bc) // LANES
    nseq = q // t
    slot = 2 * t
    span = nseq * slot - t

    for j in range(ntile):
        cs = slice(j * LANES, (j + 1) * LANES)
        for i in range(nseq):
            xpad_ref[j, i * slot:i * slot + t, :] = conv0_ref[i, :, cs]
            xpad_ref[j, i * slot + t:(i + 1) * slot, :] = xbc_ref[i * t:(i + 1) * t, cs]
        conv = cb_ref[:, cs]
        for k in range(CONV_K):
            off = t - (CONV_K - 1) + k
            conv = conv + xpad_ref[j, off:off + span, :] * cw_ref[k:k + 1, cs]
        act = _silu(jnp.concatenate([conv[i * slot:i * slot + t] for i in range(nseq)], axis=0))
        _store_act_tile(j, act, xs_ref, bb_ref, cc_ref, d_inner, bc)
        lo = j * LANES
        if d_inner <= lo < d_inner + bc:
            bbo_ref[:, lo - d_inner:lo - d_inner + LANES] = act
        elif lo >= d_inner + bc:
            cco_ref[:, lo - d_inner - bc:lo - d_inner - bc + LANES] = act

    mask, eacs = _ssd_token_math(dt_ref[...], dtb_ref, alog_ref, lhs3_ref, acst_ref, seg_len=t)
    eacs_ref[...] = eacs
    def emit(g, v):
        gs = slice(g * gch, (g + 1) * gch)
        ypart_ref[:, gs] = v["y"] + dsk_ref[:, gs] * v["xs"]
        eaxo_ref[:, gs] = v["eax"]
        xdtw_ref[:, gs] = v["xdt"] * v["dtex"]

    _staggered(N_GROUPS, [
        lambda g, _: _ssd_group_inputs(g, xs_ref, bb_ref, cc_ref, lhs3_ref, ex3_ref, hpg),
        lambda g, v: _ssd_group_diag(g, v, mask, acst_ref, hpg),
        emit,
    ])


def ssd_step_tokens(xbcz, dt, conv0, cw, cb, dtb, alog, dsk, *, t):
    n_tok = xbcz.shape[0]
    conv_dim = cw.shape[1]
    d_inner = dsk.shape[1]
    n_heads = d_inner // SSM_HEAD_DIM
    q = SSD_CHUNK
    bc = N_GROUPS * D_STATE
    assert t == SUBLANES and n_tok % q == 0
    ex3 = _expansion_matrix(n_heads)
    const = lambda s: (0, 0)
    rows = lambda s: (s, 0)
    f32 = lambda n: jax.ShapeDtypeStruct((n_tok, n), F32)
    return pl.pallas_call(
        functools.partial(_ssd_step_tokens_kernel, n_heads=n_heads, t=t),
        grid=(n_tok // q,),
        in_specs=[
            pl.BlockSpec((q, conv_dim), rows),
            pl.BlockSpec((q, LANES), rows),
            pl.BlockSpec((q // t, SUBLANES, conv_dim), lambda s: (s, 0, 0)),
            pl.BlockSpec((CONV_K, conv_dim), const),
            pl.BlockSpec((1, conv_dim), const),
            pl.BlockSpec((1, LANES), const),
            pl.BlockSpec((1, LANES), const),
            pl.BlockSpec((1, d_inner), const),
            pl.BlockSpec(ex3.shape, const),
        ],
        out_specs=[
            pl.BlockSpec((q, d_inner), rows),
            pl.BlockSpec((q, d_inner), rows),
            pl.BlockSpec((q, d_inner), rows),
            pl.BlockSpec((q, bc), rows),
            pl.BlockSpec((q, bc), rows),
            pl.BlockSpec((q, LANES), rows),
        ],
        out_shape=[f32(d_inner), f32(d_inner), f32(d_inner), f32(bc), f32(bc), f32(LANES)],
        scratch_shapes=[
            pltpu.VMEM((conv_dim // LANES, 2 * q, LANES), F32),
            pltpu.VMEM((q, d_inner), F32),
            pltpu.VMEM((q, bc), BF16),
            pltpu.VMEM((q, bc), BF16),
            pltpu.VMEM((3 * q, 3 * LANES), BF16),
            pltpu.VMEM((LANES, q), F32),
        ],
        compiler_params=_params("parallel"),
        name="ssd_step_tokens",
    )(xbcz, dt, conv0, cw, cb, dtb, alog, dsk, ex3)


def _ssd_step_state_kernel(cd_ref, st_ref, cc_ref, bb_ref, xdtw_ref, eax_ref, ypart_ref, z_ref, gw_ref,
                           *rest, n_heads, t, nseq):
    g_ref, sto_ref = rest[-2:]
    s = pl.program_id(0)
    p = SSM_HEAD_DIM
    hpg = n_heads // N_GROUPS
    gch = hpg * p
    for g in range(N_GROUPS):
        gs = slice(g * gch, (g + 1) * gch)
        ns = slice(g * D_STATE, (g + 1) * D_STATE)
        ys = []
        for i in range(nseq):
            rows = slice(i * t, (i + 1) * t)
            h0 = st_ref[0, i, gs, :]
            yoff = lax.dot_general(cc_ref[rows, ns].astype(BF16), h0.astype(BF16), _NT,
                                   preferred_element_type=F32)
            ys.append(ypart_ref[rows, gs] + yoff * eax_ref[rows, gs])
            upd = lax.dot_general(xdtw_ref[rows, gs].astype(BF16), bb_ref[rows, ns].astype(BF16), _TN,
                                  preferred_element_type=F32)
            for r in range(hpg):
                hs = slice(r * p, (r + 1) * p)
                sto_ref[0, i, g * gch + r * p:g * gch + (r + 1) * p, :] = (
                    h0[hs] * cd_ref[s * nseq + i, g * hpg + r] + upd[hs])
        gg = jnp.concatenate(ys, axis=0) * _silu(z_ref[:, gs])
        msq = jnp.mean(gg * gg, axis=-1, keepdims=True)
        g_ref[:, gs] = (gg * lax.rsqrt(msq + EPS) * gw_ref[:, gs]).astype(BF16)


def ssd_step_state(cd, state_all, layer, cc, bb, xdtw, eax, ypart, xbcz, gw, stacked_out, *, t, seqs_per_step=4):
    n_layers, n_seq, d_inner, d_state = state_all.shape
    n_tok = ypart.shape[0]
    bc = cc.shape[1]
    conv_dim = xbcz.shape[1] - d_inner
    nseq = _row_tile(n_seq, seqs_per_step)
    rt = nseq * t
    rows = lambda s, cd_: (s, 0)
    st_map = lambda s, cd_: (layer, s, 0, 0)
    in_specs = [
        pl.BlockSpec((1, nseq, d_inner, d_state), st_map),
        pl.BlockSpec((rt, bc), rows),
        pl.BlockSpec((rt, bc), rows),
        pl.BlockSpec((rt, d_inner), rows),
        pl.BlockSpec((rt, d_inner), rows),
        pl.BlockSpec((rt, d_inner), rows),
        pl.BlockSpec((rt, d_inner), lambda s, cd_: (s, conv_dim // d_inner)),
        pl.BlockSpec((1, d_inner), lambda s, cd_: (0, 0)),
    ]
    args = [cd, state_all, cc, bb, xdtw, eax, ypart, xbcz, gw]
    aliases = {}
    if stacked_out is not None:
        in_specs.append(pl.BlockSpec(memory_space=pl.ANY))
        aliases = {len(args): 1}
        args.append(stacked_out)
    return pl.pallas_call(
        functools.partial(_ssd_step_state_kernel, n_heads=d_inner // SSM_HEAD_DIM, t=t, nseq=nseq),
        grid_spec=pltpu.PrefetchScalarGridSpec(
            num_scalar_prefetch=1,
            grid=(n_seq // nseq,),
            in_specs=in_specs,
            out_specs=[
                pl.BlockSpec((rt, d_inner), rows),
                pl.BlockSpec((1, nseq, d_inner, d_state), st_map),
            ],
        ),
        out_shape=[
            jax.ShapeDtypeStruct((n_tok, d_inner), BF16),
            jax.ShapeDtypeStruct((n_layers, n_seq, d_inner, d_state), F32),
        ],
        input_output_aliases=aliases,
        compiler_params=_params("parallel"),
        name="ssd_step_state",
    )(*args)


def _attn_prompt_kernel(sink_ref, q_ref, kvc_ref, kvp_ref, kvm_ref, o_ref, *, n_q_heads):
    n = pl.program_id(1)
    d = ATTN_HEAD_DIM
    kvw = N_KV_HEADS * d
    rep = n_q_heads // N_KV_HEADS
    scale = d ** -0.5
    w = q_ref.shape[1]
    rows = rep * w
    qi = lax.broadcasted_iota(jnp.int32, (rows, w), 0) % w
    ci = lax.broadcasted_iota(jnp.int32, (rows, w), 1)
    from_prev = ci > qi
    no_prev = jnp.where(n > 0, 0.0, -jnp.inf)
    kvm, kvp, kvc = kvm_ref[...], kvp_ref[0], kvc_ref[0]
    outs = []
    for k in range(N_KV_HEADS):
        ks, vs = slice(k * d, (k + 1) * d), slice(kvw + k * d, kvw + (k + 1) * d)
        qk = q_ref[0, :, k * rep * d:(k + 1) * rep * d]
        q4 = jnp.concatenate([qk[:, r * d:(r + 1) * d] for r in range(rep)], axis=0)
        q4 = (q4 * scale).astype(BF16)
        k2 = jnp.concatenate([kvp[:, ks], kvc[:, ks]], axis=0).astype(BF16)
        v2 = jnp.concatenate([kvp[:, vs], kvc[:, vs]], axis=0).astype(BF16)
        s2 = lax.dot_general(q4, k2, _NT, preferred_element_type=F32)
        s_w = jnp.where(from_prev, s2[:, :w] + no_prev, s2[:, w:])
        s_m = lax.dot_general(q4, kvm[:, ks].astype(BF16), _NT, preferred_element_type=F32)
        p_ws, p_ms, dens = [], [], []
        for r in range(rep):
            sl = slice(r * w, (r + 1) * w)
            sink = sink_ref[k * rep + r]
            mx = jnp.maximum(jnp.maximum(jnp.max(s_w[sl], axis=-1, keepdims=True),
                                         jnp.max(s_m[sl], axis=-1, keepdims=True)), sink)
            p_w = jnp.exp(s_w[sl] - mx)
            p_m = jnp.exp(s_m[sl] - mx)
            dens.append(jnp.exp(sink - mx) + jnp.sum(p_w, axis=-1, keepdims=True)
                        + jnp.sum(p_m, axis=-1, keepdims=True))
            p_ws.append(p_w)
            p_ms.append(p_m.astype(BF16))
        p_w = jnp.concatenate(p_ws, axis=0)
        p2 = jnp.concatenate([jnp.where(from_prev, p_w, 0.0), jnp.where(from_prev, 0.0, p_w)],
                             axis=1).astype(BF16)
        o4 = (jnp.dot(p2, v2, preferred_element_type=F32)
              + jnp.dot(jnp.concatenate(p_ms, axis=0), kvm[:, vs].astype(BF16), preferred_element_type=F32))
        outs.extend(o4[r * w:(r + 1) * w, :] / dens[r] for r in range(rep))
    o_ref[0] = jnp.concatenate(outs, axis=1).astype(BF16)


def attn_prompt(q, kv, kvm, sinks):
    bsz, s, dq = q.shape
    kvd = kv.shape[2]
    nb = s // WINDOW
    return pl.pallas_call(
        functools.partial(_attn_prompt_kernel, n_q_heads=dq // ATTN_HEAD_DIM),
        grid=(bsz, nb),
        in_specs=[
            pl.BlockSpec(memory_space=pltpu.SMEM),
            pl.BlockSpec((1, WINDOW, dq), lambda b, n: (b, n, 0)),
            pl.BlockSpec((1, WINDOW, kvd), lambda b, n: (b, n, 0)),
            pl.BlockSpec((1, WINDOW, kvd), lambda b, n: (b, jnp.maximum(n - 1, 0), 0)),
            pl.BlockSpec(kvm.shape, lambda b, n: (0, 0)),
        ],
        out_specs=pl.BlockSpec((1, WINDOW, dq), lambda b, n: (b, n, 0)),
        out_shape=jax.ShapeDtypeStruct((bsz, s, dq), BF16),
        compiler_params=_params("parallel", "arbitrary"),
        name="attn_prompt",
    )(sinks, q, kv, kv, kvm)


def _attn_sample_kernel(sink_ref, q_ref, kvn_ref, ck_ref, cv_ref, kvm_ref, o_ref, *, n_q_heads, n_meta):
    d = ATTN_HEAD_DIM
    kvw = N_KV_HEADS * d
    rep = n_q_heads // N_KV_HEADS
    scale = d ** -0.5
    nseq, t, _ = q_ref.shape
    w_buf = ck_ref.shape[1]
    rows = rep * t
    nx = n_meta + t
    tq_buf = lax.broadcasted_iota(jnp.int32, (rows, w_buf), 0) % t
    pos_buf = PAST_LEN - w_buf + lax.broadcasted_iota(jnp.int32, (rows, w_buf), 1)
    buf_mask = jnp.logical_and(PAST_LEN + tq_buf - pos_buf < WINDOW, pos_buf >= n_meta)
    tq_x = lax.broadcasted_iota(jnp.int32, (rows, nx), 0) % t
    c_x = lax.broadcasted_iota(jnp.int32, (rows, nx), 1) - n_meta
    x_mask = jnp.logical_or(c_x < 0, jnp.logical_and(c_x <= tq_x, tq_x - c_x < WINDOW))
    head_in_group = lax.broadcasted_iota(jnp.int32, (rows, 1), 0) // t
    kvm = kvm_ref[...]
    sinks = []
    for k in range(N_KV_HEADS):
        sink = jnp.zeros((rows, 1), F32)
        for r in range(rep):
            sink = jnp.where(head_in_group == r, sink_ref[k * rep + r], sink)
        sinks.append(sink)
    chains = [(i, k) for i in range(nseq) for k in range(N_KV_HEADS)]

    scores, values = {}, {}
    for i in range(nseq):
        q, kvn, ck, cv = q_ref[i], kvn_ref[i], ck_ref[i], cv_ref[i]
        for k in range(N_KV_HEADS):
            ks, vs = slice(k * d, (k + 1) * d), slice(kvw + k * d, kvw + (k + 1) * d)
            q4 = jnp.concatenate([q[:, (k * rep + r) * d:(k * rep + r + 1) * d] for r in range(rep)], axis=0)
            q4 = (q4 * scale).astype(BF16)
            kx = jnp.concatenate([kvm[:, ks], kvn[:, ks]], axis=0).astype(BF16)
            vx = jnp.concatenate([kvm[:, vs], kvn[:, vs]], axis=0).astype(BF16)
            s_b = lax.dot_general(q4, ck[:, ks].astype(BF16), _NT, preferred_element_type=F32)
            s_x = lax.dot_general(q4, kx, _NT, preferred_element_type=F32)
            scores[i, k] = (jnp.where(buf_mask, s_b, -jnp.inf), jnp.where(x_mask, s_x, -jnp.inf))
            values[i, k] = (cv[:, ks].astype(BF16), vx)

    maxes = {}
    for c in chains:
        s_b, s_x = scores[c]
        maxes[c] = jnp.maximum(sinks[c[1]], jnp.maximum(jnp.max(s_b, axis=-1, keepdims=True),
                                                         jnp.max(s_x, axis=-1, keepdims=True)))
    probs, dens = {}, {}
    for c in chains:
        s_b, s_x = scores[c]
        p_b, p_x = jnp.exp(s_b - maxes[c]), jnp.exp(s_x - maxes[c])
        probs[c] = (p_b.astype(BF16), p_x.astype(BF16))
        dens[c] = (jnp.exp(sinks[c[1]] - maxes[c]) + jnp.sum(p_b, axis=-1, keepdims=True)
                   + jnp.sum(p_x, axis=-1, keepdims=True))
    outs = {}
    for c in chains:
        outs[c] = (jnp.dot(probs[c][0], values[c][0], preferred_element_type=F32)
                   + jnp.dot(probs[c][1], values[c][1], preferred_element_type=F32)) / dens[c]
    for i in range(nseq):
        heads = [outs[i, k][r * t:(r + 1) * t, :] for k in range(N_KV_HEADS) for r in range(rep)]
        o_ref[i] = jnp.concatenate(heads, axis=1).astype(BF16)


def attn_sample(q, kvn, ck, cv, kvm, sinks, *, seqs_per_step=8):
    bsz, t, dq = q.shape
    kvd = kvn.shape[2]
    w_buf = ck.shape[1]
    g = _row_tile(bsz, seqs_per_step)
    return pl.pallas_call(
        functools.partial(_attn_sample_kernel, n_q_heads=dq // ATTN_HEAD_DIM, n_meta=kvm.shape[0]),
        grid=(bsz // g,),
        in_specs=[
            pl.BlockSpec(memory_space=pltpu.SMEM),
            pl.BlockSpec((g, t, dq), lambda b: (b, 0, 0)),
            pl.BlockSpec((g, t, kvd), lambda b: (b, 0, 0)),
            pl.BlockSpec((g, w_buf, kvd // 2), lambda b: (b, 0, 0)),
            pl.BlockSpec((g, w_buf, kvd // 2), lambda b: (b, 0, 0)),
            pl.BlockSpec(kvm.shape, lambda b: (0, 0)),
        ],
        out_specs=pl.BlockSpec((g, t, dq), lambda b: (b, 0, 0)),
        out_shape=jax.ShapeDtypeStruct((bsz, t, dq), BF16),
        compiler_params=_params("parallel"),
        name="attn_sample",
    )(sinks, q, kvn, ck, cv, kvm)


def kernel(x_prompt, x_sample, state_conv, state_ssm, cache_k_win, cache_v_win, meta_tokens, a_norm_w, a_in_proj, a_conv_w, a_conv_b, a_dt_bias, a_log, a_d_skip, a_gate_norm_w, a_out_proj, kv_norm_w, w_kv, b_norm_w, w_q, attn_sinks, w_o, mlp_norm_w, w_up, w_down, final_norm_w):
    n_prompt, seq, d_model = x_prompt.shape
    n_dec, dec_seq, _ = x_sample.shape
    n_a = a_in_proj.shape[0]
    depth = w_up.shape[0]
    n_meta = meta_tokens.shape[0]
    d_inner = a_out_proj.shape[1]
    conv_dim = a_conv_w.shape[2]
    n_heads = a_log.shape[1]
    w_buf = cache_k_win.shape[1]
    kvw = N_KV_HEADS * ATTN_HEAD_DIM
    assert n_heads * SSM_HEAD_DIM == d_inner and n_heads <= LANES

    hm = meta_tokens.astype(F32)
    hp = x_prompt.reshape(n_prompt * seq, d_model)
    hs = x_sample.reshape(n_dec * dec_seq, d_model)

    def pad_lanes(v):
        return jnp.pad(v, (0, LANES - v.shape[0])).reshape(1, LANES)

    def pad_conv_state(s):
        return jnp.pad(s, ((0, 0), (SUBLANES - (CONV_K - 1), 0), (0, 0)))

    conv_p_list, ssm_p_list, conv_s_list, ssm_s_list = [], [], [], []
    step_path = dec_seq == SUBLANES and (n_dec * dec_seq) % SSD_CHUNK == 0
    ssm_in_all = state_ssm.reshape(n_a, n_dec, d_inner, D_STATE)
    ssm_s_all = None
    kvm = kv_p = kv_s = None
    for layer in range(depth):
        wu = w_up[layer].astype(BF16)
        wd = w_down[layer].astype(BF16)
        last = layer == depth - 1
        if layer < n_a:
            i = layer
            w_in = a_in_proj[i]
            w_main = jnp.concatenate([w_in[:, d_inner:d_inner + conv_dim], w_in[:, :d_inner]], axis=1).astype(BF16)
            w_dt = jnp.pad(w_in[:, d_inner + conv_dim:], ((0, 0), (0, LANES - n_heads))).astype(BF16)
            w_out = a_out_proj[i].astype(BF16)
            prm = (a_conv_w[i], a_conv_b[i].reshape(1, conv_dim), pad_lanes(a_dt_bias[i]), pad_lanes(a_log[i]),
                   jnp.repeat(a_d_skip[i], SSM_HEAD_DIM).reshape(1, d_inner), a_gate_norm_w[i].reshape(1, d_inner))

            def mixer(h, bsz, length, conv0, ssm0, shared):
                if shared and length % SSD_CHUNK == 0:
                    xs_a, bb_a, cc_a, sz_a, dt, conv_o = in_proj_conv(
                        h, a_norm_w[i], w_main, w_dt, prm[0], prm[1], conv0, seq_len=length, d_inner=d_inner)
                    split = lambda t: t.reshape(bsz, length, t.shape[1])
                    g, ssm_o = ssd_chunked(split(xs_a), split(bb_a), split(cc_a), split(sz_a), split(dt),
                                           ssm0, *prm[2:])
                else:
                    xbcz, dt = in_proj(h, a_norm_w[i], w_main, w_dt)
                    g, conv_o, ssm_o = ssd_mixer(xbcz.reshape(bsz, length, conv_dim + d_inner),
                                                 dt.reshape(bsz, length, LANES), conv0, ssm0, *prm,
                                                 shared_state=shared)
                return g.reshape(bsz * length, d_inner), conv_o, ssm_o

            zero_conv = jnp.zeros((1, SUBLANES, conv_dim), F32)
            zero_ssm = jnp.zeros((1, d_inner, D_STATE), F32)
            g_m, conv_m, ssm_m = mixer(hm, 1, n_meta, zero_conv, zero_ssm, True)
            g_p, conv_p, ssm_p = mixer(hp, n_prompt, seq, conv_m, ssm_m, True)
            conv0_s = pad_conv_state(state_conv[i])
            if step_path:
                xbcz_s, dt_s = in_proj(hs, a_norm_w[i], w_main, w_dt)
                ypart, eax, xdtw, bb_s, cc_s, eacs = ssd_step_tokens(xbcz_s, dt_s, conv0_s, *prm[:5], t=dec_seq)
                cdecay = eacs.reshape(n_dec, dec_seq, LANES)[:, dec_seq - 1]
                g_s, ssm_s_all = ssd_step_state(cdecay, ssm_in_all, i, cc_s, bb_s, xdtw, eax, ypart, xbcz_s,
                                                prm[5], ssm_s_all, t=dec_seq)
                conv_s = xbcz_s[:, :conv_dim].reshape(n_dec, dec_seq, conv_dim)[:, dec_seq - (CONV_K - 1):]
            else:
                g_s, conv_s, ssm_s = mixer(hs, n_dec, dec_seq, conv0_s, ssm_in_all[i], False)
                conv_s = conv_s[:, SUBLANES - (CONV_K - 1):]
                ssm_s_list.append(ssm_s)
            conv_p_list.append(conv_p[:, SUBLANES - (CONV_K - 1):])
            ssm_p_list.append(ssm_p.reshape(n_prompt, n_heads, SSM_HEAD_DIM, D_STATE))
            conv_s_list.append(conv_s)
            hm = mlp(hm, mlp_norm_w[layer], wu, wd, final_norm_w, final_norm=False, proj=(g_m, w_out))
            proj_p, proj_s = (g_p, w_out), (g_s, w_out)
        else:
            j = layer - n_a
            if j == 0:
                wkv = w_kv.astype(BF16)
                kvm = norm_matmul(hm, kv_norm_w, wkv)
                kv_p = norm_matmul(hp, kv_norm_w, wkv).reshape(n_prompt, seq, 2 * kvw)
                kv_s = norm_matmul(hs, kv_norm_w, wkv).reshape(n_dec, dec_seq, 2 * kvw)
            wq = w_q[j].astype(BF16)
            wo = w_o[j].astype(BF16)
            dq = wq.shape[1]
            q_p = norm_matmul(hp, b_norm_w[j], wq).reshape(n_prompt, seq, dq)
            q_s = norm_matmul(hs, b_norm_w[j], wq).reshape(n_dec, dec_seq, dq)
            o_p = attn_prompt(q_p, kv_p, kvm, attn_sinks[j])
            o_s = attn_sample(q_s, kv_s, cache_k_win.reshape(n_dec, w_buf, kvw),
                              cache_v_win.reshape(n_dec, w_buf, kvw), kvm, attn_sinks[j])
            proj_p = (o_p.reshape(n_prompt * seq, dq), wo)
            proj_s = (o_s.reshape(n_dec * dec_seq, dq), wo)
        hp = mlp(hp, mlp_norm_w[layer], wu, wd, final_norm_w, final_norm=last, proj=proj_p)
        hs = mlp(hs, mlp_norm_w[layer], wu, wd, final_norm_w, final_norm=last, proj=proj_s)

    y_prompt = hp.reshape(n_prompt, seq, d_model)
    y_sample = hs.reshape(n_dec, dec_seq, d_model)
    kv_heads = (N_KV_HEADS, ATTN_HEAD_DIM)
    k_p = kv_p[:, seq - w_buf:, :kvw].reshape((n_prompt, w_buf) + kv_heads)
    v_p = kv_p[:, seq - w_buf:, kvw:].reshape((n_prompt, w_buf) + kv_heads)
    k_s = kv_s[:, :, :kvw].reshape((n_dec, dec_seq) + kv_heads)
    v_s = kv_s[:, :, kvw:].reshape((n_dec, dec_seq) + kv_heads)
    k_s_win = jnp.concatenate([cache_k_win, k_s], axis=1)[:, -w_buf:]
    v_s_win = jnp.concatenate([cache_v_win, v_s], axis=1)[:, -w_buf:]
    if not step_path:
        ssm_s_all = jnp.stack(ssm_s_list)
    return (y_prompt, y_sample, jnp.stack(conv_p_list), jnp.stack(ssm_p_list), k_p, v_p,
            jnp.stack(conv_s_list), ssm_s_all.reshape(state_ssm.shape), k_s_win, v_s_win)
```

```python
import functools

import jax
import jax.numpy as jnp
from jax import lax
from jax.experimental import pallas as pl
from jax.experimental.pallas import tpu as pltpu

F32 = jnp.float32
BF16 = jnp.bfloat16

N_GROUPS = 8
SSM_HEAD_DIM = 64
D_STATE = 128
CONV_K = 4
SSD_CHUNK = 128
ATTN_HEAD_DIM = 64
N_KV_HEADS = 4
WINDOW = 128
PAST_LEN = 8192
EPS = 1e-5
LOG2_E = 1.4426950408889634

LANES = 128
SUBLANES = 8
VMEM_LIMIT_BYTES = 52 * 1024 * 1024

_NT = (((1,), (1,)), ((), ()))
_TN = (((0,), (0,)), ((), ()))


def _params(*sem):
    return pltpu.CompilerParams(dimension_semantics=sem, vmem_limit_bytes=VMEM_LIMIT_BYTES)


def _rms(x, w):
    ms = jnp.mean(x * x, axis=-1, keepdims=True)
    return x * lax.rsqrt(ms + EPS) * w


def _silu(x):
    s = 0.5 * x
    return s + s * jnp.tanh(s)


def _row_tile(m, cap):
    t = min(m, cap)
    assert m % t == 0, (m, t)
    return t


def _norm_matmul_kernel(x_ref, nw_ref, w_ref, o_ref, xn_ref):
    @pl.when(pl.program_id(1) == 0)
    def _():
        xn_ref[...] = _rms(x_ref[...], nw_ref[...]).astype(BF16)

    o_ref[...] = jnp.dot(xn_ref[...], w_ref[...], preferred_element_type=F32)


def norm_matmul(x, nw, w, *, tm_cap=1024, tn_cap=1024):
    m, d = x.shape
    n = w.shape[1]
    tm, tn = _row_tile(m, tm_cap), _row_tile(n, tn_cap)
    return pl.pallas_call(
        _norm_matmul_kernel,
        grid=(m // tm, n // tn),
        in_specs=[
            pl.BlockSpec((tm, d), lambda i, j: (i, 0)),
            pl.BlockSpec((1, d), lambda i, j: (0, 0)),
            pl.BlockSpec((d, tn), lambda i, j: (0, j)),
        ],
        out_specs=pl.BlockSpec((tm, tn), lambda i, j: (i, j)),
        out_shape=jax.ShapeDtypeStruct((m, n), F32),
        scratch_shapes=[pltpu.VMEM((tm, d), BF16)],
        compiler_params=_params("parallel", "arbitrary"),
        name="norm_matmul",
    )(x, nw.reshape(1, d), w)


def _in_proj_kernel(x_ref, nw_ref, w_ref, wdt_ref, o_ref, dt_ref, xn_ref):
    @pl.when(pl.program_id(1) == 0)
    def _():
        xn = _rms(x_ref[...], nw_ref[...]).astype(BF16)
        xn_ref[...] = xn
        dt_ref[...] = jnp.dot(xn, wdt_ref[...], preferred_element_type=F32)

    o_ref[...] = jnp.dot(xn_ref[...], w_ref[...], preferred_element_type=F32)


def in_proj(x, nw, w, wdt, *, d_inner, n_main, tm_cap=1024, tn_cap=1024):
    m, d = x.shape
    n = n_main
    tm, tn = _row_tile(m, tm_cap), _row_tile(n, tn_cap)
    assert d_inner % tn == 0
    return pl.pallas_call(
        _in_proj_kernel,
        grid=(m // tm, n // tn),
        in_specs=[
            pl.BlockSpec((tm, d), lambda i, j: (i, 0)),
            pl.BlockSpec((1, d), lambda i, j: (0, 0)),
            pl.BlockSpec((d, tn), lambda i, j: (0, (j + d_inner // tn) % (n // tn))),
            pl.BlockSpec((d, LANES), lambda i, j: (0, 0)),
        ],
        out_specs=[
            pl.BlockSpec((tm, tn), lambda i, j: (i, j)),
            pl.BlockSpec((tm, LANES), lambda i, j: (i, 0)),
        ],
        out_shape=[jax.ShapeDtypeStruct((m, n), F32), jax.ShapeDtypeStruct((m, LANES), F32)],
        scratch_shapes=[pltpu.VMEM((tm, d), BF16)],
        compiler_params=_params("parallel", "arbitrary"),
        name="in_proj",
    )(x, nw.reshape(1, d), w, wdt)


def _in_proj_conv_kernel(x_ref, nw_ref, w_ref, wdt_ref, cw_ref, cb_ref, conv0_ref,
                         xs_ref, bb_ref, cc_ref, sz_ref, dt_ref, tail_ref,
                         xn_ref, xpad_ref, halo_ref, *, tiles_per_seq, n_x, n_b):
    i = pl.program_id(0)
    j = pl.program_id(1)
    tm = x_ref.shape[0]
    tn = w_ref.shape[1]
    pad = SUBLANES
    sub = 2 * LANES

    @pl.when(j == 0)
    def _():
        xn = _rms(x_ref[...], nw_ref[...]).astype(BF16)
        xn_ref[...] = xn
        dt_ref[...] = jnp.dot(xn, wdt_ref[...], preferred_element_type=F32)

    def conv_tile(out_ref):
        @pl.when(i % tiles_per_seq == 0)
        def _():
            for l in range(tn // LANES):
                halo_ref[j, l] = conv0_ref[0, :, l * LANES:(l + 1) * LANES]

        def conv_act(s, raw):
            for e in range(sub // LANES):
                l = s * (sub // LANES) + e
                cs = slice(l * LANES, (l + 1) * LANES)
                xpad_ref[l, 0:pad, :] = halo_ref[j, l]
                xpad_ref[l, pad:pad + tm, :] = raw[:, e * LANES:(e + 1) * LANES]
                conv = cb_ref[:, cs]
                for k in range(CONV_K):
                    off = pad - (CONV_K - 1) + k
                    conv = conv + xpad_ref[l, off:off + tm, :] * cw_ref[k:k + 1, cs]
                out_ref[:, cs] = _silu(conv).astype(out_ref.dtype)
                last_rows = xpad_ref[l, tm:tm + pad, :]
                halo_ref[j, l] = last_rows
                tail_ref[0, :, cs] = last_rows

        _staggered(tn // sub, [matmul_cols, conv_act])

    def matmul_cols(s, _):
        return jnp.dot(xn_ref[...], w_ref[:, s * sub:(s + 1) * sub], preferred_element_type=F32)

    def gate_act(s, raw):
        sz_ref[:, s * sub:(s + 1) * sub] = _silu(raw).astype(BF16)

    pl.when(j < n_x)(lambda: conv_tile(xs_ref))
    pl.when(jnp.logical_and(j >= n_x, j < n_x + n_b))(lambda: conv_tile(bb_ref))
    pl.when(jnp.logical_and(j >= n_x + n_b, j < n_x + 2 * n_b))(lambda: conv_tile(cc_ref))
    pl.when(j >= n_x + 2 * n_b)(lambda: _staggered(tn // sub, [matmul_cols, gate_act]))


def in_proj_conv(x, nw, w, wdt, cw, cb, conv0, *, seq_len, d_inner, tm_cap=1024):
    m, d = x.shape
    conv_dim = cw.shape[1]
    bc = (conv_dim - d_inner) // 2
    tn = bc
    tm = _row_tile(seq_len, tm_cap)
    assert d_inner % tn == 0 and w.shape[1] >= conv_dim + d_inner and m % seq_len == 0
    n_x, n_b, n_z = d_inner // tn, 1, d_inner // tn
    n_conv = n_x + 2 * n_b
    conv_col = lambda i, j: (0, jnp.minimum(j, n_conv - 1))
    bf = lambda n: jax.ShapeDtypeStruct((m, n), BF16)
    tiles_per_seq = seq_len // tm
    *acts, tails = pl.pallas_call(
        functools.partial(_in_proj_conv_kernel, tiles_per_seq=tiles_per_seq, n_x=n_x, n_b=n_b),
        grid=(m // tm, n_conv + n_z),
        in_specs=[
            pl.BlockSpec((tm, d), lambda i, j: (i, 0)),
            pl.BlockSpec((1, d), lambda i, j: (0, 0)),
            pl.BlockSpec((d, tn), lambda i, j: (0, (j + n_z) % (n_conv + n_z))),
            pl.BlockSpec((d, LANES), lambda i, j: (0, 0)),
            pl.BlockSpec((CONV_K, tn), conv_col),
            pl.BlockSpec((1, tn), conv_col),
            pl.BlockSpec((1, SUBLANES, tn), lambda i, j: (0, 0, jnp.minimum(j, n_conv - 1))),
        ],
        out_specs=[
            pl.BlockSpec((tm, tn), lambda i, j: (i, jnp.minimum(j, n_x - 1))),
            pl.BlockSpec((tm, tn), lambda i, j: (i, 0)),
            pl.BlockSpec((tm, tn), lambda i, j: (i, 0)),
            pl.BlockSpec((tm, tn), lambda i, j: (i, jnp.clip(j - n_conv, 0, n_z - 1))),
            pl.BlockSpec((tm, LANES), lambda i, j: (i, 0)),
            pl.BlockSpec((1, SUBLANES, tn), lambda i, j: (i, 0, jnp.minimum(j, n_conv - 1))),
        ],
        out_shape=[jax.ShapeDtypeStruct((m, d_inner), F32), bf(bc), bf(bc), bf(d_inner),
                   jax.ShapeDtypeStruct((m, LANES), F32),
                   jax.ShapeDtypeStruct((m // tm, SUBLANES, conv_dim), F32)],
        scratch_shapes=[
            pltpu.VMEM((tm, d), BF16),
            pltpu.VMEM((tn // LANES, tm + SUBLANES, LANES), F32),
            pltpu.VMEM((n_conv, tn // LANES, SUBLANES, LANES), F32),
        ],
        compiler_params=_params("arbitrary", "arbitrary"),
        name="in_proj_conv",
    )(x, nw.reshape(1, d), w, wdt, cw, cb, conv0)
    return (*acts, tails[tiles_per_seq - 1::tiles_per_seq])


def _mlp_steps(x_fn, nw_ref, wu_ref, wd_ref, fw_ref, o_ref, xn_ref, final_norm):
    f = pl.program_id(1)

    @pl.when(f == 0)
    def _():
        x = x_fn()
        xn_ref[...] = _rms(x, nw_ref[...]).astype(BF16)
        o_ref[...] = x

    h = jnp.dot(xn_ref[...], wu_ref[...], preferred_element_type=F32)
    h = jnp.square(jnp.maximum(h, 0.0)).astype(BF16)
    o_ref[...] += jnp.dot(h, wd_ref[...], preferred_element_type=F32)

    if final_norm:
        @pl.when(f == pl.num_programs(1) - 1)
        def _():
            o_ref[...] = _rms(o_ref[...], fw_ref[...])


def _mlp_kernel(x_ref, nw_ref, wu_ref, wd_ref, fw_ref, o_ref, xn_ref, *, final_norm):
    _mlp_steps(lambda: x_ref[...], nw_ref, wu_ref, wd_ref, fw_ref, o_ref, xn_ref, final_norm)


def _proj_mlp_kernel(a_ref, wa_ref, x_ref, nw_ref, wu_ref, wd_ref, fw_ref, o_ref, xn_ref, *, final_norm):
    def block_input():
        return x_ref[...] + jnp.dot(a_ref[...], wa_ref[...], preferred_element_type=F32)

    _mlp_steps(block_input, nw_ref, wu_ref, wd_ref, fw_ref, o_ref, xn_ref, final_norm)


def mlp(x, nw, wu, wd, fw, *, final_norm, proj=None, tm_cap=1024, tf_cap=1024):
    m, d = x.shape
    dff = wu.shape[1]
    tm, tf = _row_tile(m, tm_cap), _row_tile(dff, tf_cap)
    in_specs = [
        pl.BlockSpec((tm, d), lambda i, f: (i, 0)),
        pl.BlockSpec((1, d), lambda i, f: (0, 0)),
        pl.BlockSpec((d, tf), lambda i, f: (0, f)),
        pl.BlockSpec((tf, d), lambda i, f: (f, 0)),
        pl.BlockSpec((1, d), lambda i, f: (0, 0)),
    ]
    args = [x, nw.reshape(1, d), wu, wd, fw.reshape(1, d)]
    body = _mlp_kernel
    if proj is not None:
        a, wa = proj
        k = a.shape[1]
        in_specs = [pl.BlockSpec((tm, k), lambda i, f: (i, 0)),
                    pl.BlockSpec((k, d), lambda i, f: (0, 0), pipeline_mode=pl.Buffered(1))] + in_specs
        args = [a, wa] + args
        body = _proj_mlp_kernel
    return pl.pallas_call(
        functools.partial(body, final_norm=final_norm),
        grid=(m // tm, dff // tf),
        in_specs=in_specs,
        out_specs=pl.BlockSpec((tm, d), lambda i, f: (i, 0)),
        out_shape=jax.ShapeDtypeStruct((m, d), F32),
        scratch_shapes=[pltpu.VMEM((tm, d), BF16)],
        compiler_params=_params("parallel", "arbitrary"),
        name="mlp" if proj is None else "proj_mlp",
    )(*args)


def _softplus(x):
    return jnp.maximum(x, 0.0) + jnp.log(1.0 + jnp.exp(-jnp.abs(x)))


def _split3(a):
    hi = a.astype(BF16)
    r1 = a - hi.astype(F32)
    mid = r1.astype(BF16)
    lo = (r1 - mid.astype(F32)).astype(BF16)
    return hi, mid, lo


def _ssd_kernel(xbc_ref, z_ref, dt_ref, conv0_ref, ssm0_ref,
                cw_ref, cb_ref, dtb_ref, alog_ref, dsk_ref, gw_ref,
                g_ref, convo_ref, ssmo_ref,
                xpad_ref, act_ref, state_ref, y_ref, *, q, n_heads):
    c = pl.program_id(1)
    p = SSM_HEAD_DIM
    d_inner = n_heads * p
    hpg = n_heads // N_GROUPS
    conv_dim = d_inner + 2 * N_GROUPS * D_STATE
    pad = SUBLANES

    @pl.when(c == 0)
    def _():
        xpad_ref[0:pad, :] = conv0_ref[0]
        state_ref[...] = ssm0_ref[0]

    xpad_ref[pad:pad + q, :] = xbc_ref[0]
    cblk = 512
    for j in range(conv_dim // cblk):
        cs = slice(j * cblk, (j + 1) * cblk)
        conv = cb_ref[:, cs]
        for k in range(CONV_K):
            off = pad - (CONV_K - 1) + k
            conv = conv + xpad_ref[off:off + q, cs] * cw_ref[k:k + 1, cs]
        act_ref[:, cs] = _silu(conv)
    xpad_ref[0:pad, :] = xpad_ref[q:q + pad, :]

    dt = _softplus(dt_ref[0] + dtb_ref[...])
    a = dt * (-jnp.exp(alog_ref[...]))
    row = lax.broadcasted_iota(jnp.int32, (q, q), 0)
    col = lax.broadcasted_iota(jnp.int32, (q, q), 1)
    causal = row >= col
    tri = jnp.where(causal, 1.0, 0.0).astype(BF16)
    acs = None
    for part in _split3(a):
        t = jnp.dot(tri, part, preferred_element_type=F32)
        acs = t if acs is None else acs + t
    if q < LANES:
        acs_sq = jnp.concatenate([acs, jnp.zeros((LANES - q, LANES), F32)], axis=0)
    else:
        acs_sq = acs
    acs_t = acs_sq.T
    eacs = jnp.exp(acs)
    last = acs[q - 1:q, :]
    dte = jnp.exp(last - acs)
    cdecay = jnp.exp(last)

    for g in range(N_GROUPS):
        b0 = d_inner + g * D_STATE
        c0 = d_inner + N_GROUPS * D_STATE + g * D_STATE
        bb = act_ref[:, b0:b0 + D_STATE].astype(BF16)
        cc = act_ref[:, c0:c0 + D_STATE].astype(BF16)
        cbm = lax.dot_general(cc, bb, _NT, preferred_element_type=F32)
        for pair in range(hpg // 2):
            xs2 = act_ref[:, (g * hpg + 2 * pair) * p:(g * hpg + 2 * pair + 2) * p]
            dsk2 = dsk_ref[:, (g * hpg + 2 * pair) * p:(g * hpg + 2 * pair + 2) * p]
            ys = []
            for e in range(2):
                h = g * hpg + 2 * pair + e
                xs = xs2[:, e * p:(e + 1) * p]
                seg = acs[:, h:h + 1] - acs_t[h:h + 1, 0:q]
                decay = jnp.exp(jnp.where(causal, seg, -jnp.inf))
                m = (cbm * decay).astype(BF16)
                xdt = xs * dt[:, h:h + 1]
                y = jnp.dot(m, xdt.astype(BF16), preferred_element_type=F32)
                st = state_ref[h * p:(h + 1) * p, :]
                yoff = lax.dot_general(cc, st.astype(BF16), _NT, preferred_element_type=F32)
                y = y + yoff * eacs[:, h:h + 1] + dsk2[:, e * p:(e + 1) * p] * xs
                ys.append(y)
                xdtw = (xdt * dte[:, h:h + 1]).astype(BF16)
                snew = lax.dot_general(xdtw, bb, _TN, preferred_element_type=F32)
                state_ref[h * p:(h + 1) * p, :] = (
                    jnp.broadcast_to(cdecay[:, h:h + 1], (p, D_STATE)) * st + snew)
            y_ref[:, (g * hpg + 2 * pair) * p:(g * hpg + 2 * pair + 2) * p] = (
                jnp.concatenate(ys, axis=1))

    gsz = d_inner // N_GROUPS
    for g in range(N_GROUPS):
        gs = slice(g * gsz, (g + 1) * gsz)
        gg = y_ref[:, gs] * _silu(z_ref[0, :, gs])
        ms = jnp.mean(gg * gg, axis=-1, keepdims=True)
        g_ref[0, :, gs] = (gg * lax.rsqrt(ms + EPS) * gw_ref[:, gs]).astype(BF16)

    @pl.when(c == pl.num_programs(1) - 1)
    def _():
        convo_ref[0] = xpad_ref[0:pad, :]
        ssmo_ref[0] = state_ref[...]


def ssd_mixer(xbcz, dt, conv0, ssm0, cw, cb, dtb, alog, dsk, gw, *, shared_state):
    bsz, length, _ = xbcz.shape
    conv_dim = cw.shape[1]
    d_inner = gw.shape[1]
    n_heads = d_inner // SSM_HEAD_DIM
    q = SSD_CHUNK if length % SSD_CHUNK == 0 else length
    assert q % SUBLANES == 0 and q >= SUBLANES and conv_dim % d_inner == 0
    nc = length // q
    zblk = conv_dim // d_inner
    if shared_state:
        st_map = lambda b, c: (0, 0, 0)
    else:
        st_map = lambda b, c: (b, 0, 0)
    const = lambda b, c: (0, 0)
    return pl.pallas_call(
        functools.partial(_ssd_kernel, q=q, n_heads=n_heads),
        grid=(bsz, nc),
        in_specs=[
            pl.BlockSpec((1, q, conv_dim), lambda b, c: (b, c, 0)),
            pl.BlockSpec((1, q, d_inner), lambda b, c: (b, c, zblk)),
            pl.BlockSpec((1, q, LANES), lambda b, c: (b, c, 0)),
            pl.BlockSpec((1, SUBLANES, conv_dim), st_map),
            pl.BlockSpec((1, n_heads * SSM_HEAD_DIM, D_STATE), st_map),
            pl.BlockSpec((CONV_K, conv_dim), const),
            pl.BlockSpec((1, conv_dim), const),
            pl.BlockSpec((1, LANES), const),
            pl.BlockSpec((1, LANES), const),
            pl.BlockSpec((1, d_inner), const),
            pl.BlockSpec((1, d_inner), const),
        ],
        out_specs=[
            pl.BlockSpec((1, q, d_inner), lambda b, c: (b, c, 0)),
            pl.BlockSpec((1, SUBLANES, conv_dim), lambda b, c: (b, 0, 0)),
            pl.BlockSpec((1, n_heads * SSM_HEAD_DIM, D_STATE), lambda b, c: (b, 0, 0)),
        ],
        out_shape=[
            jax.ShapeDtypeStruct((bsz, length, d_inner), BF16),
            jax.ShapeDtypeStruct((bsz, SUBLANES, conv_dim), F32),
            jax.ShapeDtypeStruct((bsz, n_heads * SSM_HEAD_DIM, D_STATE), F32),
        ],
        scratch_shapes=[
            pltpu.VMEM((q + SUBLANES, conv_dim), F32),
            pltpu.VMEM((q, conv_dim), F32),
            pltpu.VMEM((n_heads * SSM_HEAD_DIM, D_STATE), F32),
            pltpu.VMEM((q, d_inner), F32),
        ],
        compiler_params=_params("parallel", "arbitrary"),
        name="ssd_mixer",
    )(xbcz, xbcz, dt, conv0, ssm0, cw, cb, dtb, alog, dsk, gw)


def _expansion_matrix(n_heads):
    h = jnp.arange(LANES)[:, None]
    ex = h == jnp.arange(n_heads * SSM_HEAD_DIM)[None, :] // SSM_HEAD_DIM
    return jnp.tile(ex.astype(BF16), (3, 1))


def _store_act_tile(j, act, xs_ref, bb_ref, cc_ref, d_inner, bc):
    lo = j * LANES
    if lo < d_inner:
        xs_ref[:, lo:lo + LANES] = act
    elif lo < d_inner + bc:
        bb_ref[:, lo - d_inner:lo - d_inner + LANES] = act.astype(BF16)
    else:
        cc_ref[:, lo - d_inner - bc:lo - d_inner - bc + LANES] = act.astype(BF16)


def _ssd_token_math(dt_raw, dtb_ref, alog_ref, lhs3_ref, acst_ref, *, seg_len):
    q = dt_raw.shape[0]
    dt = _softplus(dt_raw + dtb_ref[...])
    a = dt * (-jnp.exp(alog_ref[...]))
    row = lax.broadcasted_iota(jnp.int32, (q, q), 0)
    col = lax.broadcasted_iota(jnp.int32, (q, q), 1)
    mask = row >= col
    if seg_len != q:
        mask = jnp.logical_and(mask, row // seg_len == col // seg_len)
        seg_end = (row // seg_len) * seg_len + (seg_len - 1)
    tri = jnp.where(mask, 1.0, 0.0).astype(BF16)
    acs = None
    for part in _split3(a):
        t = jnp.dot(tri, part, preferred_element_type=F32)
        acs = t if acs is None else acs + t
    if seg_len == q:
        last = acs[q - 1:q, :]
    else:
        sel = jnp.where(col == seg_end, 1.0, 0.0).astype(BF16)
        last = None
        for part in _split3(acs):
            t = jnp.dot(sel, part, preferred_element_type=F32)
            last = t if last is None else last + t
    acst_ref[...] = (acs * LOG2_E).T
    eacs = jnp.exp(acs)
    stack = jnp.concatenate([dt, eacs, jnp.exp(last - acs)], axis=0)
    lhs3_ref[...] = jnp.concatenate(_split3(stack), axis=1)
    return mask, eacs


def _staggered(n, stages):
    carried = {}
    for t in range(n + len(stages) - 1):
        for k, stage in enumerate(stages):
            g = t - k
            if 0 <= g < n:
                carried[g] = stage(g, carried.get(g))


def _ssd_group_inputs(g, xs_ref, bb_ref, cc_ref, lhs3_ref, ex3_ref, hpg):
    q = xs_ref.shape[0]
    gch = hpg * SSM_HEAD_DIM
    gs = slice(g * gch, (g + 1) * gch)
    bb = bb_ref[:, g * D_STATE:(g + 1) * D_STATE]
    cc = cc_ref[:, g * D_STATE:(g + 1) * D_STATE]
    cbm = lax.dot_general(cc, bb, _NT, preferred_element_type=F32)
    ex = jnp.dot(lhs3_ref[...], ex3_ref[:, gs], preferred_element_type=F32)
    xs = xs_ref[:, gs]
    return dict(bb=bb, cc=cc, cbm=cbm, xs=xs, xdt=xs * ex[0:q], eax=ex[q:2 * q], dtex=ex[2 * q:3 * q])


def _ssd_group_diag(g, v, mask, acst_ref, hpg):
    q = v["xs"].shape[0]
    p = SSM_HEAD_DIM
    lane_head = lax.broadcasted_iota(jnp.int32, (q, hpg * p), 1) // p
    ms, rhs = [], []
    for r in range(hpg):
        h = g * hpg + r
        rowb = jnp.broadcast_to(acst_ref[h:h + 1, :], (q, q))
        decay = jnp.exp2(jnp.where(mask, rowb.T - rowb, -jnp.inf))
        ms.append((v["cbm"] * decay).astype(BF16))
        rhs.append(jnp.where(lane_head == r, v["xdt"], 0.0).astype(BF16))
    y = jnp.dot(jnp.concatenate(ms, axis=1), jnp.concatenate(rhs, axis=0),
                preferred_element_type=F32)
    return dict(v, y=y, cbm=None)


def _ssd_chunk_kernel(xs_ref, bb_ref, cc_ref, sz_ref, dt_ref, ssm0_ref,
                      dtb_ref, alog_ref, dsk_ref, gw_ref, ex3_ref,
                      g_ref, ssmo_ref,
                      st_ref, lhs3_ref, acst_ref, *, n_heads):
    c = pl.program_id(1)
    q = SSD_CHUNK
    hpg = n_heads // N_GROUPS
    gch = hpg * SSM_HEAD_DIM

    @pl.when(c == 0)
    def _():
        st_ref[...] = ssm0_ref[0].T

    mask, _ = _ssd_token_math(dt_ref[0], dtb_ref, alog_ref, lhs3_ref, acst_ref, seg_len=q)

    def finish(g, v):
        gs = slice(g * gch, (g + 1) * gch)
        st = st_ref[:, gs]
        y = (v["y"] + jnp.dot(v["cc"], st.astype(BF16), preferred_element_type=F32) * v["eax"]
             + dsk_ref[:, gs] * v["xs"])
        gg = y * sz_ref[0, :, gs].astype(F32)
        msq = jnp.mean(gg * gg, axis=-1, keepdims=True)
        g_ref[0, :, gs] = (gg * lax.rsqrt(msq + EPS) * gw_ref[:, gs]).astype(BF16)
        xdtw = (v["xdt"] * v["dtex"]).astype(BF16)
        snew = lax.dot_general(v["bb"], xdtw, _TN, preferred_element_type=F32)
        st_ref[:, gs] = st * v["eax"][q - 1:q, :] + snew

    _staggered(N_GROUPS, [
        lambda g, _: _ssd_group_inputs(g, xs_ref.at[0], bb_ref.at[0], cc_ref.at[0], lhs3_ref, ex3_ref, hpg),
        lambda g, v: _ssd_group_diag(g, v, mask, acst_ref, hpg),
        finish,
    ])

    @pl.when(c == pl.num_programs(1) - 1)
    def _():
        ssmo_ref[0] = st_ref[...].T


def ssd_chunked(xs, bb, cc, sz, dt, ssm0, dtb, alog, dsk, gw):
    bsz, length, d_inner = xs.shape
    bc = bb.shape[2]
    n_heads = d_inner // SSM_HEAD_DIM
    q = SSD_CHUNK
    assert length % q == 0 and bc == N_GROUPS * D_STATE and n_heads % N_GROUPS == 0
    ex3 = _expansion_matrix(n_heads)
    rows = lambda b, c: (b, c, 0)
    const = lambda b, c: (0, 0)
    return pl.pallas_call(
        functools.partial(_ssd_chunk_kernel, n_heads=n_heads),
        grid=(bsz, length // q),
        in_specs=[
            pl.BlockSpec((1, q, d_inner), rows),
            pl.BlockSpec((1, q, bc), rows),
            pl.BlockSpec((1, q, bc), rows),
            pl.BlockSpec((1, q, d_inner), rows),
            pl.BlockSpec((1, q, LANES), rows),
            pl.BlockSpec((1, d_inner, D_STATE), lambda b, c: (0, 0, 0)),
            pl.BlockSpec((1, LANES), const),
            pl.BlockSpec((1, LANES), const),
            pl.BlockSpec((1, d_inner), const),
            pl.BlockSpec((1, d_inner), const),
            pl.BlockSpec(ex3.shape, const),
        ],
        out_specs=[
            pl.BlockSpec((1, q, d_inner), rows),
            pl.BlockSpec((1, d_inner, D_STATE), lambda b, c: (b, 0, 0)),
        ],
        out_shape=[
            jax.ShapeDtypeStruct((bsz, length, d_inner), BF16),
            jax.ShapeDtypeStruct((bsz, d_inner, D_STATE), F32),
        ],
        scratch_shapes=[
            pltpu.VMEM((D_STATE, d_inner), F32),
            pltpu.VMEM((3 * q, 3 * LANES), BF16),
            pltpu.VMEM((LANES, q), F32),
        ],
        compiler_params=_params("parallel", "arbitrary"),
        name="ssd_chunked",
    )(xs, bb, cc, sz, dt, ssm0, dtb, alog, dsk, gw, ex3)


def _ssd_step_tokens_kernel(xbc_ref, dt_ref, conv0_ref, cw_ref, cb_ref, dtb_ref, alog_ref, dsk_ref,
                            ex3_ref,
                            ypart_ref, eaxo_ref, xdtw_ref, bbo_ref, cco_ref, eacs_ref,
                            xpad_ref, xs_ref, bb_ref, cc_ref, lhs3_ref, acst_ref, *, n_heads, t):
    q = SSD_CHUNK
    p = SSM_HEAD_DIM
    d_inner = n_heads * p
    hpg = n_heads // N_GROUPS
    gch = hpg * p
    bc = N_GROUPS * D_STATE
    ntile = (d_inner + 2 * bc) // LANES
    nseq = q // t
    slot = 2 * t
    span = nseq * slot - t

    for j in range(ntile):
        cs = slice(j * LANES, (j + 1) * LANES)
        for i in range(nseq):
            xpad_ref[j, i * slot:i * slot + t, :] = conv0_ref[i, :, cs]
            xpad_ref[j, i * slot + t:(i + 1) * slot, :] = xbc_ref[i * t:(i + 1) * t, cs]
        conv = cb_ref[:, cs]
        for k in range(CONV_K):
            off = t - (CONV_K - 1) + k
            conv = conv + xpad_ref[j, off:off + span, :] * cw_ref[k:k + 1, cs]
        act = _silu(jnp.concatenate([conv[i * slot:i * slot + t] for i in range(nseq)], axis=0))
        _store_act_tile(j, act, xs_ref, bb_ref, cc_ref, d_inner, bc)
        lo = j * LANES
        if d_inner <= lo < d_inner + bc:
            bbo_ref[:, lo - d_inner:lo - d_inner + LANES] = act
        elif lo >= d_inner + bc:
            cco_ref[:, lo - d_inner - bc:lo - d_inner - bc + LANES] = act

    mask, eacs = _ssd_token_math(dt_ref[...], dtb_ref, alog_ref, lhs3_ref, acst_ref, seg_len=t)
    eacs_ref[...] = eacs
    def emit(g, v):
        gs = slice(g * gch, (g + 1) * gch)
        ypart_ref[:, gs] = v["y"] + dsk_ref[:, gs] * v["xs"]
        eaxo_ref[:, gs] = v["eax"]
        xdtw_ref[:, gs] = v["xdt"] * v["dtex"]

    _staggered(N_GROUPS, [
        lambda g, _: _ssd_group_inputs(g, xs_ref, bb_ref, cc_ref, lhs3_ref, ex3_ref, hpg),
        lambda g, v: _ssd_group_diag(g, v, mask, acst_ref, hpg),
        emit,
    ])


def ssd_step_tokens(xbcz, dt, conv0, cw, cb, dtb, alog, dsk, *, t):
    n_tok = xbcz.shape[0]
    conv_dim = cw.shape[1]
    d_inner = dsk.shape[1]
    n_heads = d_inner // SSM_HEAD_DIM
    q = SSD_CHUNK
    bc = N_GROUPS * D_STATE
    assert t == SUBLANES and n_tok % q == 0
    ex3 = _expansion_matrix(n_heads)
    const = lambda s: (0, 0)
    rows = lambda s: (s, 0)
    f32 = lambda n: jax.ShapeDtypeStruct((n_tok, n), F32)
    return pl.pallas_call(
        functools.partial(_ssd_step_tokens_kernel, n_heads=n_heads, t=t),
        grid=(n_tok // q,),
        in_specs=[
            pl.BlockSpec((q, conv_dim), rows),
            pl.BlockSpec((q, LANES), rows),
            pl.BlockSpec((q // t, SUBLANES, conv_dim), lambda s: (s, 0, 0)),
            pl.BlockSpec((CONV_K, conv_dim), const),
            pl.BlockSpec((1, conv_dim), const),
            pl.BlockSpec((1, LANES), const),
            pl.BlockSpec((1, LANES), const),
            pl.BlockSpec((1, d_inner), const),
            pl.BlockSpec(ex3.shape, const),
        ],
        out_specs=[
            pl.BlockSpec((q, d_inner), rows),
            pl.BlockSpec((q, d_inner), rows),
            pl.BlockSpec((q, d_inner), rows),
            pl.BlockSpec((q, bc), rows),
            pl.BlockSpec((q, bc), rows),
            pl.BlockSpec((q, LANES), rows),
        ],
        out_shape=[f32(d_inner), f32(d_inner), f32(d_inner), f32(bc), f32(bc), f32(LANES)],
        scratch_shapes=[
            pltpu.VMEM((conv_dim // LANES, 2 * q, LANES), F32),
            pltpu.VMEM((q, d_inner), F32),
            pltpu.VMEM((q, bc), BF16),
            pltpu.VMEM((q, bc), BF16),
            pltpu.VMEM((3 * q, 3 * LANES), BF16),
            pltpu.VMEM((LANES, q), F32),
        ],
        compiler_params=_params("parallel"),
        name="ssd_step_tokens",
    )(xbcz, dt, conv0, cw, cb, dtb, alog, dsk, ex3)


def _ssd_step_state_kernel(cd_ref, st_ref, cc_ref, bb_ref, xdtw_ref, eax_ref, ypart_ref, z_ref, gw_ref,
                           *rest, n_heads, t, nseq, n_fill):
    g_ref, sto_ref = rest[-2:]
    s = pl.program_id(1)
    p = SSM_HEAD_DIM
    hpg = n_heads // N_GROUPS
    gch = hpg * p

    def update():
        for g in range(N_GROUPS):
            gs = slice(g * gch, (g + 1) * gch)
            ns = slice(g * D_STATE, (g + 1) * D_STATE)
            ys = []
            for i in range(nseq):
                rows = slice(i * t, (i + 1) * t)
                h0 = st_ref[0, i, gs, :]
                yoff = lax.dot_general(cc_ref[rows, ns].astype(BF16), h0.astype(BF16), _NT,
                                       preferred_element_type=F32)
                ys.append(ypart_ref[rows, gs] + yoff * eax_ref[rows, gs])
                upd = lax.dot_general(xdtw_ref[rows, gs].astype(BF16), bb_ref[rows, ns].astype(BF16), _TN,
                                      preferred_element_type=F32)
                for r in range(hpg):
                    hs = slice(r * p, (r + 1) * p)
                    sto_ref[0, i, g * gch + r * p:g * gch + (r + 1) * p, :] = (
                        h0[hs] * cd_ref[s * nseq + i, g * hpg + r] + upd[hs])
            gg = jnp.concatenate(ys, axis=0) * _silu(z_ref[:, gs])
            msq = jnp.mean(gg * gg, axis=-1, keepdims=True)
            g_ref[:, gs] = (gg * lax.rsqrt(msq + EPS) * gw_ref[:, gs]).astype(BF16)

    if n_fill == 0:
        update()
    else:
        @pl.when(pl.program_id(0) < n_fill)
        def _():
            sto_ref[...] = jnp.zeros(sto_ref.shape, F32)

        pl.when(pl.program_id(0) == n_fill)(update)


def ssd_step_state(cd, state_all, layer, cc, bb, xdtw, eax, ypart, xbcz, gw, stacked_out, *, t, seqs_per_step=4):
    n_layers, n_seq, d_inner, d_state = state_all.shape
    n_tok = ypart.shape[0]
    bc = cc.shape[1]
    conv_dim = xbcz.shape[1] - d_inner
    nseq = _row_tile(n_seq, seqs_per_step)
    rt = nseq * t
    n_fill = n_layers - 1 if stacked_out is None else 0

    def active(l, s):
        return jnp.where(l == n_fill, s, 0)

    def out_layer(l):
        return jnp.where(l < n_fill, l + jnp.where(l >= layer, 1, 0), layer)

    rows = lambda l, s, cd_: (active(l, s), 0)
    st_map = lambda l, s, cd_: (layer, active(l, s), 0, 0)
    in_specs = [
        pl.BlockSpec((1, nseq, d_inner, d_state), st_map),
        pl.BlockSpec((rt, bc), rows),
        pl.BlockSpec((rt, bc), rows),
        pl.BlockSpec((rt, d_inner), rows),
        pl.BlockSpec((rt, d_inner), rows),
        pl.BlockSpec((rt, d_inner), rows),
        pl.BlockSpec((rt, d_inner), lambda l, s, cd_: (active(l, s), conv_dim // d_inner)),
        pl.BlockSpec((1, d_inner), lambda l, s, cd_: (0, 0)),
    ]
    args = [cd, state_all, cc, bb, xdtw, eax, ypart, xbcz, gw]
    aliases = {}
    if stacked_out is not None:
        in_specs.append(pl.BlockSpec(memory_space=pl.ANY))
        aliases = {len(args): 1}
        args.append(stacked_out)
    return pl.pallas_call(
        functools.partial(_ssd_step_state_kernel, n_heads=d_inner // SSM_HEAD_DIM, t=t, nseq=nseq,
                          n_fill=n_fill),
        grid_spec=pltpu.PrefetchScalarGridSpec(
            num_scalar_prefetch=1,
            grid=(n_fill + 1, n_seq // nseq),
            in_specs=in_specs,
            out_specs=[
                pl.BlockSpec((rt, d_inner), rows),
                pl.BlockSpec((1, nseq, d_inner, d_state), lambda l, s, cd_: (out_layer(l), s, 0, 0)),
            ],
        ),
        out_shape=[
            jax.ShapeDtypeStruct((n_tok, d_inner), BF16),
            jax.ShapeDtypeStruct((n_layers, n_seq, d_inner, d_state), F32),
        ],
        input_output_aliases=aliases,
        compiler_params=_params("arbitrary", "arbitrary"),
        name="ssd_step_state",
    )(*args)


def _attn_prompt_kernel(sink_ref, q_ref, kvc_ref, kvp_ref, kvm_ref, o_ref, *, n_q_heads):
    n = pl.program_id(1)
    d = ATTN_HEAD_DIM
    kvw = N_KV_HEADS * d
    rep = n_q_heads // N_KV_HEADS
    scale = d ** -0.5
    w = q_ref.shape[1]
    rows = rep * w
    qi = lax.broadcasted_iota(jnp.int32, (rows, w), 0) % w
    ci = lax.broadcasted_iota(jnp.int32, (rows, w), 1)
    from_prev = ci > qi
    no_prev = jnp.where(n > 0, 0.0, -jnp.inf)
    kvm, kvp, kvc = kvm_ref[...], kvp_ref[0], kvc_ref[0]
    outs = []
    for k in range(N_KV_HEADS):
        ks, vs = slice(k * d, (k + 1) * d), slice(kvw + k * d, kvw + (k + 1) * d)
        qk = q_ref[0, :, k * rep * d:(k + 1) * rep * d]
        q4 = jnp.concatenate([qk[:, r * d:(r + 1) * d] for r in range(rep)], axis=0)
        q4 = (q4 * scale).astype(BF16)
        k2 = jnp.concatenate([kvp[:, ks], kvc[:, ks]], axis=0).astype(BF16)
        v2 = jnp.concatenate([kvp[:, vs], kvc[:, vs]], axis=0).astype(BF16)
        s2 = lax.dot_general(q4, k2, _NT, preferred_element_type=F32)
        s_w = jnp.where(from_prev, s2[:, :w] + no_prev, s2[:, w:])
        s_m = lax.dot_general(q4, kvm[:, ks].astype(BF16), _NT, preferred_element_type=F32)
        p_ws, p_ms, dens = [], [], []
        for r in range(rep):
            sl = slice(r * w, (r + 1) * w)
            sink = sink_ref[k * rep + r]
            mx = jnp.maximum(jnp.maximum(jnp.max(s_w[sl], axis=-1, keepdims=True),
                                         jnp.max(s_m[sl], axis=-1, keepdims=True)), sink)
            p_w = jnp.exp(s_w[sl] - mx)
            p_m = jnp.exp(s_m[sl] - mx)
            dens.append(jnp.exp(sink - mx) + jnp.sum(p_w, axis=-1, keepdims=True)
                        + jnp.sum(p_m, axis=-1, keepdims=True))
            p_ws.append(p_w)
            p_ms.append(p_m.astype(BF16))
        p_w = jnp.concatenate(p_ws, axis=0)
        p2 = jnp.concatenate([jnp.where(from_prev, p_w, 0.0), jnp.where(from_prev, 0.0, p_w)],
                             axis=1).astype(BF16)
        o4 = (jnp.dot(p2, v2, preferred_element_type=F32)
              + jnp.dot(jnp.concatenate(p_ms, axis=0), kvm[:, vs].astype(BF16), preferred_element_type=F32))
        outs.extend(o4[r * w:(r + 1) * w, :] / dens[r] for r in range(rep))
    o_ref[0] = jnp.concatenate(outs, axis=1).astype(BF16)


def attn_prompt(q, kv, kvm, sinks):
    bsz, s, dq = q.shape
    kvd = kv.shape[2]
    nb = s // WINDOW
    return pl.pallas_call(
        functools.partial(_attn_prompt_kernel, n_q_heads=dq // ATTN_HEAD_DIM),
        grid=(bsz, nb),
        in_specs=[
            pl.BlockSpec(memory_space=pltpu.SMEM),
            pl.BlockSpec((1, WINDOW, dq), lambda b, n: (b, n, 0)),
            pl.BlockSpec((1, WINDOW, kvd), lambda b, n: (b, n, 0)),
            pl.BlockSpec((1, WINDOW, kvd), lambda b, n: (b, jnp.maximum(n - 1, 0), 0)),
            pl.BlockSpec(kvm.shape, lambda b, n: (0, 0)),
        ],
        out_specs=pl.BlockSpec((1, WINDOW, dq), lambda b, n: (b, n, 0)),
        out_shape=jax.ShapeDtypeStruct((bsz, s, dq), BF16),
        compiler_params=_params("parallel", "arbitrary"),
        name="attn_prompt",
    )(sinks, q, kv, kv, kvm)


def _attn_sample_kernel(sink_ref, q_ref, kvn_ref, ck_ref, cv_ref, kvm_ref, o_ref, *, n_q_heads, n_meta):
    d = ATTN_HEAD_DIM
    kvw = N_KV_HEADS * d
    rep = n_q_heads // N_KV_HEADS
    scale = d ** -0.5
    nseq, t, _ = q_ref.shape
    w_buf = ck_ref.shape[1]
    rows = rep * t
    nx = n_meta + t
    tq_buf = lax.broadcasted_iota(jnp.int32, (rows, w_buf), 0) % t
    pos_buf = PAST_LEN - w_buf + lax.broadcasted_iota(jnp.int32, (rows, w_buf), 1)
    buf_mask = jnp.logical_and(PAST_LEN + tq_buf - pos_buf < WINDOW, pos_buf >= n_meta)
    tq_x = lax.broadcasted_iota(jnp.int32, (rows, nx), 0) % t
    c_x = lax.broadcasted_iota(jnp.int32, (rows, nx), 1) - n_meta
    x_mask = jnp.logical_or(c_x < 0, jnp.logical_and(c_x <= tq_x, tq_x - c_x < WINDOW))
    head_in_group = lax.broadcasted_iota(jnp.int32, (rows, 1), 0) // t
    kvm = kvm_ref[...]
    sinks = []
    for k in range(N_KV_HEADS):
        sink = jnp.zeros((rows, 1), F32)
        for r in range(rep):
            sink = jnp.where(head_in_group == r, sink_ref[k * rep + r], sink)
        sinks.append(sink)
    chains = [(i, k) for i in range(nseq) for k in range(N_KV_HEADS)]

    scores, values = {}, {}
    for i in range(nseq):
        q, kvn, ck, cv = q_ref[i], kvn_ref[i], ck_ref[i], cv_ref[i]
        for k in range(N_KV_HEADS):
            ks, vs = slice(k * d, (k + 1) * d), slice(kvw + k * d, kvw + (k + 1) * d)
            q4 = jnp.concatenate([q[:, (k * rep + r) * d:(k * rep + r + 1) * d] for r in range(rep)], axis=0)
            q4 = (q4 * scale).astype(BF16)
            kx = jnp.concatenate([kvm[:, ks], kvn[:, ks]], axis=0).astype(BF16)
            vx = jnp.concatenate([kvm[:, vs], kvn[:, vs]], axis=0).astype(BF16)
            s_b = lax.dot_general(q4, ck[:, ks].astype(BF16), _NT, preferred_element_type=F32)
            s_x = lax.dot_general(q4, kx, _NT, preferred_element_type=F32)
            scores[i, k] = (jnp.where(buf_mask, s_b, -jnp.inf), jnp.where(x_mask, s_x, -jnp.inf))
            values[i, k] = (cv[:, ks].astype(BF16), vx)

    maxes = {}
    for c in chains:
        s_b, s_x = scores[c]
        maxes[c] = jnp.maximum(sinks[c[1]], jnp.maximum(jnp.max(s_b, axis=-1, keepdims=True),
                                                         jnp.max(s_x, axis=-1, keepdims=True)))
    probs, dens = {}, {}
    for c in chains:
        s_b, s_x = scores[c]
        p_b, p_x = jnp.exp(s_b - maxes[c]), jnp.exp(s_x - maxes[c])
        probs[c] = (p_b.astype(BF16), p_x.astype(BF16))
        dens[c] = (jnp.exp(sinks[c[1]] - maxes[c]) + jnp.sum(p_b, axis=-1, keepdims=True)
                   + jnp.sum(p_x, axis=-1, keepdims=True))
    outs = {}
    for c in chains:
        outs[c] = (jnp.dot(probs[c][0], values[c][0], preferred_element_type=F32)
                   + jnp.dot(probs[c][1], values[c][1], preferred_element_type=F32)) / dens[c]
    for i in range(nseq):
        heads = [outs[i, k][r * t:(r + 1) * t, :] for k in range(N_KV_HEADS) for r in range(rep)]
        o_ref[i] = jnp.concatenate(heads, axis=1).astype(BF16)


def attn_sample(q, kvn, ck, cv, kvm, sinks, *, seqs_per_step=8):
    bsz, t, dq = q.shape
    kvd = kvn.shape[2]
    w_buf = ck.shape[1]
    g = _row_tile(bsz, seqs_per_step)
    return pl.pallas_call(
        functools.partial(_attn_sample_kernel, n_q_heads=dq // ATTN_HEAD_DIM, n_meta=kvm.shape[0]),
        grid=(bsz // g,),
        in_specs=[
            pl.BlockSpec(memory_space=pltpu.SMEM),
            pl.BlockSpec((g, t, dq), lambda b: (b, 0, 0)),
            pl.BlockSpec((g, t, kvd), lambda b: (b, 0, 0)),
            pl.BlockSpec((g, w_buf, kvd // 2), lambda b: (b, 0, 0)),
            pl.BlockSpec((g, w_buf, kvd // 2), lambda b: (b, 0, 0)),
            pl.BlockSpec(kvm.shape, lambda b: (0, 0)),
        ],
        out_specs=pl.BlockSpec((g, t, dq), lambda b: (b, 0, 0)),
        out_shape=jax.ShapeDtypeStruct((bsz, t, dq), BF16),
        compiler_params=_params("parallel"),
        name="attn_sample",
    )(sinks, q, kvn, ck, cv, kvm)


def kernel(x_prompt, x_sample, state_conv, state_ssm, cache_k_win, cache_v_win, meta_tokens, a_norm_w, a_in_proj, a_conv_w, a_conv_b, a_dt_bias, a_log, a_d_skip, a_gate_norm_w, a_out_proj, kv_norm_w, w_kv, b_norm_w, w_q, attn_sinks, w_o, mlp_norm_w, w_up, w_down, final_norm_w):
    n_prompt, seq, d_model = x_prompt.shape
    n_dec, dec_seq, _ = x_sample.shape
    n_a = a_in_proj.shape[0]
    depth = w_up.shape[0]
    n_meta = meta_tokens.shape[0]
    d_inner = a_out_proj.shape[1]
    conv_dim = a_conv_w.shape[2]
    n_heads = a_log.shape[1]
    w_buf = cache_k_win.shape[1]
    kvw = N_KV_HEADS * ATTN_HEAD_DIM
    assert n_heads * SSM_HEAD_DIM == d_inner and n_heads <= LANES

    hm = meta_tokens.astype(F32)
    hp = x_prompt.reshape(n_prompt * seq, d_model)
    hs = x_sample.reshape(n_dec * dec_seq, d_model)

    def pad_lanes(v):
        return jnp.pad(v, (0, LANES - v.shape[0])).reshape(1, LANES)

    def pad_conv_state(s):
        return jnp.pad(s, ((0, 0), (SUBLANES - (CONV_K - 1), 0), (0, 0)))

    conv_p_list, ssm_p_list, conv_s_list, ssm_s_list = [], [], [], []
    step_path = dec_seq == SUBLANES and (n_dec * dec_seq) % SSD_CHUNK == 0
    ssm_in_all = state_ssm.reshape(n_a, n_dec, d_inner, D_STATE)
    ssm_s_all = None
    kvm = kv_p = kv_s = None
    for layer in range(depth):
        wu = w_up[layer].astype(BF16)
        wd = w_down[layer].astype(BF16)
        last = layer == depth - 1
        if layer < n_a:
            i = layer
            w_in = a_in_proj[i]
            w_main = w_in.astype(BF16)
            w_dt = jnp.pad(w_in[:, d_inner + conv_dim:], ((0, 0), (0, LANES - n_heads))).astype(BF16)
            w_out = a_out_proj[i].astype(BF16)
            prm = (a_conv_w[i], a_conv_b[i].reshape(1, conv_dim), pad_lanes(a_dt_bias[i]), pad_lanes(a_log[i]),
                   jnp.repeat(a_d_skip[i], SSM_HEAD_DIM).reshape(1, d_inner), a_gate_norm_w[i].reshape(1, d_inner))

            def mixer(h, bsz, length, conv0, ssm0, shared):
                if shared and length % SSD_CHUNK == 0:
                    xs_a, bb_a, cc_a, sz_a, dt, conv_o = in_proj_conv(
                        h, a_norm_w[i], w_main, w_dt, prm[0], prm[1], conv0, seq_len=length, d_inner=d_inner)
                    split = lambda t: t.reshape(bsz, length, t.shape[1])
                    g, ssm_o = ssd_chunked(split(xs_a), split(bb_a), split(cc_a), split(sz_a), split(dt),
                                           ssm0, *prm[2:])
                else:
                    xbcz, dt = in_proj(h, a_norm_w[i], w_main, w_dt, d_inner=d_inner, n_main=conv_dim + d_inner)
                    g, conv_o, ssm_o = ssd_mixer(xbcz.reshape(bsz, length, conv_dim + d_inner),
                                                 dt.reshape(bsz, length, LANES), conv0, ssm0, *prm,
                                                 shared_state=shared)
                return g.reshape(bsz * length, d_inner), conv_o, ssm_o

            zero_conv = jnp.zeros((1, SUBLANES, conv_dim), F32)
            zero_ssm = jnp.zeros((1, d_inner, D_STATE), F32)
            g_m, conv_m, ssm_m = mixer(hm, 1, n_meta, zero_conv, zero_ssm, True)
            g_p, conv_p, ssm_p = mixer(hp, n_prompt, seq, conv_m, ssm_m, True)
            conv0_s = pad_conv_state(state_conv[i])
            if step_path:
                xbcz_s, dt_s = in_proj(hs, a_norm_w[i], w_main, w_dt, d_inner=d_inner, n_main=conv_dim + d_inner)
                ypart, eax, xdtw, bb_s, cc_s, eacs = ssd_step_tokens(xbcz_s, dt_s, conv0_s, *prm[:5], t=dec_seq)
                cdecay = eacs.reshape(n_dec, dec_seq, LANES)[:, dec_seq - 1]
                g_s, ssm_s_all = ssd_step_state(cdecay, ssm_in_all, i, cc_s, bb_s, xdtw, eax, ypart, xbcz_s,
                                                prm[5], ssm_s_all, t=dec_seq)
                conv_s = xbcz_s[:, :conv_dim].reshape(n_dec, dec_seq, conv_dim)[:, dec_seq - (CONV_K - 1):]
            else:
                g_s, conv_s, ssm_s = mixer(hs, n_dec, dec_seq, conv0_s, ssm_in_all[i], False)
                conv_s = conv_s[:, SUBLANES - (CONV_K - 1):]
                ssm_s_list.append(ssm_s)
            conv_p_list.append(conv_p[:, SUBLANES - (CONV_K - 1):])
            ssm_p_list.append(ssm_p.reshape(n_prompt, n_heads, SSM_HEAD_DIM, D_STATE))
            conv_s_list.append(conv_s)
            hm = mlp(hm, mlp_norm_w[layer], wu, wd, final_norm_w, final_norm=False, proj=(g_m, w_out))
            proj_p, proj_s = (g_p, w_out), (g_s, w_out)
        else:
            j = layer - n_a
            if j == 0:
                wkv = w_kv.astype(BF16)
                kvm = norm_matmul(hm, kv_norm_w, wkv)
                kv_p = norm_matmul(hp, kv_norm_w, wkv).reshape(n_prompt, seq, 2 * kvw)
                kv_s = norm_matmul(hs, kv_norm_w, wkv).reshape(n_dec, dec_seq, 2 * kvw)
            wq = w_q[j].astype(BF16)
            wo = w_o[j].astype(BF16)
            dq = wq.shape[1]
            q_p = norm_matmul(hp, b_norm_w[j], wq).reshape(n_prompt, seq, dq)
            q_s = norm_matmul(hs, b_norm_w[j], wq).reshape(n_dec, dec_seq, dq)
            o_p = attn_prompt(q_p, kv_p, kvm, attn_sinks[j])
            o_s = attn_sample(q_s, kv_s, cache_k_win.reshape(n_dec, w_buf, kvw),
                              cache_v_win.reshape(n_dec, w_buf, kvw), kvm, attn_sinks[j])
            proj_p = (o_p.reshape(n_prompt * seq, dq), wo)
            proj_s = (o_s.reshape(n_dec * dec_seq, dq), wo)
        hp = mlp(hp, mlp_norm_w[layer], wu, wd, final_norm_w, final_norm=last, proj=proj_p)
        hs = mlp(hs, mlp_norm_w[layer], wu, wd, final_norm_w, final_norm=last, proj=proj_s)

    y_prompt = hp.reshape(n_prompt, seq, d_model)
    y_sample = hs.reshape(n_dec, dec_seq, d_model)
    kv_heads = (N_KV_HEADS, ATTN_HEAD_DIM)
    k_p = kv_p[:, seq - w_buf:, :kvw].reshape((n_prompt, w_buf) + kv_heads)
    v_p = kv_p[:, seq - w_buf:, kvw:].reshape((n_prompt, w_buf) + kv_heads)
    k_s = kv_s[:, :, :kvw].reshape((n_dec, dec_seq) + kv_heads)
    v_s = kv_s[:, :, kvw:].reshape((n_dec, dec_seq) + kv_heads)
    k_s_win = jnp.concatenate([cache_k_win, k_s], axis=1)[:, -w_buf:]
    v_s_win = jnp.concatenate([cache_v_win, v_s], axis=1)[:, -w_buf:]
    if not step_path:
        ssm_s_all = jnp.stack(ssm_s_list)
    return (y_prompt, y_sample, jnp.stack(conv_p_list), jnp.stack(ssm_p_list), k_p, v_p,
            jnp.stack(conv_s_list), ssm_s_all.reshape(state_ssm.shape), k_s_win, v_s_win)
```

```python
import functools

import jax
import jax.numpy as jnp
from jax import lax
from jax.experimental import pallas as pl
from jax.experimental.pallas import tpu as pltpu

F32 = jnp.float32
BF16 = jnp.bfloat16

N_GROUPS = 8
SSM_HEAD_DIM = 64
D_STATE = 128
CONV_K = 4
SSD_CHUNK = 128
ATTN_HEAD_DIM = 64
N_KV_HEADS = 4
WINDOW = 128
PAST_LEN = 8192
EPS = 1e-5
LOG2_E = 1.4426950408889634

LANES = 128
SUBLANES = 8
VMEM_LIMIT_BYTES = 52 * 1024 * 1024

_NT = (((1,), (1,)), ((), ()))
_TN = (((0,), (0,)), ((), ()))


def _params(*sem):
    return pltpu.CompilerParams(dimension_semantics=sem, vmem_limit_bytes=VMEM_LIMIT_BYTES)


def _rms(x, w):
    ms = jnp.mean(x * x, axis=-1, keepdims=True)
    return x * lax.rsqrt(ms + EPS) * w


def _silu(x):
    s = 0.5 * x
    return s + s * jnp.tanh(s)


def _row_tile(m, cap):
    t = min(m, cap)
    assert m % t == 0, (m, t)
    return t


def _norm_matmul_kernel(x_ref, nw_ref, w_ref, o_ref, xn_ref):
    @pl.when(pl.program_id(1) == 0)
    def _():
        xn_ref[...] = _rms(x_ref[...], nw_ref[...]).astype(BF16)

    o_ref[...] = jnp.dot(xn_ref[...], w_ref[...], preferred_element_type=F32)


def norm_matmul(x, nw, w, *, tm_cap=1024, tn_cap=1024):
    m, d = x.shape
    n = w.shape[1]
    tm, tn = _row_tile(m, tm_cap), _row_tile(n, tn_cap)
    return pl.pallas_call(
        _norm_matmul_kernel,
        grid=(m // tm, n // tn),
        in_specs=[
            pl.BlockSpec((tm, d), lambda i, j: (i, 0)),
            pl.BlockSpec((1, d), lambda i, j: (0, 0)),
            pl.BlockSpec((d, tn), lambda i, j: (0, j)),
        ],
        out_specs=pl.BlockSpec((tm, tn), lambda i, j: (i, j)),
        out_shape=jax.ShapeDtypeStruct((m, n), F32),
        scratch_shapes=[pltpu.VMEM((tm, d), BF16)],
        compiler_params=_params("parallel", "arbitrary"),
        name="norm_matmul",
    )(x, nw.reshape(1, d), w)


def _in_proj_kernel(x_ref, nw_ref, w_ref, wdt_ref, o_ref, dt_ref, xn_ref):
    @pl.when(pl.program_id(1) == 0)
    def _():
        xn = _rms(x_ref[...], nw_ref[...]).astype(BF16)
        xn_ref[...] = xn
        dt_ref[...] = jnp.dot(xn, wdt_ref[...], preferred_element_type=F32)

    o_ref[...] = jnp.dot(xn_ref[...], w_ref[...], preferred_element_type=F32)


def in_proj(x, nw, w, wdt, *, d_inner, n_main, tm_cap=1024, tn_cap=1024):
    m, d = x.shape
    n = n_main
    tm, tn = _row_tile(m, tm_cap), _row_tile(n, tn_cap)
    assert d_inner % tn == 0
    return pl.pallas_call(
        _in_proj_kernel,
        grid=(m // tm, n // tn),
        in_specs=[
            pl.BlockSpec((tm, d), lambda i, j: (i, 0)),
            pl.BlockSpec((1, d), lambda i, j: (0, 0)),
            pl.BlockSpec((d, tn), lambda i, j: (0, (j + d_inner // tn) % (n // tn))),
            pl.BlockSpec((d, LANES), lambda i, j: (0, 0)),
        ],
        out_specs=[
            pl.BlockSpec((tm, tn), lambda i, j: (i, j)),
            pl.BlockSpec((tm, LANES), lambda i, j: (i, 0)),
        ],
        out_shape=[jax.ShapeDtypeStruct((m, n), F32), jax.ShapeDtypeStruct((m, LANES), F32)],
        scratch_shapes=[pltpu.VMEM((tm, d), BF16)],
        compiler_params=_params("parallel", "arbitrary"),
        name="in_proj",
    )(x, nw.reshape(1, d), w, wdt)


def _in_proj_conv_kernel(x_ref, nw_ref, w_ref, wdt_ref, cw_ref, cb_ref, conv0_ref,
                         xs_ref, bb_ref, cc_ref, sz_ref, dt_ref, tail_ref,
                         xn_ref, xpad_ref, halo_ref, *, tiles_per_seq, n_x, n_b):
    i = pl.program_id(0)
    j = pl.program_id(1)
    tm = x_ref.shape[0]
    tn = w_ref.shape[1]
    pad = SUBLANES
    sub = 2 * LANES

    @pl.when(j == 0)
    def _():
        xn = _rms(x_ref[...], nw_ref[...]).astype(BF16)
        xn_ref[...] = xn
        dt_ref[...] = jnp.dot(xn, wdt_ref[...], preferred_element_type=F32)

    def conv_tile(out_ref):
        @pl.when(i % tiles_per_seq == 0)
        def _():
            for l in range(tn // LANES):
                halo_ref[j, l] = conv0_ref[0, :, l * LANES:(l + 1) * LANES]

        def conv_act(s, raw):
            for e in range(sub // LANES):
                l = s * (sub // LANES) + e
                cs = slice(l * LANES, (l + 1) * LANES)
                xpad_ref[l, 0:pad, :] = halo_ref[j, l]
                xpad_ref[l, pad:pad + tm, :] = raw[:, e * LANES:(e + 1) * LANES]
                conv = cb_ref[:, cs]
                for k in range(CONV_K):
                    off = pad - (CONV_K - 1) + k
                    conv = conv + xpad_ref[l, off:off + tm, :] * cw_ref[k:k + 1, cs]
                out_ref[:, cs] = _silu(conv).astype(out_ref.dtype)
                last_rows = xpad_ref[l, tm:tm + pad, :]
                halo_ref[j, l] = last_rows
                tail_ref[0, :, cs] = last_rows

        _staggered(tn // sub, [matmul_cols, conv_act])

    def matmul_cols(s, _):
        return jnp.dot(xn_ref[...], w_ref[:, s * sub:(s + 1) * sub], preferred_element_type=F32)

    def gate_act(s, raw):
        sz_ref[:, s * sub:(s + 1) * sub] = _silu(raw).astype(BF16)

    pl.when(j < n_x)(lambda: conv_tile(xs_ref))
    pl.when(jnp.logical_and(j >= n_x, j < n_x + n_b))(lambda: conv_tile(bb_ref))
    pl.when(jnp.logical_and(j >= n_x + n_b, j < n_x + 2 * n_b))(lambda: conv_tile(cc_ref))
    pl.when(j >= n_x + 2 * n_b)(lambda: _staggered(tn // sub, [matmul_cols, gate_act]))


def in_proj_conv(x, nw, w, wdt, cw, cb, conv0, *, seq_len, d_inner, tm_cap=1024):
    m, d = x.shape
    conv_dim = cw.shape[1]
    bc = (conv_dim - d_inner) // 2
    tn = bc
    tm = _row_tile(seq_len, tm_cap)
    assert d_inner % tn == 0 and w.shape[1] >= conv_dim + d_inner and m % seq_len == 0
    n_x, n_b, n_z = d_inner // tn, 1, d_inner // tn
    n_conv = n_x + 2 * n_b
    conv_col = lambda i, j: (0, jnp.minimum(j, n_conv - 1))
    bf = lambda n: jax.ShapeDtypeStruct((m, n), BF16)
    tiles_per_seq = seq_len // tm
    *acts, tails = pl.pallas_call(
        functools.partial(_in_proj_conv_kernel, tiles_per_seq=tiles_per_seq, n_x=n_x, n_b=n_b),
        grid=(m // tm, n_conv + n_z),
        in_specs=[
            pl.BlockSpec((tm, d), lambda i, j: (i, 0)),
            pl.BlockSpec((1, d), lambda i, j: (0, 0)),
            pl.BlockSpec((d, tn), lambda i, j: (0, (j + n_z) % (n_conv + n_z))),
            pl.BlockSpec((d, LANES), lambda i, j: (0, 0)),
            pl.BlockSpec((CONV_K, tn), conv_col),
            pl.BlockSpec((1, tn), conv_col),
            pl.BlockSpec((1, SUBLANES, tn), lambda i, j: (0, 0, jnp.minimum(j, n_conv - 1))),
        ],
        out_specs=[
            pl.BlockSpec((tm, tn), lambda i, j: (i, jnp.minimum(j, n_x - 1))),
            pl.BlockSpec((tm, tn), lambda i, j: (i, 0)),
            pl.BlockSpec((tm, tn), lambda i, j: (i, 0)),
            pl.BlockSpec((tm, tn), lambda i, j: (i, jnp.clip(j - n_conv, 0, n_z - 1))),
            pl.BlockSpec((tm, LANES), lambda i, j: (i, 0)),
            pl.BlockSpec((1, SUBLANES, tn), lambda i, j: (i, 0, jnp.minimum(j, n_conv - 1))),
        ],
        out_shape=[jax.ShapeDtypeStruct((m, d_inner), F32), bf(bc), bf(bc), bf(d_inner),
                   jax.ShapeDtypeStruct((m, LANES), F32),
                   jax.ShapeDtypeStruct((m // tm, SUBLANES, conv_dim), F32)],
        scratch_shapes=[
            pltpu.VMEM((tm, d), BF16),
            pltpu.VMEM((tn // LANES, tm + SUBLANES, LANES), F32),
            pltpu.VMEM((n_conv, tn // LANES, SUBLANES, LANES), F32),
        ],
        compiler_params=_params("arbitrary", "arbitrary"),
        name="in_proj_conv",
    )(x, nw.reshape(1, d), w, wdt, cw, cb, conv0)
    return (*acts, tails[tiles_per_seq - 1::tiles_per_seq])


def _mlp_steps(x_rows, nw_ref, wu_ref, wd_ref, fw_ref, o_ref, xn_ref, final_norm):
    f = pl.program_id(1)
    tm = o_ref.shape[0]
    rb = min(tm, 2 * LANES)

    @pl.when(f == 0)
    def _():
        def normalise(r, x):
            rows = slice(r * rb, (r + 1) * rb)
            xn_ref[rows, :] = _rms(x, nw_ref[...]).astype(BF16)
            o_ref[rows, :] = x

        _staggered(tm // rb, [lambda r, _: x_rows(slice(r * rb, (r + 1) * rb)), normalise])

    h = jnp.dot(xn_ref[...], wu_ref[...], preferred_element_type=F32)
    h = jnp.square(jnp.maximum(h, 0.0)).astype(BF16)
    o_ref[...] += jnp.dot(h, wd_ref[...], preferred_element_type=F32)

    if final_norm:
        @pl.when(f == pl.num_programs(1) - 1)
        def _():
            o_ref[...] = _rms(o_ref[...], fw_ref[...])


def _mlp_kernel(x_ref, nw_ref, wu_ref, wd_ref, fw_ref, o_ref, xn_ref, *, final_norm):
    _mlp_steps(lambda rows: x_ref[rows, :], nw_ref, wu_ref, wd_ref, fw_ref, o_ref, xn_ref, final_norm)


def _proj_mlp_kernel(a_ref, wa_ref, x_ref, nw_ref, wu_ref, wd_ref, fw_ref, o_ref, xn_ref, *, final_norm):
    def block_input(rows):
        return x_ref[rows, :] + jnp.dot(a_ref[rows, :], wa_ref[...], preferred_element_type=F32)

    _mlp_steps(block_input, nw_ref, wu_ref, wd_ref, fw_ref, o_ref, xn_ref, final_norm)


def mlp(x, nw, wu, wd, fw, *, final_norm, proj=None, tm_cap=1024, tf_cap=1024):
    m, d = x.shape
    dff = wu.shape[1]
    tm, tf = _row_tile(m, tm_cap), _row_tile(dff, tf_cap)
    in_specs = [
        pl.BlockSpec((tm, d), lambda i, f: (i, 0)),
        pl.BlockSpec((1, d), lambda i, f: (0, 0)),
        pl.BlockSpec((d, tf), lambda i, f: (0, f)),
        pl.BlockSpec((tf, d), lambda i, f: (f, 0)),
        pl.BlockSpec((1, d), lambda i, f: (0, 0)),
    ]
    args = [x, nw.reshape(1, d), wu, wd, fw.reshape(1, d)]
    body = _mlp_kernel
    if proj is not None:
        a, wa = proj
        k = a.shape[1]
        in_specs = [pl.BlockSpec((tm, k), lambda i, f: (i, 0)),
                    pl.BlockSpec((k, d), lambda i, f: (0, 0), pipeline_mode=pl.Buffered(1))] + in_specs
        args = [a, wa] + args
        body = _proj_mlp_kernel
    return pl.pallas_call(
        functools.partial(body, final_norm=final_norm),
        grid=(m // tm, dff // tf),
        in_specs=in_specs,
        out_specs=pl.BlockSpec((tm, d), lambda i, f: (i, 0)),
        out_shape=jax.ShapeDtypeStruct((m, d), F32),
        scratch_shapes=[pltpu.VMEM((tm, d), BF16)],
        compiler_params=_params("parallel", "arbitrary"),
        name="mlp" if proj is None else "proj_mlp",
    )(*args)


def _softplus(x):
    return jnp.maximum(x, 0.0) + jnp.log(1.0 + jnp.exp(-jnp.abs(x)))


def _split3(a):
    hi = a.astype(BF16)
    r1 = a - hi.astype(F32)
    mid = r1.astype(BF16)
    lo = (r1 - mid.astype(F32)).astype(BF16)
    return hi, mid, lo


def _ssd_kernel(xbc_ref, z_ref, dt_ref, conv0_ref, ssm0_ref,
                cw_ref, cb_ref, dtb_ref, alog_ref, dsk_ref, gw_ref,
                g_ref, convo_ref, ssmo_ref,
                xpad_ref, act_ref, state_ref, y_ref, *, q, n_heads):
    c = pl.program_id(1)
    p = SSM_HEAD_DIM
    d_inner = n_heads * p
    hpg = n_heads // N_GROUPS
    conv_dim = d_inner + 2 * N_GROUPS * D_STATE
    pad = SUBLANES

    @pl.when(c == 0)
    def _():
        xpad_ref[0:pad, :] = conv0_ref[0]
        state_ref[...] = ssm0_ref[0]

    xpad_ref[pad:pad + q, :] = xbc_ref[0]
    cblk = 512
    for j in range(conv_dim // cblk):
        cs = slice(j * cblk, (j + 1) * cblk)
        conv = cb_ref[:, cs]
        for k in range(CONV_K):
            off = pad - (CONV_K - 1) + k
            conv = conv + xpad_ref[off:off + q, cs] * cw_ref[k:k + 1, cs]
        act_ref[:, cs] = _silu(conv)
    xpad_ref[0:pad, :] = xpad_ref[q:q + pad, :]

    dt = _softplus(dt_ref[0] + dtb_ref[...])
    a = dt * (-jnp.exp(alog_ref[...]))
    row = lax.broadcasted_iota(jnp.int32, (q, q), 0)
    col = lax.broadcasted_iota(jnp.int32, (q, q), 1)
    causal = row >= col
    tri = jnp.where(causal, 1.0, 0.0).astype(BF16)
    acs = None
    for part in _split3(a):
        t = jnp.dot(tri, part, preferred_element_type=F32)
        acs = t if acs is None else acs + t
    if q < LANES:
        acs_sq = jnp.concatenate([acs, jnp.zeros((LANES - q, LANES), F32)], axis=0)
    else:
        acs_sq = acs
    acs_t = acs_sq.T
    eacs = jnp.exp(acs)
    last = acs[q - 1:q, :]
    dte = jnp.exp(last - acs)
    cdecay = jnp.exp(last)

    for g in range(N_GROUPS):
        b0 = d_inner + g * D_STATE
        c0 = d_inner + N_GROUPS * D_STATE + g * D_STATE
        bb = act_ref[:, b0:b0 + D_STATE].astype(BF16)
        cc = act_ref[:, c0:c0 + D_STATE].astype(BF16)
        cbm = lax.dot_general(cc, bb, _NT, preferred_element_type=F32)
        for pair in range(hpg // 2):
            xs2 = act_ref[:, (g * hpg + 2 * pair) * p:(g * hpg + 2 * pair + 2) * p]
            dsk2 = dsk_ref[:, (g * hpg + 2 * pair) * p:(g * hpg + 2 * pair + 2) * p]
            ys = []
            for e in range(2):
                h = g * hpg + 2 * pair + e
                xs = xs2[:, e * p:(e + 1) * p]
                seg = acs[:, h:h + 1] - acs_t[h:h + 1, 0:q]
                decay = jnp.exp(jnp.where(causal, seg, -jnp.inf))
                m = (cbm * decay).astype(BF16)
                xdt = xs * dt[:, h:h + 1]
                y = jnp.dot(m, xdt.astype(BF16), preferred_element_type=F32)
                st = state_ref[h * p:(h + 1) * p, :]
                yoff = lax.dot_general(cc, st.astype(BF16), _NT, preferred_element_type=F32)
                y = y + yoff * eacs[:, h:h + 1] + dsk2[:, e * p:(e + 1) * p] * xs
                ys.append(y)
                xdtw = (xdt * dte[:, h:h + 1]).astype(BF16)
                snew = lax.dot_general(xdtw, bb, _TN, preferred_element_type=F32)
                state_ref[h * p:(h + 1) * p, :] = (
                    jnp.broadcast_to(cdecay[:, h:h + 1], (p, D_STATE)) * st + snew)
            y_ref[:, (g * hpg + 2 * pair) * p:(g * hpg + 2 * pair + 2) * p] = (
                jnp.concatenate(ys, axis=1))

    gsz = d_inner // N_GROUPS
    for g in range(N_GROUPS):
        gs = slice(g * gsz, (g + 1) * gsz)
        gg = y_ref[:, gs] * _silu(z_ref[0, :, gs])
        ms = jnp.mean(gg * gg, axis=-1, keepdims=True)
        g_ref[0, :, gs] = (gg * lax.rsqrt(ms + EPS) * gw_ref[:, gs]).astype(BF16)

    @pl.when(c == pl.num_programs(1) - 1)
    def _():
        convo_ref[0] = xpad_ref[0:pad, :]
        ssmo_ref[0] = state_ref[...]


def ssd_mixer(xbcz, dt, conv0, ssm0, cw, cb, dtb, alog, dsk, gw, *, shared_state):
    bsz, length, _ = xbcz.shape
    conv_dim = cw.shape[1]
    d_inner = gw.shape[1]
    n_heads = d_inner // SSM_HEAD_DIM
    q = SSD_CHUNK if length % SSD_CHUNK == 0 else length
    assert q % SUBLANES == 0 and q >= SUBLANES and conv_dim % d_inner == 0
    nc = length // q
    zblk = conv_dim // d_inner
    if shared_state:
        st_map = lambda b, c: (0, 0, 0)
    else:
        st_map = lambda b, c: (b, 0, 0)
    const = lambda b, c: (0, 0)
    return pl.pallas_call(
        functools.partial(_ssd_kernel, q=q, n_heads=n_heads),
        grid=(bsz, nc),
        in_specs=[
            pl.BlockSpec((1, q, conv_dim), lambda b, c: (b, c, 0)),
            pl.BlockSpec((1, q, d_inner), lambda b, c: (b, c, zblk)),
            pl.BlockSpec((1, q, LANES), lambda b, c: (b, c, 0)),
            pl.BlockSpec((1, SUBLANES, conv_dim), st_map),
            pl.BlockSpec((1, n_heads * SSM_HEAD_DIM, D_STATE), st_map),
            pl.BlockSpec((CONV_K, conv_dim), const),
            pl.BlockSpec((1, conv_dim), const),
            pl.BlockSpec((1, LANES), const),
            pl.BlockSpec((1, LANES), const),
            pl.BlockSpec((1, d_inner), const),
            pl.BlockSpec((1, d_inner), const),
        ],
        out_specs=[
            pl.BlockSpec((1, q, d_inner), lambda b, c: (b, c, 0)),
            pl.BlockSpec((1, SUBLANES, conv_dim), lambda b, c: (b, 0, 0)),
            pl.BlockSpec((1, n_heads * SSM_HEAD_DIM, D_STATE), lambda b, c: (b, 0, 0)),
        ],
        out_shape=[
            jax.ShapeDtypeStruct((bsz, length, d_inner), BF16),
            jax.ShapeDtypeStruct((bsz, SUBLANES, conv_dim), F32),
            jax.ShapeDtypeStruct((bsz, n_heads * SSM_HEAD_DIM, D_STATE), F32),
        ],
        scratch_shapes=[
            pltpu.VMEM((q + SUBLANES, conv_dim), F32),
            pltpu.VMEM((q, conv_dim), F32),
            pltpu.VMEM((n_heads * SSM_HEAD_DIM, D_STATE), F32),
            pltpu.VMEM((q, d_inner), F32),
        ],
        compiler_params=_params("parallel", "arbitrary"),
        name="ssd_mixer",
    )(xbcz, xbcz, dt, conv0, ssm0, cw, cb, dtb, alog, dsk, gw)


def _expansion_matrix(n_heads):
    h = jnp.arange(LANES)[:, None]
    ex = h == jnp.arange(n_heads * SSM_HEAD_DIM)[None, :] // SSM_HEAD_DIM
    return jnp.tile(ex.astype(BF16), (3, 1))


def _store_act_tile(j, act, xs_ref, bb_ref, cc_ref, d_inner, bc):
    lo = j * LANES
    if lo < d_inner:
        xs_ref[:, lo:lo + LANES] = act
    elif lo < d_inner + bc:
        bb_ref[:, lo - d_inner:lo - d_inner + LANES] = act.astype(BF16)
    else:
        cc_ref[:, lo - d_inner - bc:lo - d_inner - bc + LANES] = act.astype(BF16)


def _ssd_token_math(dt_raw, dtb_ref, alog_ref, lhs3_ref, acst_ref, *, seg_len):
    q = dt_raw.shape[0]
    dt = _softplus(dt_raw + dtb_ref[...])
    a = dt * (-jnp.exp(alog_ref[...]))
    row = lax.broadcasted_iota(jnp.int32, (q, q), 0)
    col = lax.broadcasted_iota(jnp.int32, (q, q), 1)
    mask = row >= col
    if seg_len != q:
        mask = jnp.logical_and(mask, row // seg_len == col // seg_len)
        seg_end = (row // seg_len) * seg_len + (seg_len - 1)
    tri = jnp.where(mask, 1.0, 0.0).astype(BF16)
    acs = None
    for part in _split3(a):
        t = jnp.dot(tri, part, preferred_element_type=F32)
        acs = t if acs is None else acs + t
    if seg_len == q:
        last = acs[q - 1:q, :]
    else:
        sel = jnp.where(col == seg_end, 1.0, 0.0).astype(BF16)
        last = None
        for part in _split3(acs):
            t = jnp.dot(sel, part, preferred_element_type=F32)
            last = t if last is None else last + t
    acst_ref[...] = (acs * LOG2_E).T
    eacs = jnp.exp(acs)
    stack = jnp.concatenate([dt, eacs, jnp.exp(last - acs)], axis=0)
    lhs3_ref[...] = jnp.concatenate(_split3(stack), axis=1)
    return mask, eacs


def _staggered(n, stages):
    carried = {}
    for t in range(n + len(stages) - 1):
        for k, stage in enumerate(stages):
            g = t - k
            if 0 <= g < n:
                carried[g] = stage(g, carried.get(g))


def _ssd_group_inputs(g, xs_ref, bb_ref, cc_ref, lhs3_ref, ex3_ref, hpg):
    q = xs_ref.shape[0]
    gch = hpg * SSM_HEAD_DIM
    gs = slice(g * gch, (g + 1) * gch)
    bb = bb_ref[:, g * D_STATE:(g + 1) * D_STATE]
    cc = cc_ref[:, g * D_STATE:(g + 1) * D_STATE]
    cbm = lax.dot_general(cc, bb, _NT, preferred_element_type=F32)
    ex = jnp.dot(lhs3_ref[...], ex3_ref[:, gs], preferred_element_type=F32)
    xs = xs_ref[:, gs]
    return dict(bb=bb, cc=cc, cbm=cbm, xs=xs, xdt=xs * ex[0:q], eax=ex[q:2 * q], dtex=ex[2 * q:3 * q])


def _ssd_group_diag(g, v, mask, acst_ref, hpg):
    q = v["xs"].shape[0]
    p = SSM_HEAD_DIM
    lane_head = lax.broadcasted_iota(jnp.int32, (q, hpg * p), 1) // p
    ms, rhs = [], []
    for r in range(hpg):
        h = g * hpg + r
        rowb = jnp.broadcast_to(acst_ref[h:h + 1, :], (q, q))
        decay = jnp.exp2(jnp.where(mask, rowb.T - rowb, -jnp.inf))
        ms.append((v["cbm"] * decay).astype(BF16))
        rhs.append(jnp.where(lane_head == r, v["xdt"], 0.0).astype(BF16))
    y = jnp.dot(jnp.concatenate(ms, axis=1), jnp.concatenate(rhs, axis=0),
                preferred_element_type=F32)
    return dict(v, y=y, cbm=None)


def _ssd_chunk_kernel(xs_ref, bb_ref, cc_ref, sz_ref, dt_ref, ssm0_ref,
                      dtb_ref, alog_ref, dsk_ref, gw_ref, ex3_ref,
                      g_ref, ssmo_ref,
                      st_ref, lhs3_ref, acst_ref, *, n_heads, cps):
    c = pl.program_id(1)
    q = SSD_CHUNK
    hpg = n_heads // N_GROUPS
    gch = hpg * SSM_HEAD_DIM

    @pl.when(c == 0)
    def _():
        st_ref[...] = ssm0_ref[0].T

    rows = [pl.ds(h * q, q) for h in range(cps)]
    masks = [_ssd_token_math(dt_ref[0, rows[h], :], dtb_ref, alog_ref, lhs3_ref.at[h], acst_ref.at[h],
                             seg_len=q)[0] for h in range(cps)]

    def finish(item, v):
        h, g = divmod(item, N_GROUPS)
        gs = slice(g * gch, (g + 1) * gch)
        st = st_ref[:, gs]
        y = (v["y"] + jnp.dot(v["cc"], st.astype(BF16), preferred_element_type=F32) * v["eax"]
             + dsk_ref[:, gs] * v["xs"])
        gg = y * sz_ref[0, rows[h], gs].astype(F32)
        msq = jnp.mean(gg * gg, axis=-1, keepdims=True)
        g_ref[0, rows[h], gs] = (gg * lax.rsqrt(msq + EPS) * gw_ref[:, gs]).astype(BF16)
        xdtw = (v["xdt"] * v["dtex"]).astype(BF16)
        snew = lax.dot_general(v["bb"], xdtw, _TN, preferred_element_type=F32)
        st_ref[:, gs] = st * v["eax"][q - 1:q, :] + snew

    def inputs(item, _):
        h, g = divmod(item, N_GROUPS)
        return _ssd_group_inputs(g, xs_ref.at[0, rows[h]], bb_ref.at[0, rows[h]], cc_ref.at[0, rows[h]],
                                 lhs3_ref.at[h], ex3_ref, hpg)

    def diag(item, v):
        h, g = divmod(item, N_GROUPS)
        return _ssd_group_diag(g, v, masks[h], acst_ref.at[h], hpg)

    _staggered(cps * N_GROUPS, [inputs, diag, finish])

    @pl.when(c == pl.num_programs(1) - 1)
    def _():
        ssmo_ref[0] = st_ref[...].T


def ssd_chunked(xs, bb, cc, sz, dt, ssm0, dtb, alog, dsk, gw, *, chunks_per_step=4):
    bsz, length, d_inner = xs.shape
    bc = bb.shape[2]
    n_heads = d_inner // SSM_HEAD_DIM
    q = SSD_CHUNK
    assert length % q == 0 and bc == N_GROUPS * D_STATE and n_heads % N_GROUPS == 0
    cps = _row_tile(length // q, chunks_per_step)
    qs = cps * q
    ex3 = _expansion_matrix(n_heads)
    rows = lambda b, c: (b, c, 0)
    const = lambda b, c: (0, 0)
    return pl.pallas_call(
        functools.partial(_ssd_chunk_kernel, n_heads=n_heads, cps=cps),
        grid=(bsz, length // qs),
        in_specs=[
            pl.BlockSpec((1, qs, d_inner), rows),
            pl.BlockSpec((1, qs, bc), rows),
            pl.BlockSpec((1, qs, bc), rows),
            pl.BlockSpec((1, qs, d_inner), rows),
            pl.BlockSpec((1, qs, LANES), rows),
            pl.BlockSpec((1, d_inner, D_STATE), lambda b, c: (0, 0, 0)),
            pl.BlockSpec((1, LANES), const),
            pl.BlockSpec((1, LANES), const),
            pl.BlockSpec((1, d_inner), const),
            pl.BlockSpec((1, d_inner), const),
            pl.BlockSpec(ex3.shape, const),
        ],
        out_specs=[
            pl.BlockSpec((1, qs, d_inner), rows),
            pl.BlockSpec((1, d_inner, D_STATE), lambda b, c: (b, 0, 0)),
        ],
        out_shape=[
            jax.ShapeDtypeStruct((bsz, length, d_inner), BF16),
            jax.ShapeDtypeStruct((bsz, d_inner, D_STATE), F32),
        ],
        scratch_shapes=[
            pltpu.VMEM((D_STATE, d_inner), F32),
            pltpu.VMEM((cps, 3 * q, 3 * LANES), BF16),
            pltpu.VMEM((cps, LANES, q), F32),
        ],
        compiler_params=_params("parallel", "arbitrary"),
        name="ssd_chunked",
    )(xs, bb, cc, sz, dt, ssm0, dtb, alog, dsk, gw, ex3)


def _ssd_step_tokens_kernel(xbc_ref, dt_ref, conv0_ref, cw_ref, cb_ref, dtb_ref, alog_ref, dsk_ref,
                            ex3_ref,
                            ypart_ref, eaxo_ref, xdtw_ref, bbo_ref, cco_ref, eacs_ref,
                            xpad_ref, xs_ref, bb_ref, cc_ref, lhs3_ref, acst_ref, *, n_heads, t):
    q = SSD_CHUNK
    p = SSM_HEAD_DIM
    d_inner = n_heads * p
    hpg = n_heads // N_GROUPS
    gch = hpg * p
    bc = N_GROUPS * D_STATE
    ntile = (d_inner + 2 * bc) // LANES
    nseq = q // t
    slot = 2 * t
    span = nseq * slot - t

    for j in range(ntile):
        cs = slice(j * LANES, (j + 1) * LANES)
        for i in range(nseq):
            xpad_ref[j, i * slot:i * slot + t, :] = conv0_ref[i, :, cs]
            xpad_ref[j, i * slot + t:(i + 1) * slot, :] = xbc_ref[i * t:(i + 1) * t, cs]
        conv = cb_ref[:, cs]
        for k in range(CONV_K):
            off = t - (CONV_K - 1) + k
            conv = conv + xpad_ref[j, off:off + span, :] * cw_ref[k:k + 1, cs]
        act = _silu(jnp.concatenate([conv[i * slot:i * slot + t] for i in range(nseq)], axis=0))
        _store_act_tile(j, act, xs_ref, bb_ref, cc_ref, d_inner, bc)
        lo = j * LANES
        if d_inner <= lo < d_inner + bc:
            bbo_ref[:, lo - d_inner:lo - d_inner + LANES] = act
        elif lo >= d_inner + bc:
            cco_ref[:, lo - d_inner - bc:lo - d_inner - bc + LANES] = act

    mask, eacs = _ssd_token_math(dt_ref[...], dtb_ref, alog_ref, lhs3_ref, acst_ref, seg_len=t)
    eacs_ref[...] = eacs
    def emit(g, v):
        gs = slice(g * gch, (g + 1) * gch)
        ypart_ref[:, gs] = v["y"] + dsk_ref[:, gs] * v["xs"]
        eaxo_ref[:, gs] = v["eax"]
        xdtw_ref[:, gs] = v["xdt"] * v["dtex"]

    _staggered(N_GROUPS, [
        lambda g, _: _ssd_group_inputs(g, xs_ref, bb_ref, cc_ref, lhs3_ref, ex3_ref, hpg),
        lambda g, v: _ssd_group_diag(g, v, mask, acst_ref, hpg),
        emit,
    ])


def ssd_step_tokens(xbcz, dt, conv0, cw, cb, dtb, alog, dsk, *, t):
    n_tok = xbcz.shape[0]
    conv_dim = cw.shape[1]
    d_inner = dsk.shape[1]
    n_heads = d_inner // SSM_HEAD_DIM
    q = SSD_CHUNK
    bc = N_GROUPS * D_STATE
    assert t == SUBLANES and n_tok % q == 0
    ex3 = _expansion_matrix(n_heads)
    const = lambda s: (0, 0)
    rows = lambda s: (s, 0)
    f32 = lambda n: jax.ShapeDtypeStruct((n_tok, n), F32)
    return pl.pallas_call(
        functools.partial(_ssd_step_tokens_kernel, n_heads=n_heads, t=t),
        grid=(n_tok // q,),
        in_specs=[
            pl.BlockSpec((q, conv_dim), rows),
            pl.BlockSpec((q, LANES), rows),
            pl.BlockSpec((q // t, SUBLANES, conv_dim), lambda s: (s, 0, 0)),
            pl.BlockSpec((CONV_K, conv_dim), const),
            pl.BlockSpec((1, conv_dim), const),
            pl.BlockSpec((1, LANES), const),
            pl.BlockSpec((1, LANES), const),
            pl.BlockSpec((1, d_inner), const),
            pl.BlockSpec(ex3.shape, const),
        ],
        out_specs=[
            pl.BlockSpec((q, d_inner), rows),
            pl.BlockSpec((q, d_inner), rows),
            pl.BlockSpec((q, d_inner), rows),
            pl.BlockSpec((q, bc), rows),
            pl.BlockSpec((q, bc), rows),
            pl.BlockSpec((q, LANES), rows),
        ],
        out_shape=[f32(d_inner), f32(d_inner), f32(d_inner), f32(bc), f32(bc), f32(LANES)],
        scratch_shapes=[
            pltpu.VMEM((conv_dim // LANES, 2 * q, LANES), F32),
            pltpu.VMEM((q, d_inner), F32),
            pltpu.VMEM((q, bc), BF16),
            pltpu.VMEM((q, bc), BF16),
            pltpu.VMEM((3 * q, 3 * LANES), BF16),
            pltpu.VMEM((LANES, q), F32),
        ],
        compiler_params=_params("parallel"),
        name="ssd_step_tokens",
    )(xbcz, dt, conv0, cw, cb, dtb, alog, dsk, ex3)


def _ssd_step_state_kernel(cd_ref, st_ref, cc_ref, bb_ref, xdtw_ref, eax_ref, ypart_ref, z_ref, gw_ref,
                           *rest, n_heads, t, nseq, n_fill):
    g_ref, sto_ref = rest[-2:]
    s = pl.program_id(1)
    p = SSM_HEAD_DIM
    hpg = n_heads // N_GROUPS
    gch = hpg * p

    def update():
        for g in range(N_GROUPS):
            gs = slice(g * gch, (g + 1) * gch)
            ns = slice(g * D_STATE, (g + 1) * D_STATE)
            ys = []
            for i in range(nseq):
                rows = slice(i * t, (i + 1) * t)
                h0 = st_ref[0, i, gs, :]
                yoff = lax.dot_general(cc_ref[rows, ns].astype(BF16), h0.astype(BF16), _NT,
                                       preferred_element_type=F32)
                ys.append(ypart_ref[rows, gs] + yoff * eax_ref[rows, gs])
                upd = lax.dot_general(xdtw_ref[rows, gs].astype(BF16), bb_ref[rows, ns].astype(BF16), _TN,
                                      preferred_element_type=F32)
                for r in range(hpg):
                    hs = slice(r * p, (r + 1) * p)
                    sto_ref[0, i, g * gch + r * p:g * gch + (r + 1) * p, :] = (
                        h0[hs] * cd_ref[s * nseq + i, g * hpg + r] + upd[hs])
            gg = jnp.concatenate(ys, axis=0) * _silu(z_ref[:, gs])
            msq = jnp.mean(gg * gg, axis=-1, keepdims=True)
            g_ref[:, gs] = (gg * lax.rsqrt(msq + EPS) * gw_ref[:, gs]).astype(BF16)

    if n_fill == 0:
        update()
    else:
        @pl.when(pl.program_id(0) < n_fill)
        def _():
            sto_ref[...] = jnp.zeros(sto_ref.shape, F32)

        pl.when(pl.program_id(0) == n_fill)(update)


def ssd_step_state(cd, state_all, layer, cc, bb, xdtw, eax, ypart, xbcz, gw, stacked_out, *, t, seqs_per_step=4):
    n_layers, n_seq, d_inner, d_state = state_all.shape
    n_tok = ypart.shape[0]
    bc = cc.shape[1]
    conv_dim = xbcz.shape[1] - d_inner
    nseq = _row_tile(n_seq, seqs_per_step)
    rt = nseq * t
    n_fill = n_layers - 1 if stacked_out is None else 0

    def active(l, s):
        return jnp.where(l == n_fill, s, 0)

    def out_layer(l):
        return jnp.where(l < n_fill, l + jnp.where(l >= layer, 1, 0), layer)

    rows = lambda l, s, cd_: (active(l, s), 0)
    st_map = lambda l, s, cd_: (layer, active(l, s), 0, 0)
    in_specs = [
        pl.BlockSpec((1, nseq, d_inner, d_state), st_map),
        pl.BlockSpec((rt, bc), rows),
        pl.BlockSpec((rt, bc), rows),
        pl.BlockSpec((rt, d_inner), rows),
        pl.BlockSpec((rt, d_inner), rows),
        pl.BlockSpec((rt, d_inner), rows),
        pl.BlockSpec((rt, d_inner), lambda l, s, cd_: (active(l, s), conv_dim // d_inner)),
        pl.BlockSpec((1, d_inner), lambda l, s, cd_: (0, 0)),
    ]
    args = [cd, state_all, cc, bb, xdtw, eax, ypart, xbcz, gw]
    aliases = {}
    if stacked_out is not None:
        in_specs.append(pl.BlockSpec(memory_space=pl.ANY))
        aliases = {len(args): 1}
        args.append(stacked_out)
    return pl.pallas_call(
        functools.partial(_ssd_step_state_kernel, n_heads=d_inner // SSM_HEAD_DIM, t=t, nseq=nseq,
                          n_fill=n_fill),
        grid_spec=pltpu.PrefetchScalarGridSpec(
            num_scalar_prefetch=1,
            grid=(n_fill + 1, n_seq // nseq),
            in_specs=in_specs,
            out_specs=[
                pl.BlockSpec((rt, d_inner), rows),
                pl.BlockSpec((1, nseq, d_inner, d_state), lambda l, s, cd_: (out_layer(l), s, 0, 0)),
            ],
        ),
        out_shape=[
            jax.ShapeDtypeStruct((n_tok, d_inner), BF16),
            jax.ShapeDtypeStruct((n_layers, n_seq, d_inner, d_state), F32),
        ],
        input_output_aliases=aliases,
        compiler_params=_params("arbitrary", "arbitrary"),
        name="ssd_step_state",
    )(*args)


def _attn_prompt_kernel(sink_ref, q_ref, kvc_ref, kvp_ref, kvm_ref, o_ref, *, n_q_heads):
    n = pl.program_id(1)
    d = ATTN_HEAD_DIM
    kvw = N_KV_HEADS * d
    rep = n_q_heads // N_KV_HEADS
    scale = d ** -0.5
    w = q_ref.shape[1]
    rows = rep * w
    qi = lax.broadcasted_iota(jnp.int32, (rows, w), 0) % w
    ci = lax.broadcasted_iota(jnp.int32, (rows, w), 1)
    from_prev = ci > qi
    no_prev = jnp.where(n > 0, 0.0, -jnp.inf)
    kvm, kvp, kvc = kvm_ref[...], kvp_ref[0], kvc_ref[0]
    outs = []
    for k in range(N_KV_HEADS):
        ks, vs = slice(k * d, (k + 1) * d), slice(kvw + k * d, kvw + (k + 1) * d)
        qk = q_ref[0, :, k * rep * d:(k + 1) * rep * d]
        q4 = jnp.concatenate([qk[:, r * d:(r + 1) * d] for r in range(rep)], axis=0)
        q4 = (q4 * scale).astype(BF16)
        k2 = jnp.concatenate([kvp[:, ks], kvc[:, ks]], axis=0).astype(BF16)
        v2 = jnp.concatenate([kvp[:, vs], kvc[:, vs]], axis=0).astype(BF16)
        s2 = lax.dot_general(q4, k2, _NT, preferred_element_type=F32)
        s_w = jnp.where(from_prev, s2[:, :w] + no_prev, s2[:, w:])
        s_m = lax.dot_general(q4, kvm[:, ks].astype(BF16), _NT, preferred_element_type=F32)
        p_ws, p_ms, dens = [], [], []
        for r in range(rep):
            sl = slice(r * w, (r + 1) * w)
            sink = sink_ref[k * rep + r]
            mx = jnp.maximum(jnp.maximum(jnp.max(s_w[sl], axis=-1, keepdims=True),
                                         jnp.max(s_m[sl], axis=-1, keepdims=True)), sink)
            p_w = jnp.exp(s_w[sl] - mx)
            p_m = jnp.exp(s_m[sl] - mx)
            dens.append(jnp.exp(sink - mx) + jnp.sum(p_w, axis=-1, keepdims=True)
                        + jnp.sum(p_m, axis=-1, keepdims=True))
            p_ws.append(p_w)
            p_ms.append(p_m.astype(BF16))
        p_w = jnp.concatenate(p_ws, axis=0)
        p2 = jnp.concatenate([jnp.where(from_prev, p_w, 0.0), jnp.where(from_prev, 0.0, p_w)],
                             axis=1).astype(BF16)
        o4 = (jnp.dot(p2, v2, preferred_element_type=F32)
              + jnp.dot(jnp.concatenate(p_ms, axis=0), kvm[:, vs].astype(BF16), preferred_element_type=F32))
        outs.extend(o4[r * w:(r + 1) * w, :] / dens[r] for r in range(rep))
    o_ref[0] = jnp.concatenate(outs, axis=1).astype(BF16)


def attn_prompt(q, kv, kvm, sinks):
    bsz, s, dq = q.shape
    kvd = kv.shape[2]
    nb = s // WINDOW
    return pl.pallas_call(
        functools.partial(_attn_prompt_kernel, n_q_heads=dq // ATTN_HEAD_DIM),
        grid=(bsz, nb),
        in_specs=[
            pl.BlockSpec(memory_space=pltpu.SMEM),
            pl.BlockSpec((1, WINDOW, dq), lambda b, n: (b, n, 0)),
            pl.BlockSpec((1, WINDOW, kvd), lambda b, n: (b, n, 0)),
            pl.BlockSpec((1, WINDOW, kvd), lambda b, n: (b, jnp.maximum(n - 1, 0), 0)),
            pl.BlockSpec(kvm.shape, lambda b, n: (0, 0)),
        ],
        out_specs=pl.BlockSpec((1, WINDOW, dq), lambda b, n: (b, n, 0)),
        out_shape=jax.ShapeDtypeStruct((bsz, s, dq), BF16),
        compiler_params=_params("parallel", "arbitrary"),
        name="attn_prompt",
    )(sinks, q, kv, kv, kvm)


def _attn_sample_kernel(sink_ref, q_ref, kvn_ref, ck_ref, cv_ref, kvm_ref, o_ref, *, n_q_heads, n_meta):
    d = ATTN_HEAD_DIM
    kvw = N_KV_HEADS * d
    rep = n_q_heads // N_KV_HEADS
    scale = d ** -0.5
    nseq, t, _ = q_ref.shape
    w_buf = ck_ref.shape[1]
    rows = rep * t
    nx = n_meta + t
    tq_buf = lax.broadcasted_iota(jnp.int32, (rows, w_buf), 0) % t
    pos_buf = PAST_LEN - w_buf + lax.broadcasted_iota(jnp.int32, (rows, w_buf), 1)
    buf_mask = jnp.logical_and(PAST_LEN + tq_buf - pos_buf < WINDOW, pos_buf >= n_meta)
    tq_x = lax.broadcasted_iota(jnp.int32, (rows, nx), 0) % t
    c_x = lax.broadcasted_iota(jnp.int32, (rows, nx), 1) - n_meta
    x_mask = jnp.logical_or(c_x < 0, jnp.logical_and(c_x <= tq_x, tq_x - c_x < WINDOW))
    head_in_group = lax.broadcasted_iota(jnp.int32, (rows, 1), 0) // t
    kvm = kvm_ref[...]
    sinks = []
    for k in range(N_KV_HEADS):
        sink = jnp.zeros((rows, 1), F32)
        for r in range(rep):
            sink = jnp.where(head_in_group == r, sink_ref[k * rep + r], sink)
        sinks.append(sink)
    chains = [(i, k) for i in range(nseq) for k in range(N_KV_HEADS)]

    scores, values = {}, {}
    for i in range(nseq):
        q, kvn, ck, cv = q_ref[i], kvn_ref[i], ck_ref[i], cv_ref[i]
        for k in range(N_KV_HEADS):
            ks, vs = slice(k * d, (k + 1) * d), slice(kvw + k * d, kvw + (k + 1) * d)
            q4 = jnp.concatenate([q[:, (k * rep + r) * d:(k * rep + r + 1) * d] for r in range(rep)], axis=0)
            q4 = (q4 * scale).astype(BF16)
            kx = jnp.concatenate([kvm[:, ks], kvn[:, ks]], axis=0).astype(BF16)
            vx = jnp.concatenate([kvm[:, vs], kvn[:, vs]], axis=0).astype(BF16)
            s_b = lax.dot_general(q4, ck[:, ks].astype(BF16), _NT, preferred_element_type=F32)
            s_x = lax.dot_general(q4, kx, _NT, preferred_element_type=F32)
            scores[i, k] = (jnp.where(buf_mask, s_b, -jnp.inf), jnp.where(x_mask, s_x, -jnp.inf))
            values[i, k] = (cv[:, ks].astype(BF16), vx)

    maxes = {}
    for c in chains:
        s_b, s_x = scores[c]
        maxes[c] = jnp.maximum(sinks[c[1]], jnp.maximum(jnp.max(s_b, axis=-1, keepdims=True),
                                                         jnp.max(s_x, axis=-1, keepdims=True)))
    probs, dens = {}, {}
    for c in chains:
        s_b, s_x = scores[c]
        p_b, p_x = jnp.exp(s_b - maxes[c]), jnp.exp(s_x - maxes[c])
        probs[c] = (p_b.astype(BF16), p_x.astype(BF16))
        dens[c] = (jnp.exp(sinks[c[1]] - maxes[c]) + jnp.sum(p_b, axis=-1, keepdims=True)
                   + jnp.sum(p_x, axis=-1, keepdims=True))
    outs = {}
    for c in chains:
        outs[c] = (jnp.dot(probs[c][0], values[c][0], preferred_element_type=F32)
                   + jnp.dot(probs[c][1], values[c][1], preferred_element_type=F32)) / dens[c]
    for i in range(nseq):
        heads = [outs[i, k][r * t:(r + 1) * t, :] for k in range(N_KV_HEADS) for r in range(rep)]
        o_ref[i] = jnp.concatenate(heads, axis=1).astype(BF16)


def attn_sample(q, kvn, ck, cv, kvm, sinks, *, seqs_per_step=8):
    bsz, t, dq = q.shape
    kvd = kvn.shape[2]
    w_buf = ck.shape[1]
    g = _row_tile(bsz, seqs_per_step)
    return pl.pallas_call(
        functools.partial(_attn_sample_kernel, n_q_heads=dq // ATTN_HEAD_DIM, n_meta=kvm.shape[0]),
        grid=(bsz // g,),
        in_specs=[
            pl.BlockSpec(memory_space=pltpu.SMEM),
            pl.BlockSpec((g, t, dq), lambda b: (b, 0, 0)),
            pl.BlockSpec((g, t, kvd), lambda b: (b, 0, 0)),
            pl.BlockSpec((g, w_buf, kvd // 2), lambda b: (b, 0, 0)),
            pl.BlockSpec((g, w_buf, kvd // 2), lambda b: (b, 0, 0)),
            pl.BlockSpec(kvm.shape, lambda b: (0, 0)),
        ],
        out_specs=pl.BlockSpec((g, t, dq), lambda b: (b, 0, 0)),
        out_shape=jax.ShapeDtypeStruct((bsz, t, dq), BF16),
        compiler_params=_params("parallel"),
        name="attn_sample",
    )(sinks, q, kvn, ck, cv, kvm)


def kernel(x_prompt, x_sample, state_conv, state_ssm, cache_k_win, cache_v_win, meta_tokens, a_norm_w, a_in_proj, a_conv_w, a_conv_b, a_dt_bias, a_log, a_d_skip, a_gate_norm_w, a_out_proj, kv_norm_w, w_kv, b_norm_w, w_q, attn_sinks, w_o, mlp_norm_w, w_up, w_down, final_norm_w):
    n_prompt, seq, d_model = x_prompt.shape
    n_dec, dec_seq, _ = x_sample.shape
    n_a = a_in_proj.shape[0]
    depth = w_up.shape[0]
    n_meta = meta_tokens.shape[0]
    d_inner = a_out_proj.shape[1]
    conv_dim = a_conv_w.shape[2]
    n_heads = a_log.shape[1]
    w_buf = cache_k_win.shape[1]
    kvw = N_KV_HEADS * ATTN_HEAD_DIM
    assert n_heads * SSM_HEAD_DIM == d_inner and n_heads <= LANES

    hm = meta_tokens.astype(F32)
    hp = x_prompt.reshape(n_prompt * seq, d_model)
    hs = x_sample.reshape(n_dec * dec_seq, d_model)

    def pad_lanes(v):
        return jnp.pad(v, (0, LANES - v.shape[0])).reshape(1, LANES)

    def pad_conv_state(s):
        return jnp.pad(s, ((0, 0), (SUBLANES - (CONV_K - 1), 0), (0, 0)))

    conv_p_list, ssm_p_list, conv_s_list, ssm_s_list = [], [], [], []
    step_path = dec_seq == SUBLANES and (n_dec * dec_seq) % SSD_CHUNK == 0
    ssm_in_all = state_ssm.reshape(n_a, n_dec, d_inner, D_STATE)
    ssm_s_all = None
    kvm = kv_p = kv_s = None
    for layer in range(depth):
        wu = w_up[layer].astype(BF16)
        wd = w_down[layer].astype(BF16)
        last = layer == depth - 1
        if layer < n_a:
            i = layer
            w_in = a_in_proj[i]
            w_main = w_in.astype(BF16)
            w_dt = jnp.pad(w_in[:, d_inner + conv_dim:], ((0, 0), (0, LANES - n_heads))).astype(BF16)
            w_out = a_out_proj[i].astype(BF16)
            prm = (a_conv_w[i], a_conv_b[i].reshape(1, conv_dim), pad_lanes(a_dt_bias[i]), pad_lanes(a_log[i]),
                   jnp.repeat(a_d_skip[i], SSM_HEAD_DIM).reshape(1, d_inner), a_gate_norm_w[i].reshape(1, d_inner))

            def mixer(h, bsz, length, conv0, ssm0, shared):
                if shared and length % SSD_CHUNK == 0:
                    xs_a, bb_a, cc_a, sz_a, dt, conv_o = in_proj_conv(
                        h, a_norm_w[i], w_main, w_dt, prm[0], prm[1], conv0, seq_len=length, d_inner=d_inner)
                    split = lambda t: t.reshape(bsz, length, t.shape[1])
                    g, ssm_o = ssd_chunked(split(xs_a), split(bb_a), split(cc_a), split(sz_a), split(dt),
                                           ssm0, *prm[2:])
                else:
                    xbcz, dt = in_proj(h, a_norm_w[i], w_main, w_dt, d_inner=d_inner, n_main=conv_dim + d_inner)
                    g, conv_o, ssm_o = ssd_mixer(xbcz.reshape(bsz, length, conv_dim + d_inner),
                                                 dt.reshape(bsz, length, LANES), conv0, ssm0, *prm,
                                                 shared_state=shared)
                return g.reshape(bsz * length, d_inner), conv_o, ssm_o

            zero_conv = jnp.zeros((1, SUBLANES, conv_dim), F32)
            zero_ssm = jnp.zeros((1, d_inner, D_STATE), F32)
            g_m, conv_m, ssm_m = mixer(hm, 1, n_meta, zero_conv, zero_ssm, True)
            g_p, conv_p, ssm_p = mixer(hp, n_prompt, seq, conv_m, ssm_m, True)
            conv0_s = pad_conv_state(state_conv[i])
            if step_path:
                xbcz_s, dt_s = in_proj(hs, a_norm_w[i], w_main, w_dt, d_inner=d_inner, n_main=conv_dim + d_inner)
                ypart, eax, xdtw, bb_s, cc_s, eacs = ssd_step_tokens(xbcz_s, dt_s, conv0_s, *prm[:5], t=dec_seq)
                cdecay = eacs.reshape(n_dec, dec_seq, LANES)[:, dec_seq - 1]
                g_s, ssm_s_all = ssd_step_state(cdecay, ssm_in_all, i, cc_s, bb_s, xdtw, eax, ypart, xbcz_s,
                                                prm[5], ssm_s_all, t=dec_seq)
                conv_s = xbcz_s[:, :conv_dim].reshape(n_dec, dec_seq, conv_dim)[:, dec_seq - (CONV_K - 1):]
            else:
                g_s, conv_s, ssm_s = mixer(hs, n_dec, dec_seq, conv0_s, ssm_in_all[i], False)
                conv_s = conv_s[:, SUBLANES - (CONV_K - 1):]
                ssm_s_list.append(ssm_s)
            conv_p_list.append(conv_p[:, SUBLANES - (CONV_K - 1):])
            ssm_p_list.append(ssm_p.reshape(n_prompt, n_heads, SSM_HEAD_DIM, D_STATE))
            conv_s_list.append(conv_s)
            hm = mlp(hm, mlp_norm_w[layer], wu, wd, final_norm_w, final_norm=False, proj=(g_m, w_out))
            proj_p, proj_s = (g_p, w_out), (g_s, w_out)
        else:
            j = layer - n_a
            if j == 0:
                wkv = w_kv.astype(BF16)
                kvm = norm_matmul(hm, kv_norm_w, wkv)
                kv_p = norm_matmul(hp, kv_norm_w, wkv).reshape(n_prompt, seq, 2 * kvw)
                kv_s = norm_matmul(hs, kv_norm_w, wkv).reshape(n_dec, dec_seq, 2 * kvw)
            wq = w_q[j].astype(BF16)
            wo = w_o[j].astype(BF16)
            dq = wq.shape[1]
            q_p = norm_matmul(hp, b_norm_w[j], wq).reshape(n_prompt, seq, dq)
            q_s = norm_matmul(hs, b_norm_w[j], wq).reshape(n_dec, dec_seq, dq)
            o_p = attn_prompt(q_p, kv_p, kvm, attn_sinks[j])
            o_s = attn_sample(q_s, kv_s, cache_k_win.reshape(n_dec, w_buf, kvw),
                              cache_v_win.reshape(n_dec, w_buf, kvw), kvm, attn_sinks[j])
            proj_p = (o_p.reshape(n_prompt * seq, dq), wo)
            proj_s = (o_s.reshape(n_dec * dec_seq, dq), wo)
        hp = mlp(hp, mlp_norm_w[layer], wu, wd, final_norm_w, final_norm=last, proj=proj_p)
        hs = mlp(hs, mlp_norm_w[layer], wu, wd, final_norm_w, final_norm=last, proj=proj_s)

    y_prompt = hp.reshape(n_prompt, seq, d_model)
    y_sample = hs.reshape(n_dec, dec_seq, d_model)
    kv_heads = (N_KV_HEADS, ATTN_HEAD_DIM)
    k_p = kv_p[:, seq - w_buf:, :kvw].reshape((n_prompt, w_buf) + kv_heads)
    v_p = kv_p[:, seq - w_buf:, kvw:].reshape((n_prompt, w_buf) + kv_heads)
    k_s = kv_s[:, :, :kvw].reshape((n_dec, dec_seq) + kv_heads)
    v_s = kv_s[:, :, kvw:].reshape((n_dec, dec_seq) + kv_heads)
    k_s_win = jnp.concatenate([cache_k_win, k_s], axis=1)[:, -w_buf:]
    v_s_win = jnp.concatenate([cache_v_win, v_s], axis=1)[:, -w_buf:]
    if not step_path:
        ssm_s_all = jnp.stack(ssm_s_list)
    return (y_prompt, y_sample, jnp.stack(conv_p_list), jnp.stack(ssm_p_list), k_p, v_p,
            jnp.stack(conv_s_list), ssm_s_all.reshape(state_ssm.shape), k_s_win, v_s_win)
```

```python
import functools

import jax
import jax.numpy as jnp
from jax import lax
from jax.experimental import pallas as pl
from jax.experimental.pallas import tpu as pltpu

F32 = jnp.float32
BF16 = jnp.bfloat16

N_GROUPS = 8
SSM_HEAD_DIM = 64
D_STATE = 128
CONV_K = 4
SSD_CHUNK = 128
ATTN_HEAD_DIM = 64
N_KV_HEADS = 4
WINDOW = 128
PAST_LEN = 8192
EPS = 1e-5
LOG2_E = 1.4426950408889634

LANES = 128
SUBLANES = 8
VMEM_LIMIT_BYTES = 52 * 1024 * 1024

_NT = (((1,), (1,)), ((), ()))
_TN = (((0,), (0,)), ((), ()))


def _params(*sem):
    return pltpu.CompilerParams(dimension_semantics=sem, vmem_limit_bytes=VMEM_LIMIT_BYTES)


def _rms(x, w):
    ms = jnp.mean(x * x, axis=-1, keepdims=True)
    return x * lax.rsqrt(ms + EPS) * w


def _silu(x):
    s = 0.5 * x
    return s + s * jnp.tanh(s)


def _row_tile(m, cap):
    t = min(m, cap)
    assert m % t == 0, (m, t)
    return t


def _norm_matmul_kernel(x_ref, nw_ref, w_ref, o_ref, xn_ref):
    @pl.when(pl.program_id(1) == 0)
    def _():
        xn_ref[...] = _rms(x_ref[...], nw_ref[...]).astype(BF16)

    o_ref[...] = jnp.dot(xn_ref[...], w_ref[...], preferred_element_type=F32)


def norm_matmul(x, nw, w, *, tm_cap=1024, tn_cap=1024):
    m, d = x.shape
    n = w.shape[1]
    tm, tn = _row_tile(m, tm_cap), _row_tile(n, tn_cap)
    return pl.pallas_call(
        _norm_matmul_kernel,
        grid=(m // tm, n // tn),
        in_specs=[
            pl.BlockSpec((tm, d), lambda i, j: (i, 0)),
            pl.BlockSpec((1, d), lambda i, j: (0, 0)),
            pl.BlockSpec((d, tn), lambda i, j: (0, j)),
        ],
        out_specs=pl.BlockSpec((tm, tn), lambda i, j: (i, j)),
        out_shape=jax.ShapeDtypeStruct((m, n), F32),
        scratch_shapes=[pltpu.VMEM((tm, d), BF16)],
        compiler_params=_params("parallel", "arbitrary"),
        name="norm_matmul",
    )(x, nw.reshape(1, d), w)


def _in_proj_kernel(x_ref, nw_ref, w_ref, wdt_ref, o_ref, dt_ref, xn_ref):
    @pl.when(pl.program_id(1) == 0)
    def _():
        xn = _rms(x_ref[...], nw_ref[...]).astype(BF16)
        xn_ref[...] = xn
        dt_ref[...] = jnp.dot(xn, wdt_ref[...], preferred_element_type=F32)

    o_ref[...] = jnp.dot(xn_ref[...], w_ref[...], preferred_element_type=F32)


def in_proj(x, nw, w, wdt, *, d_inner, n_main, tm_cap=1024, tn_cap=1024):
    m, d = x.shape
    n = n_main
    tm, tn = _row_tile(m, tm_cap), _row_tile(n, tn_cap)
    assert d_inner % tn == 0
    return pl.pallas_call(
        _in_proj_kernel,
        grid=(m // tm, n // tn),
        in_specs=[
            pl.BlockSpec((tm, d), lambda i, j: (i, 0)),
            pl.BlockSpec((1, d), lambda i, j: (0, 0)),
            pl.BlockSpec((d, tn), lambda i, j: (0, (j + d_inner // tn) % (n // tn))),
            pl.BlockSpec((d, LANES), lambda i, j: (0, 0)),
        ],
        out_specs=[
            pl.BlockSpec((tm, tn), lambda i, j: (i, j)),
            pl.BlockSpec((tm, LANES), lambda i, j: (i, 0)),
        ],
        out_shape=[jax.ShapeDtypeStruct((m, n), F32), jax.ShapeDtypeStruct((m, LANES), F32)],
        scratch_shapes=[pltpu.VMEM((tm, d), BF16)],
        compiler_params=_params("parallel", "arbitrary"),
        name="in_proj",
    )(x, nw.reshape(1, d), w, wdt)


def _in_proj_conv_kernel(x_ref, nw_ref, w_ref, wdt_ref, cw_ref, cb_ref, conv0_ref,
                         xs_ref, bb_ref, cc_ref, sz_ref, dt_ref, tail_ref,
                         xn_ref, xpad_ref, halo_ref, *, tiles_per_seq, n_x, n_b):
    i = pl.program_id(0)
    j = pl.program_id(1)
    tm = x_ref.shape[0]
    tn = w_ref.shape[1]
    pad = SUBLANES
    sub = 2 * LANES

    @pl.when(j == 0)
    def _():
        xn = _rms(x_ref[...], nw_ref[...]).astype(BF16)
        xn_ref[...] = xn
        dt_ref[...] = jnp.dot(xn, wdt_ref[...], preferred_element_type=F32)

    def conv_tile(out_ref):
        @pl.when(i % tiles_per_seq == 0)
        def _():
            for l in range(tn // LANES):
                halo_ref[j, l] = conv0_ref[0, :, l * LANES:(l + 1) * LANES]

        def conv_act(s, raw):
            for e in range(sub // LANES):
                l = s * (sub // LANES) + e
                cs = slice(l * LANES, (l + 1) * LANES)
                xpad_ref[l, 0:pad, :] = halo_ref[j, l]
                xpad_ref[l, pad:pad + tm, :] = raw[:, e * LANES:(e + 1) * LANES]
                conv = cb_ref[:, cs]
                for k in range(CONV_K):
                    off = pad - (CONV_K - 1) + k
                    conv = conv + xpad_ref[l, off:off + tm, :] * cw_ref[k:k + 1, cs]
                out_ref[:, cs] = _silu(conv).astype(out_ref.dtype)
                last_rows = xpad_ref[l, tm:tm + pad, :]
                halo_ref[j, l] = last_rows
                tail_ref[0, :, cs] = last_rows

        _staggered(tn // sub, [matmul_cols, conv_act])

    def matmul_cols(s, _):
        return jnp.dot(xn_ref[...], w_ref[:, s * sub:(s + 1) * sub], preferred_element_type=F32)

    def gate_act(s, raw):
        sz_ref[:, s * sub:(s + 1) * sub] = _silu(raw).astype(BF16)

    pl.when(j < n_x)(lambda: conv_tile(xs_ref))
    pl.when(jnp.logical_and(j >= n_x, j < n_x + n_b))(lambda: conv_tile(bb_ref))
    pl.when(jnp.logical_and(j >= n_x + n_b, j < n_x + 2 * n_b))(lambda: conv_tile(cc_ref))
    pl.when(j >= n_x + 2 * n_b)(lambda: _staggered(tn // sub, [matmul_cols, gate_act]))


def in_proj_conv(x, nw, w, wdt, cw, cb, conv0, *, seq_len, d_inner, tm_cap=1024):
    m, d = x.shape
    conv_dim = cw.shape[1]
    bc = (conv_dim - d_inner) // 2
    tn = bc
    tm = _row_tile(seq_len, tm_cap)
    assert d_inner % tn == 0 and w.shape[1] >= conv_dim + d_inner and m % seq_len == 0
    n_x, n_b, n_z = d_inner // tn, 1, d_inner // tn
    n_conv = n_x + 2 * n_b
    conv_col = lambda i, j: (0, jnp.minimum(j, n_conv - 1))
    bf = lambda n: jax.ShapeDtypeStruct((m, n), BF16)
    tiles_per_seq = seq_len // tm
    *acts, tails = pl.pallas_call(
        functools.partial(_in_proj_conv_kernel, tiles_per_seq=tiles_per_seq, n_x=n_x, n_b=n_b),
        grid=(m // tm, n_conv + n_z),
        in_specs=[
            pl.BlockSpec((tm, d), lambda i, j: (i, 0)),
            pl.BlockSpec((1, d), lambda i, j: (0, 0)),
            pl.BlockSpec((d, tn), lambda i, j: (0, (j + n_z) % (n_conv + n_z))),
            pl.BlockSpec((d, LANES), lambda i, j: (0, 0)),
            pl.BlockSpec((CONV_K, tn), conv_col),
            pl.BlockSpec((1, tn), conv_col),
            pl.BlockSpec((1, SUBLANES, tn), lambda i, j: (0, 0, jnp.minimum(j, n_conv - 1))),
        ],
        out_specs=[
            pl.BlockSpec((tm, tn), lambda i, j: (i, jnp.minimum(j, n_x - 1))),
            pl.BlockSpec((tm, tn), lambda i, j: (i, 0)),
            pl.BlockSpec((tm, tn), lambda i, j: (i, 0)),
            pl.BlockSpec((tm, tn), lambda i, j: (i, jnp.clip(j - n_conv, 0, n_z - 1))),
            pl.BlockSpec((tm, LANES), lambda i, j: (i, 0)),
            pl.BlockSpec((1, SUBLANES, tn), lambda i, j: (i, 0, jnp.minimum(j, n_conv - 1))),
        ],
        out_shape=[jax.ShapeDtypeStruct((m, d_inner), F32), bf(bc), bf(bc), bf(d_inner),
                   jax.ShapeDtypeStruct((m, LANES), F32),
                   jax.ShapeDtypeStruct((m // tm, SUBLANES, conv_dim), F32)],
        scratch_shapes=[
            pltpu.VMEM((tm, d), BF16),
            pltpu.VMEM((tn // LANES, tm + SUBLANES, LANES), F32),
            pltpu.VMEM((n_conv, tn // LANES, SUBLANES, LANES), F32),
        ],
        compiler_params=_params("arbitrary", "arbitrary"),
        name="in_proj_conv",
    )(x, nw.reshape(1, d), w, wdt, cw, cb, conv0)
    return (*acts, tails[tiles_per_seq - 1::tiles_per_seq])


def _mlp_steps(x_rows, nw_ref, wu_ref, wd_ref, fw_ref, o_ref, xn_ref, final_norm):
    f = pl.program_id(1)
    tm = o_ref.shape[0]
    rb = min(tm, 2 * LANES)

    @pl.when(f == 0)
    def _():
        def normalise(r, x):
            rows = slice(r * rb, (r + 1) * rb)
            xn_ref[rows, :] = _rms(x, nw_ref[...]).astype(BF16)
            o_ref[rows, :] = x

        _staggered(tm // rb, [lambda r, _: x_rows(slice(r * rb, (r + 1) * rb)), normalise])

    h = jnp.dot(xn_ref[...], wu_ref[...], preferred_element_type=F32)
    h = jnp.square(jnp.maximum(h, 0.0)).astype(BF16)
    o_ref[...] += jnp.dot(h, wd_ref[...], preferred_element_type=F32)

    if final_norm:
        @pl.when(f == pl.num_programs(1) - 1)
        def _():
            o_ref[...] = _rms(o_ref[...], fw_ref[...])


def _mlp_kernel(x_ref, nw_ref, wu_ref, wd_ref, fw_ref, o_ref, xn_ref, *, final_norm):
    _mlp_steps(lambda rows: x_ref[rows, :], nw_ref, wu_ref, wd_ref, fw_ref, o_ref, xn_ref, final_norm)


def _proj_mlp_kernel(a_ref, wa_ref, x_ref, nw_ref, wu_ref, wd_ref, fw_ref, o_ref, xn_ref, *, final_norm):
    def block_input(rows):
        return x_ref[rows, :] + jnp.dot(a_ref[rows, :], wa_ref[...], preferred_element_type=F32)

    _mlp_steps(block_input, nw_ref, wu_ref, wd_ref, fw_ref, o_ref, xn_ref, final_norm)


def mlp(x, nw, wu, wd, layer, fw, *, final_norm, proj=None, tm_cap=1024, tf_cap=1024):
    m, d = x.shape
    dff = wu.shape[2]
    tm, tf = _row_tile(m, tm_cap), _row_tile(dff, tf_cap)
    in_specs = [
        pl.BlockSpec((tm, d), lambda i, f: (i, 0)),
        pl.BlockSpec((1, d), lambda i, f: (0, 0)),
        pl.BlockSpec((None, d, tf), lambda i, f: (layer, 0, f)),
        pl.BlockSpec((None, tf, d), lambda i, f: (layer, f, 0)),
        pl.BlockSpec((1, d), lambda i, f: (0, 0)),
    ]
    args = [x, nw.reshape(1, d), wu, wd, fw.reshape(1, d)]
    body = _mlp_kernel
    if proj is not None:
        a, wa = proj
        k = a.shape[1]
        in_specs = [pl.BlockSpec((tm, k), lambda i, f: (i, 0)),
                    pl.BlockSpec((k, d), lambda i, f: (0, 0), pipeline_mode=pl.Buffered(1))] + in_specs
        args = [a, wa] + args
        body = _proj_mlp_kernel
    return pl.pallas_call(
        functools.partial(body, final_norm=final_norm),
        grid=(m // tm, dff // tf),
        in_specs=in_specs,
        out_specs=pl.BlockSpec((tm, d), lambda i, f: (i, 0)),
        out_shape=jax.ShapeDtypeStruct((m, d), F32),
        scratch_shapes=[pltpu.VMEM((tm, d), BF16)],
        compiler_params=_params("parallel", "arbitrary"),
        name="mlp" if proj is None else "proj_mlp",
    )(*args)


def _softplus(x):
    return jnp.maximum(x, 0.0) + jnp.log(1.0 + jnp.exp(-jnp.abs(x)))


def _split3(a):
    hi = a.astype(BF16)
    r1 = a - hi.astype(F32)
    mid = r1.astype(BF16)
    lo = (r1 - mid.astype(F32)).astype(BF16)
    return hi, mid, lo


def _ssd_kernel(xbc_ref, z_ref, dt_ref, conv0_ref, ssm0_ref,
                cw_ref, cb_ref, dtb_ref, alog_ref, dsk_ref, gw_ref,
                g_ref, convo_ref, ssmo_ref,
                xpad_ref, act_ref, state_ref, y_ref, *, q, n_heads):
    c = pl.program_id(1)
    p = SSM_HEAD_DIM
    d_inner = n_heads * p
    hpg = n_heads // N_GROUPS
    conv_dim = d_inner + 2 * N_GROUPS * D_STATE
    pad = SUBLANES

    @pl.when(c == 0)
    def _():
        xpad_ref[0:pad, :] = conv0_ref[0]
        state_ref[...] = ssm0_ref[0]

    xpad_ref[pad:pad + q, :] = xbc_ref[0]
    cblk = 512
    for j in range(conv_dim // cblk):
        cs = slice(j * cblk, (j + 1) * cblk)
        conv = cb_ref[:, cs]
        for k in range(CONV_K):
            off = pad - (CONV_K - 1) + k
            conv = conv + xpad_ref[off:off + q, cs] * cw_ref[k:k + 1, cs]
        act_ref[:, cs] = _silu(conv)
    xpad_ref[0:pad, :] = xpad_ref[q:q + pad, :]

    dt = _softplus(dt_ref[0] + dtb_ref[...])
    a = dt * (-jnp.exp(alog_ref[...]))
    row = lax.broadcasted_iota(jnp.int32, (q, q), 0)
    col = lax.broadcasted_iota(jnp.int32, (q, q), 1)
    causal = row >= col
    tri = jnp.where(causal, 1.0, 0.0).astype(BF16)
    acs = None
    for part in _split3(a):
        t = jnp.dot(tri, part, preferred_element_type=F32)
        acs = t if acs is None else acs + t
    if q < LANES:
        acs_sq = jnp.concatenate([acs, jnp.zeros((LANES - q, LANES), F32)], axis=0)
    else:
        acs_sq = acs
    acs_t = acs_sq.T
    eacs = jnp.exp(acs)
    last = acs[q - 1:q, :]
    dte = jnp.exp(last - acs)
    cdecay = jnp.exp(last)

    for g in range(N_GROUPS):
        b0 = d_inner + g * D_STATE
        c0 = d_inner + N_GROUPS * D_STATE + g * D_STATE
        bb = act_ref[:, b0:b0 + D_STATE].astype(BF16)
        cc = act_ref[:, c0:c0 + D_STATE].astype(BF16)
        cbm = lax.dot_general(cc, bb, _NT, preferred_element_type=F32)
        for pair in range(hpg // 2):
            xs2 = act_ref[:, (g * hpg + 2 * pair) * p:(g * hpg + 2 * pair + 2) * p]
            dsk2 = dsk_ref[:, (g * hpg + 2 * pair) * p:(g * hpg + 2 * pair + 2) * p]
            ys = []
            for e in range(2):
                h = g * hpg + 2 * pair + e
                xs = xs2[:, e * p:(e + 1) * p]
                seg = acs[:, h:h + 1] - acs_t[h:h + 1, 0:q]
                decay = jnp.exp(jnp.where(causal, seg, -jnp.inf))
                m = (cbm * decay).astype(BF16)
                xdt = xs * dt[:, h:h + 1]
                y = jnp.dot(m, xdt.astype(BF16), preferred_element_type=F32)
                st = state_ref[h * p:(h + 1) * p, :]
                yoff = lax.dot_general(cc, st.astype(BF16), _NT, preferred_element_type=F32)
                y = y + yoff * eacs[:, h:h + 1] + dsk2[:, e * p:(e + 1) * p] * xs
                ys.append(y)
                xdtw = (xdt * dte[:, h:h + 1]).astype(BF16)
                snew = lax.dot_general(xdtw, bb, _TN, preferred_element_type=F32)
                state_ref[h * p:(h + 1) * p, :] = (
                    jnp.broadcast_to(cdecay[:, h:h + 1], (p, D_STATE)) * st + snew)
            y_ref[:, (g * hpg + 2 * pair) * p:(g * hpg + 2 * pair + 2) * p] = (
                jnp.concatenate(ys, axis=1))

    gsz = d_inner // N_GROUPS
    for g in range(N_GROUPS):
        gs = slice(g * gsz, (g + 1) * gsz)
        gg = y_ref[:, gs] * _silu(z_ref[0, :, gs])
        ms = jnp.mean(gg * gg, axis=-1, keepdims=True)
        g_ref[0, :, gs] = (gg * lax.rsqrt(ms + EPS) * gw_ref[:, gs]).astype(BF16)

    @pl.when(c == pl.num_programs(1) - 1)
    def _():
        convo_ref[0] = xpad_ref[0:pad, :]
        ssmo_ref[0] = state_ref[...]


def ssd_mixer(xbcz, dt, conv0, ssm0, cw, cb, dtb, alog, dsk, gw, *, shared_state):
    bsz, length, _ = xbcz.shape
    conv_dim = cw.shape[1]
    d_inner = gw.shape[1]
    n_heads = d_inner // SSM_HEAD_DIM
    q = SSD_CHUNK if length % SSD_CHUNK == 0 else length
    assert q % SUBLANES == 0 and q >= SUBLANES and conv_dim % d_inner == 0
    nc = length // q
    zblk = conv_dim // d_inner
    if shared_state:
        st_map = lambda b, c: (0, 0, 0)
    else:
        st_map = lambda b, c: (b, 0, 0)
    const = lambda b, c: (0, 0)
    return pl.pallas_call(
        functools.partial(_ssd_kernel, q=q, n_heads=n_heads),
        grid=(bsz, nc),
        in_specs=[
            pl.BlockSpec((1, q, conv_dim), lambda b, c: (b, c, 0)),
            pl.BlockSpec((1, q, d_inner), lambda b, c: (b, c, zblk)),
            pl.BlockSpec((1, q, LANES), lambda b, c: (b, c, 0)),
            pl.BlockSpec((1, SUBLANES, conv_dim), st_map),
            pl.BlockSpec((1, n_heads * SSM_HEAD_DIM, D_STATE), st_map),
            pl.BlockSpec((CONV_K, conv_dim), const),
            pl.BlockSpec((1, conv_dim), const),
            pl.BlockSpec((1, LANES), const),
            pl.BlockSpec((1, LANES), const),
            pl.BlockSpec((1, d_inner), const),
            pl.BlockSpec((1, d_inner), const),
        ],
        out_specs=[
            pl.BlockSpec((1, q, d_inner), lambda b, c: (b, c, 0)),
            pl.BlockSpec((1, SUBLANES, conv_dim), lambda b, c: (b, 0, 0)),
            pl.BlockSpec((1, n_heads * SSM_HEAD_DIM, D_STATE), lambda b, c: (b, 0, 0)),
        ],
        out_shape=[
            jax.ShapeDtypeStruct((bsz, length, d_inner), BF16),
            jax.ShapeDtypeStruct((bsz, SUBLANES, conv_dim), F32),
            jax.ShapeDtypeStruct((bsz, n_heads * SSM_HEAD_DIM, D_STATE), F32),
        ],
        scratch_shapes=[
            pltpu.VMEM((q + SUBLANES, conv_dim), F32),
            pltpu.VMEM((q, conv_dim), F32),
            pltpu.VMEM((n_heads * SSM_HEAD_DIM, D_STATE), F32),
            pltpu.VMEM((q, d_inner), F32),
        ],
        compiler_params=_params("parallel", "arbitrary"),
        name="ssd_mixer",
    )(xbcz, xbcz, dt, conv0, ssm0, cw, cb, dtb, alog, dsk, gw)


def _expansion_matrix(n_heads):
    h = jnp.arange(LANES)[:, None]
    ex = h == jnp.arange(n_heads * SSM_HEAD_DIM)[None, :] // SSM_HEAD_DIM
    return jnp.tile(ex.astype(BF16), (3, 1))


def _store_act_tile(j, act, xs_ref, bb_ref, cc_ref, d_inner, bc):
    lo = j * LANES
    if lo < d_inner:
        xs_ref[:, lo:lo + LANES] = act
    elif lo < d_inner + bc:
        bb_ref[:, lo - d_inner:lo - d_inner + LANES] = act.astype(BF16)
    else:
        cc_ref[:, lo - d_inner - bc:lo - d_inner - bc + LANES] = act.astype(BF16)


def _ssd_token_math(dt_raw, dtb_ref, alog_ref, lhs3_ref, acst_ref, *, seg_len):
    q = dt_raw.shape[0]
    dt = _softplus(dt_raw + dtb_ref[...])
    a = dt * (-jnp.exp(alog_ref[...]))
    row = lax.broadcasted_iota(jnp.int32, (q, q), 0)
    col = lax.broadcasted_iota(jnp.int32, (q, q), 1)
    mask = row >= col
    if seg_len != q:
        mask = jnp.logical_and(mask, row // seg_len == col // seg_len)
        seg_end = (row // seg_len) * seg_len + (seg_len - 1)
    tri = jnp.where(mask, 1.0, 0.0).astype(BF16)
    acs = None
    for part in _split3(a):
        t = jnp.dot(tri, part, preferred_element_type=F32)
        acs = t if acs is None else acs + t
    if seg_len == q:
        last = acs[q - 1:q, :]
    else:
        sel = jnp.where(col == seg_end, 1.0, 0.0).astype(BF16)
        last = None
        for part in _split3(acs):
            t = jnp.dot(sel, part, preferred_element_type=F32)
            last = t if last is None else last + t
    acst_ref[...] = (acs * LOG2_E).T
    eacs = jnp.exp(acs)
    stack = jnp.concatenate([dt, eacs, jnp.exp(last - acs)], axis=0)
    lhs3_ref[...] = jnp.concatenate(_split3(stack), axis=1)
    return mask, eacs


def _staggered(n, stages):
    carried = {}
    for t in range(n + len(stages) - 1):
        for k, stage in enumerate(stages):
            g = t - k
            if 0 <= g < n:
                carried[g] = stage(g, carried.get(g))


def _ssd_group_inputs(g, xs_ref, bb_ref, cc_ref, lhs3_ref, ex3_ref, hpg):
    q = xs_ref.shape[0]
    gch = hpg * SSM_HEAD_DIM
    gs = slice(g * gch, (g + 1) * gch)
    bb = bb_ref[:, g * D_STATE:(g + 1) * D_STATE]
    cc = cc_ref[:, g * D_STATE:(g + 1) * D_STATE]
    cbm = lax.dot_general(cc, bb, _NT, preferred_element_type=F32)
    ex = jnp.dot(lhs3_ref[...], ex3_ref[:, gs], preferred_element_type=F32)
    xs = xs_ref[:, gs]
    return dict(bb=bb, cc=cc, cbm=cbm, xs=xs, xdt=xs * ex[0:q], eax=ex[q:2 * q], dtex=ex[2 * q:3 * q])


def _ssd_group_diag(g, v, mask, acst_ref, hpg):
    q = v["xs"].shape[0]
    p = SSM_HEAD_DIM
    lane_head = lax.broadcasted_iota(jnp.int32, (q, hpg * p), 1) // p
    ms, rhs = [], []
    for r in range(hpg):
        h = g * hpg + r
        rowb = jnp.broadcast_to(acst_ref[h:h + 1, :], (q, q))
        decay = jnp.exp2(jnp.where(mask, rowb.T - rowb, -jnp.inf))
        ms.append((v["cbm"] * decay).astype(BF16))
        rhs.append(jnp.where(lane_head == r, v["xdt"], 0.0).astype(BF16))
    y = jnp.dot(jnp.concatenate(ms, axis=1), jnp.concatenate(rhs, axis=0),
                preferred_element_type=F32)
    return dict(v, y=y, cbm=None)


def _ssd_chunk_kernel(xs_ref, bb_ref, cc_ref, sz_ref, dt_ref, ssm0_ref,
                      dtb_ref, alog_ref, dsk_ref, gw_ref, ex3_ref,
                      g_ref, ssmo_ref,
                      st_ref, lhs3_ref, acst_ref, *, n_heads, cps):
    c = pl.program_id(1)
    q = SSD_CHUNK
    hpg = n_heads // N_GROUPS
    gch = hpg * SSM_HEAD_DIM

    @pl.when(c == 0)
    def _():
        st_ref[...] = ssm0_ref[0].T

    rows = [pl.ds(h * q, q) for h in range(cps)]
    masks = [_ssd_token_math(dt_ref[0, rows[h], :], dtb_ref, alog_ref, lhs3_ref.at[h], acst_ref.at[h],
                             seg_len=q)[0] for h in range(cps)]

    def finish(item, v):
        h, g = divmod(item, N_GROUPS)
        gs = slice(g * gch, (g + 1) * gch)
        st = st_ref[:, gs]
        y = (v["y"] + jnp.dot(v["cc"], st.astype(BF16), preferred_element_type=F32) * v["eax"]
             + dsk_ref[:, gs] * v["xs"])
        gg = y * sz_ref[0, rows[h], gs].astype(F32)
        msq = jnp.mean(gg * gg, axis=-1, keepdims=True)
        g_ref[0, rows[h], gs] = (gg * lax.rsqrt(msq + EPS) * gw_ref[:, gs]).astype(BF16)
        xdtw = (v["xdt"] * v["dtex"]).astype(BF16)
        snew = lax.dot_general(v["bb"], xdtw, _TN, preferred_element_type=F32)
        st_ref[:, gs] = st * v["eax"][q - 1:q, :] + snew

    def inputs(item, _):
        h, g = divmod(item, N_GROUPS)
        return _ssd_group_inputs(g, xs_ref.at[0, rows[h]], bb_ref.at[0, rows[h]], cc_ref.at[0, rows[h]],
                                 lhs3_ref.at[h], ex3_ref, hpg)

    def diag(item, v):
        h, g = divmod(item, N_GROUPS)
        return _ssd_group_diag(g, v, masks[h], acst_ref.at[h], hpg)

    _staggered(cps * N_GROUPS, [inputs, diag, finish])

    @pl.when(c == pl.num_programs(1) - 1)
    def _():
        ssmo_ref[0] = st_ref[...].T


def ssd_chunked(xs, bb, cc, sz, dt, ssm0, dtb, alog, dsk, gw, *, chunks_per_step=4):
    bsz, length, d_inner = xs.shape
    bc = bb.shape[2]
    n_heads = d_inner // SSM_HEAD_DIM
    q = SSD_CHUNK
    assert length % q == 0 and bc == N_GROUPS * D_STATE and n_heads % N_GROUPS == 0
    cps = _row_tile(length // q, chunks_per_step)
    qs = cps * q
    ex3 = _expansion_matrix(n_heads)
    rows = lambda b, c: (b, c, 0)
    const = lambda b, c: (0, 0)
    return pl.pallas_call(
        functools.partial(_ssd_chunk_kernel, n_heads=n_heads, cps=cps),
        grid=(bsz, length // qs),
        in_specs=[
            pl.BlockSpec((1, qs, d_inner), rows),
            pl.BlockSpec((1, qs, bc), rows),
            pl.BlockSpec((1, qs, bc), rows),
            pl.BlockSpec((1, qs, d_inner), rows),
            pl.BlockSpec((1, qs, LANES), rows),
            pl.BlockSpec((1, d_inner, D_STATE), lambda b, c: (0, 0, 0)),
            pl.BlockSpec((1, LANES), const),
            pl.BlockSpec((1, LANES), const),
            pl.BlockSpec((1, d_inner), const),
            pl.BlockSpec((1, d_inner), const),
            pl.BlockSpec(ex3.shape, const),
        ],
        out_specs=[
            pl.BlockSpec((1, qs, d_inner), rows),
            pl.BlockSpec((1, d_inner, D_STATE), lambda b, c: (b, 0, 0)),
        ],
        out_shape=[
            jax.ShapeDtypeStruct((bsz, length, d_inner), BF16),
            jax.ShapeDtypeStruct((bsz, d_inner, D_STATE), F32),
        ],
        scratch_shapes=[
            pltpu.VMEM((D_STATE, d_inner), F32),
            pltpu.VMEM((cps, 3 * q, 3 * LANES), BF16),
            pltpu.VMEM((cps, LANES, q), F32),
        ],
        compiler_params=_params("parallel", "arbitrary"),
        name="ssd_chunked",
    )(xs, bb, cc, sz, dt, ssm0, dtb, alog, dsk, gw, ex3)


def _ssd_step_tokens_kernel(xbc_ref, dt_ref, conv0_ref, cw_ref, cb_ref, dtb_ref, alog_ref, dsk_ref,
                            ex3_ref,
                            ypart_ref, eaxo_ref, xdtw_ref, bbo_ref, cco_ref, eacs_ref,
                            xpad_ref, xs_ref, bb_ref, cc_ref, lhs3_ref, acst_ref, *, n_heads, t):
    q = SSD_CHUNK
    p = SSM_HEAD_DIM
    d_inner = n_heads * p
    hpg = n_heads // N_GROUPS
    gch = hpg * p
    bc = N_GROUPS * D_STATE
    ntile = (d_inner + 2 * bc) // LANES
    nseq = q // t
    slot = 2 * t
    span = nseq * slot - t

    for j in range(ntile):
        cs = slice(j * LANES, (j + 1) * LANES)
        for i in range(nseq):
            xpad_ref[j, i * slot:i * slot + t, :] = conv0_ref[i, :, cs]
            xpad_ref[j, i * slot + t:(i + 1) * slot, :] = xbc_ref[i * t:(i + 1) * t, cs]
        conv = cb_ref[:, cs]
        for k in range(CONV_K):
            off = t - (CONV_K - 1) + k
            conv = conv + xpad_ref[j, off:off + span, :] * cw_ref[k:k + 1, cs]
        act = _silu(jnp.concatenate([conv[i * slot:i * slot + t] for i in range(nseq)], axis=0))
        _store_act_tile(j, act, xs_ref, bb_ref, cc_ref, d_inner, bc)
        lo = j * LANES
        if d_inner <= lo < d_inner + bc:
            bbo_ref[:, lo - d_inner:lo - d_inner + LANES] = act
        elif lo >= d_inner + bc:
            cco_ref[:, lo - d_inner - bc:lo - d_inner - bc + LANES] = act

    mask, eacs = _ssd_token_math(dt_ref[...], dtb_ref, alog_ref, lhs3_ref, acst_ref, seg_len=t)
    eacs_ref[...] = eacs
    def emit(g, v):
        gs = slice(g * gch, (g + 1) * gch)
        ypart_ref[:, gs] = v["y"] + dsk_ref[:, gs] * v["xs"]
        eaxo_ref[:, gs] = v["eax"]
        xdtw_ref[:, gs] = v["xdt"] * v["dtex"]

    _staggered(N_GROUPS, [
        lambda g, _: _ssd_group_inputs(g, xs_ref, bb_ref, cc_ref, lhs3_ref, ex3_ref, hpg),
        lambda g, v: _ssd_group_diag(g, v, mask, acst_ref, hpg),
        emit,
    ])


def ssd_step_tokens(xbcz, dt, conv0, cw, cb, dtb, alog, dsk, *, t):
    n_tok = xbcz.shape[0]
    conv_dim = cw.shape[1]
    d_inner = dsk.shape[1]
    n_heads = d_inner // SSM_HEAD_DIM
    q = SSD_CHUNK
    bc = N_GROUPS * D_STATE
    assert t == SUBLANES and n_tok % q == 0
    ex3 = _expansion_matrix(n_heads)
    const = lambda s: (0, 0)
    rows = lambda s: (s, 0)
    f32 = lambda n: jax.ShapeDtypeStruct((n_tok, n), F32)
    return pl.pallas_call(
        functools.partial(_ssd_step_tokens_kernel, n_heads=n_heads, t=t),
        grid=(n_tok // q,),
        in_specs=[
            pl.BlockSpec((q, conv_dim), rows),
            pl.BlockSpec((q, LANES), rows),
            pl.BlockSpec((q // t, SUBLANES, conv_dim), lambda s: (s, 0, 0)),
            pl.BlockSpec((CONV_K, conv_dim), const),
            pl.BlockSpec((1, conv_dim), const),
            pl.BlockSpec((1, LANES), const),
            pl.BlockSpec((1, LANES), const),
            pl.BlockSpec((1, d_inner), const),
            pl.BlockSpec(ex3.shape, const),
        ],
        out_specs=[
            pl.BlockSpec((q, d_inner), rows),
            pl.BlockSpec((q, d_inner), rows),
            pl.BlockSpec((q, d_inner), rows),
            pl.BlockSpec((q, bc), rows),
            pl.BlockSpec((q, bc), rows),
            pl.BlockSpec((q, LANES), rows),
        ],
        out_shape=[f32(d_inner), f32(d_inner), f32(d_inner), f32(bc), f32(bc), f32(LANES)],
        scratch_shapes=[
            pltpu.VMEM((conv_dim // LANES, 2 * q, LANES), F32),
            pltpu.VMEM((q, d_inner), F32),
            pltpu.VMEM((q, bc), BF16),
            pltpu.VMEM((q, bc), BF16),
            pltpu.VMEM((3 * q, 3 * LANES), BF16),
            pltpu.VMEM((LANES, q), F32),
        ],
        compiler_params=_params("parallel"),
        name="ssd_step_tokens",
    )(xbcz, dt, conv0, cw, cb, dtb, alog, dsk, ex3)


def _ssd_step_state_kernel(cd_ref, st_ref, cc_ref, bb_ref, xdtw_ref, eax_ref, ypart_ref, z_ref, gw_ref,
                           *rest, n_heads, t, nseq, n_fill):
    g_ref, sto_ref = rest[-2:]
    s = pl.program_id(1)
    p = SSM_HEAD_DIM
    hpg = n_heads // N_GROUPS
    gch = hpg * p

    def update():
        for g in range(N_GROUPS):
            gs = slice(g * gch, (g + 1) * gch)
            ns = slice(g * D_STATE, (g + 1) * D_STATE)
            ys = []
            for i in range(nseq):
                rows = slice(i * t, (i + 1) * t)
                h0 = st_ref[0, i, gs, :]
                yoff = lax.dot_general(cc_ref[rows, ns].astype(BF16), h0.astype(BF16), _NT,
                                       preferred_element_type=F32)
                ys.append(ypart_ref[rows, gs] + yoff * eax_ref[rows, gs])
                upd = lax.dot_general(xdtw_ref[rows, gs].astype(BF16), bb_ref[rows, ns].astype(BF16), _TN,
                                      preferred_element_type=F32)
                for r in range(hpg):
                    hs = slice(r * p, (r + 1) * p)
                    sto_ref[0, i, g * gch + r * p:g * gch + (r + 1) * p, :] = (
                        h0[hs] * cd_ref[s * nseq + i, g * hpg + r] + upd[hs])
            gg = jnp.concatenate(ys, axis=0) * _silu(z_ref[:, gs])
            msq = jnp.mean(gg * gg, axis=-1, keepdims=True)
            g_ref[:, gs] = (gg * lax.rsqrt(msq + EPS) * gw_ref[:, gs]).astype(BF16)

    if n_fill == 0:
        update()
    else:
        @pl.when(pl.program_id(0) < n_fill)
        def _():
            sto_ref[...] = jnp.zeros(sto_ref.shape, F32)

        pl.when(pl.program_id(0) == n_fill)(update)


def ssd_step_state(cd, state_all, layer, cc, bb, xdtw, eax, ypart, xbcz, gw, stacked_out, *, t, seqs_per_step=4):
    n_layers, n_seq, d_inner, d_state = state_all.shape
    n_tok = ypart.shape[0]
    bc = cc.shape[1]
    conv_dim = xbcz.shape[1] - d_inner
    nseq = _row_tile(n_seq, seqs_per_step)
    rt = nseq * t
    n_fill = n_layers - 1 if stacked_out is None else 0

    def active(l, s):
        return jnp.where(l == n_fill, s, 0)

    def out_layer(l):
        return jnp.where(l < n_fill, l + jnp.where(l >= layer, 1, 0), layer)

    rows = lambda l, s, cd_: (active(l, s), 0)
    st_map = lambda l, s, cd_: (layer, active(l, s), 0, 0)
    in_specs = [
        pl.BlockSpec((1, nseq, d_inner, d_state), st_map),
        pl.BlockSpec((rt, bc), rows),
        pl.BlockSpec((rt, bc), rows),
        pl.BlockSpec((rt, d_inner), rows),
        pl.BlockSpec((rt, d_inner), rows),
        pl.BlockSpec((rt, d_inner), rows),
        pl.BlockSpec((rt, d_inner), lambda l, s, cd_: (active(l, s), conv_dim // d_inner)),
        pl.BlockSpec((1, d_inner), lambda l, s, cd_: (0, 0)),
    ]
    args = [cd, state_all, cc, bb, xdtw, eax, ypart, xbcz, gw]
    aliases = {}
    if stacked_out is not None:
        in_specs.append(pl.BlockSpec(memory_space=pl.ANY))
        aliases = {len(args): 1}
        args.append(stacked_out)
    return pl.pallas_call(
        functools.partial(_ssd_step_state_kernel, n_heads=d_inner // SSM_HEAD_DIM, t=t, nseq=nseq,
                          n_fill=n_fill),
        grid_spec=pltpu.PrefetchScalarGridSpec(
            num_scalar_prefetch=1,
            grid=(n_fill + 1, n_seq // nseq),
            in_specs=in_specs,
            out_specs=[
                pl.BlockSpec((rt, d_inner), rows),
                pl.BlockSpec((1, nseq, d_inner, d_state), lambda l, s, cd_: (out_layer(l), s, 0, 0)),
            ],
        ),
        out_shape=[
            jax.ShapeDtypeStruct((n_tok, d_inner), BF16),
            jax.ShapeDtypeStruct((n_layers, n_seq, d_inner, d_state), F32),
        ],
        input_output_aliases=aliases,
        compiler_params=_params("arbitrary", "arbitrary"),
        name="ssd_step_state",
    )(*args)


def _attn_prompt_kernel(sink_ref, q_ref, kvc_ref, kvp_ref, kvm_ref, o_ref, *, n_q_heads):
    n = pl.program_id(1)
    d = ATTN_HEAD_DIM
    kvw = N_KV_HEADS * d
    rep = n_q_heads // N_KV_HEADS
    scale = d ** -0.5
    w = q_ref.shape[1]
    rows = rep * w
    qi = lax.broadcasted_iota(jnp.int32, (rows, w), 0) % w
    ci = lax.broadcasted_iota(jnp.int32, (rows, w), 1)
    from_prev = ci > qi
    no_prev = jnp.where(n > 0, 0.0, -jnp.inf)
    kvm, kvp, kvc = kvm_ref[...], kvp_ref[0], kvc_ref[0]
    outs = []
    for k in range(N_KV_HEADS):
        ks, vs = slice(k * d, (k + 1) * d), slice(kvw + k * d, kvw + (k + 1) * d)
        qk = q_ref[0, :, k * rep * d:(k + 1) * rep * d]
        q4 = jnp.concatenate([qk[:, r * d:(r + 1) * d] for r in range(rep)], axis=0)
        q4 = (q4 * scale).astype(BF16)
        k2 = jnp.concatenate([kvp[:, ks], kvc[:, ks]], axis=0).astype(BF16)
        v2 = jnp.concatenate([kvp[:, vs], kvc[:, vs]], axis=0).astype(BF16)
        s2 = lax.dot_general(q4, k2, _NT, preferred_element_type=F32)
        s_w = jnp.where(from_prev, s2[:, :w] + no_prev, s2[:, w:])
        s_m = lax.dot_general(q4, kvm[:, ks].astype(BF16), _NT, preferred_element_type=F32)
        p_ws, p_ms, dens = [], [], []
        for r in range(rep):
            sl = slice(r * w, (r + 1) * w)
            sink = sink_ref[k * rep + r]
            mx = jnp.maximum(jnp.maximum(jnp.max(s_w[sl], axis=-1, keepdims=True),
                                         jnp.max(s_m[sl], axis=-1, keepdims=True)), sink)
            p_w = jnp.exp(s_w[sl] - mx)
            p_m = jnp.exp(s_m[sl] - mx)
            dens.append(jnp.exp(sink - mx) + jnp.sum(p_w, axis=-1, keepdims=True)
                        + jnp.sum(p_m, axis=-1, keepdims=True))
            p_ws.append(p_w)
            p_ms.append(p_m.astype(BF16))
        p_w = jnp.concatenate(p_ws, axis=0)
        p2 = jnp.concatenate([jnp.where(from_prev, p_w, 0.0), jnp.where(from_prev, 0.0, p_w)],
                             axis=1).astype(BF16)
        o4 = (jnp.dot(p2, v2, preferred_element_type=F32)
              + jnp.dot(jnp.concatenate(p_ms, axis=0), kvm[:, vs].astype(BF16), preferred_element_type=F32))
        outs.extend(o4[r * w:(r + 1) * w, :] / dens[r] for r in range(rep))
    o_ref[0] = jnp.concatenate(outs, axis=1).astype(BF16)


def attn_prompt(q, kv, kvm, sinks):
    bsz, s, dq = q.shape
    kvd = kv.shape[2]
    nb = s // WINDOW
    return pl.pallas_call(
        functools.partial(_attn_prompt_kernel, n_q_heads=dq // ATTN_HEAD_DIM),
        grid=(bsz, nb),
        in_specs=[
            pl.BlockSpec(memory_space=pltpu.SMEM),
            pl.BlockSpec((1, WINDOW, dq), lambda b, n: (b, n, 0)),
            pl.BlockSpec((1, WINDOW, kvd), lambda b, n: (b, n, 0)),
            pl.BlockSpec((1, WINDOW, kvd), lambda b, n: (b, jnp.maximum(n - 1, 0), 0)),
            pl.BlockSpec(kvm.shape, lambda b, n: (0, 0)),
        ],
        out_specs=pl.BlockSpec((1, WINDOW, dq), lambda b, n: (b, n, 0)),
        out_shape=jax.ShapeDtypeStruct((bsz, s, dq), BF16),
        compiler_params=_params("parallel", "arbitrary"),
        name="attn_prompt",
    )(sinks, q, kv, kv, kvm)


def _attn_sample_kernel(sink_ref, q_ref, kvn_ref, ck_ref, cv_ref, kvm_ref, o_ref, *, n_q_heads, n_meta):
    d = ATTN_HEAD_DIM
    kvw = N_KV_HEADS * d
    rep = n_q_heads // N_KV_HEADS
    scale = d ** -0.5
    nseq, t, _ = q_ref.shape
    w_buf = ck_ref.shape[1]
    rows = rep * t
    nx = n_meta + t
    tq_buf = lax.broadcasted_iota(jnp.int32, (rows, w_buf), 0) % t
    pos_buf = PAST_LEN - w_buf + lax.broadcasted_iota(jnp.int32, (rows, w_buf), 1)
    buf_mask = jnp.logical_and(PAST_LEN + tq_buf - pos_buf < WINDOW, pos_buf >= n_meta)
    tq_x = lax.broadcasted_iota(jnp.int32, (rows, nx), 0) % t
    c_x = lax.broadcasted_iota(jnp.int32, (rows, nx), 1) - n_meta
    x_mask = jnp.logical_or(c_x < 0, jnp.logical_and(c_x <= tq_x, tq_x - c_x < WINDOW))
    head_in_group = lax.broadcasted_iota(jnp.int32, (rows, 1), 0) // t
    kvm = kvm_ref[...]
    sinks = []
    for k in range(N_KV_HEADS):
        sink = jnp.zeros((rows, 1), F32)
        for r in range(rep):
            sink = jnp.where(head_in_group == r, sink_ref[k * rep + r], sink)
        sinks.append(sink)
    chains = [(i, k) for i in range(nseq) for k in range(N_KV_HEADS)]

    scores, values = {}, {}
    for i in range(nseq):
        q, kvn, ck, cv = q_ref[i], kvn_ref[i], ck_ref[i], cv_ref[i]
        for k in range(N_KV_HEADS):
            ks, vs = slice(k * d, (k + 1) * d), slice(kvw + k * d, kvw + (k + 1) * d)
            q4 = jnp.concatenate([q[:, (k * rep + r) * d:(k * rep + r + 1) * d] for r in range(rep)], axis=0)
            q4 = (q4 * scale).astype(BF16)
            kx = jnp.concatenate([kvm[:, ks], kvn[:, ks]], axis=0).astype(BF16)
            vx = jnp.concatenate([kvm[:, vs], kvn[:, vs]], axis=0).astype(BF16)
            s_b = lax.dot_general(q4, ck[:, ks].astype(BF16), _NT, preferred_element_type=F32)
            s_x = lax.dot_general(q4, kx, _NT, preferred_element_type=F32)
            scores[i, k] = (jnp.where(buf_mask, s_b, -jnp.inf), jnp.where(x_mask, s_x, -jnp.inf))
            values[i, k] = (cv[:, ks].astype(BF16), vx)

    maxes = {}
    for c in chains:
        s_b, s_x = scores[c]
        maxes[c] = jnp.maximum(sinks[c[1]], jnp.maximum(jnp.max(s_b, axis=-1, keepdims=True),
                                                         jnp.max(s_x, axis=-1, keepdims=True)))
    probs, dens = {}, {}
    for c in chains:
        s_b, s_x = scores[c]
        p_b, p_x = jnp.exp(s_b - maxes[c]), jnp.exp(s_x - maxes[c])
        probs[c] = (p_b.astype(BF16), p_x.astype(BF16))
        dens[c] = (jnp.exp(sinks[c[1]] - maxes[c]) + jnp.sum(p_b, axis=-1, keepdims=True)
                   + jnp.sum(p_x, axis=-1, keepdims=True))
    outs = {}
    for c in chains:
        outs[c] = (jnp.dot(probs[c][0], values[c][0], preferred_element_type=F32)
                   + jnp.dot(probs[c][1], values[c][1], preferred_element_type=F32)) / dens[c]
    for i in range(nseq):
        heads = [outs[i, k][r * t:(r + 1) * t, :] for k in range(N_KV_HEADS) for r in range(rep)]
        o_ref[i] = jnp.concatenate(heads, axis=1).astype(BF16)


def attn_sample(q, kvn, ck, cv, kvm, sinks, *, seqs_per_step=8):
    bsz, t, dq = q.shape
    kvd = kvn.shape[2]
    w_buf = ck.shape[1]
    g = _row_tile(bsz, seqs_per_step)
    return pl.pallas_call(
        functools.partial(_attn_sample_kernel, n_q_heads=dq // ATTN_HEAD_DIM, n_meta=kvm.shape[0]),
        grid=(bsz // g,),
        in_specs=[
            pl.BlockSpec(memory_space=pltpu.SMEM),
            pl.BlockSpec((g, t, dq), lambda b: (b, 0, 0)),
            pl.BlockSpec((g, t, kvd), lambda b: (b, 0, 0)),
            pl.BlockSpec((g, w_buf, kvd // 2), lambda b: (b, 0, 0)),
            pl.BlockSpec((g, w_buf, kvd // 2), lambda b: (b, 0, 0)),
            pl.BlockSpec(kvm.shape, lambda b: (0, 0)),
        ],
        out_specs=pl.BlockSpec((g, t, dq), lambda b: (b, 0, 0)),
        out_shape=jax.ShapeDtypeStruct((bsz, t, dq), BF16),
        compiler_params=_params("parallel"),
        name="attn_sample",
    )(sinks, q, kvn, ck, cv, kvm)


def kernel(x_prompt, x_sample, state_conv, state_ssm, cache_k_win, cache_v_win, meta_tokens, a_norm_w, a_in_proj, a_conv_w, a_conv_b, a_dt_bias, a_log, a_d_skip, a_gate_norm_w, a_out_proj, kv_norm_w, w_kv, b_norm_w, w_q, attn_sinks, w_o, mlp_norm_w, w_up, w_down, final_norm_w):
    n_prompt, seq, d_model = x_prompt.shape
    n_dec, dec_seq, _ = x_sample.shape
    n_a = a_in_proj.shape[0]
    depth = w_up.shape[0]
    n_meta = meta_tokens.shape[0]
    d_inner = a_out_proj.shape[1]
    conv_dim = a_conv_w.shape[2]
    n_heads = a_log.shape[1]
    w_buf = cache_k_win.shape[1]
    kvw = N_KV_HEADS * ATTN_HEAD_DIM
    assert n_heads * SSM_HEAD_DIM == d_inner and n_heads <= LANES

    hm = meta_tokens.astype(F32)
    hp = x_prompt.reshape(n_prompt * seq, d_model)
    hs = x_sample.reshape(n_dec * dec_seq, d_model)

    def pad_lanes(v):
        return jnp.pad(v, (0, LANES - v.shape[0])).reshape(1, LANES)

    def pad_conv_state(s):
        return jnp.pad(s, ((0, 0), (SUBLANES - (CONV_K - 1), 0), (0, 0)))

    conv_p_list, ssm_p_list, conv_s_list, ssm_s_list = [], [], [], []
    step_path = dec_seq == SUBLANES and (n_dec * dec_seq) % SSD_CHUNK == 0
    ssm_in_all = state_ssm.reshape(n_a, n_dec, d_inner, D_STATE)
    ssm_s_all = None
    kvm = kv_p = kv_s = None
    wu, wd = w_up.astype(BF16), w_down.astype(BF16)
    for layer in range(depth):
        last = layer == depth - 1
        if layer < n_a:
            i = layer
            w_in = a_in_proj[i]
            w_main = w_in.astype(BF16)
            w_dt = jnp.pad(w_in[:, d_inner + conv_dim:], ((0, 0), (0, LANES - n_heads))).astype(BF16)
            w_out = a_out_proj[i].astype(BF16)
            prm = (a_conv_w[i], a_conv_b[i].reshape(1, conv_dim), pad_lanes(a_dt_bias[i]), pad_lanes(a_log[i]),
                   jnp.repeat(a_d_skip[i], SSM_HEAD_DIM).reshape(1, d_inner), a_gate_norm_w[i].reshape(1, d_inner))

            def mixer(h, bsz, length, conv0, ssm0, shared):
                if shared and length % SSD_CHUNK == 0:
                    xs_a, bb_a, cc_a, sz_a, dt, conv_o = in_proj_conv(
                        h, a_norm_w[i], w_main, w_dt, prm[0], prm[1], conv0, seq_len=length, d_inner=d_inner)
                    split = lambda t: t.reshape(bsz, length, t.shape[1])
                    g, ssm_o = ssd_chunked(split(xs_a), split(bb_a), split(cc_a), split(sz_a), split(dt),
                                           ssm0, *prm[2:])
                else:
                    xbcz, dt = in_proj(h, a_norm_w[i], w_main, w_dt, d_inner=d_inner, n_main=conv_dim + d_inner)
                    g, conv_o, ssm_o = ssd_mixer(xbcz.reshape(bsz, length, conv_dim + d_inner),
                                                 dt.reshape(bsz, length, LANES), conv0, ssm0, *prm,
                                                 shared_state=shared)
                return g.reshape(bsz * length, d_inner), conv_o, ssm_o

            zero_conv = jnp.zeros((1, SUBLANES, conv_dim), F32)
            zero_ssm = jnp.zeros((1, d_inner, D_STATE), F32)
            g_m, conv_m, ssm_m = mixer(hm, 1, n_meta, zero_conv, zero_ssm, True)
            g_p, conv_p, ssm_p = mixer(hp, n_prompt, seq, conv_m, ssm_m, True)
            conv0_s = pad_conv_state(state_conv[i])
            if step_path:
                xbcz_s, dt_s = in_proj(hs, a_norm_w[i], w_main, w_dt, d_inner=d_inner, n_main=conv_dim + d_inner)
                ypart, eax, xdtw, bb_s, cc_s, eacs = ssd_step_tokens(xbcz_s, dt_s, conv0_s, *prm[:5], t=dec_seq)
                cdecay = eacs.reshape(n_dec, dec_seq, LANES)[:, dec_seq - 1]
                g_s, ssm_s_all = ssd_step_state(cdecay, ssm_in_all, i, cc_s, bb_s, xdtw, eax, ypart, xbcz_s,
                                                prm[5], ssm_s_all, t=dec_seq)
                conv_s = xbcz_s.reshape(n_dec, dec_seq, -1)[:, dec_seq - (CONV_K - 1):, :conv_dim]
            else:
                g_s, conv_s, ssm_s = mixer(hs, n_dec, dec_seq, conv0_s, ssm_in_all[i], False)
                conv_s = conv_s[:, SUBLANES - (CONV_K - 1):]
                ssm_s_list.append(ssm_s)
            conv_p_list.append(conv_p[:, SUBLANES - (CONV_K - 1):])
            ssm_p_list.append(ssm_p.reshape(n_prompt, n_heads, SSM_HEAD_DIM, D_STATE))
            conv_s_list.append(conv_s)
            hm = mlp(hm, mlp_norm_w[layer], wu, wd, layer, final_norm_w, final_norm=False, proj=(g_m, w_out))
            proj_p, proj_s = (g_p, w_out), (g_s, w_out)
        else:
            j = layer - n_a
            if j == 0:
                wkv = w_kv.astype(BF16)
                kvm = norm_matmul(hm, kv_norm_w, wkv)
                kv_p = norm_matmul(hp, kv_norm_w, wkv).reshape(n_prompt, seq, 2 * kvw)
                kv_s = norm_matmul(hs, kv_norm_w, wkv).reshape(n_dec, dec_seq, 2 * kvw)
            wq = w_q[j].astype(BF16)
            wo = w_o[j].astype(BF16)
            dq = wq.shape[1]
            q_p = norm_matmul(hp, b_norm_w[j], wq).reshape(n_prompt, seq, dq)
            q_s = norm_matmul(hs, b_norm_w[j], wq).reshape(n_dec, dec_seq, dq)
            o_p = attn_prompt(q_p, kv_p, kvm, attn_sinks[j])
            o_s = attn_sample(q_s, kv_s, cache_k_win.reshape(n_dec, w_buf, kvw),
                              cache_v_win.reshape(n_dec, w_buf, kvw), kvm, attn_sinks[j])
            proj_p = (o_p.reshape(n_prompt * seq, dq), wo)
            proj_s = (o_s.reshape(n_dec * dec_seq, dq), wo)
        hp = mlp(hp, mlp_norm_w[layer], wu, wd, layer, final_norm_w, final_norm=last, proj=proj_p)
        hs = mlp(hs, mlp_norm_w[layer], wu, wd, layer, final_norm_w, final_norm=last, proj=proj_s)

    y_prompt = hp.reshape(n_prompt, seq, d_model)
    y_sample = hs.reshape(n_dec, dec_seq, d_model)
    kv_heads = (N_KV_HEADS, ATTN_HEAD_DIM)
    k_p = kv_p[:, seq - w_buf:, :kvw].reshape((n_prompt, w_buf) + kv_heads)
    v_p = kv_p[:, seq - w_buf:, kvw:].reshape((n_prompt, w_buf) + kv_heads)
    k_s = kv_s[:, :, :kvw].reshape((n_dec, dec_seq) + kv_heads)
    v_s = kv_s[:, :, kvw:].reshape((n_dec, dec_seq) + kv_heads)
    k_s_win = jnp.concatenate([cache_k_win, k_s], axis=1)[:, -w_buf:]
    v_s_win = jnp.concatenate([cache_v_win, v_s], axis=1)[:, -w_buf:]
    if not step_path:
        ssm_s_all = jnp.stack(ssm_s_list)
    return (y_prompt, y_sample, jnp.stack(conv_p_list), jnp.stack(ssm_p_list), k_p, v_p,
            jnp.stack(conv_s_list), ssm_s_all.reshape(state_ssm.shape), k_s_win, v_s_win)
```

```python
import functools

import jax
import jax.numpy as jnp
from jax import lax
from jax.experimental import pallas as pl
from jax.experimental.pallas import tpu as pltpu

F32 = jnp.float32
BF16 = jnp.bfloat16

N_GROUPS = 8
SSM_HEAD_DIM = 64
D_STATE = 128
CONV_K = 4
SSD_CHUNK = 128
ATTN_HEAD_DIM = 64
N_KV_HEADS = 4
WINDOW = 128
PAST_LEN = 8192
EPS = 1e-5
LOG2_E = 1.4426950408889634

LANES = 128
SUBLANES = 8
VMEM_LIMIT_BYTES = 52 * 1024 * 1024

_NT = (((1,), (1,)), ((), ()))
_TN = (((0,), (0,)), ((), ()))


def _params(*sem):
    return pltpu.CompilerParams(dimension_semantics=sem, vmem_limit_bytes=VMEM_LIMIT_BYTES)


def _rms(x, w):
    ms = jnp.mean(x * x, axis=-1, keepdims=True)
    return x * lax.rsqrt(ms + EPS) * w


def _silu(x):
    s = 0.5 * x
    return s + s * jnp.tanh(s)


def _row_tile(m, cap):
    t = min(m, cap)
    assert m % t == 0, (m, t)
    return t


def _norm_matmul_kernel(x_ref, nw_ref, w_ref, o_ref, xn_ref):
    @pl.when(pl.program_id(1) == 0)
    def _():
        xn_ref[...] = _rms(x_ref[...], nw_ref[...]).astype(BF16)

    o_ref[...] = jnp.dot(xn_ref[...], w_ref[...], preferred_element_type=F32)


def norm_matmul(x, nw, w, *, tm_cap=1024, tn_cap=1024):
    m, d = x.shape
    n = w.shape[1]
    tm, tn = _row_tile(m, tm_cap), _row_tile(n, tn_cap)
    return pl.pallas_call(
        _norm_matmul_kernel,
        grid=(m // tm, n // tn),
        in_specs=[
            pl.BlockSpec((tm, d), lambda i, j: (i, 0)),
            pl.BlockSpec((1, d), lambda i, j: (0, 0)),
            pl.BlockSpec((d, tn), lambda i, j: (0, j)),
        ],
        out_specs=pl.BlockSpec((tm, tn), lambda i, j: (i, j)),
        out_shape=jax.ShapeDtypeStruct((m, n), F32),
        scratch_shapes=[pltpu.VMEM((tm, d), BF16)],
        compiler_params=_params("parallel", "arbitrary"),
        name="norm_matmul",
    )(x, nw.reshape(1, d), w)


def _in_proj_kernel(x_ref, nw_ref, w_ref, wdt_ref, o_ref, dt_ref, xn_ref):
    @pl.when(pl.program_id(1) == 0)
    def _():
        xn = _rms(x_ref[...], nw_ref[...]).astype(BF16)
        xn_ref[...] = xn
        dt_ref[...] = jnp.dot(xn, wdt_ref[...], preferred_element_type=F32)

    o_ref[...] = jnp.dot(xn_ref[...], w_ref[...], preferred_element_type=F32)


def in_proj(x, nw, w, wdt, *, d_inner, n_main, tm_cap=1024, tn_cap=1024):
    m, d = x.shape
    n = n_main
    tm, tn = _row_tile(m, tm_cap), _row_tile(n, tn_cap)
    assert d_inner % tn == 0
    return pl.pallas_call(
        _in_proj_kernel,
        grid=(m // tm, n // tn),
        in_specs=[
            pl.BlockSpec((tm, d), lambda i, j: (i, 0)),
            pl.BlockSpec((1, d), lambda i, j: (0, 0)),
            pl.BlockSpec((d, tn), lambda i, j: (0, (j + d_inner // tn) % (n // tn))),
            pl.BlockSpec((d, LANES), lambda i, j: (0, 0)),
        ],
        out_specs=[
            pl.BlockSpec((tm, tn), lambda i, j: (i, j)),
            pl.BlockSpec((tm, LANES), lambda i, j: (i, 0)),
        ],
        out_shape=[jax.ShapeDtypeStruct((m, n), F32), jax.ShapeDtypeStruct((m, LANES), F32)],
        scratch_shapes=[pltpu.VMEM((tm, d), BF16)],
        compiler_params=_params("parallel", "arbitrary"),
        name="in_proj",
    )(x, nw.reshape(1, d), w, wdt)


def _in_proj_conv_kernel(x_ref, nw_ref, w_ref, wdt_ref, cw_ref, cb_ref, conv0_ref,
                         xs_ref, bb_ref, cc_ref, sz_ref, dt_ref, tail_ref,
                         xn_ref, xpad_ref, halo_ref, *, tiles_per_seq, n_x, n_b):
    i = pl.program_id(0)
    j = pl.program_id(1)
    tm = x_ref.shape[0]
    tn = w_ref.shape[1]
    pad = SUBLANES
    sub = 4 * LANES

    @pl.when(j == 0)
    def _():
        xn = _rms(x_ref[...], nw_ref[...]).astype(BF16)
        xn_ref[...] = xn
        dt_ref[...] = jnp.dot(xn, wdt_ref[...], preferred_element_type=F32)

    def conv_tile(out_ref):
        @pl.when(i % tiles_per_seq == 0)
        def _():
            for l in range(tn // LANES):
                halo_ref[j, l] = conv0_ref[0, :, l * LANES:(l + 1) * LANES]

        def conv_act(s, raw):
            for e in range(sub // LANES):
                l = s * (sub // LANES) + e
                cs = slice(l * LANES, (l + 1) * LANES)
                xpad_ref[l, 0:pad, :] = halo_ref[j, l]
                xpad_ref[l, pad:pad + tm, :] = raw[:, e * LANES:(e + 1) * LANES]
                half = cb_ref[:, cs]
                for k in range(CONV_K):
                    off = pad - (CONV_K - 1) + k
                    half = half + xpad_ref[l, off:off + tm, :] * cw_ref[k:k + 1, cs]
                out_ref[:, cs] = (half + half * jnp.tanh(half)).astype(out_ref.dtype)
                last_rows = xpad_ref[l, tm:tm + pad, :]
                halo_ref[j, l] = last_rows
                tail_ref[0, :, cs] = last_rows

        _staggered(tn // sub, [matmul_cols, conv_act])

    def matmul_cols(s, _):
        return jnp.dot(xn_ref[...], w_ref[:, s * sub:(s + 1) * sub], preferred_element_type=F32)

    def gate_act(s, raw):
        sz_ref[:, s * sub:(s + 1) * sub] = _silu(raw).astype(BF16)

    pl.when(j < n_x)(lambda: conv_tile(xs_ref))
    pl.when(jnp.logical_and(j >= n_x, j < n_x + n_b))(lambda: conv_tile(bb_ref))
    pl.when(jnp.logical_and(j >= n_x + n_b, j < n_x + 2 * n_b))(lambda: conv_tile(cc_ref))
    pl.when(j >= n_x + 2 * n_b)(lambda: _staggered(tn // sub, [matmul_cols, gate_act]))


def in_proj_conv(x, nw, w, wdt, cw, cb, conv0, *, seq_len, d_inner, tm_cap=1024):
    m, d = x.shape
    conv_dim = cw.shape[1]
    bc = (conv_dim - d_inner) // 2
    tn = bc
    tm = _row_tile(seq_len, tm_cap)
    assert d_inner % tn == 0 and w.shape[1] >= conv_dim + d_inner and m % seq_len == 0
    n_x, n_b, n_z = d_inner // tn, 1, d_inner // tn
    n_conv = n_x + 2 * n_b
    conv_col = lambda i, j: (0, jnp.minimum(j, n_conv - 1))
    bf = lambda n: jax.ShapeDtypeStruct((m, n), BF16)
    tiles_per_seq = seq_len // tm
    *acts, tails = pl.pallas_call(
        functools.partial(_in_proj_conv_kernel, tiles_per_seq=tiles_per_seq, n_x=n_x, n_b=n_b),
        grid=(m // tm, n_conv + n_z),
        in_specs=[
            pl.BlockSpec((tm, d), lambda i, j: (i, 0)),
            pl.BlockSpec((1, d), lambda i, j: (0, 0)),
            pl.BlockSpec((d, tn), lambda i, j: (0, (j + n_z) % (n_conv + n_z))),
            pl.BlockSpec((d, LANES), lambda i, j: (0, 0)),
            pl.BlockSpec((CONV_K, tn), conv_col),
            pl.BlockSpec((1, tn), conv_col),
            pl.BlockSpec((1, SUBLANES, tn), lambda i, j: (0, 0, jnp.minimum(j, n_conv - 1))),
        ],
        out_specs=[
            pl.BlockSpec((tm, tn), lambda i, j: (i, jnp.minimum(j, n_x - 1))),
            pl.BlockSpec((tm, tn), lambda i, j: (i, 0)),
            pl.BlockSpec((tm, tn), lambda i, j: (i, 0)),
            pl.BlockSpec((tm, tn), lambda i, j: (i, jnp.clip(j - n_conv, 0, n_z - 1))),
            pl.BlockSpec((tm, LANES), lambda i, j: (i, 0)),
            pl.BlockSpec((1, SUBLANES, tn), lambda i, j: (i, 0, jnp.minimum(j, n_conv - 1))),
        ],
        out_shape=[jax.ShapeDtypeStruct((m, d_inner), F32), bf(bc), bf(bc), bf(d_inner),
                   jax.ShapeDtypeStruct((m, LANES), F32),
                   jax.ShapeDtypeStruct((m // tm, SUBLANES, conv_dim), F32)],
        scratch_shapes=[
            pltpu.VMEM((tm, d), BF16),
            pltpu.VMEM((tn // LANES, tm + SUBLANES, LANES), F32),
            pltpu.VMEM((n_conv, tn // LANES, SUBLANES, LANES), F32),
        ],
        compiler_params=_params("arbitrary", "arbitrary"),
        name="in_proj_conv",
    )(x, nw.reshape(1, d), w, wdt, cw, cb, conv0)
    return (*acts, tails[tiles_per_seq - 1::tiles_per_seq])


def _mlp_steps(x_rows, nw_ref, wu_ref, wd_ref, fw_ref, o_ref, xn_ref, final_norm):
    f = pl.program_id(1)
    tm = o_ref.shape[0]
    rb = min(tm, 2 * LANES)

    @pl.when(f == 0)
    def _():
        def normalise(r, x):
            rows = slice(r * rb, (r + 1) * rb)
            xn_ref[rows, :] = _rms(x, nw_ref[...]).astype(BF16)
            o_ref[rows, :] = x

        _staggered(tm // rb, [lambda r, _: x_rows(slice(r * rb, (r + 1) * rb)), normalise])

    h = jnp.dot(xn_ref[...], wu_ref[...], preferred_element_type=F32)
    h = jnp.square(jnp.maximum(h, 0.0)).astype(BF16)
    o_ref[...] += jnp.dot(h, wd_ref[...], preferred_element_type=F32)

    if final_norm:
        @pl.when(f == pl.num_programs(1) - 1)
        def _():
            o_ref[...] = _rms(o_ref[...], fw_ref[...])


def _mlp_kernel(x_ref, nw_ref, wu_ref, wd_ref, fw_ref, o_ref, xn_ref, *, final_norm):
    _mlp_steps(lambda rows: x_ref[rows, :], nw_ref, wu_ref, wd_ref, fw_ref, o_ref, xn_ref, final_norm)


def _proj_mlp_kernel(a_ref, wa_ref, x_ref, nw_ref, wu_ref, wd_ref, fw_ref, o_ref, xn_ref, *, final_norm):
    def block_input(rows):
        return x_ref[rows, :] + jnp.dot(a_ref[rows, :], wa_ref[...], preferred_element_type=F32)

    _mlp_steps(block_input, nw_ref, wu_ref, wd_ref, fw_ref, o_ref, xn_ref, final_norm)


def mlp(x, nw, wu, wd, layer, fw, *, final_norm, proj=None, tm_cap=1024, tf_cap=1024):
    m, d = x.shape
    dff = wu.shape[2]
    tm, tf = _row_tile(m, tm_cap), _row_tile(dff, tf_cap)
    in_specs = [
        pl.BlockSpec((tm, d), lambda i, f: (i, 0)),
        pl.BlockSpec((1, d), lambda i, f: (0, 0)),
        pl.BlockSpec((None, d, tf), lambda i, f: (layer, 0, f)),
        pl.BlockSpec((None, tf, d), lambda i, f: (layer, f, 0)),
        pl.BlockSpec((1, d), lambda i, f: (0, 0)),
    ]
    args = [x, nw.reshape(1, d), wu, wd, fw.reshape(1, d)]
    body = _mlp_kernel
    if proj is not None:
        a, wa = proj
        k = a.shape[1]
        in_specs = [pl.BlockSpec((tm, k), lambda i, f: (i, 0)),
                    pl.BlockSpec((k, d), lambda i, f: (0, 0), pipeline_mode=pl.Buffered(1))] + in_specs
        args = [a, wa] + args
        body = _proj_mlp_kernel
    return pl.pallas_call(
        functools.partial(body, final_norm=final_norm),
        grid=(m // tm, dff // tf),
        in_specs=in_specs,
        out_specs=pl.BlockSpec((tm, d), lambda i, f: (i, 0)),
        out_shape=jax.ShapeDtypeStruct((m, d), F32),
        scratch_shapes=[pltpu.VMEM((tm, d), BF16)],
        compiler_params=_params("parallel", "arbitrary"),
        name="mlp" if proj is None else "proj_mlp",
    )(*args)


def _softplus(x):
    return jnp.maximum(x, 0.0) + jnp.log(1.0 + jnp.exp(-jnp.abs(x)))


def _split3(a):
    hi = a.astype(BF16)
    r1 = a - hi.astype(F32)
    mid = r1.astype(BF16)
    lo = (r1 - mid.astype(F32)).astype(BF16)
    return hi, mid, lo


def _ssd_kernel(xbc_ref, z_ref, dt_ref, conv0_ref, ssm0_ref,
                cw_ref, cb_ref, dtb_ref, alog_ref, dsk_ref, gw_ref,
                g_ref, convo_ref, ssmo_ref,
                xpad_ref, act_ref, state_ref, y_ref, *, q, n_heads):
    c = pl.program_id(1)
    p = SSM_HEAD_DIM
    d_inner = n_heads * p
    hpg = n_heads // N_GROUPS
    conv_dim = d_inner + 2 * N_GROUPS * D_STATE
    pad = SUBLANES

    @pl.when(c == 0)
    def _():
        xpad_ref[0:pad, :] = conv0_ref[0]
        state_ref[...] = ssm0_ref[0]

    xpad_ref[pad:pad + q, :] = xbc_ref[0]
    cblk = 512
    for j in range(conv_dim // cblk):
        cs = slice(j * cblk, (j + 1) * cblk)
        conv = cb_ref[:, cs]
        for k in range(CONV_K):
            off = pad - (CONV_K - 1) + k
            conv = conv + xpad_ref[off:off + q, cs] * cw_ref[k:k + 1, cs]
        act_ref[:, cs] = _silu(conv)
    xpad_ref[0:pad, :] = xpad_ref[q:q + pad, :]

    dt = _softplus(dt_ref[0] + dtb_ref[...])
    a = dt * (-jnp.exp(alog_ref[...]))
    row = lax.broadcasted_iota(jnp.int32, (q, q), 0)
    col = lax.broadcasted_iota(jnp.int32, (q, q), 1)
    causal = row >= col
    tri = jnp.where(causal, 1.0, 0.0).astype(BF16)
    acs = None
    for part in _split3(a):
        t = jnp.dot(tri, part, preferred_element_type=F32)
        acs = t if acs is None else acs + t
    if q < LANES:
        acs_sq = jnp.concatenate([acs, jnp.zeros((LANES - q, LANES), F32)], axis=0)
    else:
        acs_sq = acs
    acs_t = acs_sq.T
    eacs = jnp.exp(acs)
    last = acs[q - 1:q, :]
    dte = jnp.exp(last - acs)
    cdecay = jnp.exp(last)

    for g in range(N_GROUPS):
        b0 = d_inner + g * D_STATE
        c0 = d_inner + N_GROUPS * D_STATE + g * D_STATE
        bb = act_ref[:, b0:b0 + D_STATE].astype(BF16)
        cc = act_ref[:, c0:c0 + D_STATE].astype(BF16)
        cbm = lax.dot_general(cc, bb, _NT, preferred_element_type=F32)
        for pair in range(hpg // 2):
            xs2 = act_ref[:, (g * hpg + 2 * pair) * p:(g * hpg + 2 * pair + 2) * p]
            dsk2 = dsk_ref[:, (g * hpg + 2 * pair) * p:(g * hpg + 2 * pair + 2) * p]
            ys = []
            for e in range(2):
                h = g * hpg + 2 * pair + e
                xs = xs2[:, e * p:(e + 1) * p]
                seg = acs[:, h:h + 1] - acs_t[h:h + 1, 0:q]
                decay = jnp.exp(jnp.where(causal, seg, -jnp.inf))
                m = (cbm * decay).astype(BF16)
                xdt = xs * dt[:, h:h + 1]
                y = jnp.dot(m, xdt.astype(BF16), preferred_element_type=F32)
                st = state_ref[h * p:(h + 1) * p, :]
                yoff = lax.dot_general(cc, st.astype(BF16), _NT, preferred_element_type=F32)
                y = y + yoff * eacs[:, h:h + 1] + dsk2[:, e * p:(e + 1) * p] * xs
                ys.append(y)
                xdtw = (xdt * dte[:, h:h + 1]).astype(BF16)
                snew = lax.dot_general(xdtw, bb, _TN, preferred_element_type=F32)
                state_ref[h * p:(h + 1) * p, :] = (
                    jnp.broadcast_to(cdecay[:, h:h + 1], (p, D_STATE)) * st + snew)
            y_ref[:, (g * hpg + 2 * pair) * p:(g * hpg + 2 * pair + 2) * p] = (
                jnp.concatenate(ys, axis=1))

    gsz = d_inner // N_GROUPS
    for g in range(N_GROUPS):
        gs = slice(g * gsz, (g + 1) * gsz)
        gg = y_ref[:, gs] * _silu(z_ref[0, :, gs])
        ms = jnp.mean(gg * gg, axis=-1, keepdims=True)
        g_ref[0, :, gs] = (gg * lax.rsqrt(ms + EPS) * gw_ref[:, gs]).astype(BF16)

    @pl.when(c == pl.num_programs(1) - 1)
    def _():
        convo_ref[0] = xpad_ref[0:pad, :]
        ssmo_ref[0] = state_ref[...]


def ssd_mixer(xbcz, dt, conv0, ssm0, cw, cb, dtb, alog, dsk, gw, *, shared_state):
    bsz, length, _ = xbcz.shape
    conv_dim = cw.shape[1]
    d_inner = gw.shape[1]
    n_heads = d_inner // SSM_HEAD_DIM
    q = SSD_CHUNK if length % SSD_CHUNK == 0 else length
    assert q % SUBLANES == 0 and q >= SUBLANES and conv_dim % d_inner == 0
    nc = length // q
    zblk = conv_dim // d_inner
    if shared_state:
        st_map = lambda b, c: (0, 0, 0)
    else:
        st_map = lambda b, c: (b, 0, 0)
    const = lambda b, c: (0, 0)
    return pl.pallas_call(
        functools.partial(_ssd_kernel, q=q, n_heads=n_heads),
        grid=(bsz, nc),
        in_specs=[
            pl.BlockSpec((1, q, conv_dim), lambda b, c: (b, c, 0)),
            pl.BlockSpec((1, q, d_inner), lambda b, c: (b, c, zblk)),
            pl.BlockSpec((1, q, LANES), lambda b, c: (b, c, 0)),
            pl.BlockSpec((1, SUBLANES, conv_dim), st_map),
            pl.BlockSpec((1, n_heads * SSM_HEAD_DIM, D_STATE), st_map),
            pl.BlockSpec((CONV_K, conv_dim), const),
            pl.BlockSpec((1, conv_dim), const),
            pl.BlockSpec((1, LANES), const),
            pl.BlockSpec((1, LANES), const),
            pl.BlockSpec((1, d_inner), const),
            pl.BlockSpec((1, d_inner), const),
        ],
        out_specs=[
            pl.BlockSpec((1, q, d_inner), lambda b, c: (b, c, 0)),
            pl.BlockSpec((1, SUBLANES, conv_dim), lambda b, c: (b, 0, 0)),
            pl.BlockSpec((1, n_heads * SSM_HEAD_DIM, D_STATE), lambda b, c: (b, 0, 0)),
        ],
        out_shape=[
            jax.ShapeDtypeStruct((bsz, length, d_inner), BF16),
            jax.ShapeDtypeStruct((bsz, SUBLANES, conv_dim), F32),
            jax.ShapeDtypeStruct((bsz, n_heads * SSM_HEAD_DIM, D_STATE), F32),
        ],
        scratch_shapes=[
            pltpu.VMEM((q + SUBLANES, conv_dim), F32),
            pltpu.VMEM((q, conv_dim), F32),
            pltpu.VMEM((n_heads * SSM_HEAD_DIM, D_STATE), F32),
            pltpu.VMEM((q, d_inner), F32),
        ],
        compiler_params=_params("parallel", "arbitrary"),
        name="ssd_mixer",
    )(xbcz, xbcz, dt, conv0, ssm0, cw, cb, dtb, alog, dsk, gw)


def _expansion_matrix(n_heads):
    h = jnp.arange(LANES)[:, None]
    ex = h == jnp.arange(n_heads * SSM_HEAD_DIM)[None, :] // SSM_HEAD_DIM
    return jnp.tile(ex.astype(BF16), (3, 1))


def _store_act_tile(j, act, xs_ref, bb_ref, cc_ref, d_inner, bc):
    lo = j * LANES
    if lo < d_inner:
        xs_ref[:, lo:lo + LANES] = act
    elif lo < d_inner + bc:
        bb_ref[:, lo - d_inner:lo - d_inner + LANES] = act.astype(BF16)
    else:
        cc_ref[:, lo - d_inner - bc:lo - d_inner - bc + LANES] = act.astype(BF16)


def _ssd_token_math(dt_raw, dtb_ref, alog_ref, lhs3_ref, acst_ref, *, seg_len):
    q = dt_raw.shape[0]
    dt = _softplus(dt_raw + dtb_ref[...])
    a = dt * (-jnp.exp(alog_ref[...]))
    row = lax.broadcasted_iota(jnp.int32, (q, q), 0)
    col = lax.broadcasted_iota(jnp.int32, (q, q), 1)
    mask = row >= col
    if seg_len != q:
        mask = jnp.logical_and(mask, row // seg_len == col // seg_len)
        seg_end = (row // seg_len) * seg_len + (seg_len - 1)
    tri = jnp.where(mask, 1.0, 0.0).astype(BF16)
    acs = None
    for part in _split3(a):
        t = jnp.dot(tri, part, preferred_element_type=F32)
        acs = t if acs is None else acs + t
    if seg_len == q:
        last = acs[q - 1:q, :]
    else:
        sel = jnp.where(col == seg_end, 1.0, 0.0).astype(BF16)
        last = None
        for part in _split3(acs):
            t = jnp.dot(sel, part, preferred_element_type=F32)
            last = t if last is None else last + t
    acst_ref[...] = (acs * LOG2_E).T
    eacs = jnp.exp(acs)
    stack = jnp.concatenate([dt, eacs, jnp.exp(last - acs)], axis=0)
    lhs3_ref[...] = jnp.concatenate(_split3(stack), axis=1)
    return mask, eacs


def _staggered(n, stages):
    carried = {}
    for t in range(n + len(stages) - 1):
        for k, stage in enumerate(stages):
            g = t - k
            if 0 <= g < n:
                carried[g] = stage(g, carried.get(g))


def _ssd_group_inputs(g, xs_ref, bb_ref, cc_ref, lhs3_ref, ex3_ref, hpg):
    q = xs_ref.shape[0]
    gch = hpg * SSM_HEAD_DIM
    gs = slice(g * gch, (g + 1) * gch)
    bb = bb_ref[:, g * D_STATE:(g + 1) * D_STATE]
    cc = cc_ref[:, g * D_STATE:(g + 1) * D_STATE]
    cbm = lax.dot_general(cc, bb, _NT, preferred_element_type=F32)
    ex = jnp.dot(lhs3_ref[...], ex3_ref[:, gs], preferred_element_type=F32)
    xs = xs_ref[:, gs]
    return dict(bb=bb, cc=cc, cbm=cbm, xs=xs, xdt=xs * ex[0:q], eax=ex[q:2 * q], dtex=ex[2 * q:3 * q])


def _ssd_group_diag(g, v, mask, acst_ref, hpg):
    q = v["xs"].shape[0]
    p = SSM_HEAD_DIM
    lane_head = lax.broadcasted_iota(jnp.int32, (q, hpg * p), 1) // p
    ms, rhs = [], []
    for r in range(hpg):
        h = g * hpg + r
        rowb = jnp.broadcast_to(acst_ref[h:h + 1, :], (q, q))
        decay = jnp.exp2(jnp.where(mask, rowb.T - rowb, -jnp.inf))
        ms.append((v["cbm"] * decay).astype(BF16))
        rhs.append(jnp.where(lane_head == r, v["xdt"], 0.0).astype(BF16))
    y = jnp.dot(jnp.concatenate(ms, axis=1), jnp.concatenate(rhs, axis=0),
                preferred_element_type=F32)
    return dict(v, y=y, cbm=None)


def _ssd_chunk_kernel(xs_ref, bb_ref, cc_ref, sz_ref, dt_ref, ssm0_ref,
                      dtb_ref, alog_ref, dsk_ref, gw_ref, ex3_ref,
                      g_ref, ssmo_ref,
                      st_ref, lhs3_ref, acst_ref, *, n_heads, cps):
    c = pl.program_id(1)
    q = SSD_CHUNK
    hpg = n_heads // N_GROUPS
    gch = hpg * SSM_HEAD_DIM

    @pl.when(c == 0)
    def _():
        st_ref[...] = ssm0_ref[0].T

    rows = [pl.ds(h * q, q) for h in range(cps)]
    masks = [_ssd_token_math(dt_ref[0, rows[h], :], dtb_ref, alog_ref, lhs3_ref.at[h], acst_ref.at[h],
                             seg_len=q)[0] for h in range(cps)]

    def finish(item, v):
        h, g = divmod(item, N_GROUPS)
        gs = slice(g * gch, (g + 1) * gch)
        st = st_ref[:, gs]
        y = (v["y"] + jnp.dot(v["cc"], st.astype(BF16), preferred_element_type=F32) * v["eax"]
             + dsk_ref[:, gs] * v["xs"])
        gg = y * sz_ref[0, rows[h], gs].astype(F32)
        msq = jnp.mean(gg * gg, axis=-1, keepdims=True)
        g_ref[0, rows[h], gs] = (gg * lax.rsqrt(msq + EPS) * gw_ref[:, gs]).astype(BF16)
        xdtw = (v["xdt"] * v["dtex"]).astype(BF16)
        snew = lax.dot_general(v["bb"], xdtw, _TN, preferred_element_type=F32)
        st_ref[:, gs] = st * v["eax"][q - 1:q, :] + snew

    def inputs(item, _):
        h, g = divmod(item, N_GROUPS)
        return _ssd_group_inputs(g, xs_ref.at[0, rows[h]], bb_ref.at[0, rows[h]], cc_ref.at[0, rows[h]],
                                 lhs3_ref.at[h], ex3_ref, hpg)

    def diag(item, v):
        h, g = divmod(item, N_GROUPS)
        return _ssd_group_diag(g, v, masks[h], acst_ref.at[h], hpg)

    _staggered(cps * N_GROUPS, [inputs, diag, finish])

    @pl.when(c == pl.num_programs(1) - 1)
    def _():
        ssmo_ref[0] = st_ref[...].T


def ssd_chunked(xs, bb, cc, sz, dt, ssm0, dtb, alog, dsk, gw, *, chunks_per_step=4):
    bsz, length, d_inner = xs.shape
    bc = bb.shape[2]
    n_heads = d_inner // SSM_HEAD_DIM
    q = SSD_CHUNK
    assert length % q == 0 and bc == N_GROUPS * D_STATE and n_heads % N_GROUPS == 0
    cps = _row_tile(length // q, chunks_per_step)
    qs = cps * q
    ex3 = _expansion_matrix(n_heads)
    rows = lambda b, c: (b, c, 0)
    const = lambda b, c: (0, 0)
    return pl.pallas_call(
        functools.partial(_ssd_chunk_kernel, n_heads=n_heads, cps=cps),
        grid=(bsz, length // qs),
        in_specs=[
            pl.BlockSpec((1, qs, d_inner), rows),
            pl.BlockSpec((1, qs, bc), rows),
            pl.BlockSpec((1, qs, bc), rows),
            pl.BlockSpec((1, qs, d_inner), rows),
            pl.BlockSpec((1, qs, LANES), rows),
            pl.BlockSpec((1, d_inner, D_STATE), lambda b, c: (0, 0, 0)),
            pl.BlockSpec((1, LANES), const),
            pl.BlockSpec((1, LANES), const),
            pl.BlockSpec((1, d_inner), const),
            pl.BlockSpec((1, d_inner), const),
            pl.BlockSpec(ex3.shape, const),
        ],
        out_specs=[
            pl.BlockSpec((1, qs, d_inner), rows),
            pl.BlockSpec((1, d_inner, D_STATE), lambda b, c: (b, 0, 0)),
        ],
        out_shape=[
            jax.ShapeDtypeStruct((bsz, length, d_inner), BF16),
            jax.ShapeDtypeStruct((bsz, d_inner, D_STATE), F32),
        ],
        scratch_shapes=[
            pltpu.VMEM((D_STATE, d_inner), F32),
            pltpu.VMEM((cps, 3 * q, 3 * LANES), BF16),
            pltpu.VMEM((cps, LANES, q), F32),
        ],
        compiler_params=_params("parallel", "arbitrary"),
        name="ssd_chunked",
    )(xs, bb, cc, sz, dt, ssm0, dtb, alog, dsk, gw, ex3)


def _ssd_step_tokens_kernel(xbc_ref, dt_ref, conv0_ref, cw_ref, cb_ref, dtb_ref, alog_ref, dsk_ref,
                            ex3_ref,
                            ypart_ref, eaxo_ref, xdtw_ref, bbo_ref, cco_ref, eacs_ref,
                            xpad_ref, xs_ref, bb_ref, cc_ref, lhs3_ref, acst_ref, *, n_heads, t):
    q = SSD_CHUNK
    p = SSM_HEAD_DIM
    d_inner = n_heads * p
    hpg = n_heads // N_GROUPS
    gch = hpg * p
    bc = N_GROUPS * D_STATE
    ntile = (d_inner + 2 * bc) // LANES
    nseq = q // t
    slot = 2 * t
    span = nseq * slot - t

    for j in range(ntile):
        cs = slice(j * LANES, (j + 1) * LANES)
        for i in range(nseq):
            xpad_ref[j, i * slot:i * slot + t, :] = conv0_ref[i, :, cs]
            xpad_ref[j, i * slot + t:(i + 1) * slot, :] = xbc_ref[i * t:(i + 1) * t, cs]
        conv = cb_ref[:, cs]
        for k in range(CONV_K):
            off = t - (CONV_K - 1) + k
            conv = conv + xpad_ref[j, off:off + span, :] * cw_ref[k:k + 1, cs]
        act = _silu(jnp.concatenate([conv[i * slot:i * slot + t] for i in range(nseq)], axis=0))
        _store_act_tile(j, act, xs_ref, bb_ref, cc_ref, d_inner, bc)
        lo = j * LANES
        if d_inner <= lo < d_inner + bc:
            bbo_ref[:, lo - d_inner:lo - d_inner + LANES] = act
        elif lo >= d_inner + bc:
            cco_ref[:, lo - d_inner - bc:lo - d_inner - bc + LANES] = act

    mask, eacs = _ssd_token_math(dt_ref[...], dtb_ref, alog_ref, lhs3_ref, acst_ref, seg_len=t)
    eacs_ref[...] = eacs
    def emit(g, v):
        gs = slice(g * gch, (g + 1) * gch)
        ypart_ref[:, gs] = v["y"] + dsk_ref[:, gs] * v["xs"]
        eaxo_ref[:, gs] = v["eax"]
        xdtw_ref[:, gs] = v["xdt"] * v["dtex"]

    _staggered(N_GROUPS, [
        lambda g, _: _ssd_group_inputs(g, xs_ref, bb_ref, cc_ref, lhs3_ref, ex3_ref, hpg),
        lambda g, v: _ssd_group_diag(g, v, mask, acst_ref, hpg),
        emit,
    ])


def ssd_step_tokens(xbcz, dt, conv0, cw, cb, dtb, alog, dsk, *, t):
    n_tok = xbcz.shape[0]
    conv_dim = cw.shape[1]
    d_inner = dsk.shape[1]
    n_heads = d_inner // SSM_HEAD_DIM
    q = SSD_CHUNK
    bc = N_GROUPS * D_STATE
    assert t == SUBLANES and n_tok % q == 0
    ex3 = _expansion_matrix(n_heads)
    const = lambda s: (0, 0)
    rows = lambda s: (s, 0)
    f32 = lambda n: jax.ShapeDtypeStruct((n_tok, n), F32)
    return pl.pallas_call(
        functools.partial(_ssd_step_tokens_kernel, n_heads=n_heads, t=t),
        grid=(n_tok // q,),
        in_specs=[
            pl.BlockSpec((q, conv_dim), rows),
            pl.BlockSpec((q, LANES), rows),
            pl.BlockSpec((q // t, SUBLANES, conv_dim), lambda s: (s, 0, 0)),
            pl.BlockSpec((CONV_K, conv_dim), const),
            pl.BlockSpec((1, conv_dim), const),
            pl.BlockSpec((1, LANES), const),
            pl.BlockSpec((1, LANES), const),
            pl.BlockSpec((1, d_inner), const),
            pl.BlockSpec(ex3.shape, const),
        ],
        out_specs=[
            pl.BlockSpec((q, d_inner), rows),
            pl.BlockSpec((q, d_inner), rows),
            pl.BlockSpec((q, d_inner), rows),
            pl.BlockSpec((q, bc), rows),
            pl.BlockSpec((q, bc), rows),
            pl.BlockSpec((q, LANES), rows),
        ],
        out_shape=[f32(d_inner), f32(d_inner), f32(d_inner), f32(bc), f32(bc), f32(LANES)],
        scratch_shapes=[
            pltpu.VMEM((conv_dim // LANES, 2 * q, LANES), F32),
            pltpu.VMEM((q, d_inner), F32),
            pltpu.VMEM((q, bc), BF16),
            pltpu.VMEM((q, bc), BF16),
            pltpu.VMEM((3 * q, 3 * LANES), BF16),
            pltpu.VMEM((LANES, q), F32),
        ],
        compiler_params=_params("parallel"),
        name="ssd_step_tokens",
    )(xbcz, dt, conv0, cw, cb, dtb, alog, dsk, ex3)


def _ssd_step_state_kernel(cd_ref, st_ref, cc_ref, bb_ref, xdtw_ref, eax_ref, ypart_ref, z_ref, gw_ref,
                           *rest, n_heads, t, nseq, n_fill):
    g_ref, sto_ref = rest[-2:]
    s = pl.program_id(1)
    p = SSM_HEAD_DIM
    hpg = n_heads // N_GROUPS
    gch = hpg * p

    def update():
        for g in range(N_GROUPS):
            gs = slice(g * gch, (g + 1) * gch)
            ns = slice(g * D_STATE, (g + 1) * D_STATE)
            ys = []
            for i in range(nseq):
                rows = slice(i * t, (i + 1) * t)
                h0 = st_ref[0, i, gs, :]
                yoff = lax.dot_general(cc_ref[rows, ns].astype(BF16), h0.astype(BF16), _NT,
                                       preferred_element_type=F32)
                ys.append(ypart_ref[rows, gs] + yoff * eax_ref[rows, gs])
                upd = lax.dot_general(xdtw_ref[rows, gs].astype(BF16), bb_ref[rows, ns].astype(BF16), _TN,
                                      preferred_element_type=F32)
                for r in range(hpg):
                    hs = slice(r * p, (r + 1) * p)
                    sto_ref[0, i, g * gch + r * p:g * gch + (r + 1) * p, :] = (
                        h0[hs] * cd_ref[s * nseq + i, g * hpg + r] + upd[hs])
            gg = jnp.concatenate(ys, axis=0) * _silu(z_ref[:, gs])
            msq = jnp.mean(gg * gg, axis=-1, keepdims=True)
            g_ref[:, gs] = (gg * lax.rsqrt(msq + EPS) * gw_ref[:, gs]).astype(BF16)

    if n_fill == 0:
        update()
    else:
        @pl.when(pl.program_id(0) < n_fill)
        def _():
            sto_ref[...] = jnp.zeros(sto_ref.shape, F32)

        pl.when(pl.program_id(0) == n_fill)(update)


def ssd_step_state(cd, state_all, layer, cc, bb, xdtw, eax, ypart, xbcz, gw, stacked_out, *, t, seqs_per_step=8):
    n_layers, n_seq, d_inner, d_state = state_all.shape
    n_tok = ypart.shape[0]
    bc = cc.shape[1]
    conv_dim = xbcz.shape[1] - d_inner
    nseq = _row_tile(n_seq, seqs_per_step)
    rt = nseq * t
    n_fill = n_layers - 1 if stacked_out is None else 0

    def active(l, s):
        return jnp.where(l == n_fill, s, 0)

    def out_layer(l):
        return jnp.where(l < n_fill, l + jnp.where(l >= layer, 1, 0), layer)

    rows = lambda l, s, cd_: (active(l, s), 0)
    st_map = lambda l, s, cd_: (layer, active(l, s), 0, 0)
    in_specs = [
        pl.BlockSpec((1, nseq, d_inner, d_state), st_map),
        pl.BlockSpec((rt, bc), rows),
        pl.BlockSpec((rt, bc), rows),
        pl.BlockSpec((rt, d_inner), rows),
        pl.BlockSpec((rt, d_inner), rows),
        pl.BlockSpec((rt, d_inner), rows),
        pl.BlockSpec((rt, d_inner), lambda l, s, cd_: (active(l, s), conv_dim // d_inner)),
        pl.BlockSpec((1, d_inner), lambda l, s, cd_: (0, 0)),
    ]
    args = [cd, state_all, cc, bb, xdtw, eax, ypart, xbcz, gw]
    aliases = {}
    if stacked_out is not None:
        in_specs.append(pl.BlockSpec(memory_space=pl.ANY))
        aliases = {len(args): 1}
        args.append(stacked_out)
    return pl.pallas_call(
        functools.partial(_ssd_step_state_kernel, n_heads=d_inner // SSM_HEAD_DIM, t=t, nseq=nseq,
                          n_fill=n_fill),
        grid_spec=pltpu.PrefetchScalarGridSpec(
            num_scalar_prefetch=1,
            grid=(n_fill + 1, n_seq // nseq),
            in_specs=in_specs,
            out_specs=[
                pl.BlockSpec((rt, d_inner), rows),
                pl.BlockSpec((1, nseq, d_inner, d_state), lambda l, s, cd_: (out_layer(l), s, 0, 0)),
            ],
        ),
        out_shape=[
            jax.ShapeDtypeStruct((n_tok, d_inner), BF16),
            jax.ShapeDtypeStruct((n_layers, n_seq, d_inner, d_state), F32),
        ],
        input_output_aliases=aliases,
        compiler_params=_params("arbitrary", "arbitrary"),
        name="ssd_step_state",
    )(*args)


def _attn_prompt_kernel(sink_ref, q_ref, kvc_ref, kvp_ref, kvm_ref, o_ref, *, n_q_heads):
    n = pl.program_id(1)
    d = ATTN_HEAD_DIM
    kvw = N_KV_HEADS * d
    rep = n_q_heads // N_KV_HEADS
    scale = d ** -0.5
    w = q_ref.shape[1]
    rows = rep * w
    qi = lax.broadcasted_iota(jnp.int32, (rows, w), 0) % w
    ci = lax.broadcasted_iota(jnp.int32, (rows, w), 1)
    from_prev = ci > qi
    no_prev = jnp.where(n > 0, 0.0, -jnp.inf)
    kvm, kvp, kvc = kvm_ref[...], kvp_ref[0], kvc_ref[0]
    outs = []
    for k in range(N_KV_HEADS):
        ks, vs = slice(k * d, (k + 1) * d), slice(kvw + k * d, kvw + (k + 1) * d)
        qk = q_ref[0, :, k * rep * d:(k + 1) * rep * d]
        q4 = jnp.concatenate([qk[:, r * d:(r + 1) * d] for r in range(rep)], axis=0)
        q4 = (q4 * scale).astype(BF16)
        k2 = jnp.concatenate([kvp[:, ks], kvc[:, ks]], axis=0).astype(BF16)
        v2 = jnp.concatenate([kvp[:, vs], kvc[:, vs]], axis=0).astype(BF16)
        s2 = lax.dot_general(q4, k2, _NT, preferred_element_type=F32)
        s_w = jnp.where(from_prev, s2[:, :w] + no_prev, s2[:, w:])
        s_m = lax.dot_general(q4, kvm[:, ks].astype(BF16), _NT, preferred_element_type=F32)
        p_ws, p_ms, dens = [], [], []
        for r in range(rep):
            sl = slice(r * w, (r + 1) * w)
            sink = sink_ref[k * rep + r]
            mx = jnp.maximum(jnp.maximum(jnp.max(s_w[sl], axis=-1, keepdims=True),
                                         jnp.max(s_m[sl], axis=-1, keepdims=True)), sink)
            p_w = jnp.exp(s_w[sl] - mx)
            p_m = jnp.exp(s_m[sl] - mx)
            dens.append(jnp.exp(sink - mx) + jnp.sum(p_w, axis=-1, keepdims=True)
                        + jnp.sum(p_m, axis=-1, keepdims=True))
            p_ws.append(p_w)
            p_ms.append(p_m.astype(BF16))
        p_w = jnp.concatenate(p_ws, axis=0)
        p2 = jnp.concatenate([jnp.where(from_prev, p_w, 0.0), jnp.where(from_prev, 0.0, p_w)],
                             axis=1).astype(BF16)
        o4 = (jnp.dot(p2, v2, preferred_element_type=F32)
              + jnp.dot(jnp.concatenate(p_ms, axis=0), kvm[:, vs].astype(BF16), preferred_element_type=F32))
        outs.extend(o4[r * w:(r + 1) * w, :] / dens[r] for r in range(rep))
    o_ref[0] = jnp.concatenate(outs, axis=1).astype(BF16)


def attn_prompt(q, kv, kvm, sinks):
    bsz, s, dq = q.shape
    kvd = kv.shape[2]
    nb = s // WINDOW
    return pl.pallas_call(
        functools.partial(_attn_prompt_kernel, n_q_heads=dq // ATTN_HEAD_DIM),
        grid=(bsz, nb),
        in_specs=[
            pl.BlockSpec(memory_space=pltpu.SMEM),
            pl.BlockSpec((1, WINDOW, dq), lambda b, n: (b, n, 0)),
            pl.BlockSpec((1, WINDOW, kvd), lambda b, n: (b, n, 0)),
            pl.BlockSpec((1, WINDOW, kvd), lambda b, n: (b, jnp.maximum(n - 1, 0), 0)),
            pl.BlockSpec(kvm.shape, lambda b, n: (0, 0)),
        ],
        out_specs=pl.BlockSpec((1, WINDOW, dq), lambda b, n: (b, n, 0)),
        out_shape=jax.ShapeDtypeStruct((bsz, s, dq), BF16),
        compiler_params=_params("parallel", "arbitrary"),
        name="attn_prompt",
    )(sinks, q, kv, kv, kvm)


def _attn_sample_kernel(sink_ref, q_ref, kvn_ref, ck_ref, cv_ref, kvm_ref, o_ref, *, n_q_heads, n_meta):
    d = ATTN_HEAD_DIM
    kvw = N_KV_HEADS * d
    rep = n_q_heads // N_KV_HEADS
    scale = d ** -0.5
    nseq, t, _ = q_ref.shape
    w_buf = ck_ref.shape[1]
    rows = rep * t
    nx = n_meta + t
    tq_buf = lax.broadcasted_iota(jnp.int32, (rows, w_buf), 0) % t
    pos_buf = PAST_LEN - w_buf + lax.broadcasted_iota(jnp.int32, (rows, w_buf), 1)
    buf_mask = jnp.logical_and(PAST_LEN + tq_buf - pos_buf < WINDOW, pos_buf >= n_meta)
    tq_x = lax.broadcasted_iota(jnp.int32, (rows, nx), 0) % t
    c_x = lax.broadcasted_iota(jnp.int32, (rows, nx), 1) - n_meta
    x_mask = jnp.logical_or(c_x < 0, jnp.logical_and(c_x <= tq_x, tq_x - c_x < WINDOW))
    head_in_group = lax.broadcasted_iota(jnp.int32, (rows, 1), 0) // t
    kvm = kvm_ref[...]
    sinks = []
    for k in range(N_KV_HEADS):
        sink = jnp.zeros((rows, 1), F32)
        for r in range(rep):
            sink = jnp.where(head_in_group == r, sink_ref[k * rep + r], sink)
        sinks.append(sink)
    chains = [(i, k) for i in range(nseq) for k in range(N_KV_HEADS)]

    scores, values = {}, {}
    for i in range(nseq):
        q, kvn, ck, cv = q_ref[i], kvn_ref[i], ck_ref[i], cv_ref[i]
        for k in range(N_KV_HEADS):
            ks, vs = slice(k * d, (k + 1) * d), slice(kvw + k * d, kvw + (k + 1) * d)
            q4 = jnp.concatenate([q[:, (k * rep + r) * d:(k * rep + r + 1) * d] for r in range(rep)], axis=0)
            q4 = (q4 * scale).astype(BF16)
            kx = jnp.concatenate([kvm[:, ks], kvn[:, ks]], axis=0).astype(BF16)
            vx = jnp.concatenate([kvm[:, vs], kvn[:, vs]], axis=0).astype(BF16)
            s_b = lax.dot_general(q4, ck[:, ks].astype(BF16), _NT, preferred_element_type=F32)
            s_x = lax.dot_general(q4, kx, _NT, preferred_element_type=F32)
            scores[i, k] = (jnp.where(buf_mask, s_b, -jnp.inf), jnp.where(x_mask, s_x, -jnp.inf))
            values[i, k] = (cv[:, ks].astype(BF16), vx)

    maxes = {}
    for c in chains:
        s_b, s_x = scores[c]
        maxes[c] = jnp.maximum(sinks[c[1]], jnp.maximum(jnp.max(s_b, axis=-1, keepdims=True),
                                                         jnp.max(s_x, axis=-1, keepdims=True)))
    probs, dens = {}, {}
    for c in chains:
        s_b, s_x = scores[c]
        p_b, p_x = jnp.exp(s_b - maxes[c]), jnp.exp(s_x - maxes[c])
        probs[c] = (p_b.astype(BF16), p_x.astype(BF16))
        dens[c] = (jnp.exp(sinks[c[1]] - maxes[c]) + jnp.sum(p_b, axis=-1, keepdims=True)
                   + jnp.sum(p_x, axis=-1, keepdims=True))
    outs = {}
    for c in chains:
        outs[c] = (jnp.dot(probs[c][0], values[c][0], preferred_element_type=F32)
                   + jnp.dot(probs[c][1], values[c][1], preferred_element_type=F32)) / dens[c]
    for i in range(nseq):
        heads = [outs[i, k][r * t:(r + 1) * t, :] for k in range(N_KV_HEADS) for r in range(rep)]
        o_ref[i] = jnp.concatenate(heads, axis=1).astype(BF16)


def attn_sample(q, kvn, ck, cv, kvm, sinks, *, seqs_per_step=8):
    bsz, t, dq = q.shape
    kvd = kvn.shape[2]
    w_buf = ck.shape[1]
    g = _row_tile(bsz, seqs_per_step)
    return pl.pallas_call(
        functools.partial(_attn_sample_kernel, n_q_heads=dq // ATTN_HEAD_DIM, n_meta=kvm.shape[0]),
        grid=(bsz // g,),
        in_specs=[
            pl.BlockSpec(memory_space=pltpu.SMEM),
            pl.BlockSpec((g, t, dq), lambda b: (b, 0, 0)),
            pl.BlockSpec((g, t, kvd), lambda b: (b, 0, 0)),
            pl.BlockSpec((g, w_buf, kvd // 2), lambda b: (b, 0, 0)),
            pl.BlockSpec((g, w_buf, kvd // 2), lambda b: (b, 0, 0)),
            pl.BlockSpec(kvm.shape, lambda b: (0, 0)),
        ],
        out_specs=pl.BlockSpec((g, t, dq), lambda b: (b, 0, 0)),
        out_shape=jax.ShapeDtypeStruct((bsz, t, dq), BF16),
        compiler_params=_params("parallel"),
        name="attn_sample",
    )(sinks, q, kvn, ck, cv, kvm)


def kernel(x_prompt, x_sample, state_conv, state_ssm, cache_k_win, cache_v_win, meta_tokens, a_norm_w, a_in_proj, a_conv_w, a_conv_b, a_dt_bias, a_log, a_d_skip, a_gate_norm_w, a_out_proj, kv_norm_w, w_kv, b_norm_w, w_q, attn_sinks, w_o, mlp_norm_w, w_up, w_down, final_norm_w):
    n_prompt, seq, d_model = x_prompt.shape
    n_dec, dec_seq, _ = x_sample.shape
    n_a = a_in_proj.shape[0]
    depth = w_up.shape[0]
    n_meta = meta_tokens.shape[0]
    d_inner = a_out_proj.shape[1]
    conv_dim = a_conv_w.shape[2]
    n_heads = a_log.shape[1]
    w_buf = cache_k_win.shape[1]
    kvw = N_KV_HEADS * ATTN_HEAD_DIM
    assert n_heads * SSM_HEAD_DIM == d_inner and n_heads <= LANES

    hm = meta_tokens.astype(F32)
    hp = x_prompt.reshape(n_prompt * seq, d_model)
    hs = x_sample.reshape(n_dec * dec_seq, d_model)

    def pad_lanes(v):
        return jnp.pad(v, (0, LANES - v.shape[0])).reshape(1, LANES)

    def pad_conv_state(s):
        return jnp.pad(s, ((0, 0), (SUBLANES - (CONV_K - 1), 0), (0, 0)))

    conv_p_list, ssm_p_list, conv_s_list, ssm_s_list = [], [], [], []
    step_path = dec_seq == SUBLANES and (n_dec * dec_seq) % SSD_CHUNK == 0
    ssm_in_all = state_ssm.reshape(n_a, n_dec, d_inner, D_STATE)
    ssm_s_all = None
    kvm = kv_p = kv_s = None
    wu, wd = w_up.astype(BF16), w_down.astype(BF16)
    for layer in range(depth):
        last = layer == depth - 1
        if layer < n_a:
            i = layer
            w_in = a_in_proj[i]
            w_main = w_in.astype(BF16)
            w_dt = jnp.pad(w_in[:, d_inner + conv_dim:], ((0, 0), (0, LANES - n_heads))).astype(BF16)
            w_out = a_out_proj[i].astype(BF16)
            prm = (a_conv_w[i], a_conv_b[i].reshape(1, conv_dim), pad_lanes(a_dt_bias[i]), pad_lanes(a_log[i]),
                   jnp.repeat(a_d_skip[i], SSM_HEAD_DIM).reshape(1, d_inner), a_gate_norm_w[i].reshape(1, d_inner))

            def mixer(h, bsz, length, conv0, ssm0, shared):
                if shared and length % SSD_CHUNK == 0:
                    xs_a, bb_a, cc_a, sz_a, dt, conv_o = in_proj_conv(
                        h, a_norm_w[i], w_main, w_dt, 0.5 * prm[0], 0.5 * prm[1], conv0,
                        seq_len=length, d_inner=d_inner)
                    split = lambda t: t.reshape(bsz, length, t.shape[1])
                    g, ssm_o = ssd_chunked(split(xs_a), split(bb_a), split(cc_a), split(sz_a), split(dt),
                                           ssm0, *prm[2:])
                else:
                    xbcz, dt = in_proj(h, a_norm_w[i], w_main, w_dt, d_inner=d_inner, n_main=conv_dim + d_inner)
                    g, conv_o, ssm_o = ssd_mixer(xbcz.reshape(bsz, length, conv_dim + d_inner),
                                                 dt.reshape(bsz, length, LANES), conv0, ssm0, *prm,
                                                 shared_state=shared)
                return g.reshape(bsz * length, d_inner), conv_o, ssm_o

            zero_conv = jnp.zeros((1, SUBLANES, conv_dim), F32)
            zero_ssm = jnp.zeros((1, d_inner, D_STATE), F32)
            g_m, conv_m, ssm_m = mixer(hm, 1, n_meta, zero_conv, zero_ssm, True)
            g_p, conv_p, ssm_p = mixer(hp, n_prompt, seq, conv_m, ssm_m, True)
            conv0_s = pad_conv_state(state_conv[i])
            if step_path:
                xbcz_s, dt_s = in_proj(hs, a_norm_w[i], w_main, w_dt, d_inner=d_inner, n_main=conv_dim + d_inner)
                ypart, eax, xdtw, bb_s, cc_s, eacs = ssd_step_tokens(xbcz_s, dt_s, conv0_s, *prm[:5], t=dec_seq)
                cdecay = eacs.reshape(n_dec, dec_seq, LANES)[:, dec_seq - 1]
                g_s, ssm_s_all = ssd_step_state(cdecay, ssm_in_all, i, cc_s, bb_s, xdtw, eax, ypart, xbcz_s,
                                                prm[5], ssm_s_all, t=dec_seq)
                conv_s = xbcz_s.reshape(n_dec, dec_seq, -1)[:, dec_seq - (CONV_K - 1):, :conv_dim]
            else:
                g_s, conv_s, ssm_s = mixer(hs, n_dec, dec_seq, conv0_s, ssm_in_all[i], False)
                conv_s = conv_s[:, SUBLANES - (CONV_K - 1):]
                ssm_s_list.append(ssm_s)
            conv_p_list.append(conv_p[:, SUBLANES - (CONV_K - 1):])
            ssm_p_list.append(ssm_p.reshape(n_prompt, n_heads, SSM_HEAD_DIM, D_STATE))
            conv_s_list.append(conv_s)
            hm = mlp(hm, mlp_norm_w[layer], wu, wd, layer, final_norm_w, final_norm=False, proj=(g_m, w_out))
            proj_p, proj_s = (g_p, w_out), (g_s, w_out)
        else:
            j = layer - n_a
            if j == 0:
                wkv = w_kv.astype(BF16)
                kvm = norm_matmul(hm, kv_norm_w, wkv)
                kv_p = norm_matmul(hp, kv_norm_w, wkv).reshape(n_prompt, seq, 2 * kvw)
                kv_s = norm_matmul(hs, kv_norm_w, wkv).reshape(n_dec, dec_seq, 2 * kvw)
            wq = w_q[j].astype(BF16)
            wo = w_o[j].astype(BF16)
            dq = wq.shape[1]
            q_p = norm_matmul(hp, b_norm_w[j], wq).reshape(n_prompt, seq, dq)
            q_s = norm_matmul(hs, b_norm_w[j], wq).reshape(n_dec, dec_seq, dq)
            o_p = attn_prompt(q_p, kv_p, kvm, attn_sinks[j])
            o_s = attn_sample(q_s, kv_s, cache_k_win.reshape(n_dec, w_buf, kvw),
                              cache_v_win.reshape(n_dec, w_buf, kvw), kvm, attn_sinks[j])
            proj_p = (o_p.reshape(n_prompt * seq, dq), wo)
            proj_s = (o_s.reshape(n_dec * dec_seq, dq), wo)
        hp = mlp(hp, mlp_norm_w[layer], wu, wd, layer, final_norm_w, final_norm=last, proj=proj_p)
        hs = mlp(hs, mlp_norm_w[layer], wu, wd, layer, final_norm_w, final_norm=last, proj=proj_s)

    y_prompt = hp.reshape(n_prompt, seq, d_model)
    y_sample = hs.reshape(n_dec, dec_seq, d_model)
    kv_heads = (N_KV_HEADS, ATTN_HEAD_DIM)
    k_p = kv_p[:, seq - w_buf:, :kvw].reshape((n_prompt, w_buf) + kv_heads)
    v_p = kv_p[:, seq - w_buf:, kvw:].reshape((n_prompt, w_buf) + kv_heads)
    k_s = kv_s[:, :, :kvw].reshape((n_dec, dec_seq) + kv_heads)
    v_s = kv_s[:, :, kvw:].reshape((n_dec, dec_seq) + kv_heads)
    k_s_win = jnp.concatenate([cache_k_win, k_s], axis=1)[:, -w_buf:]
    v_s_win = jnp.concatenate([cache_v_win, v_s], axis=1)[:, -w_buf:]
    if not step_path:
        ssm_s_all = jnp.stack(ssm_s_list)
    return (y_prompt, y_sample, jnp.stack(conv_p_list), jnp.stack(ssm_p_list), k_p, v_p,
            jnp.stack(conv_s_list), ssm_s_all.reshape(state_ssm.shape), k_s_win, v_s_win)
```

```python
import functools

import jax
import jax.numpy as jnp
from jax import lax
from jax.experimental import pallas as pl
from jax.experimental.pallas import tpu as pltpu

F32 = jnp.float32
BF16 = jnp.bfloat16

N_GROUPS = 8
SSM_HEAD_DIM = 64
D_STATE = 128
CONV_K = 4
SSD_CHUNK = 128
ATTN_HEAD_DIM = 64
N_KV_HEADS = 4
WINDOW = 128
PAST_LEN = 8192
EPS = 1e-5
LOG2_E = 1.4426950408889634

LANES = 128
SUBLANES = 8
VMEM_LIMIT_BYTES = 52 * 1024 * 1024

_NT = (((1,), (1,)), ((), ()))
_TN = (((0,), (0,)), ((), ()))


def _params(*sem):
    return pltpu.CompilerParams(dimension_semantics=sem, vmem_limit_bytes=VMEM_LIMIT_BYTES)


def _rms(x, w):
    ms = jnp.mean(x * x, axis=-1, keepdims=True)
    return x * lax.rsqrt(ms + EPS) * w


def _silu(x):
    s = 0.5 * x
    return s + s * jnp.tanh(s)


def _row_tile(m, cap):
    t = min(m, cap)
    assert m % t == 0, (m, t)
    return t


def _norm_matmul_kernel(x_ref, nw_ref, w_ref, o_ref, xn_ref):
    @pl.when(pl.program_id(1) == 0)
    def _():
        xn_ref[...] = _rms(x_ref[...], nw_ref[...]).astype(BF16)

    o_ref[...] = jnp.dot(xn_ref[...], w_ref[...], preferred_element_type=F32)


def norm_matmul(x, nw, w, *, tm_cap=1024, tn_cap=1024):
    m, d = x.shape
    n = w.shape[1]
    tm, tn = _row_tile(m, tm_cap), _row_tile(n, tn_cap)
    return pl.pallas_call(
        _norm_matmul_kernel,
        grid=(m // tm, n // tn),
        in_specs=[
            pl.BlockSpec((tm, d), lambda i, j: (i, 0)),
            pl.BlockSpec((1, d), lambda i, j: (0, 0)),
            pl.BlockSpec((d, tn), lambda i, j: (0, j)),
        ],
        out_specs=pl.BlockSpec((tm, tn), lambda i, j: (i, j)),
        out_shape=jax.ShapeDtypeStruct((m, n), F32),
        scratch_shapes=[pltpu.VMEM((tm, d), BF16)],
        compiler_params=_params("parallel", "arbitrary"),
        name="norm_matmul",
    )(x, nw.reshape(1, d), w)


def _norm_matmul_pair_kernel(x_ref, nwa_ref, wa_ref, nwb_ref, wb_ref, oa_ref, ob_ref):
    tm = x_ref.shape[0]
    rb = min(tm, 2 * LANES)

    def normalise(r, _):
        x = x_ref[r * rb:(r + 1) * rb, :]
        xh = x * lax.rsqrt(jnp.mean(x * x, axis=-1, keepdims=True) + EPS)
        return (xh * nwa_ref[...]).astype(BF16), (xh * nwb_ref[...]).astype(BF16)

    def project(r, xn):
        rows = slice(r * rb, (r + 1) * rb)
        oa_ref[rows, :] = jnp.dot(xn[0], wa_ref[...], preferred_element_type=F32)
        ob_ref[rows, :] = jnp.dot(xn[1], wb_ref[...], preferred_element_type=F32)

    _staggered(tm // rb, [normalise, project])


def norm_matmul_pair(x, nwa, wa, nwb, wb, *, tm_cap=1024):
    m, d = x.shape
    tm = _row_tile(m, tm_cap)
    row = lambda i: (i, 0)
    const = lambda i: (0, 0)
    return pl.pallas_call(
        _norm_matmul_pair_kernel,
        grid=(m // tm,),
        in_specs=[
            pl.BlockSpec((tm, d), row),
            pl.BlockSpec((1, d), const),
            pl.BlockSpec(wa.shape, const),
            pl.BlockSpec((1, d), const),
            pl.BlockSpec(wb.shape, const),
        ],
        out_specs=[pl.BlockSpec((tm, wa.shape[1]), row), pl.BlockSpec((tm, wb.shape[1]), row)],
        out_shape=[jax.ShapeDtypeStruct((m, wa.shape[1]), F32), jax.ShapeDtypeStruct((m, wb.shape[1]), F32)],
        compiler_params=_params("parallel"),
        name="norm_matmul_pair",
    )(x, nwa.reshape(1, d), wa, nwb.reshape(1, d), wb)


def _in_proj_kernel(x_ref, nw_ref, w_ref, wdt_ref, o_ref, dt_ref, xn_ref):
    @pl.when(pl.program_id(1) == 0)
    def _():
        xn = _rms(x_ref[...], nw_ref[...]).astype(BF16)
        xn_ref[...] = xn
        dt_ref[...] = jnp.dot(xn, wdt_ref[...], preferred_element_type=F32)

    o_ref[...] = jnp.dot(xn_ref[...], w_ref[...], preferred_element_type=F32)


def in_proj(x, nw, w, wdt, *, d_inner, n_main, tm_cap=1024, tn_cap=1024):
    m, d = x.shape
    n = n_main
    tm, tn = _row_tile(m, tm_cap), _row_tile(n, tn_cap)
    assert d_inner % tn == 0
    return pl.pallas_call(
        _in_proj_kernel,
        grid=(m // tm, n // tn),
        in_specs=[
            pl.BlockSpec((tm, d), lambda i, j: (i, 0)),
            pl.BlockSpec((1, d), lambda i, j: (0, 0)),
            pl.BlockSpec((d, tn), lambda i, j: (0, (j + d_inner // tn) % (n // tn))),
            pl.BlockSpec((d, LANES), lambda i, j: (0, 0)),
        ],
        out_specs=[
            pl.BlockSpec((tm, tn), lambda i, j: (i, j)),
            pl.BlockSpec((tm, LANES), lambda i, j: (i, 0)),
        ],
        out_shape=[jax.ShapeDtypeStruct((m, n), F32), jax.ShapeDtypeStruct((m, LANES), F32)],
        scratch_shapes=[pltpu.VMEM((tm, d), BF16)],
        compiler_params=_params("parallel", "arbitrary"),
        name="in_proj",
    )(x, nw.reshape(1, d), w, wdt)


def _in_proj_conv_kernel(x_ref, nw_ref, w_ref, wdt_ref, cw_ref, cb_ref, conv0_ref,
                         xs_ref, bb_ref, cc_ref, sz_ref, dt_ref, tail_ref,
                         xn_ref, xpad_ref, halo_ref, *, tiles_per_seq, n_x, n_b):
    i = pl.program_id(0)
    j = pl.program_id(1)
    tm = x_ref.shape[0]
    tn = w_ref.shape[1]
    pad = SUBLANES
    sub = 4 * LANES

    @pl.when(j == 0)
    def _():
        xn = _rms(x_ref[...], nw_ref[...]).astype(BF16)
        xn_ref[...] = xn
        dt_ref[...] = jnp.dot(xn, wdt_ref[...], preferred_element_type=F32)

    def conv_tile(out_ref):
        @pl.when(i % tiles_per_seq == 0)
        def _():
            for l in range(tn // LANES):
                halo_ref[j, l] = conv0_ref[0, :, l * LANES:(l + 1) * LANES]

        def conv_act(s, raw):
            for e in range(sub // LANES):
                l = s * (sub // LANES) + e
                cs = slice(l * LANES, (l + 1) * LANES)
                xpad_ref[l, 0:pad, :] = halo_ref[j, l]
                xpad_ref[l, pad:pad + tm, :] = raw[:, e * LANES:(e + 1) * LANES]
                half = cb_ref[:, cs]
                for k in range(CONV_K):
                    off = pad - (CONV_K - 1) + k
                    half = half + xpad_ref[l, off:off + tm, :] * cw_ref[k:k + 1, cs]
                out_ref[:, cs] = (half + half * jnp.tanh(half)).astype(out_ref.dtype)
                last_rows = xpad_ref[l, tm:tm + pad, :]
                halo_ref[j, l] = last_rows
                tail_ref[0, :, cs] = last_rows

        _staggered(tn // sub, [matmul_cols, conv_act])

    def matmul_cols(s, _):
        return jnp.dot(xn_ref[...], w_ref[:, s * sub:(s + 1) * sub], preferred_element_type=F32)

    def gate_act(s, raw):
        sz_ref[:, s * sub:(s + 1) * sub] = _silu(raw).astype(BF16)

    pl.when(j < n_x)(lambda: conv_tile(xs_ref))
    pl.when(jnp.logical_and(j >= n_x, j < n_x + n_b))(lambda: conv_tile(bb_ref))
    pl.when(jnp.logical_and(j >= n_x + n_b, j < n_x + 2 * n_b))(lambda: conv_tile(cc_ref))
    pl.when(j >= n_x + 2 * n_b)(lambda: _staggered(tn // sub, [matmul_cols, gate_act]))


def in_proj_conv(x, nw, w, wdt, cw, cb, conv0, *, seq_len, d_inner, tm_cap=1024):
    m, d = x.shape
    conv_dim = cw.shape[1]
    bc = (conv_dim - d_inner) // 2
    tn = bc
    tm = _row_tile(seq_len, tm_cap)
    assert d_inner % tn == 0 and w.shape[1] >= conv_dim + d_inner and m % seq_len == 0
    n_x, n_b, n_z = d_inner // tn, 1, d_inner // tn
    n_conv = n_x + 2 * n_b
    conv_col = lambda i, j: (0, jnp.minimum(j, n_conv - 1))
    bf = lambda n: jax.ShapeDtypeStruct((m, n), BF16)
    tiles_per_seq = seq_len // tm
    *acts, tails = pl.pallas_call(
        functools.partial(_in_proj_conv_kernel, tiles_per_seq=tiles_per_seq, n_x=n_x, n_b=n_b),
        grid=(m // tm, n_conv + n_z),
        in_specs=[
            pl.BlockSpec((tm, d), lambda i, j: (i, 0)),
            pl.BlockSpec((1, d), lambda i, j: (0, 0)),
            pl.BlockSpec((d, tn), lambda i, j: (0, (j + n_z) % (n_conv + n_z))),
            pl.BlockSpec((d, LANES), lambda i, j: (0, 0)),
            pl.BlockSpec((CONV_K, tn), conv_col),
            pl.BlockSpec((1, tn), conv_col),
            pl.BlockSpec((1, SUBLANES, tn), lambda i, j: (0, 0, jnp.minimum(j, n_conv - 1))),
        ],
        out_specs=[
            pl.BlockSpec((tm, tn), lambda i, j: (i, jnp.minimum(j, n_x - 1))),
            pl.BlockSpec((tm, tn), lambda i, j: (i, 0)),
            pl.BlockSpec((tm, tn), lambda i, j: (i, 0)),
            pl.BlockSpec((tm, tn), lambda i, j: (i, jnp.clip(j - n_conv, 0, n_z - 1))),
            pl.BlockSpec((tm, LANES), lambda i, j: (i, 0)),
            pl.BlockSpec((1, SUBLANES, tn), lambda i, j: (i, 0, jnp.minimum(j, n_conv - 1))),
        ],
        out_shape=[jax.ShapeDtypeStruct((m, d_inner), F32), bf(bc), bf(bc), bf(d_inner),
                   jax.ShapeDtypeStruct((m, LANES), F32),
                   jax.ShapeDtypeStruct((m // tm, SUBLANES, conv_dim), F32)],
        scratch_shapes=[
            pltpu.VMEM((tm, d), BF16),
            pltpu.VMEM((tn // LANES, tm + SUBLANES, LANES), F32),
            pltpu.VMEM((n_conv, tn // LANES, SUBLANES, LANES), F32),
        ],
        compiler_params=_params("arbitrary", "arbitrary"),
        name="in_proj_conv",
    )(x, nw.reshape(1, d), w, wdt, cw, cb, conv0)
    return (*acts, tails[tiles_per_seq - 1::tiles_per_seq])


def _mlp_steps(x_rows, nw_ref, wu_ref, wd_ref, fw_ref, o_ref, xn_ref, final_norm):
    f = pl.program_id(1)
    tm = o_ref.shape[0]
    rb = min(tm, 2 * LANES)

    @pl.when(f == 0)
    def _():
        def normalise(r, x):
            rows = slice(r * rb, (r + 1) * rb)
            xn_ref[rows, :] = _rms(x, nw_ref[...]).astype(BF16)
            o_ref[rows, :] = x

        _staggered(tm // rb, [lambda r, _: x_rows(slice(r * rb, (r + 1) * rb)), normalise])

    h = jnp.dot(xn_ref[...], wu_ref[...], preferred_element_type=F32)
    h = jnp.square(jnp.maximum(h, 0.0)).astype(BF16)
    o_ref[...] += jnp.dot(h, wd_ref[...], preferred_element_type=F32)

    if final_norm:
        @pl.when(f == pl.num_programs(1) - 1)
        def _():
            o_ref[...] = _rms(o_ref[...], fw_ref[...])


def _mlp_kernel(x_ref, nw_ref, wu_ref, wd_ref, fw_ref, o_ref, xn_ref, *, final_norm):
    _mlp_steps(lambda rows: x_ref[rows, :], nw_ref, wu_ref, wd_ref, fw_ref, o_ref, xn_ref, final_norm)


def _proj_mlp_kernel(a_ref, wa_ref, x_ref, nw_ref, wu_ref, wd_ref, fw_ref, o_ref, xn_ref, *, final_norm):
    def block_input(rows):
        return x_ref[rows, :] + jnp.dot(a_ref[rows, :], wa_ref[...], preferred_element_type=F32)

    _mlp_steps(block_input, nw_ref, wu_ref, wd_ref, fw_ref, o_ref, xn_ref, final_norm)


def mlp(x, nw, wu, wd, layer, fw, *, final_norm, proj=None, tm_cap=1024, tf_cap=1024):
    m, d = x.shape
    dff = wu.shape[2]
    tm, tf = _row_tile(m, tm_cap), _row_tile(dff, tf_cap)
    in_specs = [
        pl.BlockSpec((tm, d), lambda i, f: (i, 0)),
        pl.BlockSpec((1, d), lambda i, f: (0, 0)),
        pl.BlockSpec((None, d, tf), lambda i, f: (layer, 0, f)),
        pl.BlockSpec((None, tf, d), lambda i, f: (layer, f, 0)),
        pl.BlockSpec((1, d), lambda i, f: (0, 0)),
    ]
    args = [x, nw.reshape(1, d), wu, wd, fw.reshape(1, d)]
    body = _mlp_kernel
    if proj is not None:
        a, wa = proj
        k = a.shape[1]
        in_specs = [pl.BlockSpec((tm, k), lambda i, f: (i, 0)),
                    pl.BlockSpec((k, d), lambda i, f: (0, 0), pipeline_mode=pl.Buffered(1))] + in_specs
        args = [a, wa] + args
        body = _proj_mlp_kernel
    return pl.pallas_call(
        functools.partial(body, final_norm=final_norm),
        grid=(m // tm, dff // tf),
        in_specs=in_specs,
        out_specs=pl.BlockSpec((tm, d), lambda i, f: (i, 0)),
        out_shape=jax.ShapeDtypeStruct((m, d), F32),
        scratch_shapes=[pltpu.VMEM((tm, d), BF16)],
        compiler_params=_params("parallel", "arbitrary"),
        name="mlp" if proj is None else "proj_mlp",
    )(*args)


def _softplus(x):
    return jnp.maximum(x, 0.0) + jnp.log(1.0 + jnp.exp(-jnp.abs(x)))


def _split3(a):
    hi = a.astype(BF16)
    r1 = a - hi.astype(F32)
    mid = r1.astype(BF16)
    lo = (r1 - mid.astype(F32)).astype(BF16)
    return hi, mid, lo


def _ssd_kernel(xbc_ref, z_ref, dt_ref, conv0_ref, ssm0_ref,
                cw_ref, cb_ref, dtb_ref, alog_ref, dsk_ref, gw_ref,
                g_ref, convo_ref, ssmo_ref,
                xpad_ref, act_ref, state_ref, y_ref, *, q, n_heads):
    c = pl.program_id(1)
    p = SSM_HEAD_DIM
    d_inner = n_heads * p
    hpg = n_heads // N_GROUPS
    conv_dim = d_inner + 2 * N_GROUPS * D_STATE
    pad = SUBLANES

    @pl.when(c == 0)
    def _():
        xpad_ref[0:pad, :] = conv0_ref[0]
        state_ref[...] = ssm0_ref[0]

    xpad_ref[pad:pad + q, :] = xbc_ref[0]
    cblk = 512
    for j in range(conv_dim // cblk):
        cs = slice(j * cblk, (j + 1) * cblk)
        conv = cb_ref[:, cs]
        for k in range(CONV_K):
            off = pad - (CONV_K - 1) + k
            conv = conv + xpad_ref[off:off + q, cs] * cw_ref[k:k + 1, cs]
        act_ref[:, cs] = _silu(conv)
    xpad_ref[0:pad, :] = xpad_ref[q:q + pad, :]

    dt = _softplus(dt_ref[0] + dtb_ref[...])
    a = dt * (-jnp.exp(alog_ref[...]))
    row = lax.broadcasted_iota(jnp.int32, (q, q), 0)
    col = lax.broadcasted_iota(jnp.int32, (q, q), 1)
    causal = row >= col
    tri = jnp.where(causal, 1.0, 0.0).astype(BF16)
    acs = None
    for part in _split3(a):
        t = jnp.dot(tri, part, preferred_element_type=F32)
        acs = t if acs is None else acs + t
    if q < LANES:
        acs_sq = jnp.concatenate([acs, jnp.zeros((LANES - q, LANES), F32)], axis=0)
    else:
        acs_sq = acs
    acs_t = acs_sq.T
    eacs = jnp.exp(acs)
    last = acs[q - 1:q, :]
    dte = jnp.exp(last - acs)
    cdecay = jnp.exp(last)

    for g in range(N_GROUPS):
        b0 = d_inner + g * D_STATE
        c0 = d_inner + N_GROUPS * D_STATE + g * D_STATE
        bb = act_ref[:, b0:b0 + D_STATE].astype(BF16)
        cc = act_ref[:, c0:c0 + D_STATE].astype(BF16)
        cbm = lax.dot_general(cc, bb, _NT, preferred_element_type=F32)
        for pair in range(hpg // 2):
            xs2 = act_ref[:, (g * hpg + 2 * pair) * p:(g * hpg + 2 * pair + 2) * p]
            dsk2 = dsk_ref[:, (g * hpg + 2 * pair) * p:(g * hpg + 2 * pair + 2) * p]
            ys = []
            for e in range(2):
                h = g * hpg + 2 * pair + e
                xs = xs2[:, e * p:(e + 1) * p]
                seg = acs[:, h:h + 1] - acs_t[h:h + 1, 0:q]
                decay = jnp.exp(jnp.where(causal, seg, -jnp.inf))
                m = (cbm * decay).astype(BF16)
                xdt = xs * dt[:, h:h + 1]
                y = jnp.dot(m, xdt.astype(BF16), preferred_element_type=F32)
                st = state_ref[h * p:(h + 1) * p, :]
                yoff = lax.dot_general(cc, st.astype(BF16), _NT, preferred_element_type=F32)
                y = y + yoff * eacs[:, h:h + 1] + dsk2[:, e * p:(e + 1) * p] * xs
                ys.append(y)
                xdtw = (xdt * dte[:, h:h + 1]).astype(BF16)
                snew = lax.dot_general(xdtw, bb, _TN, preferred_element_type=F32)
                state_ref[h * p:(h + 1) * p, :] = (
                    jnp.broadcast_to(cdecay[:, h:h + 1], (p, D_STATE)) * st + snew)
            y_ref[:, (g * hpg + 2 * pair) * p:(g * hpg + 2 * pair + 2) * p] = (
                jnp.concatenate(ys, axis=1))

    gsz = d_inner // N_GROUPS
    for g in range(N_GROUPS):
        gs = slice(g * gsz, (g + 1) * gsz)
        gg = y_ref[:, gs] * _silu(z_ref[0, :, gs])
        ms = jnp.mean(gg * gg, axis=-1, keepdims=True)
        g_ref[0, :, gs] = (gg * lax.rsqrt(ms + EPS) * gw_ref[:, gs]).astype(BF16)

    @pl.when(c == pl.num_programs(1) - 1)
    def _():
        convo_ref[0] = xpad_ref[0:pad, :]
        ssmo_ref[0] = state_ref[...]


def ssd_mixer(xbcz, dt, conv0, ssm0, cw, cb, dtb, alog, dsk, gw, *, shared_state):
    bsz, length, _ = xbcz.shape
    conv_dim = cw.shape[1]
    d_inner = gw.shape[1]
    n_heads = d_inner // SSM_HEAD_DIM
    q = SSD_CHUNK if length % SSD_CHUNK == 0 else length
    assert q % SUBLANES == 0 and q >= SUBLANES and conv_dim % d_inner == 0
    nc = length // q
    zblk = conv_dim // d_inner
    if shared_state:
        st_map = lambda b, c: (0, 0, 0)
    else:
        st_map = lambda b, c: (b, 0, 0)
    const = lambda b, c: (0, 0)
    return pl.pallas_call(
        functools.partial(_ssd_kernel, q=q, n_heads=n_heads),
        grid=(bsz, nc),
        in_specs=[
            pl.BlockSpec((1, q, conv_dim), lambda b, c: (b, c, 0)),
            pl.BlockSpec((1, q, d_inner), lambda b, c: (b, c, zblk)),
            pl.BlockSpec((1, q, LANES), lambda b, c: (b, c, 0)),
            pl.BlockSpec((1, SUBLANES, conv_dim), st_map),
            pl.BlockSpec((1, n_heads * SSM_HEAD_DIM, D_STATE), st_map),
            pl.BlockSpec((CONV_K, conv_dim), const),
            pl.BlockSpec((1, conv_dim), const),
            pl.BlockSpec((1, LANES), const),
            pl.BlockSpec((1, LANES), const),
            pl.BlockSpec((1, d_inner), const),
            pl.BlockSpec((1, d_inner), const),
        ],
        out_specs=[
            pl.BlockSpec((1, q, d_inner), lambda b, c: (b, c, 0)),
            pl.BlockSpec((1, SUBLANES, conv_dim), lambda b, c: (b, 0, 0)),
            pl.BlockSpec((1, n_heads * SSM_HEAD_DIM, D_STATE), lambda b, c: (b, 0, 0)),
        ],
        out_shape=[
            jax.ShapeDtypeStruct((bsz, length, d_inner), BF16),
            jax.ShapeDtypeStruct((bsz, SUBLANES, conv_dim), F32),
            jax.ShapeDtypeStruct((bsz, n_heads * SSM_HEAD_DIM, D_STATE), F32),
        ],
        scratch_shapes=[
            pltpu.VMEM((q + SUBLANES, conv_dim), F32),
            pltpu.VMEM((q, conv_dim), F32),
            pltpu.VMEM((n_heads * SSM_HEAD_DIM, D_STATE), F32),
            pltpu.VMEM((q, d_inner), F32),
        ],
        compiler_params=_params("parallel", "arbitrary"),
        name="ssd_mixer",
    )(xbcz, xbcz, dt, conv0, ssm0, cw, cb, dtb, alog, dsk, gw)


def _expansion_matrix(n_heads):
    h = jnp.arange(LANES)[:, None]
    ex = h == jnp.arange(n_heads * SSM_HEAD_DIM)[None, :] // SSM_HEAD_DIM
    return jnp.tile(ex.astype(BF16), (3, 1))


def _store_act_tile(j, act, xs_ref, bb_ref, cc_ref, d_inner, bc):
    lo = j * LANES
    if lo < d_inner:
        xs_ref[:, lo:lo + LANES] = act
    elif lo < d_inner + bc:
        bb_ref[:, lo - d_inner:lo - d_inner + LANES] = act.astype(BF16)
    else:
        cc_ref[:, lo - d_inner - bc:lo - d_inner - bc + LANES] = act.astype(BF16)


def _ssd_token_math(dt_raw, dtb_ref, alog_ref, lhs3_ref, acst_ref, *, seg_len):
    q = dt_raw.shape[0]
    dt = _softplus(dt_raw + dtb_ref[...])
    a = dt * (-jnp.exp(alog_ref[...]))
    row = lax.broadcasted_iota(jnp.int32, (q, q), 0)
    col = lax.broadcasted_iota(jnp.int32, (q, q), 1)
    mask = row >= col
    if seg_len != q:
        mask = jnp.logical_and(mask, row // seg_len == col // seg_len)
        seg_end = (row // seg_len) * seg_len + (seg_len - 1)
    tri = jnp.where(mask, 1.0, 0.0).astype(BF16)
    acs = None
    for part in _split3(a):
        t = jnp.dot(tri, part, preferred_element_type=F32)
        acs = t if acs is None else acs + t
    if seg_len == q:
        last = acs[q - 1:q, :]
    else:
        sel = jnp.where(col == seg_end, 1.0, 0.0).astype(BF16)
        last = None
        for part in _split3(acs):
            t = jnp.dot(sel, part, preferred_element_type=F32)
            last = t if last is None else last + t
    acst_ref[...] = (acs * LOG2_E).T
    eacs = jnp.exp(acs)
    stack = jnp.concatenate([dt, eacs, jnp.exp(last - acs)], axis=0)
    lhs3_ref[...] = jnp.concatenate(_split3(stack), axis=1)
    return mask, eacs


def _staggered(n, stages):
    carried = {}
    for t in range(n + len(stages) - 1):
        for k, stage in enumerate(stages):
            g = t - k
            if 0 <= g < n:
                carried[g] = stage(g, carried.get(g))


def _ssd_group_inputs(g, xs_ref, bb_ref, cc_ref, lhs3_ref, ex3_ref, hpg):
    q = xs_ref.shape[0]
    gch = hpg * SSM_HEAD_DIM
    gs = slice(g * gch, (g + 1) * gch)
    bb = bb_ref[:, g * D_STATE:(g + 1) * D_STATE]
    cc = cc_ref[:, g * D_STATE:(g + 1) * D_STATE]
    cbm = lax.dot_general(cc, bb, _NT, preferred_element_type=F32)
    ex = jnp.dot(lhs3_ref[...], ex3_ref[:, gs], preferred_element_type=F32)
    xs = xs_ref[:, gs]
    return dict(bb=bb, cc=cc, cbm=cbm, xs=xs, xdt=xs * ex[0:q], eax=ex[q:2 * q], dtex=ex[2 * q:3 * q])


def _ssd_group_diag(g, v, mask, acst_ref, hpg):
    q = v["xs"].shape[0]
    p = SSM_HEAD_DIM
    lane_head = lax.broadcasted_iota(jnp.int32, (q, hpg * p), 1) // p
    ms, rhs = [], []
    for r in range(hpg):
        h = g * hpg + r
        rowb = jnp.broadcast_to(acst_ref[h:h + 1, :], (q, q))
        decay = jnp.exp2(jnp.where(mask, rowb.T - rowb, -jnp.inf))
        ms.append((v["cbm"] * decay).astype(BF16))
        rhs.append(jnp.where(lane_head == r, v["xdt"], 0.0).astype(BF16))
    y = jnp.dot(jnp.concatenate(ms, axis=1), jnp.concatenate(rhs, axis=0),
                preferred_element_type=F32)
    return dict(v, y=y, cbm=None)


def _ssd_chunk_kernel(xs_ref, bb_ref, cc_ref, sz_ref, dt_ref, ssm0_ref,
                      dtb_ref, alog_ref, dsk_ref, gw_ref, ex3_ref,
                      g_ref, ssmo_ref,
                      st_ref, lhs3_ref, acst_ref, *, n_heads, cps):
    c = pl.program_id(1)
    q = SSD_CHUNK
    hpg = n_heads // N_GROUPS
    gch = hpg * SSM_HEAD_DIM

    @pl.when(c == 0)
    def _():
        st_ref[...] = ssm0_ref[0].T

    rows = [pl.ds(h * q, q) for h in range(cps)]
    masks = [_ssd_token_math(dt_ref[0, rows[h], :], dtb_ref, alog_ref, lhs3_ref.at[h], acst_ref.at[h],
                             seg_len=q)[0] for h in range(cps)]

    def finish(item, v):
        h, g = divmod(item, N_GROUPS)
        gs = slice(g * gch, (g + 1) * gch)
        st = st_ref[:, gs]
        y = (v["y"] + jnp.dot(v["cc"], st.astype(BF16), preferred_element_type=F32) * v["eax"]
             + dsk_ref[:, gs] * v["xs"])
        gg = y * sz_ref[0, rows[h], gs].astype(F32)
        msq = jnp.mean(gg * gg, axis=-1, keepdims=True)
        g_ref[0, rows[h], gs] = (gg * lax.rsqrt(msq + EPS) * gw_ref[:, gs]).astype(BF16)
        xdtw = (v["xdt"] * v["dtex"]).astype(BF16)
        snew = lax.dot_general(v["bb"], xdtw, _TN, preferred_element_type=F32)
        st_ref[:, gs] = st * v["eax"][q - 1:q, :] + snew

    def inputs(item, _):
        h, g = divmod(item, N_GROUPS)
        return _ssd_group_inputs(g, xs_ref.at[0, rows[h]], bb_ref.at[0, rows[h]], cc_ref.at[0, rows[h]],
                                 lhs3_ref.at[h], ex3_ref, hpg)

    def diag(item, v):
        h, g = divmod(item, N_GROUPS)
        return _ssd_group_diag(g, v, masks[h], acst_ref.at[h], hpg)

    _staggered(cps * N_GROUPS, [inputs, diag, finish])

    @pl.when(c == pl.num_programs(1) - 1)
    def _():
        ssmo_ref[0] = st_ref[...].T


def ssd_chunked(xs, bb, cc, sz, dt, ssm0, dtb, alog, dsk, gw, *, chunks_per_step=4):
    bsz, length, d_inner = xs.shape
    bc = bb.shape[2]
    n_heads = d_inner // SSM_HEAD_DIM
    q = SSD_CHUNK
    assert length % q == 0 and bc == N_GROUPS * D_STATE and n_heads % N_GROUPS == 0
    cps = _row_tile(length // q, chunks_per_step)
    qs = cps * q
    ex3 = _expansion_matrix(n_heads)
    rows = lambda b, c: (b, c, 0)
    const = lambda b, c: (0, 0)
    return pl.pallas_call(
        functools.partial(_ssd_chunk_kernel, n_heads=n_heads, cps=cps),
        grid=(bsz, length // qs),
        in_specs=[
            pl.BlockSpec((1, qs, d_inner), rows),
            pl.BlockSpec((1, qs, bc), rows),
            pl.BlockSpec((1, qs, bc), rows),
            pl.BlockSpec((1, qs, d_inner), rows),
            pl.BlockSpec((1, qs, LANES), rows),
            pl.BlockSpec((1, d_inner, D_STATE), lambda b, c: (0, 0, 0)),
            pl.BlockSpec((1, LANES), const),
            pl.BlockSpec((1, LANES), const),
            pl.BlockSpec((1, d_inner), const),
            pl.BlockSpec((1, d_inner), const),
            pl.BlockSpec(ex3.shape, const),
        ],
        out_specs=[
            pl.BlockSpec((1, qs, d_inner), rows),
            pl.BlockSpec((1, d_inner, D_STATE), lambda b, c: (b, 0, 0)),
        ],
        out_shape=[
            jax.ShapeDtypeStruct((bsz, length, d_inner), BF16),
            jax.ShapeDtypeStruct((bsz, d_inner, D_STATE), F32),
        ],
        scratch_shapes=[
            pltpu.VMEM((D_STATE, d_inner), F32),
            pltpu.VMEM((cps, 3 * q, 3 * LANES), BF16),
            pltpu.VMEM((cps, LANES, q), F32),
        ],
        compiler_params=_params("parallel", "arbitrary"),
        name="ssd_chunked",
    )(xs, bb, cc, sz, dt, ssm0, dtb, alog, dsk, gw, ex3)


def _ssd_step_tokens_kernel(xbc_ref, dt_ref, conv0_ref, cw_ref, cb_ref, dtb_ref, alog_ref, dsk_ref,
                            ex3_ref,
                            ypart_ref, eaxo_ref, xdtw_ref, bbo_ref, cco_ref, eacs_ref,
                            xpad_ref, xs_ref, bb_ref, cc_ref, lhs3_ref, acst_ref, *, n_heads, t):
    q = SSD_CHUNK
    p = SSM_HEAD_DIM
    d_inner = n_heads * p
    hpg = n_heads // N_GROUPS
    gch = hpg * p
    bc = N_GROUPS * D_STATE
    ntile = (d_inner + 2 * bc) // LANES
    nseq = q // t
    slot = 2 * t
    span = nseq * slot - t

    for j in range(ntile):
        cs = slice(j * LANES, (j + 1) * LANES)
        for i in range(nseq):
            xpad_ref[j, i * slot:i * slot + t, :] = conv0_ref[i, :, cs]
            xpad_ref[j, i * slot + t:(i + 1) * slot, :] = xbc_ref[i * t:(i + 1) * t, cs]
        conv = cb_ref[:, cs]
        for k in range(CONV_K):
            off = t - (CONV_K - 1) + k
            conv = conv + xpad_ref[j, off:off + span, :] * cw_ref[k:k + 1, cs]
        act = _silu(jnp.concatenate([conv[i * slot:i * slot + t] for i in range(nseq)], axis=0))
        _store_act_tile(j, act, xs_ref, bb_ref, cc_ref, d_inner, bc)
        lo = j * LANES
        if d_inner <= lo < d_inner + bc:
            bbo_ref[:, lo - d_inner:lo - d_inner + LANES] = act
        elif lo >= d_inner + bc:
            cco_ref[:, lo - d_inner - bc:lo - d_inner - bc + LANES] = act

    mask, eacs = _ssd_token_math(dt_ref[...], dtb_ref, alog_ref, lhs3_ref, acst_ref, seg_len=t)
    eacs_ref[...] = eacs
    def emit(g, v):
        gs = slice(g * gch, (g + 1) * gch)
        ypart_ref[:, gs] = v["y"] + dsk_ref[:, gs] * v["xs"]
        eaxo_ref[:, gs] = v["eax"]
        xdtw_ref[:, gs] = v["xdt"] * v["dtex"]

    _staggered(N_GROUPS, [
        lambda g, _: _ssd_group_inputs(g, xs_ref, bb_ref, cc_ref, lhs3_ref, ex3_ref, hpg),
        lambda g, v: _ssd_group_diag(g, v, mask, acst_ref, hpg),
        emit,
    ])


def ssd_step_tokens(xbcz, dt, conv0, cw, cb, dtb, alog, dsk, *, t):
    n_tok = xbcz.shape[0]
    conv_dim = cw.shape[1]
    d_inner = dsk.shape[1]
    n_heads = d_inner // SSM_HEAD_DIM
    q = SSD_CHUNK
    bc = N_GROUPS * D_STATE
    assert t == SUBLANES and n_tok % q == 0
    ex3 = _expansion_matrix(n_heads)
    const = lambda s: (0, 0)
    rows = lambda s: (s, 0)
    f32 = lambda n: jax.ShapeDtypeStruct((n_tok, n), F32)
    return pl.pallas_call(
        functools.partial(_ssd_step_tokens_kernel, n_heads=n_heads, t=t),
        grid=(n_tok // q,),
        in_specs=[
            pl.BlockSpec((q, conv_dim), rows),
            pl.BlockSpec((q, LANES), rows),
            pl.BlockSpec((q // t, SUBLANES, conv_dim), lambda s: (s, 0, 0)),
            pl.BlockSpec((CONV_K, conv_dim), const),
            pl.BlockSpec((1, conv_dim), const),
            pl.BlockSpec((1, LANES), const),
            pl.BlockSpec((1, LANES), const),
            pl.BlockSpec((1, d_inner), const),
            pl.BlockSpec(ex3.shape, const),
        ],
        out_specs=[
            pl.BlockSpec((q, d_inner), rows),
            pl.BlockSpec((q, d_inner), rows),
            pl.BlockSpec((q, d_inner), rows),
            pl.BlockSpec((q, bc), rows),
            pl.BlockSpec((q, bc), rows),
            pl.BlockSpec((q, LANES), rows),
        ],
        out_shape=[f32(d_inner), f32(d_inner), f32(d_inner), f32(bc), f32(bc), f32(LANES)],
        scratch_shapes=[
            pltpu.VMEM((conv_dim // LANES, 2 * q, LANES), F32),
            pltpu.VMEM((q, d_inner), F32),
            pltpu.VMEM((q, bc), BF16),
            pltpu.VMEM((q, bc), BF16),
            pltpu.VMEM((3 * q, 3 * LANES), BF16),
            pltpu.VMEM((LANES, q), F32),
        ],
        compiler_params=_params("parallel"),
        name="ssd_step_tokens",
    )(xbcz, dt, conv0, cw, cb, dtb, alog, dsk, ex3)


def _ssd_step_state_kernel(cd_ref, st_ref, cc_ref, bb_ref, xdtw_ref, eax_ref, ypart_ref, z_ref, gw_ref,
                           *rest, n_heads, t, nseq, n_fill):
    g_ref, sto_ref = rest[-2:]
    s = pl.program_id(1)
    p = SSM_HEAD_DIM
    hpg = n_heads // N_GROUPS
    gch = hpg * p

    def update():
        for g in range(N_GROUPS):
            gs = slice(g * gch, (g + 1) * gch)
            ns = slice(g * D_STATE, (g + 1) * D_STATE)
            ys = []
            for i in range(nseq):
                rows = slice(i * t, (i + 1) * t)
                h0 = st_ref[0, i, gs, :]
                yoff = lax.dot_general(cc_ref[rows, ns].astype(BF16), h0.astype(BF16), _NT,
                                       preferred_element_type=F32)
                ys.append(ypart_ref[rows, gs] + yoff * eax_ref[rows, gs])
                upd = lax.dot_general(xdtw_ref[rows, gs].astype(BF16), bb_ref[rows, ns].astype(BF16), _TN,
                                      preferred_element_type=F32)
                for r in range(hpg):
                    hs = slice(r * p, (r + 1) * p)
                    sto_ref[0, i, g * gch + r * p:g * gch + (r + 1) * p, :] = (
                        h0[hs] * cd_ref[s * nseq + i, g * hpg + r] + upd[hs])
            gg = jnp.concatenate(ys, axis=0) * _silu(z_ref[:, gs])
            msq = jnp.mean(gg * gg, axis=-1, keepdims=True)
            g_ref[:, gs] = (gg * lax.rsqrt(msq + EPS) * gw_ref[:, gs]).astype(BF16)

    if n_fill == 0:
        update()
    else:
        @pl.when(pl.program_id(0) < n_fill)
        def _():
            sto_ref[...] = jnp.zeros(sto_ref.shape, F32)

        pl.when(pl.program_id(0) == n_fill)(update)


def ssd_step_state(cd, state_all, layer, cc, bb, xdtw, eax, ypart, xbcz, gw, stacked_out, *, t, seqs_per_step=8):
    n_layers, n_seq, d_inner, d_state = state_all.shape
    n_tok = ypart.shape[0]
    bc = cc.shape[1]
    conv_dim = xbcz.shape[1] - d_inner
    nseq = _row_tile(n_seq, seqs_per_step)
    rt = nseq * t
    n_fill = n_layers - 1 if stacked_out is None else 0

    def active(l, s):
        return jnp.where(l == n_fill, s, 0)

    def out_layer(l):
        return jnp.where(l < n_fill, l + jnp.where(l >= layer, 1, 0), layer)

    rows = lambda l, s, cd_: (active(l, s), 0)
    st_map = lambda l, s, cd_: (layer, active(l, s), 0, 0)
    in_specs = [
        pl.BlockSpec((1, nseq, d_inner, d_state), st_map),
        pl.BlockSpec((rt, bc), rows),
        pl.BlockSpec((rt, bc), rows),
        pl.BlockSpec((rt, d_inner), rows),
        pl.BlockSpec((rt, d_inner), rows),
        pl.BlockSpec((rt, d_inner), rows),
        pl.BlockSpec((rt, d_inner), lambda l, s, cd_: (active(l, s), conv_dim // d_inner)),
        pl.BlockSpec((1, d_inner), lambda l, s, cd_: (0, 0)),
    ]
    args = [cd, state_all, cc, bb, xdtw, eax, ypart, xbcz, gw]
    aliases = {}
    if stacked_out is not None:
        in_specs.append(pl.BlockSpec(memory_space=pl.ANY))
        aliases = {len(args): 1}
        args.append(stacked_out)
    return pl.pallas_call(
        functools.partial(_ssd_step_state_kernel, n_heads=d_inner // SSM_HEAD_DIM, t=t, nseq=nseq,
                          n_fill=n_fill),
        grid_spec=pltpu.PrefetchScalarGridSpec(
            num_scalar_prefetch=1,
            grid=(n_fill + 1, n_seq // nseq),
            in_specs=in_specs,
            out_specs=[
                pl.BlockSpec((rt, d_inner), rows),
                pl.BlockSpec((1, nseq, d_inner, d_state), lambda l, s, cd_: (out_layer(l), s, 0, 0)),
            ],
        ),
        out_shape=[
            jax.ShapeDtypeStruct((n_tok, d_inner), BF16),
            jax.ShapeDtypeStruct((n_layers, n_seq, d_inner, d_state), F32),
        ],
        input_output_aliases=aliases,
        compiler_params=_params("arbitrary", "arbitrary"),
        name="ssd_step_state",
    )(*args)


def _attn_prompt_kernel(sink_ref, q_ref, kvc_ref, kvp_ref, kvm_ref, o_ref, *, n_q_heads):
    n = pl.program_id(1)
    d = ATTN_HEAD_DIM
    kvw = N_KV_HEADS * d
    rep = n_q_heads // N_KV_HEADS
    scale = d ** -0.5
    w = q_ref.shape[1]
    rows = rep * w
    qi = lax.broadcasted_iota(jnp.int32, (rows, w), 0) % w
    ci = lax.broadcasted_iota(jnp.int32, (rows, w), 1)
    from_prev = ci > qi
    no_prev = jnp.where(n > 0, 0.0, -jnp.inf)
    kvm, kvp, kvc = kvm_ref[...], kvp_ref[0], kvc_ref[0]
    outs = []
    for k in range(N_KV_HEADS):
        ks, vs = slice(k * d, (k + 1) * d), slice(kvw + k * d, kvw + (k + 1) * d)
        qk = q_ref[0, :, k * rep * d:(k + 1) * rep * d]
        q4 = jnp.concatenate([qk[:, r * d:(r + 1) * d] for r in range(rep)], axis=0)
        q4 = (q4 * scale).astype(BF16)
        k2 = jnp.concatenate([kvp[:, ks], kvc[:, ks]], axis=0).astype(BF16)
        v2 = jnp.concatenate([kvp[:, vs], kvc[:, vs]], axis=0).astype(BF16)
        s2 = lax.dot_general(q4, k2, _NT, preferred_element_type=F32)
        s_w = jnp.where(from_prev, s2[:, :w] + no_prev, s2[:, w:])
        s_m = lax.dot_general(q4, kvm[:, ks].astype(BF16), _NT, preferred_element_type=F32)
        p_ws, p_ms, dens = [], [], []
        for r in range(rep):
            sl = slice(r * w, (r + 1) * w)
            sink = sink_ref[k * rep + r]
            mx = jnp.maximum(jnp.maximum(jnp.max(s_w[sl], axis=-1, keepdims=True),
                                         jnp.max(s_m[sl], axis=-1, keepdims=True)), sink)
            p_w = jnp.exp(s_w[sl] - mx)
            p_m = jnp.exp(s_m[sl] - mx)
            dens.append(jnp.exp(sink - mx) + jnp.sum(p_w, axis=-1, keepdims=True)
                        + jnp.sum(p_m, axis=-1, keepdims=True))
            p_ws.append(p_w)
            p_ms.append(p_m.astype(BF16))
        p_w = jnp.concatenate(p_ws, axis=0)
        p2 = jnp.concatenate([jnp.where(from_prev, p_w, 0.0), jnp.where(from_prev, 0.0, p_w)],
                             axis=1).astype(BF16)
        o4 = (jnp.dot(p2, v2, preferred_element_type=F32)
              + jnp.dot(jnp.concatenate(p_ms, axis=0), kvm[:, vs].astype(BF16), preferred_element_type=F32))
        outs.extend(o4[r * w:(r + 1) * w, :] / dens[r] for r in range(rep))
    o_ref[0] = jnp.concatenate(outs, axis=1).astype(BF16)


def attn_prompt(q, kv, kvm, sinks):
    bsz, s, dq = q.shape
    kvd = kv.shape[2]
    nb = s // WINDOW
    return pl.pallas_call(
        functools.partial(_attn_prompt_kernel, n_q_heads=dq // ATTN_HEAD_DIM),
        grid=(bsz, nb),
        in_specs=[
            pl.BlockSpec(memory_space=pltpu.SMEM),
            pl.BlockSpec((1, WINDOW, dq), lambda b, n: (b, n, 0)),
            pl.BlockSpec((1, WINDOW, kvd), lambda b, n: (b, n, 0)),
            pl.BlockSpec((1, WINDOW, kvd), lambda b, n: (b, jnp.maximum(n - 1, 0), 0)),
            pl.BlockSpec(kvm.shape, lambda b, n: (0, 0)),
        ],
        out_specs=pl.BlockSpec((1, WINDOW, dq), lambda b, n: (b, n, 0)),
        out_shape=jax.ShapeDtypeStruct((bsz, s, dq), BF16),
        compiler_params=_params("parallel", "arbitrary"),
        name="attn_prompt",
    )(sinks, q, kv, kv, kvm)


def _attn_sample_kernel(sink_ref, q_ref, kvn_ref, ck_ref, cv_ref, kvm_ref, o_ref, *, n_q_heads, n_meta):
    d = ATTN_HEAD_DIM
    kvw = N_KV_HEADS * d
    rep = n_q_heads // N_KV_HEADS
    scale = d ** -0.5
    nseq, t, _ = q_ref.shape
    w_buf = ck_ref.shape[1]
    rows = rep * t
    nx = n_meta + t
    tq_buf = lax.broadcasted_iota(jnp.int32, (rows, w_buf), 0) % t
    pos_buf = PAST_LEN - w_buf + lax.broadcasted_iota(jnp.int32, (rows, w_buf), 1)
    buf_mask = jnp.logical_and(PAST_LEN + tq_buf - pos_buf < WINDOW, pos_buf >= n_meta)
    tq_x = lax.broadcasted_iota(jnp.int32, (rows, nx), 0) % t
    c_x = lax.broadcasted_iota(jnp.int32, (rows, nx), 1) - n_meta
    x_mask = jnp.logical_or(c_x < 0, jnp.logical_and(c_x <= tq_x, tq_x - c_x < WINDOW))
    head_in_group = lax.broadcasted_iota(jnp.int32, (rows, 1), 0) // t
    kvm = kvm_ref[...]
    sinks = []
    for k in range(N_KV_HEADS):
        sink = jnp.zeros((rows, 1), F32)
        for r in range(rep):
            sink = jnp.where(head_in_group == r, sink_ref[k * rep + r], sink)
        sinks.append(sink)
    chains = [(i, k) for i in range(nseq) for k in range(N_KV_HEADS)]

    scores, values = {}, {}
    for i in range(nseq):
        q, kvn, ck, cv = q_ref[i], kvn_ref[i], ck_ref[i], cv_ref[i]
        for k in range(N_KV_HEADS):
            ks, vs = slice(k * d, (k + 1) * d), slice(kvw + k * d, kvw + (k + 1) * d)
            q4 = jnp.concatenate([q[:, (k * rep + r) * d:(k * rep + r + 1) * d] for r in range(rep)], axis=0)
            q4 = (q4 * scale).astype(BF16)
            kx = jnp.concatenate([kvm[:, ks], kvn[:, ks]], axis=0).astype(BF16)
            vx = jnp.concatenate([kvm[:, vs], kvn[:, vs]], axis=0).astype(BF16)
            s_b = lax.dot_general(q4, ck[:, ks].astype(BF16), _NT, preferred_element_type=F32)
            s_x = lax.dot_general(q4, kx, _NT, preferred_element_type=F32)
            scores[i, k] = (jnp.where(buf_mask, s_b, -jnp.inf), jnp.where(x_mask, s_x, -jnp.inf))
            values[i, k] = (cv[:, ks].astype(BF16), vx)

    maxes = {}
    for c in chains:
        s_b, s_x = scores[c]
        maxes[c] = jnp.maximum(sinks[c[1]], jnp.maximum(jnp.max(s_b, axis=-1, keepdims=True),
                                                         jnp.max(s_x, axis=-1, keepdims=True)))
    probs, dens = {}, {}
    for c in chains:
        s_b, s_x = scores[c]
        p_b, p_x = jnp.exp(s_b - maxes[c]), jnp.exp(s_x - maxes[c])
        probs[c] = (p_b.astype(BF16), p_x.astype(BF16))
        dens[c] = (jnp.exp(sinks[c[1]] - maxes[c]) + jnp.sum(p_b, axis=-1, keepdims=True)
                   + jnp.sum(p_x, axis=-1, keepdims=True))
    outs = {}
    for c in chains:
        outs[c] = (jnp.dot(probs[c][0], values[c][0], preferred_element_type=F32)
                   + jnp.dot(probs[c][1], values[c][1], preferred_element_type=F32)) / dens[c]
    for i in range(nseq):
        heads = [outs[i, k][r * t:(r + 1) * t, :] for k in range(N_KV_HEADS) for r in range(rep)]
        o_ref[i] = jnp.concatenate(heads, axis=1).astype(BF16)


def attn_sample(q, kvn, ck, cv, kvm, sinks, *, seqs_per_step=8):
    bsz, t, dq = q.shape
    kvd = kvn.shape[2]
    w_buf = ck.shape[1]
    g = _row_tile(bsz, seqs_per_step)
    return pl.pallas_call(
        functools.partial(_attn_sample_kernel, n_q_heads=dq // ATTN_HEAD_DIM, n_meta=kvm.shape[0]),
        grid=(bsz // g,),
        in_specs=[
            pl.BlockSpec(memory_space=pltpu.SMEM),
            pl.BlockSpec((g, t, dq), lambda b: (b, 0, 0)),
            pl.BlockSpec((g, t, kvd), lambda b: (b, 0, 0)),
            pl.BlockSpec((g, w_buf, kvd // 2), lambda b: (b, 0, 0)),
            pl.BlockSpec((g, w_buf, kvd // 2), lambda b: (b, 0, 0)),
            pl.BlockSpec(kvm.shape, lambda b: (0, 0)),
        ],
        out_specs=pl.BlockSpec((g, t, dq), lambda b: (b, 0, 0)),
        out_shape=jax.ShapeDtypeStruct((bsz, t, dq), BF16),
        compiler_params=_params("parallel"),
        name="attn_sample",
    )(sinks, q, kvn, ck, cv, kvm)


def kernel(x_prompt, x_sample, state_conv, state_ssm, cache_k_win, cache_v_win, meta_tokens, a_norm_w, a_in_proj, a_conv_w, a_conv_b, a_dt_bias, a_log, a_d_skip, a_gate_norm_w, a_out_proj, kv_norm_w, w_kv, b_norm_w, w_q, attn_sinks, w_o, mlp_norm_w, w_up, w_down, final_norm_w):
    n_prompt, seq, d_model = x_prompt.shape
    n_dec, dec_seq, _ = x_sample.shape
    n_a = a_in_proj.shape[0]
    depth = w_up.shape[0]
    n_meta = meta_tokens.shape[0]
    d_inner = a_out_proj.shape[1]
    conv_dim = a_conv_w.shape[2]
    n_heads = a_log.shape[1]
    w_buf = cache_k_win.shape[1]
    kvw = N_KV_HEADS * ATTN_HEAD_DIM
    assert n_heads * SSM_HEAD_DIM == d_inner and n_heads <= LANES

    hm = meta_tokens.astype(F32)
    hp = x_prompt.reshape(n_prompt * seq, d_model)
    hs = x_sample.reshape(n_dec * dec_seq, d_model)

    def pad_lanes(v):
        return jnp.pad(v, (0, LANES - v.shape[0])).reshape(1, LANES)

    def pad_conv_state(s):
        return jnp.pad(s, ((0, 0), (SUBLANES - (CONV_K - 1), 0), (0, 0)))

    conv_p_list, ssm_p_list, conv_s_list, ssm_s_list = [], [], [], []
    step_path = dec_seq == SUBLANES and (n_dec * dec_seq) % SSD_CHUNK == 0
    ssm_in_all = state_ssm.reshape(n_a, n_dec, d_inner, D_STATE)
    ssm_s_all = None
    kvm = kv_p = kv_s = None
    wu, wd = w_up.astype(BF16), w_down.astype(BF16)
    for layer in range(depth):
        last = layer == depth - 1
        if layer < n_a:
            i = layer
            w_in = a_in_proj[i]
            w_main = w_in.astype(BF16)
            w_dt = jnp.pad(w_in[:, d_inner + conv_dim:], ((0, 0), (0, LANES - n_heads))).astype(BF16)
            w_out = a_out_proj[i].astype(BF16)
            prm = (a_conv_w[i], a_conv_b[i].reshape(1, conv_dim), pad_lanes(a_dt_bias[i]), pad_lanes(a_log[i]),
                   jnp.repeat(a_d_skip[i], SSM_HEAD_DIM).reshape(1, d_inner), a_gate_norm_w[i].reshape(1, d_inner))

            def mixer(h, bsz, length, conv0, ssm0, shared):
                if shared and length % SSD_CHUNK == 0:
                    xs_a, bb_a, cc_a, sz_a, dt, conv_o = in_proj_conv(
                        h, a_norm_w[i], w_main, w_dt, 0.5 * prm[0], 0.5 * prm[1], conv0,
                        seq_len=length, d_inner=d_inner)
                    split = lambda t: t.reshape(bsz, length, t.shape[1])
                    g, ssm_o = ssd_chunked(split(xs_a), split(bb_a), split(cc_a), split(sz_a), split(dt),
                                           ssm0, *prm[2:])
                else:
                    xbcz, dt = in_proj(h, a_norm_w[i], w_main, w_dt, d_inner=d_inner, n_main=conv_dim + d_inner)
                    g, conv_o, ssm_o = ssd_mixer(xbcz.reshape(bsz, length, conv_dim + d_inner),
                                                 dt.reshape(bsz, length, LANES), conv0, ssm0, *prm,
                                                 shared_state=shared)
                return g.reshape(bsz * length, d_inner), conv_o, ssm_o

            zero_conv = jnp.zeros((1, SUBLANES, conv_dim), F32)
            zero_ssm = jnp.zeros((1, d_inner, D_STATE), F32)
            g_m, conv_m, ssm_m = mixer(hm, 1, n_meta, zero_conv, zero_ssm, True)
            g_p, conv_p, ssm_p = mixer(hp, n_prompt, seq, conv_m, ssm_m, True)
            conv0_s = pad_conv_state(state_conv[i])
            if step_path:
                xbcz_s, dt_s = in_proj(hs, a_norm_w[i], w_main, w_dt, d_inner=d_inner, n_main=conv_dim + d_inner)
                ypart, eax, xdtw, bb_s, cc_s, eacs = ssd_step_tokens(xbcz_s, dt_s, conv0_s, *prm[:5], t=dec_seq)
                cdecay = eacs.reshape(n_dec, dec_seq, LANES)[:, dec_seq - 1]
                g_s, ssm_s_all = ssd_step_state(cdecay, ssm_in_all, i, cc_s, bb_s, xdtw, eax, ypart, xbcz_s,
                                                prm[5], ssm_s_all, t=dec_seq)
                conv_s = xbcz_s.reshape(n_dec, dec_seq, -1)[:, dec_seq - (CONV_K - 1):, :conv_dim]
            else:
                g_s, conv_s, ssm_s = mixer(hs, n_dec, dec_seq, conv0_s, ssm_in_all[i], False)
                conv_s = conv_s[:, SUBLANES - (CONV_K - 1):]
                ssm_s_list.append(ssm_s)
            conv_p_list.append(conv_p[:, SUBLANES - (CONV_K - 1):])
            ssm_p_list.append(ssm_p.reshape(n_prompt, n_heads, SSM_HEAD_DIM, D_STATE))
            conv_s_list.append(conv_s)
            hm = mlp(hm, mlp_norm_w[layer], wu, wd, layer, final_norm_w, final_norm=False, proj=(g_m, w_out))
            proj_p, proj_s = (g_p, w_out), (g_s, w_out)
        else:
            j = layer - n_a
            wq = w_q[j].astype(BF16)
            wo = w_o[j].astype(BF16)
            dq = wq.shape[1]
            if j == 0:
                wkv = w_kv.astype(BF16)
                kvm = norm_matmul(hm, kv_norm_w, wkv)
                q_p, kv_p = norm_matmul_pair(hp, b_norm_w[j], wq, kv_norm_w, wkv)
                q_s, kv_s = norm_matmul_pair(hs, b_norm_w[j], wq, kv_norm_w, wkv)
                kv_p = kv_p.reshape(n_prompt, seq, 2 * kvw)
                kv_s = kv_s.reshape(n_dec, dec_seq, 2 * kvw)
            else:
                q_p = norm_matmul(hp, b_norm_w[j], wq)
                q_s = norm_matmul(hs, b_norm_w[j], wq)
            q_p = q_p.reshape(n_prompt, seq, dq)
            q_s = q_s.reshape(n_dec, dec_seq, dq)
            o_p = attn_prompt(q_p, kv_p, kvm, attn_sinks[j])
            o_s = attn_sample(q_s, kv_s, cache_k_win.reshape(n_dec, w_buf, kvw),
                              cache_v_win.reshape(n_dec, w_buf, kvw), kvm, attn_sinks[j])
            proj_p = (o_p.reshape(n_prompt * seq, dq), wo)
            proj_s = (o_s.reshape(n_dec * dec_seq, dq), wo)
        hp = mlp(hp, mlp_norm_w[layer], wu, wd, layer, final_norm_w, final_norm=last, proj=proj_p)
        hs = mlp(hs, mlp_norm_w[layer], wu, wd, layer, final_norm_w, final_norm=last, proj=proj_s)

    y_prompt = hp.reshape(n_prompt, seq, d_model)
    y_sample = hs.reshape(n_dec, dec_seq, d_model)
    kv_heads = (N_KV_HEADS, ATTN_HEAD_DIM)
    k_p = kv_p[:, seq - w_buf:, :kvw].reshape((n_prompt, w_buf) + kv_heads)
    v_p = kv_p[:, seq - w_buf:, kvw:].reshape((n_prompt, w_buf) + kv_heads)
    k_s = kv_s[:, :, :kvw].reshape((n_dec, dec_seq) + kv_heads)
    v_s = kv_s[:, :, kvw:].reshape((n_dec, dec_seq) + kv_heads)
    k_s_win = jnp.concatenate([cache_k_win, k_s], axis=1)[:, -w_buf:]
    v_s_win = jnp.concatenate([cache_v_win, v_s], axis=1)[:, -w_buf:]
    if not step_path:
        ssm_s_all = jnp.stack(ssm_s_list)
    return (y_prompt, y_sample, jnp.stack(conv_p_list), jnp.stack(ssm_p_list), k_p, v_p,
            jnp.stack(conv_s_list), ssm_s_all.reshape(state_ssm.shape), k_s_win, v_s_win)
```

```python
import functools

import jax
import jax.numpy as jnp
from jax import lax
from jax.experimental import pallas as pl
from jax.experimental.pallas import tpu as pltpu

F32 = jnp.float32
BF16 = jnp.bfloat16

N_GROUPS = 8
SSM_HEAD_DIM = 64
D_STATE = 128
CONV_K = 4
SSD_CHUNK = 128
ATTN_HEAD_DIM = 64
N_KV_HEADS = 4
WINDOW = 128
PAST_LEN = 8192
EPS = 1e-5
LOG2_E = 1.4426950408889634

LANES = 128
SUBLANES = 8
VMEM_LIMIT_BYTES = 52 * 1024 * 1024

_NT = (((1,), (1,)), ((), ()))
_TN = (((0,), (0,)), ((), ()))


def _params(*sem):
    return pltpu.CompilerParams(dimension_semantics=sem, vmem_limit_bytes=VMEM_LIMIT_BYTES)


def _rms(x, w):
    ms = jnp.mean(x * x, axis=-1, keepdims=True)
    return x * lax.rsqrt(ms + EPS) * w


def _silu(x):
    s = 0.5 * x
    return s + s * jnp.tanh(s)


def _row_tile(m, cap):
    t = min(m, cap)
    assert m % t == 0, (m, t)
    return t


def _norm_matmul_kernel(x_ref, nw_ref, w_ref, o_ref, xn_ref):
    @pl.when(pl.program_id(1) == 0)
    def _():
        xn_ref[...] = _rms(x_ref[...], nw_ref[...]).astype(BF16)

    o_ref[...] = jnp.dot(xn_ref[...], w_ref[...], preferred_element_type=F32)


def norm_matmul(x, nw, w, *, tm_cap=1024, tn_cap=1024):
    m, d = x.shape
    n = w.shape[1]
    tm, tn = _row_tile(m, tm_cap), _row_tile(n, tn_cap)
    return pl.pallas_call(
        _norm_matmul_kernel,
        grid=(m // tm, n // tn),
        in_specs=[
            pl.BlockSpec((tm, d), lambda i, j: (i, 0)),
            pl.BlockSpec((1, d), lambda i, j: (0, 0)),
            pl.BlockSpec((d, tn), lambda i, j: (0, j)),
        ],
        out_specs=pl.BlockSpec((tm, tn), lambda i, j: (i, j)),
        out_shape=jax.ShapeDtypeStruct((m, n), F32),
        scratch_shapes=[pltpu.VMEM((tm, d), BF16)],
        compiler_params=_params("parallel", "arbitrary"),
        name="norm_matmul",
    )(x, nw.reshape(1, d), w)


def _norm_matmul_pair_kernel(x_ref, nwa_ref, wa_ref, nwb_ref, wb_ref, oa_ref, ob_ref):
    tm = x_ref.shape[0]
    rb = min(tm, 2 * LANES)

    def normalise(r, _):
        x = x_ref[r * rb:(r + 1) * rb, :]
        xh = x * lax.rsqrt(jnp.mean(x * x, axis=-1, keepdims=True) + EPS)
        return (xh * nwa_ref[...]).astype(BF16), (xh * nwb_ref[...]).astype(BF16)

    def project(r, xn):
        rows = slice(r * rb, (r + 1) * rb)
        oa_ref[rows, :] = jnp.dot(xn[0], wa_ref[...], preferred_element_type=F32)
        ob_ref[rows, :] = jnp.dot(xn[1], wb_ref[...], preferred_element_type=F32)

    _staggered(tm // rb, [normalise, project])


def norm_matmul_pair(x, nwa, wa, nwb, wb, *, tm_cap=1024):
    m, d = x.shape
    tm = _row_tile(m, tm_cap)
    row = lambda i: (i, 0)
    const = lambda i: (0, 0)
    return pl.pallas_call(
        _norm_matmul_pair_kernel,
        grid=(m // tm,),
        in_specs=[
            pl.BlockSpec((tm, d), row),
            pl.BlockSpec((1, d), const),
            pl.BlockSpec(wa.shape, const),
            pl.BlockSpec((1, d), const),
            pl.BlockSpec(wb.shape, const),
        ],
        out_specs=[pl.BlockSpec((tm, wa.shape[1]), row), pl.BlockSpec((tm, wb.shape[1]), row)],
        out_shape=[jax.ShapeDtypeStruct((m, wa.shape[1]), F32), jax.ShapeDtypeStruct((m, wb.shape[1]), F32)],
        compiler_params=_params("parallel"),
        name="norm_matmul_pair",
    )(x, nwa.reshape(1, d), wa, nwb.reshape(1, d), wb)


def _in_proj_kernel(x_ref, nw_ref, w_ref, wdt_ref, o_ref, dt_ref, xn_ref):
    @pl.when(pl.program_id(1) == 0)
    def _():
        xn = _rms(x_ref[...], nw_ref[...]).astype(BF16)
        xn_ref[...] = xn
        dt_ref[...] = jnp.dot(xn, wdt_ref[...], preferred_element_type=F32)

    o_ref[...] = jnp.dot(xn_ref[...], w_ref[...], preferred_element_type=F32)


def in_proj(x, nw, w, wdt, *, d_inner, n_main, tm_cap=1024, tn_cap=1024):
    m, d = x.shape
    n = n_main
    tm, tn = _row_tile(m, tm_cap), _row_tile(n, tn_cap)
    assert d_inner % tn == 0
    return pl.pallas_call(
        _in_proj_kernel,
        grid=(m // tm, n // tn),
        in_specs=[
            pl.BlockSpec((tm, d), lambda i, j: (i, 0)),
            pl.BlockSpec((1, d), lambda i, j: (0, 0)),
            pl.BlockSpec((d, tn), lambda i, j: (0, (j + d_inner // tn) % (n // tn))),
            pl.BlockSpec((d, LANES), lambda i, j: (0, 0)),
        ],
        out_specs=[
            pl.BlockSpec((tm, tn), lambda i, j: (i, j)),
            pl.BlockSpec((tm, LANES), lambda i, j: (i, 0)),
        ],
        out_shape=[jax.ShapeDtypeStruct((m, n), F32), jax.ShapeDtypeStruct((m, LANES), F32)],
        scratch_shapes=[pltpu.VMEM((tm, d), BF16)],
        compiler_params=_params("parallel", "arbitrary"),
        name="in_proj",
    )(x, nw.reshape(1, d), w, wdt)


def _in_proj_conv_kernel(x_ref, nw_ref, w_ref, wdt_ref, cw_ref, cb_ref, conv0_ref,
                         xs_ref, bb_ref, cc_ref, sz_ref, dt_ref, tail_ref,
                         xn_ref, xpad_ref, halo_ref, *, tiles_per_seq, d_inner, bc):
    i = pl.program_id(0)
    tm = x_ref.shape[0]
    conv_dim = d_inner + 2 * bc
    pad = SUBLANES
    sub = 4 * LANES
    per = sub // LANES
    n_conv, n_z = conv_dim // sub, d_inner // sub

    @pl.when(i % tiles_per_seq == 0)
    def _():
        for l in range(conv_dim // LANES):
            halo_ref[l] = conv0_ref[0, :, l * LANES:(l + 1) * LANES]

    xn = _rms(x_ref[...], nw_ref[...]).astype(BF16)
    xn_ref[...] = xn
    dt_ref[...] = jnp.dot(xn, wdt_ref[...], preferred_element_type=F32)

    def w_cols(s):
        start = d_inner + s * sub if s < n_conv else (s - n_conv) * sub
        return slice(start, start + sub)

    def matmul_cols(s, _):
        return jnp.dot(xn_ref[...], w_ref[:, w_cols(s)], preferred_element_type=F32)

    def activate(s, raw):
        if s >= n_conv:
            z0 = (s - n_conv) * sub
            sz_ref[:, z0:z0 + sub] = _silu(raw).astype(BF16)
            return
        for e in range(per):
            l = s * per + e
            lo = l * LANES
            slot = (s % 2) * per + e
            xpad_ref[slot, 0:pad, :] = halo_ref[l]
            xpad_ref[slot, pad:pad + tm, :] = raw[:, e * LANES:(e + 1) * LANES]
            half = cb_ref[:, lo:lo + LANES]
            for k in range(CONV_K):
                off = pad - (CONV_K - 1) + k
                half = half + xpad_ref[slot, off:off + tm, :] * cw_ref[k:k + 1, lo:lo + LANES]
            act = half + half * jnp.tanh(half)
            if lo < d_inner:
                xs_ref[:, lo:lo + LANES] = act
            elif lo < d_inner + bc:
                bb_ref[:, lo - d_inner:lo - d_inner + LANES] = act.astype(BF16)
            else:
                cc_ref[:, lo - d_inner - bc:lo - d_inner - bc + LANES] = act.astype(BF16)
            last_rows = xpad_ref[slot, tm:tm + pad, :]
            halo_ref[l] = last_rows
            tail_ref[0, :, lo:lo + LANES] = last_rows

    _staggered(n_conv + n_z, [matmul_cols, activate])


def in_proj_conv(x, nw, w, wdt, cw, cb, conv0, *, seq_len, d_inner, tm_cap=512):
    m, d = x.shape
    conv_dim = cw.shape[1]
    bc = (conv_dim - d_inner) // 2
    tm = _row_tile(seq_len, tm_cap)
    sub = 4 * LANES
    assert d_inner % sub == 0 and bc % sub == 0 and w.shape[1] >= conv_dim + d_inner and m % seq_len == 0
    bf = lambda n: jax.ShapeDtypeStruct((m, n), BF16)
    tiles_per_seq = seq_len // tm
    row = lambda i: (i, 0)
    const = lambda i: (0, 0)
    *acts, tails = pl.pallas_call(
        functools.partial(_in_proj_conv_kernel, tiles_per_seq=tiles_per_seq, d_inner=d_inner, bc=bc),
        grid=(m // tm,),
        in_specs=[
            pl.BlockSpec((tm, d), row),
            pl.BlockSpec((1, d), const),
            pl.BlockSpec(w.shape, const, pipeline_mode=pl.Buffered(1)),
            pl.BlockSpec((d, LANES), const),
            pl.BlockSpec((CONV_K, conv_dim), const),
            pl.BlockSpec((1, conv_dim), const),
            pl.BlockSpec((1, SUBLANES, conv_dim), lambda i: (0, 0, 0)),
        ],
        out_specs=[
            pl.BlockSpec((tm, d_inner), row),
            pl.BlockSpec((tm, bc), row),
            pl.BlockSpec((tm, bc), row),
            pl.BlockSpec((tm, d_inner), row),
            pl.BlockSpec((tm, LANES), row),
            pl.BlockSpec((1, SUBLANES, conv_dim), lambda i: (i, 0, 0)),
        ],
        out_shape=[jax.ShapeDtypeStruct((m, d_inner), F32), bf(bc), bf(bc), bf(d_inner),
                   jax.ShapeDtypeStruct((m, LANES), F32),
                   jax.ShapeDtypeStruct((m // tm, SUBLANES, conv_dim), F32)],
        scratch_shapes=[
            pltpu.VMEM((tm, d), BF16),
            pltpu.VMEM((2 * sub // LANES, tm + SUBLANES, LANES), F32),
            pltpu.VMEM((conv_dim // LANES, SUBLANES, LANES), F32),
        ],
        compiler_params=_params("arbitrary"),
        name="in_proj_conv",
    )(x, nw.reshape(1, d), w, wdt, cw, cb, conv0)
    return (*acts, tails[tiles_per_seq - 1::tiles_per_seq])


def _mlp_steps(x_rows, nw_ref, wu_ref, wd_ref, fw_ref, o_ref, xn_ref, final_norm):
    f = pl.program_id(1)
    tm = o_ref.shape[0]
    rb = min(tm, 2 * LANES)

    @pl.when(f == 0)
    def _():
        def normalise(r, x):
            rows = slice(r * rb, (r + 1) * rb)
            xn_ref[rows, :] = _rms(x, nw_ref[...]).astype(BF16)
            o_ref[rows, :] = x

        _staggered(tm // rb, [lambda r, _: x_rows(slice(r * rb, (r + 1) * rb)), normalise])

    h = jnp.dot(xn_ref[...], wu_ref[...], preferred_element_type=F32)
    h = jnp.square(jnp.maximum(h, 0.0)).astype(BF16)
    o_ref[...] += jnp.dot(h, wd_ref[...], preferred_element_type=F32)

    if final_norm:
        @pl.when(f == pl.num_programs(1) - 1)
        def _():
            o_ref[...] = _rms(o_ref[...], fw_ref[...])


def _mlp_kernel(x_ref, nw_ref, wu_ref, wd_ref, fw_ref, o_ref, xn_ref, *, final_norm):
    _mlp_steps(lambda rows: x_ref[rows, :], nw_ref, wu_ref, wd_ref, fw_ref, o_ref, xn_ref, final_norm)


def _proj_mlp_kernel(a_ref, wa_ref, x_ref, nw_ref, wu_ref, wd_ref, fw_ref, o_ref, xn_ref, *, final_norm):
    def block_input(rows):
        return x_ref[rows, :] + jnp.dot(a_ref[rows, :], wa_ref[...], preferred_element_type=F32)

    _mlp_steps(block_input, nw_ref, wu_ref, wd_ref, fw_ref, o_ref, xn_ref, final_norm)


def mlp(x, nw, wu, wd, layer, fw, *, final_norm, proj=None, tm_cap=1024, tf_cap=1024):
    m, d = x.shape
    dff = wu.shape[2]
    tm, tf = _row_tile(m, tm_cap), _row_tile(dff, tf_cap)
    in_specs = [
        pl.BlockSpec((tm, d), lambda i, f: (i, 0)),
        pl.BlockSpec((1, d), lambda i, f: (0, 0)),
        pl.BlockSpec((None, d, tf), lambda i, f: (layer, 0, f)),
        pl.BlockSpec((None, tf, d), lambda i, f: (layer, f, 0)),
        pl.BlockSpec((1, d), lambda i, f: (0, 0)),
    ]
    args = [x, nw.reshape(1, d), wu, wd, fw.reshape(1, d)]
    body = _mlp_kernel
    if proj is not None:
        a, wa = proj
        k = a.shape[1]
        in_specs = [pl.BlockSpec((tm, k), lambda i, f: (i, 0)),
                    pl.BlockSpec((k, d), lambda i, f: (0, 0), pipeline_mode=pl.Buffered(1))] + in_specs
        args = [a, wa] + args
        body = _proj_mlp_kernel
    return pl.pallas_call(
        functools.partial(body, final_norm=final_norm),
        grid=(m // tm, dff // tf),
        in_specs=in_specs,
        out_specs=pl.BlockSpec((tm, d), lambda i, f: (i, 0)),
        out_shape=jax.ShapeDtypeStruct((m, d), F32),
        scratch_shapes=[pltpu.VMEM((tm, d), BF16)],
        compiler_params=_params("parallel", "arbitrary"),
        name="mlp" if proj is None else "proj_mlp",
    )(*args)


def _softplus(x):
    return jnp.maximum(x, 0.0) + jnp.log(1.0 + jnp.exp(-jnp.abs(x)))


def _split3(a):
    hi = a.astype(BF16)
    r1 = a - hi.astype(F32)
    mid = r1.astype(BF16)
    lo = (r1 - mid.astype(F32)).astype(BF16)
    return hi, mid, lo


def _ssd_kernel(xbc_ref, z_ref, dt_ref, conv0_ref, ssm0_ref,
                cw_ref, cb_ref, dtb_ref, alog_ref, dsk_ref, gw_ref,
                g_ref, convo_ref, ssmo_ref,
                xpad_ref, act_ref, state_ref, y_ref, *, q, n_heads):
    c = pl.program_id(1)
    p = SSM_HEAD_DIM
    d_inner = n_heads * p
    hpg = n_heads // N_GROUPS
    conv_dim = d_inner + 2 * N_GROUPS * D_STATE
    pad = SUBLANES

    @pl.when(c == 0)
    def _():
        xpad_ref[0:pad, :] = conv0_ref[0]
        state_ref[...] = ssm0_ref[0]

    xpad_ref[pad:pad + q, :] = xbc_ref[0]
    cblk = 512
    for j in range(conv_dim // cblk):
        cs = slice(j * cblk, (j + 1) * cblk)
        conv = cb_ref[:, cs]
        for k in range(CONV_K):
            off = pad - (CONV_K - 1) + k
            conv = conv + xpad_ref[off:off + q, cs] * cw_ref[k:k + 1, cs]
        act_ref[:, cs] = _silu(conv)
    xpad_ref[0:pad, :] = xpad_ref[q:q + pad, :]

    dt = _softplus(dt_ref[0] + dtb_ref[...])
    a = dt * (-jnp.exp(alog_ref[...]))
    row = lax.broadcasted_iota(jnp.int32, (q, q), 0)
    col = lax.broadcasted_iota(jnp.int32, (q, q), 1)
    causal = row >= col
    tri = jnp.where(causal, 1.0, 0.0).astype(BF16)
    acs = None
    for part in _split3(a):
        t = jnp.dot(tri, part, preferred_element_type=F32)
        acs = t if acs is None else acs + t
    if q < LANES:
        acs_sq = jnp.concatenate([acs, jnp.zeros((LANES - q, LANES), F32)], axis=0)
    else:
        acs_sq = acs
    acs_t = acs_sq.T
    eacs = jnp.exp(acs)
    last = acs[q - 1:q, :]
    dte = jnp.exp(last - acs)
    cdecay = jnp.exp(last)

    for g in range(N_GROUPS):
        b0 = d_inner + g * D_STATE
        c0 = d_inner + N_GROUPS * D_STATE + g * D_STATE
        bb = act_ref[:, b0:b0 + D_STATE].astype(BF16)
        cc = act_ref[:, c0:c0 + D_STATE].astype(BF16)
        cbm = lax.dot_general(cc, bb, _NT, preferred_element_type=F32)
        for pair in range(hpg // 2):
            xs2 = act_ref[:, (g * hpg + 2 * pair) * p:(g * hpg + 2 * pair + 2) * p]
            dsk2 = dsk_ref[:, (g * hpg + 2 * pair) * p:(g * hpg + 2 * pair + 2) * p]
            ys = []
            for e in range(2):
                h = g * hpg + 2 * pair + e
                xs = xs2[:, e * p:(e + 1) * p]
                seg = acs[:, h:h + 1] - acs_t[h:h + 1, 0:q]
                decay = jnp.exp(jnp.where(causal, seg, -jnp.inf))
                m = (cbm * decay).astype(BF16)
                xdt = xs * dt[:, h:h + 1]
                y = jnp.dot(m, xdt.astype(BF16), preferred_element_type=F32)
                st = state_ref[h * p:(h + 1) * p, :]
                yoff = lax.dot_general(cc, st.astype(BF16), _NT, preferred_element_type=F32)
                y = y + yoff * eacs[:, h:h + 1] + dsk2[:, e * p:(e + 1) * p] * xs
                ys.append(y)
                xdtw = (xdt * dte[:, h:h + 1]).astype(BF16)
                snew = lax.dot_general(xdtw, bb, _TN, preferred_element_type=F32)
                state_ref[h * p:(h + 1) * p, :] = (
                    jnp.broadcast_to(cdecay[:, h:h + 1], (p, D_STATE)) * st + snew)
            y_ref[:, (g * hpg + 2 * pair) * p:(g * hpg + 2 * pair + 2) * p] = (
                jnp.concatenate(ys, axis=1))

    gsz = d_inner // N_GROUPS
    for g in range(N_GROUPS):
        gs = slice(g * gsz, (g + 1) * gsz)
        gg = y_ref[:, gs] * _silu(z_ref[0, :, gs])
        ms = jnp.mean(gg * gg, axis=-1, keepdims=True)
        g_ref[0, :, gs] = (gg * lax.rsqrt(ms + EPS) * gw_ref[:, gs]).astype(BF16)

    @pl.when(c == pl.num_programs(1) - 1)
    def _():
        convo_ref[0] = xpad_ref[0:pad, :]
        ssmo_ref[0] = state_ref[...]


def ssd_mixer(xbcz, dt, conv0, ssm0, cw, cb, dtb, alog, dsk, gw, *, shared_state):
    bsz, length, _ = xbcz.shape
    conv_dim = cw.shape[1]
    d_inner = gw.shape[1]
    n_heads = d_inner // SSM_HEAD_DIM
    q = SSD_CHUNK if length % SSD_CHUNK == 0 else length
    assert q % SUBLANES == 0 and q >= SUBLANES and conv_dim % d_inner == 0
    nc = length // q
    zblk = conv_dim // d_inner
    if shared_state:
        st_map = lambda b, c: (0, 0, 0)
    else:
        st_map = lambda b, c: (b, 0, 0)
    const = lambda b, c: (0, 0)
    return pl.pallas_call(
        functools.partial(_ssd_kernel, q=q, n_heads=n_heads),
        grid=(bsz, nc),
        in_specs=[
            pl.BlockSpec((1, q, conv_dim), lambda b, c: (b, c, 0)),
            pl.BlockSpec((1, q, d_inner), lambda b, c: (b, c, zblk)),
            pl.BlockSpec((1, q, LANES), lambda b, c: (b, c, 0)),
            pl.BlockSpec((1, SUBLANES, conv_dim), st_map),
            pl.BlockSpec((1, n_heads * SSM_HEAD_DIM, D_STATE), st_map),
            pl.BlockSpec((CONV_K, conv_dim), const),
            pl.BlockSpec((1, conv_dim), const),
            pl.BlockSpec((1, LANES), const),
            pl.BlockSpec((1, LANES), const),
            pl.BlockSpec((1, d_inner), const),
            pl.BlockSpec((1, d_inner), const),
        ],
        out_specs=[
            pl.BlockSpec((1, q, d_inner), lambda b, c: (b, c, 0)),
            pl.BlockSpec((1, SUBLANES, conv_dim), lambda b, c: (b, 0, 0)),
            pl.BlockSpec((1, n_heads * SSM_HEAD_DIM, D_STATE), lambda b, c: (b, 0, 0)),
        ],
        out_shape=[
            jax.ShapeDtypeStruct((bsz, length, d_inner), BF16),
            jax.ShapeDtypeStruct((bsz, SUBLANES, conv_dim), F32),
            jax.ShapeDtypeStruct((bsz, n_heads * SSM_HEAD_DIM, D_STATE), F32),
        ],
        scratch_shapes=[
            pltpu.VMEM((q + SUBLANES, conv_dim), F32),
            pltpu.VMEM((q, conv_dim), F32),
            pltpu.VMEM((n_heads * SSM_HEAD_DIM, D_STATE), F32),
            pltpu.VMEM((q, d_inner), F32),
        ],
        compiler_params=_params("parallel", "arbitrary"),
        name="ssd_mixer",
    )(xbcz, xbcz, dt, conv0, ssm0, cw, cb, dtb, alog, dsk, gw)


def _expansion_matrix(n_heads):
    h = jnp.arange(LANES)[:, None]
    ex = h == jnp.arange(n_heads * SSM_HEAD_DIM)[None, :] // SSM_HEAD_DIM
    return jnp.tile(ex.astype(BF16), (3, 1))


def _store_act_tile(j, act, xs_ref, bb_ref, cc_ref, d_inner, bc):
    lo = j * LANES
    if lo < d_inner:
        xs_ref[:, lo:lo + LANES] = act
    elif lo < d_inner + bc:
        bb_ref[:, lo - d_inner:lo - d_inner + LANES] = act.astype(BF16)
    else:
        cc_ref[:, lo - d_inner - bc:lo - d_inner - bc + LANES] = act.astype(BF16)


def _ssd_token_math(dt_raw, dtb_ref, alog_ref, lhs3_ref, acst_ref, *, seg_len):
    q = dt_raw.shape[0]
    dt = _softplus(dt_raw + dtb_ref[...])
    a = dt * (-jnp.exp(alog_ref[...]))
    row = lax.broadcasted_iota(jnp.int32, (q, q), 0)
    col = lax.broadcasted_iota(jnp.int32, (q, q), 1)
    mask = row >= col
    if seg_len != q:
        mask = jnp.logical_and(mask, row // seg_len == col // seg_len)
        seg_end = (row // seg_len) * seg_len + (seg_len - 1)
    tri = jnp.where(mask, 1.0, 0.0).astype(BF16)
    acs = None
    for part in _split3(a):
        t = jnp.dot(tri, part, preferred_element_type=F32)
        acs = t if acs is None else acs + t
    if seg_len == q:
        last = acs[q - 1:q, :]
    else:
        sel = jnp.where(col == seg_end, 1.0, 0.0).astype(BF16)
        last = None
        for part in _split3(acs):
            t = jnp.dot(sel, part, preferred_element_type=F32)
            last = t if last is None else last + t
    acst_ref[...] = (acs * LOG2_E).T
    eacs = jnp.exp(acs)
    stack = jnp.concatenate([dt, eacs, jnp.exp(last - acs)], axis=0)
    lhs3_ref[...] = jnp.concatenate(_split3(stack), axis=1)
    return mask, eacs


def _staggered(n, stages):
    carried = {}
    for t in range(n + len(stages) - 1):
        for k, stage in enumerate(stages):
            g = t - k
            if 0 <= g < n:
                carried[g] = stage(g, carried.get(g))


def _ssd_group_inputs(g, xs_ref, bb_ref, cc_ref, lhs3_ref, ex3_ref, hpg):
    q = xs_ref.shape[0]
    gch = hpg * SSM_HEAD_DIM
    gs = slice(g * gch, (g + 1) * gch)
    bb = bb_ref[:, g * D_STATE:(g + 1) * D_STATE]
    cc = cc_ref[:, g * D_STATE:(g + 1) * D_STATE]
    cbm = lax.dot_general(cc, bb, _NT, preferred_element_type=F32)
    ex = jnp.dot(lhs3_ref[...], ex3_ref[:, gs], preferred_element_type=F32)
    xs = xs_ref[:, gs]
    return dict(bb=bb, cc=cc, cbm=cbm, xs=xs, xdt=xs * ex[0:q], eax=ex[q:2 * q], dtex=ex[2 * q:3 * q])


def _ssd_group_diag(g, v, mask, acst_ref, hpg):
    q = v["xs"].shape[0]
    p = SSM_HEAD_DIM
    lane_head = lax.broadcasted_iota(jnp.int32, (q, hpg * p), 1) // p
    ms, rhs = [], []
    for r in range(hpg):
        h = g * hpg + r
        rowb = jnp.broadcast_to(acst_ref[h:h + 1, :], (q, q))
        decay = jnp.exp2(jnp.where(mask, rowb.T - rowb, -jnp.inf))
        ms.append((v["cbm"] * decay).astype(BF16))
        rhs.append(jnp.where(lane_head == r, v["xdt"], 0.0).astype(BF16))
    y = jnp.dot(jnp.concatenate(ms, axis=1), jnp.concatenate(rhs, axis=0),
                preferred_element_type=F32)
    return dict(v, y=y, cbm=None)


def _ssd_chunk_kernel(xs_ref, bb_ref, cc_ref, sz_ref, dt_ref, ssm0_ref,
                      dtb_ref, alog_ref, dsk_ref, gw_ref, ex3_ref,
                      g_ref, ssmo_ref,
                      st_ref, lhs3_ref, acst_ref, *, n_heads, cps):
    c = pl.program_id(1)
    q = SSD_CHUNK
    hpg = n_heads // N_GROUPS
    gch = hpg * SSM_HEAD_DIM

    @pl.when(c == 0)
    def _():
        st_ref[...] = ssm0_ref[0].T

    rows = [pl.ds(h * q, q) for h in range(cps)]
    masks = [_ssd_token_math(dt_ref[0, rows[h], :], dtb_ref, alog_ref, lhs3_ref.at[h], acst_ref.at[h],
                             seg_len=q)[0] for h in range(cps)]

    def finish(item, v):
        h, g = divmod(item, N_GROUPS)
        gs = slice(g * gch, (g + 1) * gch)
        st = st_ref[:, gs]
        y = (v["y"] + jnp.dot(v["cc"], st.astype(BF16), preferred_element_type=F32) * v["eax"]
             + dsk_ref[:, gs] * v["xs"])
        gg = y * sz_ref[0, rows[h], gs].astype(F32)
        msq = jnp.mean(gg * gg, axis=-1, keepdims=True)
        g_ref[0, rows[h], gs] = (gg * lax.rsqrt(msq + EPS) * gw_ref[:, gs]).astype(BF16)
        xdtw = (v["xdt"] * v["dtex"]).astype(BF16)
        snew = lax.dot_general(v["bb"], xdtw, _TN, preferred_element_type=F32)
        st_ref[:, gs] = st * v["eax"][q - 1:q, :] + snew

    def inputs(item, _):
        h, g = divmod(item, N_GROUPS)
        return _ssd_group_inputs(g, xs_ref.at[0, rows[h]], bb_ref.at[0, rows[h]], cc_ref.at[0, rows[h]],
                                 lhs3_ref.at[h], ex3_ref, hpg)

    def diag(item, v):
        h, g = divmod(item, N_GROUPS)
        return _ssd_group_diag(g, v, masks[h], acst_ref.at[h], hpg)

    _staggered(cps * N_GROUPS, [inputs, diag, finish])

    @pl.when(c == pl.num_programs(1) - 1)
    def _():
        ssmo_ref[0] = st_ref[...].T


def ssd_chunked(xs, bb, cc, sz, dt, ssm0, dtb, alog, dsk, gw, *, chunks_per_step=4):
    bsz, length, d_inner = xs.shape
    bc = bb.shape[2]
    n_heads = d_inner // SSM_HEAD_DIM
    q = SSD_CHUNK
    assert length % q == 0 and bc == N_GROUPS * D_STATE and n_heads % N_GROUPS == 0
    cps = _row_tile(length // q, chunks_per_step)
    qs = cps * q
    ex3 = _expansion_matrix(n_heads)
    rows = lambda b, c: (b, c, 0)
    const = lambda b, c: (0, 0)
    return pl.pallas_call(
        functools.partial(_ssd_chunk_kernel, n_heads=n_heads, cps=cps),
        grid=(bsz, length // qs),
        in_specs=[
            pl.BlockSpec((1, qs, d_inner), rows),
            pl.BlockSpec((1, qs, bc), rows),
            pl.BlockSpec((1, qs, bc), rows),
            pl.BlockSpec((1, qs, d_inner), rows),
            pl.BlockSpec((1, qs, LANES), rows),
            pl.BlockSpec((1, d_inner, D_STATE), lambda b, c: (0, 0, 0)),
            pl.BlockSpec((1, LANES), const),
            pl.BlockSpec((1, LANES), const),
            pl.BlockSpec((1, d_inner), const),
            pl.BlockSpec((1, d_inner), const),
            pl.BlockSpec(ex3.shape, const),
        ],
        out_specs=[
            pl.BlockSpec((1, qs, d_inner), rows),
            pl.BlockSpec((1, d_inner, D_STATE), lambda b, c: (b, 0, 0)),
        ],
        out_shape=[
            jax.ShapeDtypeStruct((bsz, length, d_inner), BF16),
            jax.ShapeDtypeStruct((bsz, d_inner, D_STATE), F32),
        ],
        scratch_shapes=[
            pltpu.VMEM((D_STATE, d_inner), F32),
            pltpu.VMEM((cps, 3 * q, 3 * LANES), BF16),
            pltpu.VMEM((cps, LANES, q), F32),
        ],
        compiler_params=_params("parallel", "arbitrary"),
        name="ssd_chunked",
    )(xs, bb, cc, sz, dt, ssm0, dtb, alog, dsk, gw, ex3)


def _ssd_step_tokens_kernel(xbc_ref, dt_ref, conv0_ref, cw_ref, cb_ref, dtb_ref, alog_ref, dsk_ref,
                            ex3_ref,
                            ypart_ref, eaxo_ref, xdtw_ref, bbo_ref, cco_ref, eacs_ref,
                            xpad_ref, xs_ref, bb_ref, cc_ref, lhs3_ref, acst_ref, *, n_heads, t):
    q = SSD_CHUNK
    p = SSM_HEAD_DIM
    d_inner = n_heads * p
    hpg = n_heads // N_GROUPS
    gch = hpg * p
    bc = N_GROUPS * D_STATE
    ntile = (d_inner + 2 * bc) // LANES
    nseq = q // t
    slot = 2 * t
    span = nseq * slot - t

    for j in range(ntile):
        cs = slice(j * LANES, (j + 1) * LANES)
        for i in range(nseq):
            xpad_ref[j, i * slot:i * slot + t, :] = conv0_ref[i, :, cs]
            xpad_ref[j, i * slot + t:(i + 1) * slot, :] = xbc_ref[i * t:(i + 1) * t, cs]
        conv = cb_ref[:, cs]
        for k in range(CONV_K):
            off = t - (CONV_K - 1) + k
            conv = conv + xpad_ref[j, off:off + span, :] * cw_ref[k:k + 1, cs]
        act = _silu(jnp.concatenate([conv[i * slot:i * slot + t] for i in range(nseq)], axis=0))
        _store_act_tile(j, act, xs_ref, bb_ref, cc_ref, d_inner, bc)
        lo = j * LANES
        if d_inner <= lo < d_inner + bc:
            bbo_ref[:, lo - d_inner:lo - d_inner + LANES] = act
        elif lo >= d_inner + bc:
            cco_ref[:, lo - d_inner - bc:lo - d_inner - bc + LANES] = act

    mask, eacs = _ssd_token_math(dt_ref[...], dtb_ref, alog_ref, lhs3_ref, acst_ref, seg_len=t)
    eacs_ref[...] = eacs
    def emit(g, v):
        gs = slice(g * gch, (g + 1) * gch)
        ypart_ref[:, gs] = v["y"] + dsk_ref[:, gs] * v["xs"]
        eaxo_ref[:, gs] = v["eax"]
        xdtw_ref[:, gs] = v["xdt"] * v["dtex"]

    _staggered(N_GROUPS, [
        lambda g, _: _ssd_group_inputs(g, xs_ref, bb_ref, cc_ref, lhs3_ref, ex3_ref, hpg),
        lambda g, v: _ssd_group_diag(g, v, mask, acst_ref, hpg),
        emit,
    ])


def ssd_step_tokens(xbcz, dt, conv0, cw, cb, dtb, alog, dsk, *, t):
    n_tok = xbcz.shape[0]
    conv_dim = cw.shape[1]
    d_inner = dsk.shape[1]
    n_heads = d_inner // SSM_HEAD_DIM
    q = SSD_CHUNK
    bc = N_GROUPS * D_STATE
    assert t == SUBLANES and n_tok % q == 0
    ex3 = _expansion_matrix(n_heads)
    const = lambda s: (0, 0)
    rows = lambda s: (s, 0)
    f32 = lambda n: jax.ShapeDtypeStruct((n_tok, n), F32)
    return pl.pallas_call(
        functools.partial(_ssd_step_tokens_kernel, n_heads=n_heads, t=t),
        grid=(n_tok // q,),
        in_specs=[
            pl.BlockSpec((q, conv_dim), rows),
            pl.BlockSpec((q, LANES), rows),
            pl.BlockSpec((q // t, SUBLANES, conv_dim), lambda s: (s, 0, 0)),
            pl.BlockSpec((CONV_K, conv_dim), const),
            pl.BlockSpec((1, conv_dim), const),
            pl.BlockSpec((1, LANES), const),
            pl.BlockSpec((1, LANES), const),
            pl.BlockSpec((1, d_inner), const),
            pl.BlockSpec(ex3.shape, const),
        ],
        out_specs=[
            pl.BlockSpec((q, d_inner), rows),
            pl.BlockSpec((q, d_inner), rows),
            pl.BlockSpec((q, d_inner), rows),
            pl.BlockSpec((q, bc), rows),
            pl.BlockSpec((q, bc), rows),
            pl.BlockSpec((q, LANES), rows),
        ],
        out_shape=[f32(d_inner), f32(d_inner), f32(d_inner), f32(bc), f32(bc), f32(LANES)],
        scratch_shapes=[
            pltpu.VMEM((conv_dim // LANES, 2 * q, LANES), F32),
            pltpu.VMEM((q, d_inner), F32),
            pltpu.VMEM((q, bc), BF16),
            pltpu.VMEM((q, bc), BF16),
            pltpu.VMEM((3 * q, 3 * LANES), BF16),
            pltpu.VMEM((LANES, q), F32),
        ],
        compiler_params=_params("parallel"),
        name="ssd_step_tokens",
    )(xbcz, dt, conv0, cw, cb, dtb, alog, dsk, ex3)


def _ssd_step_state_kernel(cd_ref, st_ref, cc_ref, bb_ref, xdtw_ref, eax_ref, ypart_ref, z_ref, gw_ref,
                           *rest, n_heads, t, nseq, n_fill):
    g_ref, sto_ref = rest[-2:]
    s = pl.program_id(1)
    p = SSM_HEAD_DIM
    hpg = n_heads // N_GROUPS
    gch = hpg * p

    def update():
        for g in range(N_GROUPS):
            gs = slice(g * gch, (g + 1) * gch)
            ns = slice(g * D_STATE, (g + 1) * D_STATE)
            ys = []
            for i in range(nseq):
                rows = slice(i * t, (i + 1) * t)
                h0 = st_ref[0, i, gs, :]
                yoff = lax.dot_general(cc_ref[rows, ns].astype(BF16), h0.astype(BF16), _NT,
                                       preferred_element_type=F32)
                ys.append(ypart_ref[rows, gs] + yoff * eax_ref[rows, gs])
                upd = lax.dot_general(xdtw_ref[rows, gs].astype(BF16), bb_ref[rows, ns].astype(BF16), _TN,
                                      preferred_element_type=F32)
                for r in range(hpg):
                    hs = slice(r * p, (r + 1) * p)
                    sto_ref[0, i, g * gch + r * p:g * gch + (r + 1) * p, :] = (
                        h0[hs] * cd_ref[s * nseq + i, g * hpg + r] + upd[hs])
            gg = jnp.concatenate(ys, axis=0) * _silu(z_ref[:, gs])
            msq = jnp.mean(gg * gg, axis=-1, keepdims=True)
            g_ref[:, gs] = (gg * lax.rsqrt(msq + EPS) * gw_ref[:, gs]).astype(BF16)

    if n_fill == 0:
        update()
    else:
        @pl.when(pl.program_id(0) < n_fill)
        def _():
            sto_ref[...] = jnp.zeros(sto_ref.shape, F32)

        pl.when(pl.program_id(0) == n_fill)(update)


def ssd_step_state(cd, state_all, layer, cc, bb, xdtw, eax, ypart, xbcz, gw, stacked_out, *, t, seqs_per_step=8):
    n_layers, n_seq, d_inner, d_state = state_all.shape
    n_tok = ypart.shape[0]
    bc = cc.shape[1]
    conv_dim = xbcz.shape[1] - d_inner
    nseq = _row_tile(n_seq, seqs_per_step)
    rt = nseq * t
    n_fill = n_layers - 1 if stacked_out is None else 0

    def active(l, s):
        return jnp.where(l == n_fill, s, 0)

    def out_layer(l):
        return jnp.where(l < n_fill, l + jnp.where(l >= layer, 1, 0), layer)

    rows = lambda l, s, cd_: (active(l, s), 0)
    st_map = lambda l, s, cd_: (layer, active(l, s), 0, 0)
    in_specs = [
        pl.BlockSpec((1, nseq, d_inner, d_state), st_map),
        pl.BlockSpec((rt, bc), rows),
        pl.BlockSpec((rt, bc), rows),
        pl.BlockSpec((rt, d_inner), rows),
        pl.BlockSpec((rt, d_inner), rows),
        pl.BlockSpec((rt, d_inner), rows),
        pl.BlockSpec((rt, d_inner), lambda l, s, cd_: (active(l, s), conv_dim // d_inner)),
        pl.BlockSpec((1, d_inner), lambda l, s, cd_: (0, 0)),
    ]
    args = [cd, state_all, cc, bb, xdtw, eax, ypart, xbcz, gw]
    aliases = {}
    if stacked_out is not None:
        in_specs.append(pl.BlockSpec(memory_space=pl.ANY))
        aliases = {len(args): 1}
        args.append(stacked_out)
    return pl.pallas_call(
        functools.partial(_ssd_step_state_kernel, n_heads=d_inner // SSM_HEAD_DIM, t=t, nseq=nseq,
                          n_fill=n_fill),
        grid_spec=pltpu.PrefetchScalarGridSpec(
            num_scalar_prefetch=1,
            grid=(n_fill + 1, n_seq // nseq),
            in_specs=in_specs,
            out_specs=[
                pl.BlockSpec((rt, d_inner), rows),
                pl.BlockSpec((1, nseq, d_inner, d_state), lambda l, s, cd_: (out_layer(l), s, 0, 0)),
            ],
        ),
        out_shape=[
            jax.ShapeDtypeStruct((n_tok, d_inner), BF16),
            jax.ShapeDtypeStruct((n_layers, n_seq, d_inner, d_state), F32),
        ],
        input_output_aliases=aliases,
        compiler_params=_params("arbitrary", "arbitrary"),
        name="ssd_step_state",
    )(*args)


def _attn_prompt_kernel(sink_ref, q_ref, kvc_ref, kvp_ref, kvm_ref, o_ref, *, n_q_heads):
    n = pl.program_id(1)
    d = ATTN_HEAD_DIM
    kvw = N_KV_HEADS * d
    rep = n_q_heads // N_KV_HEADS
    scale = d ** -0.5
    w = q_ref.shape[1]
    rows = rep * w
    qi = lax.broadcasted_iota(jnp.int32, (rows, w), 0) % w
    ci = lax.broadcasted_iota(jnp.int32, (rows, w), 1)
    from_prev = ci > qi
    no_prev = jnp.where(n > 0, 0.0, -jnp.inf)
    kvm, kvp, kvc = kvm_ref[...], kvp_ref[0], kvc_ref[0]
    outs = []
    for k in range(N_KV_HEADS):
        ks, vs = slice(k * d, (k + 1) * d), slice(kvw + k * d, kvw + (k + 1) * d)
        qk = q_ref[0, :, k * rep * d:(k + 1) * rep * d]
        q4 = jnp.concatenate([qk[:, r * d:(r + 1) * d] for r in range(rep)], axis=0)
        q4 = (q4 * scale).astype(BF16)
        k2 = jnp.concatenate([kvp[:, ks], kvc[:, ks]], axis=0).astype(BF16)
        v2 = jnp.concatenate([kvp[:, vs], kvc[:, vs]], axis=0).astype(BF16)
        s2 = lax.dot_general(q4, k2, _NT, preferred_element_type=F32)
        s_w = jnp.where(from_prev, s2[:, :w] + no_prev, s2[:, w:])
        s_m = lax.dot_general(q4, kvm[:, ks].astype(BF16), _NT, preferred_element_type=F32)
        p_ws, p_ms, dens = [], [], []
        for r in range(rep):
            sl = slice(r * w, (r + 1) * w)
            sink = sink_ref[k * rep + r]
            mx = jnp.maximum(jnp.maximum(jnp.max(s_w[sl], axis=-1, keepdims=True),
                                         jnp.max(s_m[sl], axis=-1, keepdims=True)), sink)
            p_w = jnp.exp(s_w[sl] - mx)
            p_m = jnp.exp(s_m[sl] - mx)
            dens.append(jnp.exp(sink - mx) + jnp.sum(p_w, axis=-1, keepdims=True)
                        + jnp.sum(p_m, axis=-1, keepdims=True))
            p_ws.append(p_w)
            p_ms.append(p_m.astype(BF16))
        p_w = jnp.concatenate(p_ws, axis=0)
        p2 = jnp.concatenate([jnp.where(from_prev, p_w, 0.0), jnp.where(from_prev, 0.0, p_w)],
                             axis=1).astype(BF16)
        o4 = (jnp.dot(p2, v2, preferred_element_type=F32)
              + jnp.dot(jnp.concatenate(p_ms, axis=0), kvm[:, vs].astype(BF16), preferred_element_type=F32))
        outs.extend(o4[r * w:(r + 1) * w, :] / dens[r] for r in range(rep))
    o_ref[0] = jnp.concatenate(outs, axis=1).astype(BF16)


def attn_prompt(q, kv, kvm, sinks):
    bsz, s, dq = q.shape
    kvd = kv.shape[2]
    nb = s // WINDOW
    return pl.pallas_call(
        functools.partial(_attn_prompt_kernel, n_q_heads=dq // ATTN_HEAD_DIM),
        grid=(bsz, nb),
        in_specs=[
            pl.BlockSpec(memory_space=pltpu.SMEM),
            pl.BlockSpec((1, WINDOW, dq), lambda b, n: (b, n, 0)),
            pl.BlockSpec((1, WINDOW, kvd), lambda b, n: (b, n, 0)),
            pl.BlockSpec((1, WINDOW, kvd), lambda b, n: (b, jnp.maximum(n - 1, 0), 0)),
            pl.BlockSpec(kvm.shape, lambda b, n: (0, 0)),
        ],
        out_specs=pl.BlockSpec((1, WINDOW, dq), lambda b, n: (b, n, 0)),
        out_shape=jax.ShapeDtypeStruct((bsz, s, dq), BF16),
        compiler_params=_params("parallel", "arbitrary"),
        name="attn_prompt",
    )(sinks, q, kv, kv, kvm)


def _attn_sample_kernel(sink_ref, q_ref, kvn_ref, ck_ref, cv_ref, kvm_ref, o_ref, *, n_q_heads, n_meta):
    d = ATTN_HEAD_DIM
    kvw = N_KV_HEADS * d
    rep = n_q_heads // N_KV_HEADS
    scale = d ** -0.5
    nseq, t, _ = q_ref.shape
    w_buf = ck_ref.shape[1]
    rows = rep * t
    nx = n_meta + t
    tq_buf = lax.broadcasted_iota(jnp.int32, (rows, w_buf), 0) % t
    pos_buf = PAST_LEN - w_buf + lax.broadcasted_iota(jnp.int32, (rows, w_buf), 1)
    buf_mask = jnp.logical_and(PAST_LEN + tq_buf - pos_buf < WINDOW, pos_buf >= n_meta)
    tq_x = lax.broadcasted_iota(jnp.int32, (rows, nx), 0) % t
    c_x = lax.broadcasted_iota(jnp.int32, (rows, nx), 1) - n_meta
    x_mask = jnp.logical_or(c_x < 0, jnp.logical_and(c_x <= tq_x, tq_x - c_x < WINDOW))
    head_in_group = lax.broadcasted_iota(jnp.int32, (rows, 1), 0) // t
    kvm = kvm_ref[...]
    sinks = []
    for k in range(N_KV_HEADS):
        sink = jnp.zeros((rows, 1), F32)
        for r in range(rep):
            sink = jnp.where(head_in_group == r, sink_ref[k * rep + r], sink)
        sinks.append(sink)
    chains = [(i, k) for i in range(nseq) for k in range(N_KV_HEADS)]

    scores, values = {}, {}
    for i in range(nseq):
        q, kvn, ck, cv = q_ref[i], kvn_ref[i], ck_ref[i], cv_ref[i]
        for k in range(N_KV_HEADS):
            ks, vs = slice(k * d, (k + 1) * d), slice(kvw + k * d, kvw + (k + 1) * d)
            q4 = jnp.concatenate([q[:, (k * rep + r) * d:(k * rep + r + 1) * d] for r in range(rep)], axis=0)
            q4 = (q4 * scale).astype(BF16)
            kx = jnp.concatenate([kvm[:, ks], kvn[:, ks]], axis=0).astype(BF16)
            vx = jnp.concatenate([kvm[:, vs], kvn[:, vs]], axis=0).astype(BF16)
            s_b = lax.dot_general(q4, ck[:, ks].astype(BF16), _NT, preferred_element_type=F32)
            s_x = lax.dot_general(q4, kx, _NT, preferred_element_type=F32)
            scores[i, k] = (jnp.where(buf_mask, s_b, -jnp.inf), jnp.where(x_mask, s_x, -jnp.inf))
            values[i, k] = (cv[:, ks].astype(BF16), vx)

    maxes = {}
    for c in chains:
        s_b, s_x = scores[c]
        maxes[c] = jnp.maximum(sinks[c[1]], jnp.maximum(jnp.max(s_b, axis=-1, keepdims=True),
                                                         jnp.max(s_x, axis=-1, keepdims=True)))
    probs, dens = {}, {}
    for c in chains:
        s_b, s_x = scores[c]
        p_b, p_x = jnp.exp(s_b - maxes[c]), jnp.exp(s_x - maxes[c])
        probs[c] = (p_b.astype(BF16), p_x.astype(BF16))
        dens[c] = (jnp.exp(sinks[c[1]] - maxes[c]) + jnp.sum(p_b, axis=-1, keepdims=True)
                   + jnp.sum(p_x, axis=-1, keepdims=True))
    outs = {}
    for c in chains:
        outs[c] = (jnp.dot(probs[c][0], values[c][0], preferred_element_type=F32)
                   + jnp.dot(probs[c][1], values[c][1], preferred_element_type=F32)) / dens[c]
    for i in range(nseq):
        heads = [outs[i, k][r * t:(r + 1) * t, :] for k in range(N_KV_HEADS) for r in range(rep)]
        o_ref[i] = jnp.concatenate(heads, axis=1).astype(BF16)


def attn_sample(q, kvn, ck, cv, kvm, sinks, *, seqs_per_step=8):
    bsz, t, dq = q.shape
    kvd = kvn.shape[2]
    w_buf = ck.shape[1]
    g = _row_tile(bsz, seqs_per_step)
    return pl.pallas_call(
        functools.partial(_attn_sample_kernel, n_q_heads=dq // ATTN_HEAD_DIM, n_meta=kvm.shape[0]),
        grid=(bsz // g,),
        in_specs=[
            pl.BlockSpec(memory_space=pltpu.SMEM),
            pl.BlockSpec((g, t, dq), lambda b: (b, 0, 0)),
            pl.BlockSpec((g, t, kvd), lambda b: (b, 0, 0)),
            pl.BlockSpec((g, w_buf, kvd // 2), lambda b: (b, 0, 0)),
            pl.BlockSpec((g, w_buf, kvd // 2), lambda b: (b, 0, 0)),
            pl.BlockSpec(kvm.shape, lambda b: (0, 0)),
        ],
        out_specs=pl.BlockSpec((g, t, dq), lambda b: (b, 0, 0)),
        out_shape=jax.ShapeDtypeStruct((bsz, t, dq), BF16),
        compiler_params=_params("parallel"),
        name="attn_sample",
    )(sinks, q, kvn, ck, cv, kvm)


def kernel(x_prompt, x_sample, state_conv, state_ssm, cache_k_win, cache_v_win, meta_tokens, a_norm_w, a_in_proj, a_conv_w, a_conv_b, a_dt_bias, a_log, a_d_skip, a_gate_norm_w, a_out_proj, kv_norm_w, w_kv, b_norm_w, w_q, attn_sinks, w_o, mlp_norm_w, w_up, w_down, final_norm_w):
    n_prompt, seq, d_model = x_prompt.shape
    n_dec, dec_seq, _ = x_sample.shape
    n_a = a_in_proj.shape[0]
    depth = w_up.shape[0]
    n_meta = meta_tokens.shape[0]
    d_inner = a_out_proj.shape[1]
    conv_dim = a_conv_w.shape[2]
    n_heads = a_log.shape[1]
    w_buf = cache_k_win.shape[1]
    kvw = N_KV_HEADS * ATTN_HEAD_DIM
    assert n_heads * SSM_HEAD_DIM == d_inner and n_heads <= LANES

    hm = meta_tokens.astype(F32)
    hp = x_prompt.reshape(n_prompt * seq, d_model)
    hs = x_sample.reshape(n_dec * dec_seq, d_model)

    def pad_lanes(v):
        return jnp.pad(v, (0, LANES - v.shape[0])).reshape(1, LANES)

    def pad_conv_state(s):
        return jnp.pad(s, ((0, 0), (SUBLANES - (CONV_K - 1), 0), (0, 0)))

    conv_p_list, ssm_p_list, conv_s_list, ssm_s_list = [], [], [], []
    step_path = dec_seq == SUBLANES and (n_dec * dec_seq) % SSD_CHUNK == 0
    ssm_in_all = state_ssm.reshape(n_a, n_dec, d_inner, D_STATE)
    ssm_s_all = None
    kvm = kv_p = kv_s = None
    wu, wd = w_up.astype(BF16), w_down.astype(BF16)
    for layer in range(depth):
        last = layer == depth - 1
        if layer < n_a:
            i = layer
            w_in = a_in_proj[i]
            w_main = w_in.astype(BF16)
            w_dt = jnp.pad(w_in[:, d_inner + conv_dim:], ((0, 0), (0, LANES - n_heads))).astype(BF16)
            w_out = a_out_proj[i].astype(BF16)
            prm = (a_conv_w[i], a_conv_b[i].reshape(1, conv_dim), pad_lanes(a_dt_bias[i]), pad_lanes(a_log[i]),
                   jnp.repeat(a_d_skip[i], SSM_HEAD_DIM).reshape(1, d_inner), a_gate_norm_w[i].reshape(1, d_inner))

            def mixer(h, bsz, length, conv0, ssm0, shared):
                if shared and length % SSD_CHUNK == 0:
                    xs_a, bb_a, cc_a, sz_a, dt, conv_o = in_proj_conv(
                        h, a_norm_w[i], w_main, w_dt, 0.5 * prm[0], 0.5 * prm[1], conv0,
                        seq_len=length, d_inner=d_inner)
                    split = lambda t: t.reshape(bsz, length, t.shape[1])
                    g, ssm_o = ssd_chunked(split(xs_a), split(bb_a), split(cc_a), split(sz_a), split(dt),
                                           ssm0, *prm[2:])
                else:
                    xbcz, dt = in_proj(h, a_norm_w[i], w_main, w_dt, d_inner=d_inner, n_main=conv_dim + d_inner)
                    g, conv_o, ssm_o = ssd_mixer(xbcz.reshape(bsz, length, conv_dim + d_inner),
                                                 dt.reshape(bsz, length, LANES), conv0, ssm0, *prm,
                                                 shared_state=shared)
                return g.reshape(bsz * length, d_inner), conv_o, ssm_o

            zero_conv = jnp.zeros((1, SUBLANES, conv_dim), F32)
            zero_ssm = jnp.zeros((1, d_inner, D_STATE), F32)
            g_m, conv_m, ssm_m = mixer(hm, 1, n_meta, zero_conv, zero_ssm, True)
            g_p, conv_p, ssm_p = mixer(hp, n_prompt, seq, conv_m, ssm_m, True)
            conv0_s = pad_conv_state(state_conv[i])
            if step_path:
                xbcz_s, dt_s = in_proj(hs, a_norm_w[i], w_main, w_dt, d_inner=d_inner, n_main=conv_dim + d_inner)
                ypart, eax, xdtw, bb_s, cc_s, eacs = ssd_step_tokens(xbcz_s, dt_s, conv0_s, *prm[:5], t=dec_seq)
                cdecay = eacs.reshape(n_dec, dec_seq, LANES)[:, dec_seq - 1]
                g_s, ssm_s_all = ssd_step_state(cdecay, ssm_in_all, i, cc_s, bb_s, xdtw, eax, ypart, xbcz_s,
                                                prm[5], ssm_s_all, t=dec_seq)
                conv_s = xbcz_s.reshape(n_dec, dec_seq, -1)[:, dec_seq - (CONV_K - 1):, :conv_dim]
            else:
                g_s, conv_s, ssm_s = mixer(hs, n_dec, dec_seq, conv0_s, ssm_in_all[i], False)
                conv_s = conv_s[:, SUBLANES - (CONV_K - 1):]
                ssm_s_list.append(ssm_s)
            conv_p_list.append(conv_p[:, SUBLANES - (CONV_K - 1):])
            ssm_p_list.append(ssm_p.reshape(n_prompt, n_heads, SSM_HEAD_DIM, D_STATE))
            conv_s_list.append(conv_s)
            hm = mlp(hm, mlp_norm_w[layer], wu, wd, layer, final_norm_w, final_norm=False, proj=(g_m, w_out))
            proj_p, proj_s = (g_p, w_out), (g_s, w_out)
        else:
            j = layer - n_a
            wq = w_q[j].astype(BF16)
            wo = w_o[j].astype(BF16)
            dq = wq.shape[1]
            if j == 0:
                wkv = w_kv.astype(BF16)
                kvm = norm_matmul(hm, kv_norm_w, wkv)
                q_p, kv_p = norm_matmul_pair(hp, b_norm_w[j], wq, kv_norm_w, wkv)
                q_s, kv_s = norm_matmul_pair(hs, b_norm_w[j], wq, kv_norm_w, wkv)
                kv_p = kv_p.reshape(n_prompt, seq, 2 * kvw)
                kv_s = kv_s.reshape(n_dec, dec_seq, 2 * kvw)
            else:
                q_p = norm_matmul(hp, b_norm_w[j], wq)
                q_s = norm_matmul(hs, b_norm_w[j], wq)
            q_p = q_p.reshape(n_prompt, seq, dq)
            q_s = q_s.reshape(n_dec, dec_seq, dq)
            o_p = attn_prompt(q_p, kv_p, kvm, attn_sinks[j])
            o_s = attn_sample(q_s, kv_s, cache_k_win.reshape(n_dec, w_buf, kvw),
                              cache_v_win.reshape(n_dec, w_buf, kvw), kvm, attn_sinks[j])
            proj_p = (o_p.reshape(n_prompt * seq, dq), wo)
            proj_s = (o_s.reshape(n_dec * dec_seq, dq), wo)
        hp = mlp(hp, mlp_norm_w[layer], wu, wd, layer, final_norm_w, final_norm=last, proj=proj_p)
        hs = mlp(hs, mlp_norm_w[layer], wu, wd, layer, final_norm_w, final_norm=last, proj=proj_s)

    y_prompt = hp.reshape(n_prompt, seq, d_model)
    y_sample = hs.reshape(n_dec, dec_seq, d_model)
    kv_heads = (N_KV_HEADS, ATTN_HEAD_DIM)
    k_p = kv_p[:, seq - w_buf:, :kvw].reshape((n_prompt, w_buf) + kv_heads)
    v_p = kv_p[:, seq - w_buf:, kvw:].reshape((n_prompt, w_buf) + kv_heads)
    k_s = kv_s[:, :, :kvw].reshape((n_dec, dec_seq) + kv_heads)
    v_s = kv_s[:, :, kvw:].reshape((n_dec, dec_seq) + kv_heads)
    k_s_win = jnp.concatenate([cache_k_win, k_s], axis=1)[:, -w_buf:]
    v_s_win = jnp.concatenate([cache_v_win, v_s], axis=1)[:, -w_buf:]
    if not step_path:
        ssm_s_all = jnp.stack(ssm_s_list)
    return (y_prompt, y_sample, jnp.stack(conv_p_list), jnp.stack(ssm_p_list), k_p, v_p,
            jnp.stack(conv_s_list), ssm_s_all.reshape(state_ssm.shape), k_s_win, v_s_win)
```

```python
import functools

import jax
import jax.numpy as jnp
from jax import lax
from jax.experimental import pallas as pl
from jax.experimental.pallas import tpu as pltpu

F32 = jnp.float32
BF16 = jnp.bfloat16

N_GROUPS = 8
SSM_HEAD_DIM = 64
D_STATE = 128
CONV_K = 4
SSD_CHUNK = 128
ATTN_HEAD_DIM = 64
N_KV_HEADS = 4
WINDOW = 128
PAST_LEN = 8192
EPS = 1e-5
LOG2_E = 1.4426950408889634

LANES = 128
SUBLANES = 8
VMEM_LIMIT_BYTES = 52 * 1024 * 1024

_NT = (((1,), (1,)), ((), ()))
_TN = (((0,), (0,)), ((), ()))


def _params(*sem):
    return pltpu.CompilerParams(dimension_semantics=sem, vmem_limit_bytes=VMEM_LIMIT_BYTES)


def _rms(x, w):
    ms = jnp.mean(x * x, axis=-1, keepdims=True)
    return x * lax.rsqrt(ms + EPS) * w


def _silu(x):
    s = 0.5 * x
    return s + s * jnp.tanh(s)


def _row_tile(m, cap):
    t = min(m, cap)
    assert m % t == 0, (m, t)
    return t


def _norm_matmul_kernel(x_ref, nw_ref, w_ref, o_ref, xn_ref):
    @pl.when(pl.program_id(1) == 0)
    def _():
        xn_ref[...] = _rms(x_ref[...], nw_ref[...]).astype(BF16)

    o_ref[...] = jnp.dot(xn_ref[...], w_ref[...], preferred_element_type=F32)


def norm_matmul(x, nw, w, *, tm_cap=1024, tn_cap=1024):
    m, d = x.shape
    n = w.shape[1]
    tm, tn = _row_tile(m, tm_cap), _row_tile(n, tn_cap)
    return pl.pallas_call(
        _norm_matmul_kernel,
        grid=(m // tm, n // tn),
        in_specs=[
            pl.BlockSpec((tm, d), lambda i, j: (i, 0)),
            pl.BlockSpec((1, d), lambda i, j: (0, 0)),
            pl.BlockSpec((d, tn), lambda i, j: (0, j)),
        ],
        out_specs=pl.BlockSpec((tm, tn), lambda i, j: (i, j)),
        out_shape=jax.ShapeDtypeStruct((m, n), F32),
        scratch_shapes=[pltpu.VMEM((tm, d), BF16)],
        compiler_params=_params("parallel", "arbitrary"),
        name="norm_matmul",
    )(x, nw.reshape(1, d), w)


def _norm_matmul_heads_kernel(x_ref, *refs):
    n = len(refs) // 3
    gains, weights, outs = refs[0:2 * n:2], refs[1:2 * n:2], refs[2 * n:]
    tm = x_ref.shape[0]
    rb = min(tm, 2 * LANES)

    def normalise(r, _):
        x = x_ref[r * rb:(r + 1) * rb, :]
        xh = x * lax.rsqrt(jnp.mean(x * x, axis=-1, keepdims=True) + EPS)
        return [(xh * g[...]).astype(BF16) for g in gains]

    def project(r, xn):
        rows = slice(r * rb, (r + 1) * rb)
        for xn_h, w, o in zip(xn, weights, outs):
            o[rows, :] = jnp.dot(xn_h, w[...], preferred_element_type=F32)

    _staggered(tm // rb, [normalise, project])


def norm_matmul_heads(x, heads, *, tm_cap=1024):
    m, d = x.shape
    tm = _row_tile(m, tm_cap)
    row = lambda i: (i, 0)
    const = lambda i: (0, 0)
    in_specs, args = [pl.BlockSpec((tm, d), row)], [x]
    for gain, w in heads:
        in_specs += [pl.BlockSpec((1, d), const), pl.BlockSpec(w.shape, const)]
        args += [gain.reshape(1, d), w]
    return pl.pallas_call(
        _norm_matmul_heads_kernel,
        grid=(m // tm,),
        in_specs=in_specs,
        out_specs=[pl.BlockSpec((tm, w.shape[1]), row) for _, w in heads],
        out_shape=[jax.ShapeDtypeStruct((m, w.shape[1]), F32) for _, w in heads],
        compiler_params=_params("parallel"),
        name="norm_matmul_heads",
    )(*args)


def _in_proj_kernel(x_ref, nw_ref, w_ref, wdt_ref, o_ref, dt_ref, xn_ref):
    @pl.when(pl.program_id(1) == 0)
    def _():
        xn = _rms(x_ref[...], nw_ref[...]).astype(BF16)
        xn_ref[...] = xn
        dt_ref[...] = jnp.dot(xn, wdt_ref[...], preferred_element_type=F32)

    o_ref[...] = jnp.dot(xn_ref[...], w_ref[...], preferred_element_type=F32)


def in_proj(x, nw, w, wdt, *, d_inner, n_main, tm_cap=1024, tn_cap=1024):
    m, d = x.shape
    n = n_main
    tm, tn = _row_tile(m, tm_cap), _row_tile(n, tn_cap)
    assert d_inner % tn == 0
    return pl.pallas_call(
        _in_proj_kernel,
        grid=(m // tm, n // tn),
        in_specs=[
            pl.BlockSpec((tm, d), lambda i, j: (i, 0)),
            pl.BlockSpec((1, d), lambda i, j: (0, 0)),
            pl.BlockSpec((d, tn), lambda i, j: (0, (j + d_inner // tn) % (n // tn))),
            pl.BlockSpec((d, LANES), lambda i, j: (0, 0)),
        ],
        out_specs=[
            pl.BlockSpec((tm, tn), lambda i, j: (i, j)),
            pl.BlockSpec((tm, LANES), lambda i, j: (i, 0)),
        ],
        out_shape=[jax.ShapeDtypeStruct((m, n), F32), jax.ShapeDtypeStruct((m, LANES), F32)],
        scratch_shapes=[pltpu.VMEM((tm, d), BF16)],
        compiler_params=_params("parallel", "arbitrary"),
        name="in_proj",
    )(x, nw.reshape(1, d), w, wdt)


def _in_proj_conv_kernel(x_ref, nw_ref, w_ref, wdt_ref, cw_ref, cb_ref, conv0_ref,
                         xs_ref, bb_ref, cc_ref, sz_ref, dt_ref, tail_ref,
                         xn_ref, xpad_ref, halo_ref, *, tiles_per_seq, d_inner, bc):
    i = pl.program_id(0)
    tm = x_ref.shape[0]
    conv_dim = d_inner + 2 * bc
    pad = SUBLANES
    sub = 8 * LANES
    per = sub // LANES
    n_conv, n_z = conv_dim // sub, d_inner // sub

    @pl.when(i % tiles_per_seq == 0)
    def _():
        for l in range(conv_dim // LANES):
            halo_ref[l] = conv0_ref[0, :, l * LANES:(l + 1) * LANES]

    xn = _rms(x_ref[...], nw_ref[...]).astype(BF16)
    xn_ref[...] = xn
    dt_ref[...] = jnp.dot(xn, wdt_ref[...], preferred_element_type=F32)

    def w_cols(s):
        start = d_inner + s * sub if s < n_conv else (s - n_conv) * sub
        return slice(start, start + sub)

    def matmul_cols(s, _):
        return jnp.dot(xn_ref[...], w_ref[:, w_cols(s)], preferred_element_type=F32)

    def activate(s, raw):
        if s >= n_conv:
            z0 = (s - n_conv) * sub
            sz_ref[:, z0:z0 + sub] = _silu(raw).astype(BF16)
            return
        for e in range(per):
            l = s * per + e
            lo = l * LANES
            slot = (s % 2) * per + e
            xpad_ref[slot, 0:pad, :] = halo_ref[l]
            xpad_ref[slot, pad:pad + tm, :] = raw[:, e * LANES:(e + 1) * LANES]
            half = cb_ref[:, lo:lo + LANES]
            for k in range(CONV_K):
                off = pad - (CONV_K - 1) + k
                half = half + xpad_ref[slot, off:off + tm, :] * cw_ref[k:k + 1, lo:lo + LANES]
            act = half + half * jnp.tanh(half)
            if lo < d_inner:
                xs_ref[:, lo:lo + LANES] = act
            elif lo < d_inner + bc:
                bb_ref[:, lo - d_inner:lo - d_inner + LANES] = act.astype(BF16)
            else:
                cc_ref[:, lo - d_inner - bc:lo - d_inner - bc + LANES] = act.astype(BF16)
            last_rows = xpad_ref[slot, tm:tm + pad, :]
            halo_ref[l] = last_rows
            tail_ref[0, :, lo:lo + LANES] = last_rows

    _staggered(n_conv + n_z, [matmul_cols, activate])


def in_proj_conv(x, nw, w, wdt, cw, cb, conv0, *, seq_len, d_inner, tm_cap=512):
    m, d = x.shape
    conv_dim = cw.shape[1]
    bc = (conv_dim - d_inner) // 2
    tm = _row_tile(seq_len, tm_cap)
    sub = 8 * LANES
    assert d_inner % sub == 0 and bc % sub == 0 and w.shape[1] >= conv_dim + d_inner and m % seq_len == 0
    bf = lambda n: jax.ShapeDtypeStruct((m, n), BF16)
    tiles_per_seq = seq_len // tm
    row = lambda i: (i, 0)
    const = lambda i: (0, 0)
    *acts, tails = pl.pallas_call(
        functools.partial(_in_proj_conv_kernel, tiles_per_seq=tiles_per_seq, d_inner=d_inner, bc=bc),
        grid=(m // tm,),
        in_specs=[
            pl.BlockSpec((tm, d), row),
            pl.BlockSpec((1, d), const),
            pl.BlockSpec(w.shape, const, pipeline_mode=pl.Buffered(1)),
            pl.BlockSpec((d, LANES), const),
            pl.BlockSpec((CONV_K, conv_dim), const),
            pl.BlockSpec((1, conv_dim), const),
            pl.BlockSpec((1, SUBLANES, conv_dim), lambda i: (0, 0, 0)),
        ],
        out_specs=[
            pl.BlockSpec((tm, d_inner), row),
            pl.BlockSpec((tm, bc), row),
            pl.BlockSpec((tm, bc), row),
            pl.BlockSpec((tm, d_inner), row),
            pl.BlockSpec((tm, LANES), row),
            pl.BlockSpec((1, SUBLANES, conv_dim), lambda i: (i, 0, 0)),
        ],
        out_shape=[jax.ShapeDtypeStruct((m, d_inner), F32), bf(bc), bf(bc), bf(d_inner),
                   jax.ShapeDtypeStruct((m, LANES), F32),
                   jax.ShapeDtypeStruct((m // tm, SUBLANES, conv_dim), F32)],
        scratch_shapes=[
            pltpu.VMEM((tm, d), BF16),
            pltpu.VMEM((2 * sub // LANES, tm + SUBLANES, LANES), F32),
            pltpu.VMEM((conv_dim // LANES, SUBLANES, LANES), F32),
        ],
        compiler_params=_params("arbitrary"),
        name="in_proj_conv",
    )(x, nw.reshape(1, d), w, wdt, cw, cb, conv0)
    return (*acts, tails[tiles_per_seq - 1::tiles_per_seq])


def _mlp_steps(x_rows, nw_ref, wu_ref, wd_ref, fw_ref, o_ref, xn_ref, final_norm):
    f = pl.program_id(1)
    tm = o_ref.shape[0]
    rb = min(tm, 2 * LANES)

    @pl.when(f == 0)
    def _():
        def normalise(r, x):
            rows = slice(r * rb, (r + 1) * rb)
            xn_ref[rows, :] = _rms(x, nw_ref[...]).astype(BF16)
            o_ref[rows, :] = x

        _staggered(tm // rb, [lambda r, _: x_rows(slice(r * rb, (r + 1) * rb)), normalise])

    h = jnp.dot(xn_ref[...], wu_ref[...], preferred_element_type=F32)
    h = jnp.square(jnp.maximum(h, 0.0)).astype(BF16)
    o_ref[...] += jnp.dot(h, wd_ref[...], preferred_element_type=F32)

    if final_norm:
        @pl.when(f == pl.num_programs(1) - 1)
        def _():
            o_ref[...] = _rms(o_ref[...], fw_ref[...])


def _mlp_kernel(x_ref, nw_ref, wu_ref, wd_ref, fw_ref, o_ref, xn_ref, *, final_norm):
    _mlp_steps(lambda rows: x_ref[rows, :], nw_ref, wu_ref, wd_ref, fw_ref, o_ref, xn_ref, final_norm)


def _proj_mlp_kernel(a_ref, wa_ref, x_ref, nw_ref, wu_ref, wd_ref, fw_ref, o_ref, xn_ref, *, final_norm):
    def block_input(rows):
        return x_ref[rows, :] + jnp.dot(a_ref[rows, :], wa_ref[...], preferred_element_type=F32)

    _mlp_steps(block_input, nw_ref, wu_ref, wd_ref, fw_ref, o_ref, xn_ref, final_norm)


def mlp(x, nw, wu, wd, layer, fw, *, final_norm, proj=None, tm_cap=1024, tf_cap=1024):
    m, d = x.shape
    dff = wu.shape[2]
    tm, tf = _row_tile(m, tm_cap), _row_tile(dff, tf_cap)
    in_specs = [
        pl.BlockSpec((tm, d), lambda i, f: (i, 0)),
        pl.BlockSpec((1, d), lambda i, f: (0, 0)),
        pl.BlockSpec((None, d, tf), lambda i, f: (layer, 0, f)),
        pl.BlockSpec((None, tf, d), lambda i, f: (layer, f, 0)),
        pl.BlockSpec((1, d), lambda i, f: (0, 0)),
    ]
    args = [x, nw.reshape(1, d), wu, wd, fw.reshape(1, d)]
    body = _mlp_kernel
    if proj is not None:
        a, wa = proj
        k = a.shape[1]
        in_specs = [pl.BlockSpec((tm, k), lambda i, f: (i, 0)),
                    pl.BlockSpec((k, d), lambda i, f: (0, 0), pipeline_mode=pl.Buffered(1))] + in_specs
        args = [a, wa] + args
        body = _proj_mlp_kernel
    return pl.pallas_call(
        functools.partial(body, final_norm=final_norm),
        grid=(m // tm, dff // tf),
        in_specs=in_specs,
        out_specs=pl.BlockSpec((tm, d), lambda i, f: (i, 0)),
        out_shape=jax.ShapeDtypeStruct((m, d), F32),
        scratch_shapes=[pltpu.VMEM((tm, d), BF16)],
        compiler_params=_params("parallel", "arbitrary"),
        name="mlp" if proj is None else "proj_mlp",
    )(*args)


def _softplus(x):
    return jnp.maximum(x, 0.0) + jnp.log(1.0 + jnp.exp(-jnp.abs(x)))


def _split3(a):
    hi = a.astype(BF16)
    r1 = a - hi.astype(F32)
    mid = r1.astype(BF16)
    lo = (r1 - mid.astype(F32)).astype(BF16)
    return hi, mid, lo


def _ssd_kernel(xbc_ref, z_ref, dt_ref, conv0_ref, ssm0_ref,
                cw_ref, cb_ref, dtb_ref, alog_ref, dsk_ref, gw_ref,
                g_ref, convo_ref, ssmo_ref,
                xpad_ref, act_ref, state_ref, y_ref, *, q, n_heads):
    c = pl.program_id(1)
    p = SSM_HEAD_DIM
    d_inner = n_heads * p
    hpg = n_heads // N_GROUPS
    conv_dim = d_inner + 2 * N_GROUPS * D_STATE
    pad = SUBLANES

    @pl.when(c == 0)
    def _():
        xpad_ref[0:pad, :] = conv0_ref[0]
        state_ref[...] = ssm0_ref[0]

    xpad_ref[pad:pad + q, :] = xbc_ref[0]
    cblk = 512
    for j in range(conv_dim // cblk):
        cs = slice(j * cblk, (j + 1) * cblk)
        conv = cb_ref[:, cs]
        for k in range(CONV_K):
            off = pad - (CONV_K - 1) + k
            conv = conv + xpad_ref[off:off + q, cs] * cw_ref[k:k + 1, cs]
        act_ref[:, cs] = _silu(conv)
    xpad_ref[0:pad, :] = xpad_ref[q:q + pad, :]

    dt = _softplus(dt_ref[0] + dtb_ref[...])
    a = dt * (-jnp.exp(alog_ref[...]))
    row = lax.broadcasted_iota(jnp.int32, (q, q), 0)
    col = lax.broadcasted_iota(jnp.int32, (q, q), 1)
    causal = row >= col
    tri = jnp.where(causal, 1.0, 0.0).astype(BF16)
    acs = None
    for part in _split3(a):
        t = jnp.dot(tri, part, preferred_element_type=F32)
        acs = t if acs is None else acs + t
    if q < LANES:
        acs_sq = jnp.concatenate([acs, jnp.zeros((LANES - q, LANES), F32)], axis=0)
    else:
        acs_sq = acs
    acs_t = acs_sq.T
    eacs = jnp.exp(acs)
    last = acs[q - 1:q, :]
    dte = jnp.exp(last - acs)
    cdecay = jnp.exp(last)

    for g in range(N_GROUPS):
        b0 = d_inner + g * D_STATE
        c0 = d_inner + N_GROUPS * D_STATE + g * D_STATE
        bb = act_ref[:, b0:b0 + D_STATE].astype(BF16)
        cc = act_ref[:, c0:c0 + D_STATE].astype(BF16)
        cbm = lax.dot_general(cc, bb, _NT, preferred_element_type=F32)
        for pair in range(hpg // 2):
            xs2 = act_ref[:, (g * hpg + 2 * pair) * p:(g * hpg + 2 * pair + 2) * p]
            dsk2 = dsk_ref[:, (g * hpg + 2 * pair) * p:(g * hpg + 2 * pair + 2) * p]
            ys = []
            for e in range(2):
                h = g * hpg + 2 * pair + e
                xs = xs2[:, e * p:(e + 1) * p]
                seg = acs[:, h:h + 1] - acs_t[h:h + 1, 0:q]
                decay = jnp.exp(jnp.where(causal, seg, -jnp.inf))
                m = (cbm * decay).astype(BF16)
                xdt = xs * dt[:, h:h + 1]
                y = jnp.dot(m, xdt.astype(BF16), preferred_element_type=F32)
                st = state_ref[h * p:(h + 1) * p, :]
                yoff = lax.dot_general(cc, st.astype(BF16), _NT, preferred_element_type=F32)
                y = y + yoff * eacs[:, h:h + 1] + dsk2[:, e * p:(e + 1) * p] * xs
                ys.append(y)
                xdtw = (xdt * dte[:, h:h + 1]).astype(BF16)
                snew = lax.dot_general(xdtw, bb, _TN, preferred_element_type=F32)
                state_ref[h * p:(h + 1) * p, :] = (
                    jnp.broadcast_to(cdecay[:, h:h + 1], (p, D_STATE)) * st + snew)
            y_ref[:, (g * hpg + 2 * pair) * p:(g * hpg + 2 * pair + 2) * p] = (
                jnp.concatenate(ys, axis=1))

    gsz = d_inner // N_GROUPS
    for g in range(N_GROUPS):
        gs = slice(g * gsz, (g + 1) * gsz)
        gg = y_ref[:, gs] * _silu(z_ref[0, :, gs])
        ms = jnp.mean(gg * gg, axis=-1, keepdims=True)
        g_ref[0, :, gs] = (gg * lax.rsqrt(ms + EPS) * gw_ref[:, gs]).astype(BF16)

    @pl.when(c == pl.num_programs(1) - 1)
    def _():
        convo_ref[0] = xpad_ref[0:pad, :]
        ssmo_ref[0] = state_ref[...]


def ssd_mixer(xbcz, dt, conv0, ssm0, cw, cb, dtb, alog, dsk, gw, *, shared_state):
    bsz, length, _ = xbcz.shape
    conv_dim = cw.shape[1]
    d_inner = gw.shape[1]
    n_heads = d_inner // SSM_HEAD_DIM
    q = SSD_CHUNK if length % SSD_CHUNK == 0 else length
    assert q % SUBLANES == 0 and q >= SUBLANES and conv_dim % d_inner == 0
    nc = length // q
    zblk = conv_dim // d_inner
    if shared_state:
        st_map = lambda b, c: (0, 0, 0)
    else:
        st_map = lambda b, c: (b, 0, 0)
    const = lambda b, c: (0, 0)
    return pl.pallas_call(
        functools.partial(_ssd_kernel, q=q, n_heads=n_heads),
        grid=(bsz, nc),
        in_specs=[
            pl.BlockSpec((1, q, conv_dim), lambda b, c: (b, c, 0)),
            pl.BlockSpec((1, q, d_inner), lambda b, c: (b, c, zblk)),
            pl.BlockSpec((1, q, LANES), lambda b, c: (b, c, 0)),
            pl.BlockSpec((1, SUBLANES, conv_dim), st_map),
            pl.BlockSpec((1, n_heads * SSM_HEAD_DIM, D_STATE), st_map),
            pl.BlockSpec((CONV_K, conv_dim), const),
            pl.BlockSpec((1, conv_dim), const),
            pl.BlockSpec((1, LANES), const),
            pl.BlockSpec((1, LANES), const),
            pl.BlockSpec((1, d_inner), const),
            pl.BlockSpec((1, d_inner), const),
        ],
        out_specs=[
            pl.BlockSpec((1, q, d_inner), lambda b, c: (b, c, 0)),
            pl.BlockSpec((1, SUBLANES, conv_dim), lambda b, c: (b, 0, 0)),
            pl.BlockSpec((1, n_heads * SSM_HEAD_DIM, D_STATE), lambda b, c: (b, 0, 0)),
        ],
        out_shape=[
            jax.ShapeDtypeStruct((bsz, length, d_inner), BF16),
            jax.ShapeDtypeStruct((bsz, SUBLANES, conv_dim), F32),
            jax.ShapeDtypeStruct((bsz, n_heads * SSM_HEAD_DIM, D_STATE), F32),
        ],
        scratch_shapes=[
            pltpu.VMEM((q + SUBLANES, conv_dim), F32),
            pltpu.VMEM((q, conv_dim), F32),
            pltpu.VMEM((n_heads * SSM_HEAD_DIM, D_STATE), F32),
            pltpu.VMEM((q, d_inner), F32),
        ],
        compiler_params=_params("parallel", "arbitrary"),
        name="ssd_mixer",
    )(xbcz, xbcz, dt, conv0, ssm0, cw, cb, dtb, alog, dsk, gw)


def _expansion_matrix(n_heads):
    h = jnp.arange(LANES)[:, None]
    ex = h == jnp.arange(n_heads * SSM_HEAD_DIM)[None, :] // SSM_HEAD_DIM
    return jnp.tile(ex.astype(BF16), (3, 1))


def _store_act_tile(j, act, xs_ref, bb_ref, cc_ref, d_inner, bc):
    lo = j * LANES
    if lo < d_inner:
        xs_ref[:, lo:lo + LANES] = act
    elif lo < d_inner + bc:
        bb_ref[:, lo - d_inner:lo - d_inner + LANES] = act.astype(BF16)
    else:
        cc_ref[:, lo - d_inner - bc:lo - d_inner - bc + LANES] = act.astype(BF16)


def _ssd_token_math(dt_raw, dtb_ref, alog_ref, lhs3_ref, acst_ref, *, seg_len):
    q = dt_raw.shape[0]
    dt = _softplus(dt_raw + dtb_ref[...])
    a = dt * (-jnp.exp(alog_ref[...]))
    row = lax.broadcasted_iota(jnp.int32, (q, q), 0)
    col = lax.broadcasted_iota(jnp.int32, (q, q), 1)
    mask = row >= col
    if seg_len != q:
        mask = jnp.logical_and(mask, row // seg_len == col // seg_len)
        seg_end = (row // seg_len) * seg_len + (seg_len - 1)
    tri = jnp.where(mask, 1.0, 0.0).astype(BF16)
    acs = None
    for part in _split3(a):
        t = jnp.dot(tri, part, preferred_element_type=F32)
        acs = t if acs is None else acs + t
    if seg_len == q:
        last = acs[q - 1:q, :]
    else:
        sel = jnp.where(col == seg_end, 1.0, 0.0).astype(BF16)
        last = None
        for part in _split3(acs):
            t = jnp.dot(sel, part, preferred_element_type=F32)
            last = t if last is None else last + t
    acst_ref[...] = (acs * LOG2_E).T
    eacs = jnp.exp(acs)
    stack = jnp.concatenate([dt, eacs, jnp.exp(last - acs)], axis=0)
    lhs3_ref[...] = jnp.concatenate(_split3(stack), axis=1)
    return mask, eacs


def _staggered(n, stages):
    carried = {}
    for t in range(n + len(stages) - 1):
        for k, stage in enumerate(stages):
            g = t - k
            if 0 <= g < n:
                carried[g] = stage(g, carried.get(g))


def _ssd_group_inputs(g, xs_ref, bb_ref, cc_ref, lhs3_ref, ex3_ref, hpg):
    q = xs_ref.shape[0]
    gch = hpg * SSM_HEAD_DIM
    gs = slice(g * gch, (g + 1) * gch)
    bb = bb_ref[:, g * D_STATE:(g + 1) * D_STATE]
    cc = cc_ref[:, g * D_STATE:(g + 1) * D_STATE]
    cbm = lax.dot_general(cc, bb, _NT, preferred_element_type=F32)
    ex = jnp.dot(lhs3_ref[...], ex3_ref[:, gs], preferred_element_type=F32)
    xs = xs_ref[:, gs]
    return dict(bb=bb, cc=cc, cbm=cbm, xs=xs, xdt=xs * ex[0:q], eax=ex[q:2 * q], dtex=ex[2 * q:3 * q])


def _ssd_group_diag(g, v, mask, acst_ref, hpg):
    q = v["xs"].shape[0]
    p = SSM_HEAD_DIM
    lane_head = lax.broadcasted_iota(jnp.int32, (q, hpg * p), 1) // p
    ms, rhs = [], []
    for r in range(hpg):
        h = g * hpg + r
        rowb = jnp.broadcast_to(acst_ref[h:h + 1, :], (q, q))
        decay = jnp.exp2(jnp.where(mask, rowb.T - rowb, -jnp.inf))
        ms.append((v["cbm"] * decay).astype(BF16))
        rhs.append(jnp.where(lane_head == r, v["xdt"], 0.0).astype(BF16))
    y = jnp.dot(jnp.concatenate(ms, axis=1), jnp.concatenate(rhs, axis=0),
                preferred_element_type=F32)
    return dict(v, y=y, cbm=None)


def _ssd_chunk_kernel(xs_ref, bb_ref, cc_ref, sz_ref, dt_ref, ssm0_ref,
                      dtb_ref, alog_ref, dsk_ref, gw_ref, ex3_ref,
                      g_ref, ssmo_ref,
                      st_ref, lhs3_ref, acst_ref, *, n_heads, cps):
    c = pl.program_id(1)
    q = SSD_CHUNK
    hpg = n_heads // N_GROUPS
    gch = hpg * SSM_HEAD_DIM

    @pl.when(c == 0)
    def _():
        st_ref[...] = ssm0_ref[0].T

    rows = [pl.ds(h * q, q) for h in range(cps)]
    masks = [_ssd_token_math(dt_ref[0, rows[h], :], dtb_ref, alog_ref, lhs3_ref.at[h], acst_ref.at[h],
                             seg_len=q)[0] for h in range(cps)]

    def finish(item, v):
        h, g = divmod(item, N_GROUPS)
        gs = slice(g * gch, (g + 1) * gch)
        st = st_ref[:, gs]
        y = (v["y"] + jnp.dot(v["cc"], st.astype(BF16), preferred_element_type=F32) * v["eax"]
             + dsk_ref[:, gs] * v["xs"])
        gg = y * sz_ref[0, rows[h], gs].astype(F32)
        msq = jnp.mean(gg * gg, axis=-1, keepdims=True)
        g_ref[0, rows[h], gs] = (gg * lax.rsqrt(msq + EPS) * gw_ref[:, gs]).astype(BF16)
        xdtw = (v["xdt"] * v["dtex"]).astype(BF16)
        snew = lax.dot_general(v["bb"], xdtw, _TN, preferred_element_type=F32)
        st_ref[:, gs] = st * v["eax"][q - 1:q, :] + snew

    def inputs(item, _):
        h, g = divmod(item, N_GROUPS)
        return _ssd_group_inputs(g, xs_ref.at[0, rows[h]], bb_ref.at[0, rows[h]], cc_ref.at[0, rows[h]],
                                 lhs3_ref.at[h], ex3_ref, hpg)

    def diag(item, v):
        h, g = divmod(item, N_GROUPS)
        return _ssd_group_diag(g, v, masks[h], acst_ref.at[h], hpg)

    _staggered(cps * N_GROUPS, [inputs, diag, finish])

    @pl.when(c == pl.num_programs(1) - 1)
    def _():
        ssmo_ref[0] = st_ref[...].T


def ssd_chunked(xs, bb, cc, sz, dt, ssm0, dtb, alog, dsk, gw, *, chunks_per_step=4):
    bsz, length, d_inner = xs.shape
    bc = bb.shape[2]
    n_heads = d_inner // SSM_HEAD_DIM
    q = SSD_CHUNK
    assert length % q == 0 and bc == N_GROUPS * D_STATE and n_heads % N_GROUPS == 0
    cps = _row_tile(length // q, chunks_per_step)
    qs = cps * q
    ex3 = _expansion_matrix(n_heads)
    rows = lambda b, c: (b, c, 0)
    const = lambda b, c: (0, 0)
    return pl.pallas_call(
        functools.partial(_ssd_chunk_kernel, n_heads=n_heads, cps=cps),
        grid=(bsz, length // qs),
        in_specs=[
            pl.BlockSpec((1, qs, d_inner), rows),
            pl.BlockSpec((1, qs, bc), rows),
            pl.BlockSpec((1, qs, bc), rows),
            pl.BlockSpec((1, qs, d_inner), rows),
            pl.BlockSpec((1, qs, LANES), rows),
            pl.BlockSpec((1, d_inner, D_STATE), lambda b, c: (0, 0, 0)),
            pl.BlockSpec((1, LANES), const),
            pl.BlockSpec((1, LANES), const),
            pl.BlockSpec((1, d_inner), const),
            pl.BlockSpec((1, d_inner), const),
            pl.BlockSpec(ex3.shape, const),
        ],
        out_specs=[
            pl.BlockSpec((1, qs, d_inner), rows),
            pl.BlockSpec((1, d_inner, D_STATE), lambda b, c: (b, 0, 0)),
        ],
        out_shape=[
            jax.ShapeDtypeStruct((bsz, length, d_inner), BF16),
            jax.ShapeDtypeStruct((bsz, d_inner, D_STATE), F32),
        ],
        scratch_shapes=[
            pltpu.VMEM((D_STATE, d_inner), F32),
            pltpu.VMEM((cps, 3 * q, 3 * LANES), BF16),
            pltpu.VMEM((cps, LANES, q), F32),
        ],
        compiler_params=_params("parallel", "arbitrary"),
        name="ssd_chunked",
    )(xs, bb, cc, sz, dt, ssm0, dtb, alog, dsk, gw, ex3)


def _ssd_step_tokens_kernel(xbc_ref, dt_ref, conv0_ref, cw_ref, cb_ref, dtb_ref, alog_ref, dsk_ref,
                            ex3_ref,
                            ypart_ref, eaxo_ref, xdtw_ref, bbo_ref, cco_ref, eacs_ref,
                            xpad_ref, xs_ref, bb_ref, cc_ref, lhs3_ref, acst_ref, *, n_heads, t):
    q = SSD_CHUNK
    p = SSM_HEAD_DIM
    d_inner = n_heads * p
    hpg = n_heads // N_GROUPS
    gch = hpg * p
    bc = N_GROUPS * D_STATE
    ntile = (d_inner + 2 * bc) // LANES
    nseq = q // t
    slot = 2 * t
    span = nseq * slot - t

    for j in range(ntile):
        cs = slice(j * LANES, (j + 1) * LANES)
        for i in range(nseq):
            xpad_ref[j, i * slot:i * slot + t, :] = conv0_ref[i, :, cs]
            xpad_ref[j, i * slot + t:(i + 1) * slot, :] = xbc_ref[i * t:(i + 1) * t, cs]
        conv = cb_ref[:, cs]
        for k in range(CONV_K):
            off = t - (CONV_K - 1) + k
            conv = conv + xpad_ref[j, off:off + span, :] * cw_ref[k:k + 1, cs]
        act = _silu(jnp.concatenate([conv[i * slot:i * slot + t] for i in range(nseq)], axis=0))
        _store_act_tile(j, act, xs_ref, bb_ref, cc_ref, d_inner, bc)
        lo = j * LANES
        if d_inner <= lo < d_inner + bc:
            bbo_ref[:, lo - d_inner:lo - d_inner + LANES] = act
        elif lo >= d_inner + bc:
            cco_ref[:, lo - d_inner - bc:lo - d_inner - bc + LANES] = act

    mask, eacs = _ssd_token_math(dt_ref[...], dtb_ref, alog_ref, lhs3_ref, acst_ref, seg_len=t)
    eacs_ref[...] = eacs
    def emit(g, v):
        gs = slice(g * gch, (g + 1) * gch)
        ypart_ref[:, gs] = v["y"] + dsk_ref[:, gs] * v["xs"]
        eaxo_ref[:, gs] = v["eax"]
        xdtw_ref[:, gs] = v["xdt"] * v["dtex"]

    _staggered(N_GROUPS, [
        lambda g, _: _ssd_group_inputs(g, xs_ref, bb_ref, cc_ref, lhs3_ref, ex3_ref, hpg),
        lambda g, v: _ssd_group_diag(g, v, mask, acst_ref, hpg),
        emit,
    ])


def ssd_step_tokens(xbcz, dt, conv0, cw, cb, dtb, alog, dsk, *, t):
    n_tok = xbcz.shape[0]
    conv_dim = cw.shape[1]
    d_inner = dsk.shape[1]
    n_heads = d_inner // SSM_HEAD_DIM
    q = SSD_CHUNK
    bc = N_GROUPS * D_STATE
    assert t == SUBLANES and n_tok % q == 0
    ex3 = _expansion_matrix(n_heads)
    const = lambda s: (0, 0)
    rows = lambda s: (s, 0)
    f32 = lambda n: jax.ShapeDtypeStruct((n_tok, n), F32)
    return pl.pallas_call(
        functools.partial(_ssd_step_tokens_kernel, n_heads=n_heads, t=t),
        grid=(n_tok // q,),
        in_specs=[
            pl.BlockSpec((q, conv_dim), rows),
            pl.BlockSpec((q, LANES), rows),
            pl.BlockSpec((q // t, SUBLANES, conv_dim), lambda s: (s, 0, 0)),
            pl.BlockSpec((CONV_K, conv_dim), const),
            pl.BlockSpec((1, conv_dim), const),
            pl.BlockSpec((1, LANES), const),
            pl.BlockSpec((1, LANES), const),
            pl.BlockSpec((1, d_inner), const),
            pl.BlockSpec(ex3.shape, const),
        ],
        out_specs=[
            pl.BlockSpec((q, d_inner), rows),
            pl.BlockSpec((q, d_inner), rows),
            pl.BlockSpec((q, d_inner), rows),
            pl.BlockSpec((q, bc), rows),
            pl.BlockSpec((q, bc), rows),
            pl.BlockSpec((q, LANES), rows),
        ],
        out_shape=[f32(d_inner), f32(d_inner), f32(d_inner), f32(bc), f32(bc), f32(LANES)],
        scratch_shapes=[
            pltpu.VMEM((conv_dim // LANES, 2 * q, LANES), F32),
            pltpu.VMEM((q, d_inner), F32),
            pltpu.VMEM((q, bc), BF16),
            pltpu.VMEM((q, bc), BF16),
            pltpu.VMEM((3 * q, 3 * LANES), BF16),
            pltpu.VMEM((LANES, q), F32),
        ],
        compiler_params=_params("parallel"),
        name="ssd_step_tokens",
    )(xbcz, dt, conv0, cw, cb, dtb, alog, dsk, ex3)


def _ssd_step_state_kernel(cd_ref, st_ref, cc_ref, bb_ref, xdtw_ref, eax_ref, ypart_ref, z_ref, gw_ref,
                           *rest, n_heads, t, nseq, n_fill):
    g_ref, sto_ref = rest[-2:]
    s = pl.program_id(1)
    p = SSM_HEAD_DIM
    hpg = n_heads // N_GROUPS
    gch = hpg * p

    def update():
        for g in range(N_GROUPS):
            gs = slice(g * gch, (g + 1) * gch)
            ns = slice(g * D_STATE, (g + 1) * D_STATE)
            ys = []
            for i in range(nseq):
                rows = slice(i * t, (i + 1) * t)
                h0 = st_ref[0, i, gs, :]
                yoff = lax.dot_general(cc_ref[rows, ns].astype(BF16), h0.astype(BF16), _NT,
                                       preferred_element_type=F32)
                ys.append(ypart_ref[rows, gs] + yoff * eax_ref[rows, gs])
                upd = lax.dot_general(xdtw_ref[rows, gs].astype(BF16), bb_ref[rows, ns].astype(BF16), _TN,
                                      preferred_element_type=F32)
                for r in range(hpg):
                    hs = slice(r * p, (r + 1) * p)
                    sto_ref[0, i, g * gch + r * p:g * gch + (r + 1) * p, :] = (
                        h0[hs] * cd_ref[s * nseq + i, g * hpg + r] + upd[hs])
            gg = jnp.concatenate(ys, axis=0) * _silu(z_ref[:, gs])
            msq = jnp.mean(gg * gg, axis=-1, keepdims=True)
            g_ref[:, gs] = (gg * lax.rsqrt(msq + EPS) * gw_ref[:, gs]).astype(BF16)

    if n_fill == 0:
        update()
    else:
        @pl.when(pl.program_id(0) < n_fill)
        def _():
            sto_ref[...] = jnp.zeros(sto_ref.shape, F32)

        pl.when(pl.program_id(0) == n_fill)(update)


def ssd_step_state(cd, state_all, layer, cc, bb, xdtw, eax, ypart, xbcz, gw, stacked_out, *, t, seqs_per_step=8):
    n_layers, n_seq, d_inner, d_state = state_all.shape
    n_tok = ypart.shape[0]
    bc = cc.shape[1]
    conv_dim = xbcz.shape[1] - d_inner
    nseq = _row_tile(n_seq, seqs_per_step)
    rt = nseq * t
    n_fill = n_layers - 1 if stacked_out is None else 0

    def active(l, s):
        return jnp.where(l == n_fill, s, 0)

    def out_layer(l):
        return jnp.where(l < n_fill, l + jnp.where(l >= layer, 1, 0), layer)

    rows = lambda l, s, cd_: (active(l, s), 0)
    st_map = lambda l, s, cd_: (layer, active(l, s), 0, 0)
    in_specs = [
        pl.BlockSpec((1, nseq, d_inner, d_state), st_map),
        pl.BlockSpec((rt, bc), rows),
        pl.BlockSpec((rt, bc), rows),
        pl.BlockSpec((rt, d_inner), rows),
        pl.BlockSpec((rt, d_inner), rows),
        pl.BlockSpec((rt, d_inner), rows),
        pl.BlockSpec((rt, d_inner), lambda l, s, cd_: (active(l, s), conv_dim // d_inner)),
        pl.BlockSpec((1, d_inner), lambda l, s, cd_: (0, 0)),
    ]
    args = [cd, state_all, cc, bb, xdtw, eax, ypart, xbcz, gw]
    aliases = {}
    if stacked_out is not None:
        in_specs.append(pl.BlockSpec(memory_space=pl.ANY))
        aliases = {len(args): 1}
        args.append(stacked_out)
    return pl.pallas_call(
        functools.partial(_ssd_step_state_kernel, n_heads=d_inner // SSM_HEAD_DIM, t=t, nseq=nseq,
                          n_fill=n_fill),
        grid_spec=pltpu.PrefetchScalarGridSpec(
            num_scalar_prefetch=1,
            grid=(n_fill + 1, n_seq // nseq),
            in_specs=in_specs,
            out_specs=[
                pl.BlockSpec((rt, d_inner), rows),
                pl.BlockSpec((1, nseq, d_inner, d_state), lambda l, s, cd_: (out_layer(l), s, 0, 0)),
            ],
        ),
        out_shape=[
            jax.ShapeDtypeStruct((n_tok, d_inner), BF16),
            jax.ShapeDtypeStruct((n_layers, n_seq, d_inner, d_state), F32),
        ],
        input_output_aliases=aliases,
        compiler_params=_params("arbitrary", "arbitrary"),
        name="ssd_step_state",
    )(*args)


def _attn_prompt_kernel(sink_ref, q_ref, kvc_ref, kvp_ref, kvm_ref, o_ref, *, n_q_heads):
    n = pl.program_id(1)
    d = ATTN_HEAD_DIM
    kvw = N_KV_HEADS * d
    rep = n_q_heads // N_KV_HEADS
    scale = d ** -0.5
    w = q_ref.shape[1]
    rows = rep * w
    qi = lax.broadcasted_iota(jnp.int32, (rows, w), 0) % w
    ci = lax.broadcasted_iota(jnp.int32, (rows, w), 1)
    from_prev = ci > qi
    no_prev = jnp.where(n > 0, 0.0, -jnp.inf)
    kvm, kvp, kvc = kvm_ref[...], kvp_ref[0], kvc_ref[0]
    outs = []
    for k in range(N_KV_HEADS):
        ks, vs = slice(k * d, (k + 1) * d), slice(kvw + k * d, kvw + (k + 1) * d)
        qk = q_ref[0, :, k * rep * d:(k + 1) * rep * d]
        q4 = jnp.concatenate([qk[:, r * d:(r + 1) * d] for r in range(rep)], axis=0)
        q4 = (q4 * scale).astype(BF16)
        k2 = jnp.concatenate([kvp[:, ks], kvc[:, ks]], axis=0).astype(BF16)
        v2 = jnp.concatenate([kvp[:, vs], kvc[:, vs]], axis=0).astype(BF16)
        s2 = lax.dot_general(q4, k2, _NT, preferred_element_type=F32)
        s_w = jnp.where(from_prev, s2[:, :w] + no_prev, s2[:, w:])
        s_m = lax.dot_general(q4, kvm[:, ks].astype(BF16), _NT, preferred_element_type=F32)
        p_ws, p_ms, dens = [], [], []
        for r in range(rep):
            sl = slice(r * w, (r + 1) * w)
            sink = sink_ref[k * rep + r]
            mx = jnp.maximum(jnp.maximum(jnp.max(s_w[sl], axis=-1, keepdims=True),
                                         jnp.max(s_m[sl], axis=-1, keepdims=True)), sink)
            p_w = jnp.exp(s_w[sl] - mx)
            p_m = jnp.exp(s_m[sl] - mx)
            dens.append(jnp.exp(sink - mx) + jnp.sum(p_w, axis=-1, keepdims=True)
                        + jnp.sum(p_m, axis=-1, keepdims=True))
            p_ws.append(p_w)
            p_ms.append(p_m.astype(BF16))
        p_w = jnp.concatenate(p_ws, axis=0)
        p2 = jnp.concatenate([jnp.where(from_prev, p_w, 0.0), jnp.where(from_prev, 0.0, p_w)],
                             axis=1).astype(BF16)
        o4 = (jnp.dot(p2, v2, preferred_element_type=F32)
              + jnp.dot(jnp.concatenate(p_ms, axis=0), kvm[:, vs].astype(BF16), preferred_element_type=F32))
        outs.extend(o4[r * w:(r + 1) * w, :] / dens[r] for r in range(rep))
    o_ref[0] = jnp.concatenate(outs, axis=1).astype(BF16)


def attn_prompt(q, kv, kvm, sinks):
    bsz, s, dq = q.shape
    kvd = kv.shape[2]
    nb = s // WINDOW
    return pl.pallas_call(
        functools.partial(_attn_prompt_kernel, n_q_heads=dq // ATTN_HEAD_DIM),
        grid=(bsz, nb),
        in_specs=[
            pl.BlockSpec(memory_space=pltpu.SMEM),
            pl.BlockSpec((1, WINDOW, dq), lambda b, n: (b, n, 0)),
            pl.BlockSpec((1, WINDOW, kvd), lambda b, n: (b, n, 0)),
            pl.BlockSpec((1, WINDOW, kvd), lambda b, n: (b, jnp.maximum(n - 1, 0), 0)),
            pl.BlockSpec(kvm.shape, lambda b, n: (0, 0)),
        ],
        out_specs=pl.BlockSpec((1, WINDOW, dq), lambda b, n: (b, n, 0)),
        out_shape=jax.ShapeDtypeStruct((bsz, s, dq), BF16),
        compiler_params=_params("parallel", "arbitrary"),
        name="attn_prompt",
    )(sinks, q, kv, kv, kvm)


def _attn_sample_kernel(sink_ref, q_ref, kvn_ref, ck_ref, cv_ref, kvm_ref, o_ref, *, n_q_heads, n_meta):
    d = ATTN_HEAD_DIM
    kvw = N_KV_HEADS * d
    rep = n_q_heads // N_KV_HEADS
    scale = d ** -0.5
    nseq, t, _ = q_ref.shape
    w_buf = ck_ref.shape[1]
    rows = rep * t
    nx = n_meta + t
    tq_buf = lax.broadcasted_iota(jnp.int32, (rows, w_buf), 0) % t
    pos_buf = PAST_LEN - w_buf + lax.broadcasted_iota(jnp.int32, (rows, w_buf), 1)
    buf_mask = jnp.logical_and(PAST_LEN + tq_buf - pos_buf < WINDOW, pos_buf >= n_meta)
    tq_x = lax.broadcasted_iota(jnp.int32, (rows, nx), 0) % t
    c_x = lax.broadcasted_iota(jnp.int32, (rows, nx), 1) - n_meta
    x_mask = jnp.logical_or(c_x < 0, jnp.logical_and(c_x <= tq_x, tq_x - c_x < WINDOW))
    head_in_group = lax.broadcasted_iota(jnp.int32, (rows, 1), 0) // t
    kvm = kvm_ref[...]
    sinks = []
    for k in range(N_KV_HEADS):
        sink = jnp.zeros((rows, 1), F32)
        for r in range(rep):
            sink = jnp.where(head_in_group == r, sink_ref[k * rep + r], sink)
        sinks.append(sink)
    chains = [(i, k) for i in range(nseq) for k in range(N_KV_HEADS)]

    scores, values = {}, {}
    for i in range(nseq):
        q, kvn, ck, cv = q_ref[i], kvn_ref[i], ck_ref[i], cv_ref[i]
        for k in range(N_KV_HEADS):
            ks, vs = slice(k * d, (k + 1) * d), slice(kvw + k * d, kvw + (k + 1) * d)
            q4 = jnp.concatenate([q[:, (k * rep + r) * d:(k * rep + r + 1) * d] for r in range(rep)], axis=0)
            q4 = (q4 * scale).astype(BF16)
            kx = jnp.concatenate([kvm[:, ks], kvn[:, ks]], axis=0).astype(BF16)
            vx = jnp.concatenate([kvm[:, vs], kvn[:, vs]], axis=0).astype(BF16)
            s_b = lax.dot_general(q4, ck[:, ks].astype(BF16), _NT, preferred_element_type=F32)
            s_x = lax.dot_general(q4, kx, _NT, preferred_element_type=F32)
            scores[i, k] = (jnp.where(buf_mask, s_b, -jnp.inf), jnp.where(x_mask, s_x, -jnp.inf))
            values[i, k] = (cv[:, ks].astype(BF16), vx)

    maxes = {}
    for c in chains:
        s_b, s_x = scores[c]
        maxes[c] = jnp.maximum(sinks[c[1]], jnp.maximum(jnp.max(s_b, axis=-1, keepdims=True),
                                                         jnp.max(s_x, axis=-1, keepdims=True)))
    probs, dens = {}, {}
    for c in chains:
        s_b, s_x = scores[c]
        p_b, p_x = jnp.exp(s_b - maxes[c]), jnp.exp(s_x - maxes[c])
        probs[c] = (p_b.astype(BF16), p_x.astype(BF16))
        dens[c] = (jnp.exp(sinks[c[1]] - maxes[c]) + jnp.sum(p_b, axis=-1, keepdims=True)
                   + jnp.sum(p_x, axis=-1, keepdims=True))
    outs = {}
    for c in chains:
        outs[c] = (jnp.dot(probs[c][0], values[c][0], preferred_element_type=F32)
                   + jnp.dot(probs[c][1], values[c][1], preferred_element_type=F32)) / dens[c]
    for i in range(nseq):
        heads = [outs[i, k][r * t:(r + 1) * t, :] for k in range(N_KV_HEADS) for r in range(rep)]
        o_ref[i] = jnp.concatenate(heads, axis=1).astype(BF16)


def attn_sample(q, kvn, ck, cv, kvm, sinks, *, seqs_per_step=8):
    bsz, t, dq = q.shape
    kvd = kvn.shape[2]
    w_buf = ck.shape[1]
    g = _row_tile(bsz, seqs_per_step)
    return pl.pallas_call(
        functools.partial(_attn_sample_kernel, n_q_heads=dq // ATTN_HEAD_DIM, n_meta=kvm.shape[0]),
        grid=(bsz // g,),
        in_specs=[
            pl.BlockSpec(memory_space=pltpu.SMEM),
            pl.BlockSpec((g, t, dq), lambda b: (b, 0, 0)),
            pl.BlockSpec((g, t, kvd), lambda b: (b, 0, 0)),
            pl.BlockSpec((g, w_buf, kvd // 2), lambda b: (b, 0, 0)),
            pl.BlockSpec((g, w_buf, kvd // 2), lambda b: (b, 0, 0)),
            pl.BlockSpec(kvm.shape, lambda b: (0, 0)),
        ],
        out_specs=pl.BlockSpec((g, t, dq), lambda b: (b, 0, 0)),
        out_shape=jax.ShapeDtypeStruct((bsz, t, dq), BF16),
        compiler_params=_params("parallel"),
        name="attn_sample",
    )(sinks, q, kvn, ck, cv, kvm)


def kernel(x_prompt, x_sample, state_conv, state_ssm, cache_k_win, cache_v_win, meta_tokens, a_norm_w, a_in_proj, a_conv_w, a_conv_b, a_dt_bias, a_log, a_d_skip, a_gate_norm_w, a_out_proj, kv_norm_w, w_kv, b_norm_w, w_q, attn_sinks, w_o, mlp_norm_w, w_up, w_down, final_norm_w):
    n_prompt, seq, d_model = x_prompt.shape
    n_dec, dec_seq, _ = x_sample.shape
    n_a = a_in_proj.shape[0]
    depth = w_up.shape[0]
    n_meta = meta_tokens.shape[0]
    d_inner = a_out_proj.shape[1]
    conv_dim = a_conv_w.shape[2]
    n_heads = a_log.shape[1]
    w_buf = cache_k_win.shape[1]
    kvw = N_KV_HEADS * ATTN_HEAD_DIM
    assert n_heads * SSM_HEAD_DIM == d_inner and n_heads <= LANES

    hm = meta_tokens.astype(F32)
    hp = x_prompt.reshape(n_prompt * seq, d_model)
    hs = x_sample.reshape(n_dec * dec_seq, d_model)

    def pad_lanes(v):
        return jnp.pad(v, (0, LANES - v.shape[0])).reshape(1, LANES)

    def pad_conv_state(s):
        return jnp.pad(s, ((0, 0), (SUBLANES - (CONV_K - 1), 0), (0, 0)))

    conv_p_list, ssm_p_list, conv_s_list, ssm_s_list = [], [], [], []
    step_path = dec_seq == SUBLANES and (n_dec * dec_seq) % SSD_CHUNK == 0
    ssm_in_all = state_ssm.reshape(n_a, n_dec, d_inner, D_STATE)
    ssm_s_all = None
    kvm = kv_p = kv_s = None
    wu, wd = w_up.astype(BF16), w_down.astype(BF16)
    for layer in range(depth):
        last = layer == depth - 1
        if layer < n_a:
            i = layer
            w_in = a_in_proj[i]
            w_main = w_in.astype(BF16)
            w_dt = jnp.pad(w_in[:, d_inner + conv_dim:], ((0, 0), (0, LANES - n_heads))).astype(BF16)
            w_out = a_out_proj[i].astype(BF16)
            prm = (a_conv_w[i], a_conv_b[i].reshape(1, conv_dim), pad_lanes(a_dt_bias[i]), pad_lanes(a_log[i]),
                   jnp.repeat(a_d_skip[i], SSM_HEAD_DIM).reshape(1, d_inner), a_gate_norm_w[i].reshape(1, d_inner))

            def mixer(h, bsz, length, conv0, ssm0, shared):
                if shared and length % SSD_CHUNK == 0:
                    xs_a, bb_a, cc_a, sz_a, dt, conv_o = in_proj_conv(
                        h, a_norm_w[i], w_main, w_dt, 0.5 * prm[0], 0.5 * prm[1], conv0,
                        seq_len=length, d_inner=d_inner)
                    split = lambda t: t.reshape(bsz, length, t.shape[1])
                    g, ssm_o = ssd_chunked(split(xs_a), split(bb_a), split(cc_a), split(sz_a), split(dt),
                                           ssm0, *prm[2:])
                else:
                    xbcz, dt = in_proj(h, a_norm_w[i], w_main, w_dt, d_inner=d_inner, n_main=conv_dim + d_inner)
                    g, conv_o, ssm_o = ssd_mixer(xbcz.reshape(bsz, length, conv_dim + d_inner),
                                                 dt.reshape(bsz, length, LANES), conv0, ssm0, *prm,
                                                 shared_state=shared)
                return g.reshape(bsz * length, d_inner), conv_o, ssm_o

            zero_conv = jnp.zeros((1, SUBLANES, conv_dim), F32)
            zero_ssm = jnp.zeros((1, d_inner, D_STATE), F32)
            g_m, conv_m, ssm_m = mixer(hm, 1, n_meta, zero_conv, zero_ssm, True)
            g_p, conv_p, ssm_p = mixer(hp, n_prompt, seq, conv_m, ssm_m, True)
            conv0_s = pad_conv_state(state_conv[i])
            if step_path:
                xbcz_s, dt_s = in_proj(hs, a_norm_w[i], w_main, w_dt, d_inner=d_inner, n_main=conv_dim + d_inner)
                ypart, eax, xdtw, bb_s, cc_s, eacs = ssd_step_tokens(xbcz_s, dt_s, conv0_s, *prm[:5], t=dec_seq)
                cdecay = eacs.reshape(n_dec, dec_seq, LANES)[:, dec_seq - 1]
                g_s, ssm_s_all = ssd_step_state(cdecay, ssm_in_all, i, cc_s, bb_s, xdtw, eax, ypart, xbcz_s,
                                                prm[5], ssm_s_all, t=dec_seq)
                conv_s = xbcz_s.reshape(n_dec, dec_seq, -1)[:, dec_seq - (CONV_K - 1):, :conv_dim]
            else:
                g_s, conv_s, ssm_s = mixer(hs, n_dec, dec_seq, conv0_s, ssm_in_all[i], False)
                conv_s = conv_s[:, SUBLANES - (CONV_K - 1):]
                ssm_s_list.append(ssm_s)
            conv_p_list.append(conv_p[:, SUBLANES - (CONV_K - 1):])
            ssm_p_list.append(ssm_p.reshape(n_prompt, n_heads, SSM_HEAD_DIM, D_STATE))
            conv_s_list.append(conv_s)
            hm = mlp(hm, mlp_norm_w[layer], wu, wd, layer, final_norm_w, final_norm=False, proj=(g_m, w_out))
            proj_p, proj_s = (g_p, w_out), (g_s, w_out)
        else:
            j = layer - n_a
            wq = w_q[j].astype(BF16)
            wo = w_o[j].astype(BF16)
            dq = wq.shape[1]
            if j == 0:
                wkv = w_kv.astype(BF16)
                kvm = norm_matmul(hm, kv_norm_w, wkv)
                q_p, kv_p = norm_matmul_heads(hp, [(b_norm_w[j], wq), (kv_norm_w, wkv)])
                q_s, kv_s = norm_matmul_heads(hs, [(b_norm_w[j], wq), (kv_norm_w, wkv)])
                kv_p = kv_p.reshape(n_prompt, seq, 2 * kvw)
                kv_s = kv_s.reshape(n_dec, dec_seq, 2 * kvw)
            else:
                q_p, = norm_matmul_heads(hp, [(b_norm_w[j], wq)])
                q_s, = norm_matmul_heads(hs, [(b_norm_w[j], wq)])
            q_p = q_p.reshape(n_prompt, seq, dq)
            q_s = q_s.reshape(n_dec, dec_seq, dq)
            o_p = attn_prompt(q_p, kv_p, kvm, attn_sinks[j])
            o_s = attn_sample(q_s, kv_s, cache_k_win.reshape(n_dec, w_buf, kvw),
                              cache_v_win.reshape(n_dec, w_buf, kvw), kvm, attn_sinks[j])
            proj_p = (o_p.reshape(n_prompt * seq, dq), wo)
            proj_s = (o_s.reshape(n_dec * dec_seq, dq), wo)
        hp = mlp(hp, mlp_norm_w[layer], wu, wd, layer, final_norm_w, final_norm=last, proj=proj_p)
        hs = mlp(hs, mlp_norm_w[layer], wu, wd, layer, final_norm_w, final_norm=last, proj=proj_s)

    y_prompt = hp.reshape(n_prompt, seq, d_model)
    y_sample = hs.reshape(n_dec, dec_seq, d_model)
    kv_heads = (N_KV_HEADS, ATTN_HEAD_DIM)
    k_p = kv_p[:, seq - w_buf:, :kvw].reshape((n_prompt, w_buf) + kv_heads)
    v_p = kv_p[:, seq - w_buf:, kvw:].reshape((n_prompt, w_buf) + kv_heads)
    k_s = kv_s[:, :, :kvw].reshape((n_dec, dec_seq) + kv_heads)
    v_s = kv_s[:, :, kvw:].reshape((n_dec, dec_seq) + kv_heads)
    k_s_win = jnp.concatenate([cache_k_win, k_s], axis=1)[:, -w_buf:]
    v_s_win = jnp.concatenate([cache_v_win, v_s], axis=1)[:, -w_buf:]
    if not step_path:
        ssm_s_all = jnp.stack(ssm_s_list)
    return (y_prompt, y_sample, jnp.stack(conv_p_list), jnp.stack(ssm_p_list), k_p, v_p,
            jnp.stack(conv_s_list), ssm_s_all.reshape(state_ssm.shape), k_s_win, v_s_win)
```

```python
import functools

import jax
import jax.numpy as jnp
from jax import lax
from jax.experimental import pallas as pl
from jax.experimental.pallas import tpu as pltpu

F32 = jnp.float32
BF16 = jnp.bfloat16

N_GROUPS = 8
SSM_HEAD_DIM = 64
D_STATE = 128
CONV_K = 4
SSD_CHUNK = 128
ATTN_HEAD_DIM = 64
N_KV_HEADS = 4
WINDOW = 128
PAST_LEN = 8192
EPS = 1e-5
LOG2_E = 1.4426950408889634

LANES = 128
SUBLANES = 8
VMEM_LIMIT_BYTES = 52 * 1024 * 1024

_NT = (((1,), (1,)), ((), ()))
_TN = (((0,), (0,)), ((), ()))


def _params(*sem):
    return pltpu.CompilerParams(dimension_semantics=sem, vmem_limit_bytes=VMEM_LIMIT_BYTES)


def _rms(x, w):
    ms = jnp.mean(x * x, axis=-1, keepdims=True)
    return x * lax.rsqrt(ms + EPS) * w


def _silu(x):
    s = 0.5 * x
    return s + s * jnp.tanh(s)


def _row_tile(m, cap):
    t = min(m, cap)
    assert m % t == 0, (m, t)
    return t


def _norm_matmul_heads_kernel(x_ref, *refs):
    n = len(refs) // 3
    gains, weights, outs = refs[0:2 * n:2], refs[1:2 * n:2], refs[2 * n:]
    tm = x_ref.shape[0]
    rb = min(tm, 2 * LANES)

    def normalise(r, _):
        x = x_ref[r * rb:(r + 1) * rb, :]
        xh = x * lax.rsqrt(jnp.mean(x * x, axis=-1, keepdims=True) + EPS)
        return [(xh * g[...]).astype(BF16) for g in gains]

    def project(r, xn):
        rows = slice(r * rb, (r + 1) * rb)
        for xn_h, w, o in zip(xn, weights, outs):
            o[rows, :] = jnp.dot(xn_h, w[...], preferred_element_type=F32)

    _staggered(tm // rb, [normalise, project])


def norm_matmul_heads(x, heads, *, tm_cap=1024):
    m, d = x.shape
    tm = _row_tile(m, tm_cap)
    row = lambda i: (i, 0)
    const = lambda i: (0, 0)
    in_specs, args = [pl.BlockSpec((tm, d), row)], [x]
    for gain, w in heads:
        in_specs += [pl.BlockSpec((1, d), const), pl.BlockSpec(w.shape, const)]
        args += [gain.reshape(1, d), w]
    return pl.pallas_call(
        _norm_matmul_heads_kernel,
        grid=(m // tm,),
        in_specs=in_specs,
        out_specs=[pl.BlockSpec((tm, w.shape[1]), row) for _, w in heads],
        out_shape=[jax.ShapeDtypeStruct((m, w.shape[1]), F32) for _, w in heads],
        compiler_params=_params("parallel"),
        name="norm_matmul_heads",
    )(*args)


def _in_proj_kernel(x_ref, nw_ref, w_ref, wdt_ref, o_ref, dt_ref, xn_ref):
    @pl.when(pl.program_id(1) == 0)
    def _():
        xn = _rms(x_ref[...], nw_ref[...]).astype(BF16)
        xn_ref[...] = xn
        dt_ref[...] = jnp.dot(xn, wdt_ref[...], preferred_element_type=F32)

    o_ref[...] = jnp.dot(xn_ref[...], w_ref[...], preferred_element_type=F32)


def in_proj(x, nw, w, wdt, *, d_inner, n_main, tm_cap=1024, tn_cap=1024):
    m, d = x.shape
    n = n_main
    tm, tn = _row_tile(m, tm_cap), _row_tile(n, tn_cap)
    assert d_inner % tn == 0
    return pl.pallas_call(
        _in_proj_kernel,
        grid=(m // tm, n // tn),
        in_specs=[
            pl.BlockSpec((tm, d), lambda i, j: (i, 0)),
            pl.BlockSpec((1, d), lambda i, j: (0, 0)),
            pl.BlockSpec((d, tn), lambda i, j: (0, (j + d_inner // tn) % (n // tn))),
            pl.BlockSpec((d, LANES), lambda i, j: (0, 0)),
        ],
        out_specs=[
            pl.BlockSpec((tm, tn), lambda i, j: (i, j)),
            pl.BlockSpec((tm, LANES), lambda i, j: (i, 0)),
        ],
        out_shape=[jax.ShapeDtypeStruct((m, n), F32), jax.ShapeDtypeStruct((m, LANES), F32)],
        scratch_shapes=[pltpu.VMEM((tm, d), BF16)],
        compiler_params=_params("parallel", "arbitrary"),
        name="in_proj",
    )(x, nw.reshape(1, d), w, wdt)


def _in_proj_conv_kernel(x_ref, nw_ref, w_ref, wdt_ref, cw_ref, cb_ref, conv0_ref,
                         xs_ref, bb_ref, cc_ref, sz_ref, dt_ref, tail_ref,
                         xn_ref, xpad_ref, halo_ref, *, tiles_per_seq, d_inner, bc):
    i = pl.program_id(0)
    tm = x_ref.shape[0]
    conv_dim = d_inner + 2 * bc
    pad = SUBLANES
    sub = 8 * LANES
    per = sub // LANES
    n_conv, n_z = conv_dim // sub, d_inner // sub

    @pl.when(i % tiles_per_seq == 0)
    def _():
        for l in range(conv_dim // LANES):
            halo_ref[l] = conv0_ref[0, :, l * LANES:(l + 1) * LANES]

    xn = _rms(x_ref[...], nw_ref[...]).astype(BF16)
    xn_ref[...] = xn
    dt_ref[...] = jnp.dot(xn, wdt_ref[...], preferred_element_type=F32)

    def w_cols(s):
        start = d_inner + s * sub if s < n_conv else (s - n_conv) * sub
        return slice(start, start + sub)

    def matmul_cols(s, _):
        return jnp.dot(xn_ref[...], w_ref[:, w_cols(s)], preferred_element_type=F32)

    def activate(s, raw):
        if s >= n_conv:
            z0 = (s - n_conv) * sub
            sz_ref[:, z0:z0 + sub] = _silu(raw).astype(BF16)
            return
        for e in range(per):
            l = s * per + e
            lo = l * LANES
            slot = (s % 2) * per + e
            xpad_ref[slot, 0:pad, :] = halo_ref[l]
            xpad_ref[slot, pad:pad + tm, :] = raw[:, e * LANES:(e + 1) * LANES]
            half = cb_ref[:, lo:lo + LANES]
            for k in range(CONV_K):
                off = pad - (CONV_K - 1) + k
                half = half + xpad_ref[slot, off:off + tm, :] * cw_ref[k:k + 1, lo:lo + LANES]
            act = half + half * jnp.tanh(half)
            if lo < d_inner:
                xs_ref[:, lo:lo + LANES] = act
            elif lo < d_inner + bc:
                bb_ref[:, lo - d_inner:lo - d_inner + LANES] = act.astype(BF16)
            else:
                cc_ref[:, lo - d_inner - bc:lo - d_inner - bc + LANES] = act.astype(BF16)
            last_rows = xpad_ref[slot, tm:tm + pad, :]
            halo_ref[l] = last_rows
            tail_ref[0, :, lo:lo + LANES] = last_rows

    _staggered(n_conv + n_z, [matmul_cols, activate])


def in_proj_conv(x, nw, w, wdt, cw, cb, conv0, *, seq_len, d_inner, tm_cap=512):
    m, d = x.shape
    conv_dim = cw.shape[1]
    bc = (conv_dim - d_inner) // 2
    tm = _row_tile(seq_len, tm_cap)
    sub = 8 * LANES
    assert d_inner % sub == 0 and bc % sub == 0 and w.shape[1] >= conv_dim + d_inner and m % seq_len == 0
    bf = lambda n: jax.ShapeDtypeStruct((m, n), BF16)
    tiles_per_seq = seq_len // tm
    row = lambda i: (i, 0)
    const = lambda i: (0, 0)
    *acts, tails = pl.pallas_call(
        functools.partial(_in_proj_conv_kernel, tiles_per_seq=tiles_per_seq, d_inner=d_inner, bc=bc),
        grid=(m // tm,),
        in_specs=[
            pl.BlockSpec((tm, d), row),
            pl.BlockSpec((1, d), const),
            pl.BlockSpec(w.shape, const, pipeline_mode=pl.Buffered(1)),
            pl.BlockSpec((d, LANES), const),
            pl.BlockSpec((CONV_K, conv_dim), const),
            pl.BlockSpec((1, conv_dim), const),
            pl.BlockSpec((1, SUBLANES, conv_dim), lambda i: (0, 0, 0)),
        ],
        out_specs=[
            pl.BlockSpec((tm, d_inner), row),
            pl.BlockSpec((tm, bc), row),
            pl.BlockSpec((tm, bc), row),
            pl.BlockSpec((tm, d_inner), row),
            pl.BlockSpec((tm, LANES), row),
            pl.BlockSpec((1, SUBLANES, conv_dim), lambda i: (i, 0, 0)),
        ],
        out_shape=[jax.ShapeDtypeStruct((m, d_inner), F32), bf(bc), bf(bc), bf(d_inner),
                   jax.ShapeDtypeStruct((m, LANES), F32),
                   jax.ShapeDtypeStruct((m // tm, SUBLANES, conv_dim), F32)],
        scratch_shapes=[
            pltpu.VMEM((tm, d), BF16),
            pltpu.VMEM((2 * sub // LANES, tm + SUBLANES, LANES), F32),
            pltpu.VMEM((conv_dim // LANES, SUBLANES, LANES), F32),
        ],
        compiler_params=_params("arbitrary"),
        name="in_proj_conv",
    )(x, nw.reshape(1, d), w, wdt, cw, cb, conv0)
    return (*acts, tails[tiles_per_seq - 1::tiles_per_seq])


def _mlp_steps(x_rows, nw_ref, wu_ref, wd_ref, fw_ref, o_ref, xn_ref, final_norm):
    f = pl.program_id(1)
    tm = o_ref.shape[0]
    rb = min(tm, 2 * LANES)

    @pl.when(f == 0)
    def _():
        def normalise(r, x):
            rows = slice(r * rb, (r + 1) * rb)
            xn_ref[rows, :] = _rms(x, nw_ref[...]).astype(BF16)
            o_ref[rows, :] = x

        _staggered(tm // rb, [lambda r, _: x_rows(slice(r * rb, (r + 1) * rb)), normalise])

    h = jnp.dot(xn_ref[...], wu_ref[...], preferred_element_type=F32)
    h = jnp.square(jnp.maximum(h, 0.0)).astype(BF16)
    o_ref[...] += jnp.dot(h, wd_ref[...], preferred_element_type=F32)

    if final_norm:
        @pl.when(f == pl.num_programs(1) - 1)
        def _():
            o_ref[...] = _rms(o_ref[...], fw_ref[...])


def _mlp_kernel(x_ref, nw_ref, wu_ref, wd_ref, fw_ref, o_ref, xn_ref, *, final_norm):
    _mlp_steps(lambda rows: x_ref[rows, :], nw_ref, wu_ref, wd_ref, fw_ref, o_ref, xn_ref, final_norm)


def _proj_mlp_kernel(a_ref, wa_ref, x_ref, nw_ref, wu_ref, wd_ref, fw_ref, o_ref, xn_ref, *, final_norm):
    def block_input(rows):
        return x_ref[rows, :] + jnp.dot(a_ref[rows, :], wa_ref[...], preferred_element_type=F32)

    _mlp_steps(block_input, nw_ref, wu_ref, wd_ref, fw_ref, o_ref, xn_ref, final_norm)


def mlp(x, nw, wu, wd, layer, fw, *, final_norm, proj=None, tm_cap=1024, tf_cap=1024):
    m, d = x.shape
    dff = wu.shape[2]
    tm, tf = _row_tile(m, tm_cap), _row_tile(dff, tf_cap)
    in_specs = [
        pl.BlockSpec((tm, d), lambda i, f: (i, 0)),
        pl.BlockSpec((1, d), lambda i, f: (0, 0)),
        pl.BlockSpec((None, d, tf), lambda i, f: (layer, 0, f)),
        pl.BlockSpec((None, tf, d), lambda i, f: (layer, f, 0)),
        pl.BlockSpec((1, d), lambda i, f: (0, 0)),
    ]
    args = [x, nw.reshape(1, d), wu, wd, fw.reshape(1, d)]
    body = _mlp_kernel
    if proj is not None:
        a, wa = proj
        k = a.shape[1]
        in_specs = [pl.BlockSpec((tm, k), lambda i, f: (i, 0)),
                    pl.BlockSpec((k, d), lambda i, f: (0, 0), pipeline_mode=pl.Buffered(1))] + in_specs
        args = [a, wa] + args
        body = _proj_mlp_kernel
    return pl.pallas_call(
        functools.partial(body, final_norm=final_norm),
        grid=(m // tm, dff // tf),
        in_specs=in_specs,
        out_specs=pl.BlockSpec((tm, d), lambda i, f: (i, 0)),
        out_shape=jax.ShapeDtypeStruct((m, d), F32),
        scratch_shapes=[pltpu.VMEM((tm, d), BF16)],
        compiler_params=_params("parallel", "arbitrary"),
        name="mlp" if proj is None else "proj_mlp",
    )(*args)


def _softplus(x):
    return jnp.maximum(x, 0.0) + jnp.log(1.0 + jnp.exp(-jnp.abs(x)))


def _split3(a):
    hi = a.astype(BF16)
    r1 = a - hi.astype(F32)
    mid = r1.astype(BF16)
    lo = (r1 - mid.astype(F32)).astype(BF16)
    return hi, mid, lo


def _ssd_kernel(xbc_ref, z_ref, dt_ref, conv0_ref, ssm0_ref,
                cw_ref, cb_ref, dtb_ref, alog_ref, dsk_ref, gw_ref,
                g_ref, convo_ref, ssmo_ref,
                xpad_ref, act_ref, state_ref, y_ref, *, q, n_heads):
    c = pl.program_id(1)
    p = SSM_HEAD_DIM
    d_inner = n_heads * p
    hpg = n_heads // N_GROUPS
    conv_dim = d_inner + 2 * N_GROUPS * D_STATE
    pad = SUBLANES

    @pl.when(c == 0)
    def _():
        xpad_ref[0:pad, :] = conv0_ref[0]
        state_ref[...] = ssm0_ref[0]

    xpad_ref[pad:pad + q, :] = xbc_ref[0]
    cblk = 512
    for j in range(conv_dim // cblk):
        cs = slice(j * cblk, (j + 1) * cblk)
        conv = cb_ref[:, cs]
        for k in range(CONV_K):
            off = pad - (CONV_K - 1) + k
            conv = conv + xpad_ref[off:off + q, cs] * cw_ref[k:k + 1, cs]
        act_ref[:, cs] = _silu(conv)
    xpad_ref[0:pad, :] = xpad_ref[q:q + pad, :]

    dt = _softplus(dt_ref[0] + dtb_ref[...])
    a = dt * (-jnp.exp(alog_ref[...]))
    row = lax.broadcasted_iota(jnp.int32, (q, q), 0)
    col = lax.broadcasted_iota(jnp.int32, (q, q), 1)
    causal = row >= col
    tri = jnp.where(causal, 1.0, 0.0).astype(BF16)
    acs = None
    for part in _split3(a):
        t = jnp.dot(tri, part, preferred_element_type=F32)
        acs = t if acs is None else acs + t
    if q < LANES:
        acs_sq = jnp.concatenate([acs, jnp.zeros((LANES - q, LANES), F32)], axis=0)
    else:
        acs_sq = acs
    acs_t = acs_sq.T
    eacs = jnp.exp(acs)
    last = acs[q - 1:q, :]
    dte = jnp.exp(last - acs)
    cdecay = jnp.exp(last)

    for g in range(N_GROUPS):
        b0 = d_inner + g * D_STATE
        c0 = d_inner + N_GROUPS * D_STATE + g * D_STATE
        bb = act_ref[:, b0:b0 + D_STATE].astype(BF16)
        cc = act_ref[:, c0:c0 + D_STATE].astype(BF16)
        cbm = lax.dot_general(cc, bb, _NT, preferred_element_type=F32)
        for pair in range(hpg // 2):
            xs2 = act_ref[:, (g * hpg + 2 * pair) * p:(g * hpg + 2 * pair + 2) * p]
            dsk2 = dsk_ref[:, (g * hpg + 2 * pair) * p:(g * hpg + 2 * pair + 2) * p]
            ys = []
            for e in range(2):
                h = g * hpg + 2 * pair + e
                xs = xs2[:, e * p:(e + 1) * p]
                seg = acs[:, h:h + 1] - acs_t[h:h + 1, 0:q]
                decay = jnp.exp(jnp.where(causal, seg, -jnp.inf))
                m = (cbm * decay).astype(BF16)
                xdt = xs * dt[:, h:h + 1]
                y = jnp.dot(m, xdt.astype(BF16), preferred_element_type=F32)
                st = state_ref[h * p:(h + 1) * p, :]
                yoff = lax.dot_general(cc, st.astype(BF16), _NT, preferred_element_type=F32)
                y = y + yoff * eacs[:, h:h + 1] + dsk2[:, e * p:(e + 1) * p] * xs
                ys.append(y)
                xdtw = (xdt * dte[:, h:h + 1]).astype(BF16)
                snew = lax.dot_general(xdtw, bb, _TN, preferred_element_type=F32)
                state_ref[h * p:(h + 1) * p, :] = (
                    jnp.broadcast_to(cdecay[:, h:h + 1], (p, D_STATE)) * st + snew)
            y_ref[:, (g * hpg + 2 * pair) * p:(g * hpg + 2 * pair + 2) * p] = (
                jnp.concatenate(ys, axis=1))

    gsz = d_inner // N_GROUPS
    for g in range(N_GROUPS):
        gs = slice(g * gsz, (g + 1) * gsz)
        gg = y_ref[:, gs] * _silu(z_ref[0, :, gs])
        ms = jnp.mean(gg * gg, axis=-1, keepdims=True)
        g_ref[0, :, gs] = (gg * lax.rsqrt(ms + EPS) * gw_ref[:, gs]).astype(BF16)

    @pl.when(c == pl.num_programs(1) - 1)
    def _():
        convo_ref[0] = xpad_ref[0:pad, :]
        ssmo_ref[0] = state_ref[...]


def ssd_mixer(xbcz, dt, conv0, ssm0, cw, cb, dtb, alog, dsk, gw, *, shared_state):
    bsz, length, _ = xbcz.shape
    conv_dim = cw.shape[1]
    d_inner = gw.shape[1]
    n_heads = d_inner // SSM_HEAD_DIM
    q = SSD_CHUNK if length % SSD_CHUNK == 0 else length
    assert q % SUBLANES == 0 and q >= SUBLANES and conv_dim % d_inner == 0
    nc = length // q
    zblk = conv_dim // d_inner
    if shared_state:
        st_map = lambda b, c: (0, 0, 0)
    else:
        st_map = lambda b, c: (b, 0, 0)
    const = lambda b, c: (0, 0)
    return pl.pallas_call(
        functools.partial(_ssd_kernel, q=q, n_heads=n_heads),
        grid=(bsz, nc),
        in_specs=[
            pl.BlockSpec((1, q, conv_dim), lambda b, c: (b, c, 0)),
            pl.BlockSpec((1, q, d_inner), lambda b, c: (b, c, zblk)),
            pl.BlockSpec((1, q, LANES), lambda b, c: (b, c, 0)),
            pl.BlockSpec((1, SUBLANES, conv_dim), st_map),
            pl.BlockSpec((1, n_heads * SSM_HEAD_DIM, D_STATE), st_map),
            pl.BlockSpec((CONV_K, conv_dim), const),
            pl.BlockSpec((1, conv_dim), const),
            pl.BlockSpec((1, LANES), const),
            pl.BlockSpec((1, LANES), const),
            pl.BlockSpec((1, d_inner), const),
            pl.BlockSpec((1, d_inner), const),
        ],
        out_specs=[
            pl.BlockSpec((1, q, d_inner), lambda b, c: (b, c, 0)),
            pl.BlockSpec((1, SUBLANES, conv_dim), lambda b, c: (b, 0, 0)),
            pl.BlockSpec((1, n_heads * SSM_HEAD_DIM, D_STATE), lambda b, c: (b, 0, 0)),
        ],
        out_shape=[
            jax.ShapeDtypeStruct((bsz, length, d_inner), BF16),
            jax.ShapeDtypeStruct((bsz, SUBLANES, conv_dim), F32),
            jax.ShapeDtypeStruct((bsz, n_heads * SSM_HEAD_DIM, D_STATE), F32),
        ],
        scratch_shapes=[
            pltpu.VMEM((q + SUBLANES, conv_dim), F32),
            pltpu.VMEM((q, conv_dim), F32),
            pltpu.VMEM((n_heads * SSM_HEAD_DIM, D_STATE), F32),
            pltpu.VMEM((q, d_inner), F32),
        ],
        compiler_params=_params("parallel", "arbitrary"),
        name="ssd_mixer",
    )(xbcz, xbcz, dt, conv0, ssm0, cw, cb, dtb, alog, dsk, gw)


def _expansion_matrix(n_heads):
    h = jnp.arange(LANES)[:, None]
    ex = h == jnp.arange(n_heads * SSM_HEAD_DIM)[None, :] // SSM_HEAD_DIM
    return jnp.tile(ex.astype(BF16), (3, 1))


def _store_act_tile(j, act, xs_ref, bb_ref, cc_ref, d_inner, bc):
    lo = j * LANES
    if lo < d_inner:
        xs_ref[:, lo:lo + LANES] = act
    elif lo < d_inner + bc:
        bb_ref[:, lo - d_inner:lo - d_inner + LANES] = act.astype(BF16)
    else:
        cc_ref[:, lo - d_inner - bc:lo - d_inner - bc + LANES] = act.astype(BF16)


def _ssd_token_math(dt_raw, dtb_ref, alog_ref, lhs3_ref, acst_ref, *, seg_len):
    q = dt_raw.shape[0]
    dt = _softplus(dt_raw + dtb_ref[...])
    a = dt * (-jnp.exp(alog_ref[...]))
    row = lax.broadcasted_iota(jnp.int32, (q, q), 0)
    col = lax.broadcasted_iota(jnp.int32, (q, q), 1)
    mask = row >= col
    if seg_len != q:
        mask = jnp.logical_and(mask, row // seg_len == col // seg_len)
        seg_end = (row // seg_len) * seg_len + (seg_len - 1)
    tri = jnp.where(mask, 1.0, 0.0).astype(BF16)
    acs = None
    for part in _split3(a):
        t = jnp.dot(tri, part, preferred_element_type=F32)
        acs = t if acs is None else acs + t
    if seg_len == q:
        last = acs[q - 1:q, :]
    else:
        sel = jnp.where(col == seg_end, 1.0, 0.0).astype(BF16)
        last = None
        for part in _split3(acs):
            t = jnp.dot(sel, part, preferred_element_type=F32)
            last = t if last is None else last + t
    acst_ref[...] = (acs * LOG2_E).T
    eacs = jnp.exp(acs)
    stack = jnp.concatenate([dt, eacs, jnp.exp(last - acs)], axis=0)
    lhs3_ref[...] = jnp.concatenate(_split3(stack), axis=1)
    return mask, eacs


def _staggered(n, stages):
    carried = {}
    for t in range(n + len(stages) - 1):
        for k, stage in enumerate(stages):
            g = t - k
            if 0 <= g < n:
                carried[g] = stage(g, carried.get(g))


def _ssd_group_inputs(g, xs_ref, bb_ref, cc_ref, lhs3_ref, ex3_ref, hpg):
    q = xs_ref.shape[0]
    gch = hpg * SSM_HEAD_DIM
    gs = slice(g * gch, (g + 1) * gch)
    bb = bb_ref[:, g * D_STATE:(g + 1) * D_STATE]
    cc = cc_ref[:, g * D_STATE:(g + 1) * D_STATE]
    cbm = lax.dot_general(cc, bb, _NT, preferred_element_type=F32)
    ex = jnp.dot(lhs3_ref[...], ex3_ref[:, gs], preferred_element_type=F32)
    xs = xs_ref[:, gs]
    return dict(bb=bb, cc=cc, cbm=cbm, xs=xs, xdt=xs * ex[0:q], eax=ex[q:2 * q], dtex=ex[2 * q:3 * q])


def _ssd_group_diag(g, v, mask, acst_ref, hpg):
    q = v["xs"].shape[0]
    p = SSM_HEAD_DIM
    lane_head = lax.broadcasted_iota(jnp.int32, (q, hpg * p), 1) // p
    ms, rhs = [], []
    for r in range(hpg):
        h = g * hpg + r
        rowb = jnp.broadcast_to(acst_ref[h:h + 1, :], (q, q))
        decay = jnp.exp2(jnp.where(mask, rowb.T - rowb, -jnp.inf))
        ms.append((v["cbm"] * decay).astype(BF16))
        rhs.append(jnp.where(lane_head == r, v["xdt"], 0.0).astype(BF16))
    y = jnp.dot(jnp.concatenate(ms, axis=1), jnp.concatenate(rhs, axis=0),
                preferred_element_type=F32)
    return dict(v, y=y, cbm=None)


def _ssd_chunk_kernel(xs_ref, bb_ref, cc_ref, sz_ref, dt_ref, ssm0_ref,
                      dtb_ref, alog_ref, dsk_ref, gw_ref, ex3_ref,
                      g_ref, ssmo_ref,
                      st_ref, lhs3_ref, acst_ref, *, n_heads, cps):
    c = pl.program_id(1)
    q = SSD_CHUNK
    hpg = n_heads // N_GROUPS
    gch = hpg * SSM_HEAD_DIM

    @pl.when(c == 0)
    def _():
        st_ref[...] = ssm0_ref[0].T

    rows = [pl.ds(h * q, q) for h in range(cps)]
    masks = [_ssd_token_math(dt_ref[0, rows[h], :], dtb_ref, alog_ref, lhs3_ref.at[h], acst_ref.at[h],
                             seg_len=q)[0] for h in range(cps)]

    def finish(item, v):
        h, g = divmod(item, N_GROUPS)
        gs = slice(g * gch, (g + 1) * gch)
        st = st_ref[:, gs]
        y = (v["y"] + jnp.dot(v["cc"], st.astype(BF16), preferred_element_type=F32) * v["eax"]
             + dsk_ref[:, gs] * v["xs"])
        gg = y * sz_ref[0, rows[h], gs].astype(F32)
        msq = jnp.mean(gg * gg, axis=-1, keepdims=True)
        g_ref[0, rows[h], gs] = (gg * lax.rsqrt(msq + EPS) * gw_ref[:, gs]).astype(BF16)
        xdtw = (v["xdt"] * v["dtex"]).astype(BF16)
        snew = lax.dot_general(v["bb"], xdtw, _TN, preferred_element_type=F32)
        st_ref[:, gs] = st * v["eax"][q - 1:q, :] + snew

    def inputs(item, _):
        h, g = divmod(item, N_GROUPS)
        return _ssd_group_inputs(g, xs_ref.at[0, rows[h]], bb_ref.at[0, rows[h]], cc_ref.at[0, rows[h]],
                                 lhs3_ref.at[h], ex3_ref, hpg)

    def diag(item, v):
        h, g = divmod(item, N_GROUPS)
        return _ssd_group_diag(g, v, masks[h], acst_ref.at[h], hpg)

    _staggered(cps * N_GROUPS, [inputs, diag, finish])

    @pl.when(c == pl.num_programs(1) - 1)
    def _():
        ssmo_ref[0] = st_ref[...].T


def ssd_chunked(xs, bb, cc, sz, dt, ssm0, dtb, alog, dsk, gw, *, chunks_per_step=4):
    bsz, length, d_inner = xs.shape
    bc = bb.shape[2]
    n_heads = d_inner // SSM_HEAD_DIM
    q = SSD_CHUNK
    assert length % q == 0 and bc == N_GROUPS * D_STATE and n_heads % N_GROUPS == 0
    cps = _row_tile(length // q, chunks_per_step)
    qs = cps * q
    ex3 = _expansion_matrix(n_heads)
    rows = lambda b, c: (b, c, 0)
    const = lambda b, c: (0, 0)
    return pl.pallas_call(
        functools.partial(_ssd_chunk_kernel, n_heads=n_heads, cps=cps),
        grid=(bsz, length // qs),
        in_specs=[
            pl.BlockSpec((1, qs, d_inner), rows),
            pl.BlockSpec((1, qs, bc), rows),
            pl.BlockSpec((1, qs, bc), rows),
            pl.BlockSpec((1, qs, d_inner), rows),
            pl.BlockSpec((1, qs, LANES), rows),
            pl.BlockSpec((1, d_inner, D_STATE), lambda b, c: (0, 0, 0)),
            pl.BlockSpec((1, LANES), const),
            pl.BlockSpec((1, LANES), const),
            pl.BlockSpec((1, d_inner), const),
            pl.BlockSpec((1, d_inner), const),
            pl.BlockSpec(ex3.shape, const),
        ],
        out_specs=[
            pl.BlockSpec((1, qs, d_inner), rows),
            pl.BlockSpec((1, d_inner, D_STATE), lambda b, c: (b, 0, 0)),
        ],
        out_shape=[
            jax.ShapeDtypeStruct((bsz, length, d_inner), BF16),
            jax.ShapeDtypeStruct((bsz, d_inner, D_STATE), F32),
        ],
        scratch_shapes=[
            pltpu.VMEM((D_STATE, d_inner), F32),
            pltpu.VMEM((cps, 3 * q, 3 * LANES), BF16),
            pltpu.VMEM((cps, LANES, q), F32),
        ],
        compiler_params=_params("parallel", "arbitrary"),
        name="ssd_chunked",
    )(xs, bb, cc, sz, dt, ssm0, dtb, alog, dsk, gw, ex3)


def _ssd_step_tokens_kernel(xbc_ref, dt_ref, conv0_ref, cw_ref, cb_ref, dtb_ref, alog_ref, dsk_ref,
                            ex3_ref,
                            ypart_ref, eaxo_ref, xdtw_ref, bbo_ref, cco_ref, eacs_ref,
                            xpad_ref, xs_ref, bb_ref, cc_ref, lhs3_ref, acst_ref, *, n_heads, t):
    q = SSD_CHUNK
    p = SSM_HEAD_DIM
    d_inner = n_heads * p
    hpg = n_heads // N_GROUPS
    gch = hpg * p
    bc = N_GROUPS * D_STATE
    ntile = (d_inner + 2 * bc) // LANES
    nseq = q // t
    slot = 2 * t
    span = nseq * slot - t

    for j in range(ntile):
        cs = slice(j * LANES, (j + 1) * LANES)
        for i in range(nseq):
            xpad_ref[j, i * slot:i * slot + t, :] = conv0_ref[i, :, cs]
            xpad_ref[j, i * slot + t:(i + 1) * slot, :] = xbc_ref[i * t:(i + 1) * t, cs]
        conv = cb_ref[:, cs]
        for k in range(CONV_K):
            off = t - (CONV_K - 1) + k
            conv = conv + xpad_ref[j, off:off + span, :] * cw_ref[k:k + 1, cs]
        act = _silu(jnp.concatenate([conv[i * slot:i * slot + t] for i in range(nseq)], axis=0))
        _store_act_tile(j, act, xs_ref, bb_ref, cc_ref, d_inner, bc)
        lo = j * LANES
        if d_inner <= lo < d_inner + bc:
            bbo_ref[:, lo - d_inner:lo - d_inner + LANES] = act
        elif lo >= d_inner + bc:
            cco_ref[:, lo - d_inner - bc:lo - d_inner - bc + LANES] = act

    mask, eacs = _ssd_token_math(dt_ref[...], dtb_ref, alog_ref, lhs3_ref, acst_ref, seg_len=t)
    eacs_ref[...] = eacs
    def emit(g, v):
        gs = slice(g * gch, (g + 1) * gch)
        ypart_ref[:, gs] = v["y"] + dsk_ref[:, gs] * v["xs"]
        eaxo_ref[:, gs] = v["eax"]
        xdtw_ref[:, gs] = v["xdt"] * v["dtex"]

    _staggered(N_GROUPS, [
        lambda g, _: _ssd_group_inputs(g, xs_ref, bb_ref, cc_ref, lhs3_ref, ex3_ref, hpg),
        lambda g, v: _ssd_group_diag(g, v, mask, acst_ref, hpg),
        emit,
    ])


def ssd_step_tokens(xbcz, dt, conv0, cw, cb, dtb, alog, dsk, *, t):
    n_tok = xbcz.shape[0]
    conv_dim = cw.shape[1]
    d_inner = dsk.shape[1]
    n_heads = d_inner // SSM_HEAD_DIM
    q = SSD_CHUNK
    bc = N_GROUPS * D_STATE
    assert t == SUBLANES and n_tok % q == 0
    ex3 = _expansion_matrix(n_heads)
    const = lambda s: (0, 0)
    rows = lambda s: (s, 0)
    f32 = lambda n: jax.ShapeDtypeStruct((n_tok, n), F32)
    return pl.pallas_call(
        functools.partial(_ssd_step_tokens_kernel, n_heads=n_heads, t=t),
        grid=(n_tok // q,),
        in_specs=[
            pl.BlockSpec((q, conv_dim), rows),
            pl.BlockSpec((q, LANES), rows),
            pl.BlockSpec((q // t, SUBLANES, conv_dim), lambda s: (s, 0, 0)),
            pl.BlockSpec((CONV_K, conv_dim), const),
            pl.BlockSpec((1, conv_dim), const),
            pl.BlockSpec((1, LANES), const),
            pl.BlockSpec((1, LANES), const),
            pl.BlockSpec((1, d_inner), const),
            pl.BlockSpec(ex3.shape, const),
        ],
        out_specs=[
            pl.BlockSpec((q, d_inner), rows),
            pl.BlockSpec((q, d_inner), rows),
            pl.BlockSpec((q, d_inner), rows),
            pl.BlockSpec((q, bc), rows),
            pl.BlockSpec((q, bc), rows),
            pl.BlockSpec((q, LANES), rows),
        ],
        out_shape=[f32(d_inner), f32(d_inner), f32(d_inner), f32(bc), f32(bc), f32(LANES)],
        scratch_shapes=[
            pltpu.VMEM((conv_dim // LANES, 2 * q, LANES), F32),
            pltpu.VMEM((q, d_inner), F32),
            pltpu.VMEM((q, bc), BF16),
            pltpu.VMEM((q, bc), BF16),
            pltpu.VMEM((3 * q, 3 * LANES), BF16),
            pltpu.VMEM((LANES, q), F32),
        ],
        compiler_params=_params("parallel"),
        name="ssd_step_tokens",
    )(xbcz, dt, conv0, cw, cb, dtb, alog, dsk, ex3)


def _ssd_step_state_kernel(cd_ref, st_ref, cc_ref, bb_ref, xdtw_ref, eax_ref, ypart_ref, z_ref, gw_ref,
                           *rest, n_heads, t, nseq, n_fill):
    g_ref, sto_ref = rest[-2:]
    s = pl.program_id(1)
    p = SSM_HEAD_DIM
    hpg = n_heads // N_GROUPS
    gch = hpg * p

    def update():
        for g in range(N_GROUPS):
            gs = slice(g * gch, (g + 1) * gch)
            ns = slice(g * D_STATE, (g + 1) * D_STATE)
            ys = []
            for i in range(nseq):
                rows = slice(i * t, (i + 1) * t)
                h0 = st_ref[0, i, gs, :]
                yoff = lax.dot_general(cc_ref[rows, ns].astype(BF16), h0.astype(BF16), _NT,
                                       preferred_element_type=F32)
                ys.append(ypart_ref[rows, gs] + yoff * eax_ref[rows, gs])
                upd = lax.dot_general(xdtw_ref[rows, gs].astype(BF16), bb_ref[rows, ns].astype(BF16), _TN,
                                      preferred_element_type=F32)
                for r in range(hpg):
                    hs = slice(r * p, (r + 1) * p)
                    sto_ref[0, i, g * gch + r * p:g * gch + (r + 1) * p, :] = (
                        h0[hs] * cd_ref[s * nseq + i, g * hpg + r] + upd[hs])
            gg = jnp.concatenate(ys, axis=0) * _silu(z_ref[:, gs])
            msq = jnp.mean(gg * gg, axis=-1, keepdims=True)
            g_ref[:, gs] = (gg * lax.rsqrt(msq + EPS) * gw_ref[:, gs]).astype(BF16)

    if n_fill == 0:
        update()
    else:
        @pl.when(pl.program_id(0) < n_fill)
        def _():
            sto_ref[...] = jnp.zeros(sto_ref.shape, F32)

        pl.when(pl.program_id(0) == n_fill)(update)


def ssd_step_state(cd, state_all, layer, cc, bb, xdtw, eax, ypart, xbcz, gw, stacked_out, *, t, seqs_per_step=8):
    n_layers, n_seq, d_inner, d_state = state_all.shape
    n_tok = ypart.shape[0]
    bc = cc.shape[1]
    conv_dim = xbcz.shape[1] - d_inner
    nseq = _row_tile(n_seq, seqs_per_step)
    rt = nseq * t
    n_fill = n_layers - 1 if stacked_out is None else 0

    def active(l, s):
        return jnp.where(l == n_fill, s, 0)

    def out_layer(l):
        return jnp.where(l < n_fill, l + jnp.where(l >= layer, 1, 0), layer)

    rows = lambda l, s, cd_: (active(l, s), 0)
    st_map = lambda l, s, cd_: (layer, active(l, s), 0, 0)
    in_specs = [
        pl.BlockSpec((1, nseq, d_inner, d_state), st_map),
        pl.BlockSpec((rt, bc), rows),
        pl.BlockSpec((rt, bc), rows),
        pl.BlockSpec((rt, d_inner), rows),
        pl.BlockSpec((rt, d_inner), rows),
        pl.BlockSpec((rt, d_inner), rows),
        pl.BlockSpec((rt, d_inner), lambda l, s, cd_: (active(l, s), conv_dim // d_inner)),
        pl.BlockSpec((1, d_inner), lambda l, s, cd_: (0, 0)),
    ]
    args = [cd, state_all, cc, bb, xdtw, eax, ypart, xbcz, gw]
    aliases = {}
    if stacked_out is not None:
        in_specs.append(pl.BlockSpec(memory_space=pl.ANY))
        aliases = {len(args): 1}
        args.append(stacked_out)
    return pl.pallas_call(
        functools.partial(_ssd_step_state_kernel, n_heads=d_inner // SSM_HEAD_DIM, t=t, nseq=nseq,
                          n_fill=n_fill),
        grid_spec=pltpu.PrefetchScalarGridSpec(
            num_scalar_prefetch=1,
            grid=(n_fill + 1, n_seq // nseq),
            in_specs=in_specs,
            out_specs=[
                pl.BlockSpec((rt, d_inner), rows),
                pl.BlockSpec((1, nseq, d_inner, d_state), lambda l, s, cd_: (out_layer(l), s, 0, 0)),
            ],
        ),
        out_shape=[
            jax.ShapeDtypeStruct((n_tok, d_inner), BF16),
            jax.ShapeDtypeStruct((n_layers, n_seq, d_inner, d_state), F32),
        ],
        input_output_aliases=aliases,
        compiler_params=_params("arbitrary", "arbitrary"),
        name="ssd_step_state",
    )(*args)


def _attn_prompt_kernel(sink_ref, q_ref, kvc_ref, kvp_ref, kvm_ref, o_ref, *, n_q_heads):
    n = pl.program_id(1)
    d = ATTN_HEAD_DIM
    kvw = N_KV_HEADS * d
    rep = n_q_heads // N_KV_HEADS
    scale = d ** -0.5
    w = q_ref.shape[1]
    rows = rep * w
    qi = lax.broadcasted_iota(jnp.int32, (rows, w), 0) % w
    ci = lax.broadcasted_iota(jnp.int32, (rows, w), 1)
    from_prev = ci > qi
    no_prev = jnp.where(n > 0, 0.0, -jnp.inf)
    kvm, kvp, kvc = kvm_ref[...], kvp_ref[0], kvc_ref[0]
    outs = []
    for k in range(N_KV_HEADS):
        ks, vs = slice(k * d, (k + 1) * d), slice(kvw + k * d, kvw + (k + 1) * d)
        qk = q_ref[0, :, k * rep * d:(k + 1) * rep * d]
        q4 = jnp.concatenate([qk[:, r * d:(r + 1) * d] for r in range(rep)], axis=0)
        q4 = (q4 * scale).astype(BF16)
        k2 = jnp.concatenate([kvp[:, ks], kvc[:, ks]], axis=0).astype(BF16)
        v2 = jnp.concatenate([kvp[:, vs], kvc[:, vs]], axis=0).astype(BF16)
        s2 = lax.dot_general(q4, k2, _NT, preferred_element_type=F32)
        s_w = jnp.where(from_prev, s2[:, :w] + no_prev, s2[:, w:])
        s_m = lax.dot_general(q4, kvm[:, ks].astype(BF16), _NT, preferred_element_type=F32)
        p_ws, p_ms, dens = [], [], []
        for r in range(rep):
            sl = slice(r * w, (r + 1) * w)
            sink = sink_ref[k * rep + r]
            mx = jnp.maximum(jnp.maximum(jnp.max(s_w[sl], axis=-1, keepdims=True),
                                         jnp.max(s_m[sl], axis=-1, keepdims=True)), sink)
            p_w = jnp.exp(s_w[sl] - mx)
            p_m = jnp.exp(s_m[sl] - mx)
            dens.append(jnp.exp(sink - mx) + jnp.sum(p_w, axis=-1, keepdims=True)
                        + jnp.sum(p_m, axis=-1, keepdims=True))
            p_ws.append(p_w)
            p_ms.append(p_m.astype(BF16))
        p_w = jnp.concatenate(p_ws, axis=0)
        p2 = jnp.concatenate([jnp.where(from_prev, p_w, 0.0), jnp.where(from_prev, 0.0, p_w)],
                             axis=1).astype(BF16)
        o4 = (jnp.dot(p2, v2, preferred_element_type=F32)
              + jnp.dot(jnp.concatenate(p_ms, axis=0), kvm[:, vs].astype(BF16), preferred_element_type=F32))
        outs.extend(o4[r * w:(r + 1) * w, :] / dens[r] for r in range(rep))
    o_ref[0] = jnp.concatenate(outs, axis=1).astype(BF16)


def attn_prompt(q, kv, kvm, sinks):
    bsz, s, dq = q.shape
    kvd = kv.shape[2]
    nb = s // WINDOW
    return pl.pallas_call(
        functools.partial(_attn_prompt_kernel, n_q_heads=dq // ATTN_HEAD_DIM),
        grid=(bsz, nb),
        in_specs=[
            pl.BlockSpec(memory_space=pltpu.SMEM),
            pl.BlockSpec((1, WINDOW, dq), lambda b, n: (b, n, 0)),
            pl.BlockSpec((1, WINDOW, kvd), lambda b, n: (b, n, 0)),
            pl.BlockSpec((1, WINDOW, kvd), lambda b, n: (b, jnp.maximum(n - 1, 0), 0)),
            pl.BlockSpec(kvm.shape, lambda b, n: (0, 0)),
        ],
        out_specs=pl.BlockSpec((1, WINDOW, dq), lambda b, n: (b, n, 0)),
        out_shape=jax.ShapeDtypeStruct((bsz, s, dq), BF16),
        compiler_params=_params("parallel", "arbitrary"),
        name="attn_prompt",
    )(sinks, q, kv, kv, kvm)


def _attn_sample_kernel(sink_ref, q_ref, kvn_ref, ck_ref, cv_ref, kvm_ref, o_ref, *, n_q_heads, n_meta):
    d = ATTN_HEAD_DIM
    kvw = N_KV_HEADS * d
    rep = n_q_heads // N_KV_HEADS
    scale = d ** -0.5
    nseq, t, _ = q_ref.shape
    w_buf = ck_ref.shape[1]
    rows = rep * t
    nx = n_meta + t
    tq_buf = lax.broadcasted_iota(jnp.int32, (rows, w_buf), 0) % t
    pos_buf = PAST_LEN - w_buf + lax.broadcasted_iota(jnp.int32, (rows, w_buf), 1)
    buf_mask = jnp.logical_and(PAST_LEN + tq_buf - pos_buf < WINDOW, pos_buf >= n_meta)
    tq_x = lax.broadcasted_iota(jnp.int32, (rows, nx), 0) % t
    c_x = lax.broadcasted_iota(jnp.int32, (rows, nx), 1) - n_meta
    x_mask = jnp.logical_or(c_x < 0, jnp.logical_and(c_x <= tq_x, tq_x - c_x < WINDOW))
    head_in_group = lax.broadcasted_iota(jnp.int32, (rows, 1), 0) // t
    kvm = kvm_ref[...]
    sinks = []
    for k in range(N_KV_HEADS):
        sink = jnp.zeros((rows, 1), F32)
        for r in range(rep):
            sink = jnp.where(head_in_group == r, sink_ref[k * rep + r], sink)
        sinks.append(sink)
    chains = [(i, k) for i in range(nseq) for k in range(N_KV_HEADS)]

    scores, values = {}, {}
    for i in range(nseq):
        q, kvn, ck, cv = q_ref[i], kvn_ref[i], ck_ref[i], cv_ref[i]
        for k in range(N_KV_HEADS):
            ks, vs = slice(k * d, (k + 1) * d), slice(kvw + k * d, kvw + (k + 1) * d)
            q4 = jnp.concatenate([q[:, (k * rep + r) * d:(k * rep + r + 1) * d] for r in range(rep)], axis=0)
            q4 = (q4 * scale).astype(BF16)
            kx = jnp.concatenate([kvm[:, ks], kvn[:, ks]], axis=0).astype(BF16)
            vx = jnp.concatenate([kvm[:, vs], kvn[:, vs]], axis=0).astype(BF16)
            s_b = lax.dot_general(q4, ck[:, ks].astype(BF16), _NT, preferred_element_type=F32)
            s_x = lax.dot_general(q4, kx, _NT, preferred_element_type=F32)
            scores[i, k] = (jnp.where(buf_mask, s_b, -jnp.inf), jnp.where(x_mask, s_x, -jnp.inf))
            values[i, k] = (cv[:, ks].astype(BF16), vx)

    maxes = {}
    for c in chains:
        s_b, s_x = scores[c]
        maxes[c] = jnp.maximum(sinks[c[1]], jnp.maximum(jnp.max(s_b, axis=-1, keepdims=True),
                                                         jnp.max(s_x, axis=-1, keepdims=True)))
    probs, dens = {}, {}
    for c in chains:
        s_b, s_x = scores[c]
        p_b, p_x = jnp.exp(s_b - maxes[c]), jnp.exp(s_x - maxes[c])
        probs[c] = (p_b.astype(BF16), p_x.astype(BF16))
        dens[c] = (jnp.exp(sinks[c[1]] - maxes[c]) + jnp.sum(p_b, axis=-1, keepdims=True)
                   + jnp.sum(p_x, axis=-1, keepdims=True))
    outs = {}
    for c in chains:
        outs[c] = (jnp.dot(probs[c][0], values[c][0], preferred_element_type=F32)
                   + jnp.dot(probs[c][1], values[c][1], preferred_element_type=F32)) / dens[c]
    for i in range(nseq):
        heads = [outs[i, k][r * t:(r + 1) * t, :] for k in range(N_KV_HEADS) for r in range(rep)]
        o_ref[i] = jnp.concatenate(heads, axis=1).astype(BF16)


def attn_sample(q, kvn, ck, cv, kvm, sinks, *, seqs_per_step=8):
    bsz, t, dq = q.shape
    kvd = kvn.shape[2]
    w_buf = ck.shape[1]
    g = _row_tile(bsz, seqs_per_step)
    return pl.pallas_call(
        functools.partial(_attn_sample_kernel, n_q_heads=dq // ATTN_HEAD_DIM, n_meta=kvm.shape[0]),
        grid=(bsz // g,),
        in_specs=[
            pl.BlockSpec(memory_space=pltpu.SMEM),
            pl.BlockSpec((g, t, dq), lambda b: (b, 0, 0)),
            pl.BlockSpec((g, t, kvd), lambda b: (b, 0, 0)),
            pl.BlockSpec((g, w_buf, kvd // 2), lambda b: (b, 0, 0)),
            pl.BlockSpec((g, w_buf, kvd // 2), lambda b: (b, 0, 0)),
            pl.BlockSpec(kvm.shape, lambda b: (0, 0)),
        ],
        out_specs=pl.BlockSpec((g, t, dq), lambda b: (b, 0, 0)),
        out_shape=jax.ShapeDtypeStruct((bsz, t, dq), BF16),
        compiler_params=_params("parallel"),
        name="attn_sample",
    )(sinks, q, kvn, ck, cv, kvm)


def kernel(x_prompt, x_sample, state_conv, state_ssm, cache_k_win, cache_v_win, meta_tokens, a_norm_w, a_in_proj, a_conv_w, a_conv_b, a_dt_bias, a_log, a_d_skip, a_gate_norm_w, a_out_proj, kv_norm_w, w_kv, b_norm_w, w_q, attn_sinks, w_o, mlp_norm_w, w_up, w_down, final_norm_w):
    n_prompt, seq, d_model = x_prompt.shape
    n_dec, dec_seq, _ = x_sample.shape
    n_a = a_in_proj.shape[0]
    depth = w_up.shape[0]
    n_meta = meta_tokens.shape[0]
    d_inner = a_out_proj.shape[1]
    conv_dim = a_conv_w.shape[2]
    n_heads = a_log.shape[1]
    w_buf = cache_k_win.shape[1]
    kvw = N_KV_HEADS * ATTN_HEAD_DIM
    assert n_heads * SSM_HEAD_DIM == d_inner and n_heads <= LANES

    hm = meta_tokens.astype(F32)
    hp = x_prompt.reshape(n_prompt * seq, d_model)
    hs = x_sample.reshape(n_dec * dec_seq, d_model)

    def pad_lanes(v):
        return jnp.pad(v, (0, LANES - v.shape[0])).reshape(1, LANES)

    def pad_conv_state(s):
        return jnp.pad(s, ((0, 0), (SUBLANES - (CONV_K - 1), 0), (0, 0)))

    conv_p_list, ssm_p_list, conv_s_list, ssm_s_list = [], [], [], []
    step_path = dec_seq == SUBLANES and (n_dec * dec_seq) % SSD_CHUNK == 0
    ssm_in_all = state_ssm.reshape(n_a, n_dec, d_inner, D_STATE)
    ssm_s_all = None
    kvm = kv_p = kv_s = None
    wu, wd = w_up.astype(BF16), w_down.astype(BF16)
    for layer in range(depth):
        last = layer == depth - 1
        if layer < n_a:
            i = layer
            w_in = a_in_proj[i]
            w_main = w_in.astype(BF16)
            w_dt = jnp.pad(w_in[:, d_inner + conv_dim:], ((0, 0), (0, LANES - n_heads))).astype(BF16)
            w_out = a_out_proj[i].astype(BF16)
            prm = (a_conv_w[i], a_conv_b[i].reshape(1, conv_dim), pad_lanes(a_dt_bias[i]), pad_lanes(a_log[i]),
                   jnp.repeat(a_d_skip[i], SSM_HEAD_DIM).reshape(1, d_inner), a_gate_norm_w[i].reshape(1, d_inner))

            def mixer(h, bsz, length, conv0, ssm0, shared):
                if shared and length % SSD_CHUNK == 0:
                    xs_a, bb_a, cc_a, sz_a, dt, conv_o = in_proj_conv(
                        h, a_norm_w[i], w_main, w_dt, 0.5 * prm[0], 0.5 * prm[1], conv0,
                        seq_len=length, d_inner=d_inner)
                    split = lambda t: t.reshape(bsz, length, t.shape[1])
                    g, ssm_o = ssd_chunked(split(xs_a), split(bb_a), split(cc_a), split(sz_a), split(dt),
                                           ssm0, *prm[2:])
                else:
                    xbcz, dt = in_proj(h, a_norm_w[i], w_main, w_dt, d_inner=d_inner, n_main=conv_dim + d_inner)
                    g, conv_o, ssm_o = ssd_mixer(xbcz.reshape(bsz, length, conv_dim + d_inner),
                                                 dt.reshape(bsz, length, LANES), conv0, ssm0, *prm,
                                                 shared_state=shared)
                return g.reshape(bsz * length, d_inner), conv_o, ssm_o

            zero_conv = jnp.zeros((1, SUBLANES, conv_dim), F32)
            zero_ssm = jnp.zeros((1, d_inner, D_STATE), F32)
            g_m, conv_m, ssm_m = mixer(hm, 1, n_meta, zero_conv, zero_ssm, True)
            g_p, conv_p, ssm_p = mixer(hp, n_prompt, seq, conv_m, ssm_m, True)
            conv0_s = pad_conv_state(state_conv[i])
            if step_path:
                xbcz_s, dt_s = in_proj(hs, a_norm_w[i], w_main, w_dt, d_inner=d_inner, n_main=conv_dim + d_inner)
                ypart, eax, xdtw, bb_s, cc_s, eacs = ssd_step_tokens(xbcz_s, dt_s, conv0_s, *prm[:5], t=dec_seq)
                cdecay = eacs.reshape(n_dec, dec_seq, LANES)[:, dec_seq - 1]
                g_s, ssm_s_all = ssd_step_state(cdecay, ssm_in_all, i, cc_s, bb_s, xdtw, eax, ypart, xbcz_s,
                                                prm[5], ssm_s_all, t=dec_seq)
                conv_s = xbcz_s.reshape(n_dec, dec_seq, -1)[:, dec_seq - (CONV_K - 1):, :conv_dim]
            else:
                g_s, conv_s, ssm_s = mixer(hs, n_dec, dec_seq, conv0_s, ssm_in_all[i], False)
                conv_s = conv_s[:, SUBLANES - (CONV_K - 1):]
                ssm_s_list.append(ssm_s)
            conv_p_list.append(conv_p[:, SUBLANES - (CONV_K - 1):])
            ssm_p_list.append(ssm_p.reshape(n_prompt, n_heads, SSM_HEAD_DIM, D_STATE))
            conv_s_list.append(conv_s)
            hm = mlp(hm, mlp_norm_w[layer], wu, wd, layer, final_norm_w, final_norm=False, proj=(g_m, w_out))
            proj_p, proj_s = (g_p, w_out), (g_s, w_out)
        else:
            j = layer - n_a
            wq = w_q[j].astype(BF16)
            wo = w_o[j].astype(BF16)
            dq = wq.shape[1]
            if j == 0:
                wkv = w_kv.astype(BF16)
                kvm, = norm_matmul_heads(hm, [(kv_norm_w, wkv)])
                q_p, kv_p = norm_matmul_heads(hp, [(b_norm_w[j], wq), (kv_norm_w, wkv)])
                q_s, kv_s = norm_matmul_heads(hs, [(b_norm_w[j], wq), (kv_norm_w, wkv)])
                kv_p = kv_p.reshape(n_prompt, seq, 2 * kvw)
                kv_s = kv_s.reshape(n_dec, dec_seq, 2 * kvw)
            else:
                q_p, = norm_matmul_heads(hp, [(b_norm_w[j], wq)])
                q_s, = norm_matmul_heads(hs, [(b_norm_w[j], wq)])
            q_p = q_p.reshape(n_prompt, seq, dq)
            q_s = q_s.reshape(n_dec, dec_seq, dq)
            o_p = attn_prompt(q_p, kv_p, kvm, attn_sinks[j])
            o_s = attn_sample(q_s, kv_s, cache_k_win.reshape(n_dec, w_buf, kvw),
                              cache_v_win.reshape(n_dec, w_buf, kvw), kvm, attn_sinks[j])
            proj_p = (o_p.reshape(n_prompt * seq, dq), wo)
            proj_s = (o_s.reshape(n_dec * dec_seq, dq), wo)
        hp = mlp(hp, mlp_norm_w[layer], wu, wd, layer, final_norm_w, final_norm=last, proj=proj_p)
        hs = mlp(hs, mlp_norm_w[layer], wu, wd, layer, final_norm_w, final_norm=last, proj=proj_s)

    y_prompt = hp.reshape(n_prompt, seq, d_model)
    y_sample = hs.reshape(n_dec, dec_seq, d_model)
    kv_heads = (N_KV_HEADS, ATTN_HEAD_DIM)
    k_p = kv_p[:, seq - w_buf:, :kvw].reshape((n_prompt, w_buf) + kv_heads)
    v_p = kv_p[:, seq - w_buf:, kvw:].reshape((n_prompt, w_buf) + kv_heads)
    k_s = kv_s[:, :, :kvw].reshape((n_dec, dec_seq) + kv_heads)
    v_s = kv_s[:, :, kvw:].reshape((n_dec, dec_seq) + kv_heads)
    k_s_win = jnp.concatenate([cache_k_win, k_s], axis=1)[:, -w_buf:]
    v_s_win = jnp.concatenate([cache_v_win, v_s], axis=1)[:, -w_buf:]
    if not step_path:
        ssm_s_all = jnp.stack(ssm_s_list)
    return (y_prompt, y_sample, jnp.stack(conv_p_list), jnp.stack(ssm_p_list), k_p, v_p,
            jnp.stack(conv_s_list), ssm_s_all.reshape(state_ssm.shape), k_s_win, v_s_win)
```
